```python
import jax, jax.numpy as jnp
from jax import lax
import numpy as np

D_MODEL = 2048
BATCH = 8
SEQ = 4096
DEPTH = 2

N_META = 16
D_ATTN = D_MODEL // 2
N_HEADS_SB = 16
HEAD_DIM_SB = D_ATTN // N_HEADS_SB
D_POOL = D_MODEL // 4
POOL_WINDOWS = (2, 4, 8, 16)
N_POOL_GROUPS = len(POOL_WINDOWS)
POOL_GROUP_DIM = D_POOL // N_POOL_GROUPS
D_CONV = D_MODEL // 4
CONV_WIDTH = 31
D_MIX = D_ATTN + D_POOL + D_CONV
D_IN_PROJ = 3 * D_ATTN + D_POOL + 2 * D_CONV
D_FF = ((8 * D_MODEL + 3 * 256 - 1) // (3 * 256)) * 256
BLOCK_Q = 128
EPS = 1e-6

kernel_name = "hybrid_stickbreak_pool_conformer_trunk"


def rmsnorm(x, g):
    xf = x.astype(jnp.float32)
    y = xf * lax.rsqrt(jnp.mean(xf * xf, axis=-1, keepdims=True) + EPS)
    return (y * g.astype(jnp.float32)).astype(x.dtype)


def layernorm(x, g, b):
    xf = x.astype(jnp.float32)
    mu = jnp.mean(xf, axis=-1, keepdims=True)
    var = jnp.mean(jnp.square(xf - mu), axis=-1, keepdims=True)
    y = (xf - mu) * lax.rsqrt(var + EPS)
    return (y * g.astype(jnp.float32) + b.astype(jnp.float32)).astype(x.dtype)


def stick_breaking_attention(q, k, v):
    b, l, h, dh = q.shape
    pad = (-l) % BLOCK_Q
    padw = ((0, 0), (pad, 0), (0, 0), (0, 0))
    qf = jnp.pad(q.astype(jnp.float32) * (dh ** -0.5), padw)
    kf = jnp.pad(k.astype(jnp.float32), padw)
    vf = jnp.pad(v.astype(jnp.float32), padw)
    lp = l + pad
    n_blocks = lp // BLOCK_Q
    key_pos = jnp.arange(lp) - pad

    def one_block(i):
        q_blk = lax.dynamic_slice_in_dim(qf, i * BLOCK_Q, BLOCK_Q, axis=1)
        q_pos = i * BLOCK_Q + jnp.arange(BLOCK_Q) - pad
        z = jnp.einsum('bqhd,bkhd->bhqk', q_blk, kf)
        mask = (key_pos[None, :] < q_pos[:, None]) & (key_pos[None, :] >= 0)
        log_not_beta = jnp.where(mask, -jax.nn.softplus(z), 0.0)
        later = lax.cumsum(log_not_beta, axis=3, reverse=True) - log_not_beta
        weights = jnp.where(mask, jnp.exp(jax.nn.log_sigmoid(z) + later), 0.0)
        return jnp.einsum('bhqk,bkhd->bqhd', weights, vf)

    out = lax.map(one_block, jnp.arange(n_blocks))
    out = jnp.moveaxis(out, 0, 1).reshape(b, lp, h, dh)[:, pad:]
    return out.astype(v.dtype)


def multiscale_pool(u, w_pool, pool_scale):
    b, l, _ = u.shape
    uf = u.astype(jnp.float32)
    cs = jnp.pad(jnp.cumsum(uf, axis=1), ((0, 0), (1, 0), (0, 0)))
    t = jnp.arange(l)
    groups = []
    for g, w in enumerate(POOL_WINDOWS):
        lo, hi = g * POOL_GROUP_DIM, (g + 1) * POOL_GROUP_DIM
        start = jnp.maximum(t + 1 - w, 0)
        csg = cs[:, :, lo:hi]
        window_sum = csg[:, 1:] - csg[:, start]
        count = (t + 1 - start).astype(jnp.float32)[None, :, None]
        groups.append(window_sum / count - uf[:, :, lo:hi])
    pooled = jnp.stack(groups, axis=2)
    mixed = jnp.einsum('blgc,gcd->blgd', pooled, w_pool.astype(jnp.float32))
    mixed = mixed.reshape(b, l, D_POOL) * pool_scale.astype(jnp.float32)
    return mixed.astype(u.dtype)


def conformer_conv(u_glu, w_dw, b_dw, ln_g, ln_b, w_pw):
    a, gate = jnp.split(u_glu, 2, axis=-1)
    u = a * jax.nn.sigmoid(gate)
    y = lax.conv_general_dilated(
        u, w_dw[:, None, :].astype(u.dtype), window_strides=(1,),
        padding=[(CONV_WIDTH - 1, 0)],
        dimension_numbers=('NWC', 'WIO', 'NWC'),
        feature_group_count=D_CONV) + b_dw
    y = layernorm(y, ln_g, ln_b)
    y = jax.nn.silu(y)
    return y @ w_pw


def _fwd_setup_inputs(seed: int = 0) -> dict:
    key = jax.random.key(seed)
    ks = jax.random.split(key, 20)
    f32 = jnp.float32

    def nrm(k, shape, scale):
        return jax.random.normal(k, shape, f32) * scale

    def gain(k, shape):
        return 1.0 + 0.05 * jax.random.normal(k, shape, f32)

    return {
        "x": nrm(ks[0], (BATCH, SEQ, D_MODEL), 1.0),
        "meta_tokens": nrm(ks[1], (N_META, D_MODEL), 1.0),
        "pre_mix_g": gain(ks[2], (DEPTH, D_MODEL)),
        "w_in": nrm(ks[3], (DEPTH, D_MODEL, D_IN_PROJ), D_MODEL ** -0.5),
        "w_pool": nrm(ks[4], (DEPTH, N_POOL_GROUPS, POOL_GROUP_DIM, POOL_GROUP_DIM), POOL_GROUP_DIM ** -0.5),
        "pool_scale": gain(ks[5], (DEPTH, D_POOL)),
        "w_dw": nrm(ks[6], (DEPTH, CONV_WIDTH, D_CONV), CONV_WIDTH ** -0.5),
        "b_dw": nrm(ks[7], (DEPTH, D_CONV), 0.02),
        "conv_ln_g": gain(ks[8], (DEPTH, D_CONV)),
        "conv_ln_b": nrm(ks[9], (DEPTH, D_CONV), 0.02),
        "w_pw": nrm(ks[10], (DEPTH, D_CONV, D_CONV), D_CONV ** -0.5),
        "mix_out_g": gain(ks[11], (DEPTH, D_MIX)),
        "w_out": nrm(ks[12], (DEPTH, D_MIX, D_MODEL), D_MIX ** -0.5),
        "post_mix_g": gain(ks[13], (DEPTH, D_MODEL)),
        "pre_ffn_g": gain(ks[14], (DEPTH, D_MODEL)),
        "w_gate": nrm(ks[15], (DEPTH, D_MODEL, D_FF), D_MODEL ** -0.5),
        "w_up": nrm(ks[16], (DEPTH, D_MODEL, D_FF), D_MODEL ** -0.5),
        "w_down": nrm(ks[17], (DEPTH, D_FF, D_MODEL), D_FF ** -0.5),
        "post_ffn_g": gain(ks[18], (DEPTH, D_MODEL)),
    }


def _fwd_reference(x, meta_tokens, pre_mix_g, w_in, w_pool, pool_scale, w_dw, b_dw,
              conv_ln_g, conv_ln_b, w_pw, mix_out_g, w_out, post_mix_g,
              pre_ffn_g, w_gate, w_up, w_down, post_ffn_g):
    b = x.shape[0]
    meta = jnp.broadcast_to(meta_tokens[None].astype(x.dtype), (b, N_META, D_MODEL))
    h = jnp.concatenate([meta, x], axis=1)
    l = h.shape[1]
    splits = [D_ATTN, 2 * D_ATTN, 3 * D_ATTN, 3 * D_ATTN + D_POOL]

    for i in range(DEPTH):
        u = rmsnorm(h, pre_mix_g[i])
        proj = u @ w_in[i]
        q, k, v, u_pool, u_conv = jnp.split(proj, splits, axis=-1)
        heads = (b, l, N_HEADS_SB, HEAD_DIM_SB)
        o_attn = stick_breaking_attention(q.reshape(heads), k.reshape(heads), v.reshape(heads))
        o_attn = o_attn.reshape(b, l, D_ATTN)
        o_pool = multiscale_pool(u_pool, w_pool[i], pool_scale[i])
        o_conv = conformer_conv(u_conv, w_dw[i], b_dw[i], conv_ln_g[i], conv_ln_b[i], w_pw[i])
        g = mix_out_g[i]
        merged = jnp.concatenate([
            rmsnorm(o_attn, g[:D_ATTN]),
            rmsnorm(o_pool, g[D_ATTN:D_ATTN + D_POOL]),
            rmsnorm(o_conv, g[D_ATTN + D_POOL:]),
        ], axis=-1)
        h = h + rmsnorm(merged @ w_out[i], post_mix_g[i])

        u = rmsnorm(h, pre_ffn_g[i])
        ff = (jax.nn.silu(u @ w_gate[i]) * (u @ w_up[i])) @ w_down[i]
        h = h + rmsnorm(ff, post_ffn_g[i])

    return h[:, N_META:]


import jax as _jax
import jax.numpy as _jnp

TWIN_FORMAT = 'train_step'
FWD_PARAMS = ['x', 'meta_tokens', 'pre_mix_g', 'w_in', 'w_pool', 'pool_scale', 'w_dw', 'b_dw', 'conv_ln_g', 'conv_ln_b', 'w_pw', 'mix_out_g', 'w_out', 'post_mix_g', 'pre_ffn_g', 'w_gate', 'w_up', 'w_down', 'post_ffn_g']
TWIN_WEIGHTS = ['meta_tokens', 'pre_mix_g', 'w_in', 'w_pool', 'pool_scale', 'w_dw', 'b_dw', 'conv_ln_g', 'conv_ln_b', 'w_pw', 'mix_out_g', 'w_out', 'post_mix_g', 'pre_ffn_g', 'w_gate', 'w_up', 'w_down', 'post_ffn_g']
TWIN_DIFF_INPUT = 'x'
TWIN_INPUTS = ['x', 'meta_tokens', 'pre_mix_g', 'w_in', 'w_pool', 'pool_scale', 'w_dw', 'b_dw', 'conv_ln_g', 'conv_ln_b', 'w_pw', 'mix_out_g', 'w_out', 'post_mix_g', 'pre_ffn_g', 'w_gate', 'w_up', 'w_down', 'post_ffn_g', 'loss_target', 'm_meta_tokens', 'm_pre_mix_g', 'm_w_in', 'm_w_pool', 'm_pool_scale', 'm_w_dw', 'm_b_dw', 'm_conv_ln_g', 'm_conv_ln_b', 'm_w_pw', 'm_mix_out_g', 'm_w_out', 'm_post_mix_g', 'm_pre_ffn_g', 'm_w_gate', 'm_w_up', 'm_w_down', 'm_post_ffn_g', 'v_meta_tokens', 'v_pre_mix_g', 'v_w_in', 'v_w_pool', 'v_pool_scale', 'v_w_dw', 'v_b_dw', 'v_conv_ln_g', 'v_conv_ln_b', 'v_w_pw', 'v_mix_out_g', 'v_w_out', 'v_post_mix_g', 'v_pre_ffn_g', 'v_w_gate', 'v_w_up', 'v_w_down', 'v_post_ffn_g']
TWIN_OUTPUTS = ['loss', 'grad_x', 'grad_meta_tokens', 'grad_pre_mix_g', 'grad_w_in', 'grad_w_pool', 'grad_pool_scale', 'grad_w_dw', 'grad_b_dw', 'grad_conv_ln_g', 'grad_conv_ln_b', 'grad_w_pw', 'grad_mix_out_g', 'grad_w_out', 'grad_post_mix_g', 'grad_pre_ffn_g', 'grad_w_gate', 'grad_w_up', 'grad_w_down', 'grad_post_ffn_g', 'delta_meta_tokens', 'delta_pre_mix_g', 'delta_w_in', 'delta_w_pool', 'delta_pool_scale', 'delta_w_dw', 'delta_b_dw', 'delta_conv_ln_g', 'delta_conv_ln_b', 'delta_w_pw', 'delta_mix_out_g', 'delta_w_out', 'delta_post_mix_g', 'delta_pre_ffn_g', 'delta_w_gate', 'delta_w_up', 'delta_w_down', 'delta_post_ffn_g', 'new_m_meta_tokens', 'new_m_pre_mix_g', 'new_m_w_in', 'new_m_w_pool', 'new_m_pool_scale', 'new_m_w_dw', 'new_m_b_dw', 'new_m_conv_ln_g', 'new_m_conv_ln_b', 'new_m_w_pw', 'new_m_mix_out_g', 'new_m_w_out', 'new_m_post_mix_g', 'new_m_pre_ffn_g', 'new_m_w_gate', 'new_m_w_up', 'new_m_w_down', 'new_m_post_ffn_g', 'new_v_meta_tokens', 'new_v_pre_mix_g', 'new_v_w_in', 'new_v_w_pool', 'new_v_pool_scale', 'new_v_w_dw', 'new_v_b_dw', 'new_v_conv_ln_g', 'new_v_conv_ln_b', 'new_v_w_pw', 'new_v_mix_out_g', 'new_v_w_out', 'new_v_post_mix_g', 'new_v_pre_ffn_g', 'new_v_w_gate', 'new_v_w_up', 'new_v_w_down', 'new_v_post_ffn_g']
TWIN_LEAF_KINDS = {'loss': 'loss', 'grad_x': 'grad_x', 'grad_meta_tokens': 'grad_w', 'grad_pre_mix_g': 'grad_w', 'grad_w_in': 'grad_w', 'grad_w_pool': 'grad_w', 'grad_pool_scale': 'grad_w', 'grad_w_dw': 'grad_w', 'grad_b_dw': 'grad_w', 'grad_conv_ln_g': 'grad_w', 'grad_conv_ln_b': 'grad_w', 'grad_w_pw': 'grad_w', 'grad_mix_out_g': 'grad_w', 'grad_w_out': 'grad_w', 'grad_post_mix_g': 'grad_w', 'grad_pre_ffn_g': 'grad_w', 'grad_w_gate': 'grad_w', 'grad_w_up': 'grad_w', 'grad_w_down': 'grad_w', 'grad_post_ffn_g': 'grad_w', 'delta_meta_tokens': 'delta_w', 'delta_pre_mix_g': 'delta_w', 'delta_w_in': 'delta_w', 'delta_w_pool': 'delta_w', 'delta_pool_scale': 'delta_w', 'delta_w_dw': 'delta_w', 'delta_b_dw': 'delta_w', 'delta_conv_ln_g': 'delta_w', 'delta_conv_ln_b': 'delta_w', 'delta_w_pw': 'delta_w', 'delta_mix_out_g': 'delta_w', 'delta_w_out': 'delta_w', 'delta_post_mix_g': 'delta_w', 'delta_pre_ffn_g': 'delta_w', 'delta_w_gate': 'delta_w', 'delta_w_up': 'delta_w', 'delta_w_down': 'delta_w', 'delta_post_ffn_g': 'delta_w', 'new_m_meta_tokens': 'new_m', 'new_m_pre_mix_g': 'new_m', 'new_m_w_in': 'new_m', 'new_m_w_pool': 'new_m', 'new_m_pool_scale': 'new_m', 'new_m_w_dw': 'new_m', 'new_m_b_dw': 'new_m', 'new_m_conv_ln_g': 'new_m', 'new_m_conv_ln_b': 'new_m', 'new_m_w_pw': 'new_m', 'new_m_mix_out_g': 'new_m', 'new_m_w_out': 'new_m', 'new_m_post_mix_g': 'new_m', 'new_m_pre_ffn_g': 'new_m', 'new_m_w_gate': 'new_m', 'new_m_w_up': 'new_m', 'new_m_w_down': 'new_m', 'new_m_post_ffn_g': 'new_m', 'new_v_meta_tokens': 'new_v', 'new_v_pre_mix_g': 'new_v', 'new_v_w_in': 'new_v', 'new_v_w_pool': 'new_v', 'new_v_pool_scale': 'new_v', 'new_v_w_dw': 'new_v', 'new_v_b_dw': 'new_v', 'new_v_conv_ln_g': 'new_v', 'new_v_conv_ln_b': 'new_v', 'new_v_w_pw': 'new_v', 'new_v_mix_out_g': 'new_v', 'new_v_w_out': 'new_v', 'new_v_post_mix_g': 'new_v', 'new_v_pre_ffn_g': 'new_v', 'new_v_w_gate': 'new_v', 'new_v_w_up': 'new_v', 'new_v_w_down': 'new_v', 'new_v_post_ffn_g': 'new_v'}


def _forward(args):
    return _fwd_reference(*[args[k] for k in FWD_PARAMS])


def _output_shape():
    def fwd():
        inp = _fwd_setup_inputs(0)
        return _fwd_reference(*[inp[k] for k in FWD_PARAMS])
    out = _jax.eval_shape(fwd)
    return out.shape, out.dtype

N_MICROBATCH = 1
ADAM_LR = 0.001
ADAM_B1 = 0.9
ADAM_B2 = 0.999
ADAM_EPS = 1e-08
ADAM_WD = 0.01
ADAM_STEP = 10
PER_EXAMPLE_BATCH_AXIS = {'x': 0, 'loss_target': 0}
SHARED_INPUTS = []
_WEIGHT_DTYPES = {'meta_tokens': _jnp.float32, 'pre_mix_g': _jnp.float32, 'w_in': _jnp.float32, 'w_pool': _jnp.float32, 'pool_scale': _jnp.float32, 'w_dw': _jnp.float32, 'b_dw': _jnp.float32, 'conv_ln_g': _jnp.float32, 'conv_ln_b': _jnp.float32, 'w_pw': _jnp.float32, 'mix_out_g': _jnp.float32, 'w_out': _jnp.float32, 'post_mix_g': _jnp.float32, 'pre_ffn_g': _jnp.float32, 'w_gate': _jnp.float32, 'w_up': _jnp.float32, 'w_down': _jnp.float32, 'post_ffn_g': _jnp.float32}
MOMENT_SCALE = {'meta_tokens': 2.075225e-02, 'pre_mix_g': 4.959602e-01, 'w_in': 3.197163e-01, 'w_pool': 5.454954e-01, 'pool_scale': 5.748926e-01, 'w_dw': 4.448056e-01, 'b_dw': 5.104793e+00, 'conv_ln_g': 1.813864e+00, 'conv_ln_b': 2.763251e+00, 'w_pw': 1.149821e+00, 'mix_out_g': 6.916813e-01, 'w_out': 7.117798e-01, 'post_mix_g': 1.602871e+01, 'pre_ffn_g': 4.662409e-01, 'w_gate': 1.711437e-01, 'w_up': 2.232750e-01, 'w_down': 3.726047e-01, 'post_ffn_g': 1.603429e+01}


def _to_microbatches(a, axis):
    t = _jnp.moveaxis(a, axis, 0)
    t = t.reshape((N_MICROBATCH, t.shape[0] // N_MICROBATCH) + t.shape[1:])
    return _jnp.moveaxis(t, 1, axis + 1)


def setup_inputs(seed: int = 0) -> dict:
    inp = _fwd_setup_inputs(seed)
    key = _jax.random.fold_in(_jax.random.key(seed), 7919)
    shape, _ = _output_shape()
    out = dict(inp)
    out["loss_target"] = _jax.random.normal(_jax.random.fold_in(key, 0), shape, _jnp.float32)
    for i, name in enumerate(TWIN_WEIGHTS):
        w = inp[name].astype(_jnp.float32)
        if MOMENT_SCALE is None:
            s = _jnp.sqrt(_jnp.mean(_jnp.square(w)) + 1e-30)
        else:
            s = MOMENT_SCALE[name]
        km, kv = _jax.random.split(_jax.random.fold_in(key, i + 1))
        out[name] = w
        out["m_" + name] = s * _jax.random.normal(km, w.shape, _jnp.float32)
        out["v_" + name] = (s * s) * _jax.random.uniform(kv, w.shape, _jnp.float32, 0.5, 1.5)
    if N_MICROBATCH > 1:
        for name, axis in PER_EXAMPLE_BATCH_AXIS.items():
            out[name] = _to_microbatches(out[name], axis)
    return {'x': out['x'], 'meta_tokens': out['meta_tokens'], 'pre_mix_g': out['pre_mix_g'], 'w_in': out['w_in'], 'w_pool': out['w_pool'], 'pool_scale': out['pool_scale'], 'w_dw': out['w_dw'], 'b_dw': out['b_dw'], 'conv_ln_g': out['conv_ln_g'], 'conv_ln_b': out['conv_ln_b'], 'w_pw': out['w_pw'], 'mix_out_g': out['mix_out_g'], 'w_out': out['w_out'], 'post_mix_g': out['post_mix_g'], 'pre_ffn_g': out['pre_ffn_g'], 'w_gate': out['w_gate'], 'w_up': out['w_up'], 'w_down': out['w_down'], 'post_ffn_g': out['post_ffn_g'], 'loss_target': out['loss_target'], 'm_meta_tokens': out['m_meta_tokens'], 'm_pre_mix_g': out['m_pre_mix_g'], 'm_w_in': out['m_w_in'], 'm_w_pool': out['m_w_pool'], 'm_pool_scale': out['m_pool_scale'], 'm_w_dw': out['m_w_dw'], 'm_b_dw': out['m_b_dw'], 'm_conv_ln_g': out['m_conv_ln_g'], 'm_conv_ln_b': out['m_conv_ln_b'], 'm_w_pw': out['m_w_pw'], 'm_mix_out_g': out['m_mix_out_g'], 'm_w_out': out['m_w_out'], 'm_post_mix_g': out['m_post_mix_g'], 'm_pre_ffn_g': out['m_pre_ffn_g'], 'm_w_gate': out['m_w_gate'], 'm_w_up': out['m_w_up'], 'm_w_down': out['m_w_down'], 'm_post_ffn_g': out['m_post_ffn_g'], 'v_meta_tokens': out['v_meta_tokens'], 'v_pre_mix_g': out['v_pre_mix_g'], 'v_w_in': out['v_w_in'], 'v_w_pool': out['v_w_pool'], 'v_pool_scale': out['v_pool_scale'], 'v_w_dw': out['v_w_dw'], 'v_b_dw': out['v_b_dw'], 'v_conv_ln_g': out['v_conv_ln_g'], 'v_conv_ln_b': out['v_conv_ln_b'], 'v_w_pw': out['v_w_pw'], 'v_mix_out_g': out['v_mix_out_g'], 'v_w_out': out['v_w_out'], 'v_post_mix_g': out['v_post_mix_g'], 'v_pre_ffn_g': out['v_pre_ffn_g'], 'v_w_gate': out['v_w_gate'], 'v_w_up': out['v_w_up'], 'v_w_down': out['v_w_down'], 'v_post_ffn_g': out['v_post_ffn_g']}


def _loss(weights, diff, rest, loss_target):
    with _jax.named_scope("forward"):
        args = {**rest, TWIN_DIFF_INPUT: diff, **{k: w.astype(_WEIGHT_DTYPES[k]) for k, w in weights.items()}}
        y = _forward(args)
    with _jax.named_scope("loss_head"):
        err = _jnp.square(y.astype(_jnp.float32) - loss_target)
        return 0.5 * _jnp.sum(_jnp.mean(err, axis=-1)) if err.ndim else 0.5 * err


def _adamw(w, g, m, v):
    m = ADAM_B1 * m + (1.0 - ADAM_B1) * g
    v = ADAM_B2 * v + (1.0 - ADAM_B2) * _jnp.square(g)
    m_hat = m / (1.0 - ADAM_B1 ** ADAM_STEP)
    v_hat = v / (1.0 - ADAM_B2 ** ADAM_STEP)
    delta = -ADAM_LR * (m_hat / (_jnp.sqrt(v_hat) + ADAM_EPS) + ADAM_WD * w)
    return delta, m, v


def reference(x, meta_tokens, pre_mix_g, w_in, w_pool, pool_scale, w_dw, b_dw, conv_ln_g, conv_ln_b, w_pw, mix_out_g, w_out, post_mix_g, pre_ffn_g, w_gate, w_up, w_down, post_ffn_g, loss_target, m_meta_tokens, m_pre_mix_g, m_w_in, m_w_pool, m_pool_scale, m_w_dw, m_b_dw, m_conv_ln_g, m_conv_ln_b, m_w_pw, m_mix_out_g, m_w_out, m_post_mix_g, m_pre_ffn_g, m_w_gate, m_w_up, m_w_down, m_post_ffn_g, v_meta_tokens, v_pre_mix_g, v_w_in, v_w_pool, v_pool_scale, v_w_dw, v_b_dw, v_conv_ln_g, v_conv_ln_b, v_w_pw, v_mix_out_g, v_w_out, v_post_mix_g, v_pre_ffn_g, v_w_gate, v_w_up, v_w_down, v_post_ffn_g):
    given = dict(x=x, meta_tokens=meta_tokens, pre_mix_g=pre_mix_g, w_in=w_in, w_pool=w_pool, pool_scale=pool_scale, w_dw=w_dw, b_dw=b_dw, conv_ln_g=conv_ln_g, conv_ln_b=conv_ln_b, w_pw=w_pw, mix_out_g=mix_out_g, w_out=w_out, post_mix_g=post_mix_g, pre_ffn_g=pre_ffn_g, w_gate=w_gate, w_up=w_up, w_down=w_down, post_ffn_g=post_ffn_g, loss_target=loss_target, m_meta_tokens=m_meta_tokens, m_pre_mix_g=m_pre_mix_g, m_w_in=m_w_in, m_w_pool=m_w_pool, m_pool_scale=m_pool_scale, m_w_dw=m_w_dw, m_b_dw=m_b_dw, m_conv_ln_g=m_conv_ln_g, m_conv_ln_b=m_conv_ln_b, m_w_pw=m_w_pw, m_mix_out_g=m_mix_out_g, m_w_out=m_w_out, m_post_mix_g=m_post_mix_g, m_pre_ffn_g=m_pre_ffn_g, m_w_gate=m_w_gate, m_w_up=m_w_up, m_w_down=m_w_down, m_post_ffn_g=m_post_ffn_g, v_meta_tokens=v_meta_tokens, v_pre_mix_g=v_pre_mix_g, v_w_in=v_w_in, v_w_pool=v_w_pool, v_pool_scale=v_pool_scale, v_w_dw=v_w_dw, v_b_dw=v_b_dw, v_conv_ln_g=v_conv_ln_g, v_conv_ln_b=v_conv_ln_b, v_w_pw=v_w_pw, v_mix_out_g=v_mix_out_g, v_w_out=v_w_out, v_post_mix_g=v_post_mix_g, v_pre_ffn_g=v_pre_ffn_g, v_w_gate=v_w_gate, v_w_up=v_w_up, v_w_down=v_w_down, v_post_ffn_g=v_post_ffn_g)
    weights = {n: given[n] for n in TWIN_WEIGHTS}
    shared = {n: given[n] for n in SHARED_INPUTS}
    per_example = {n: given[n] for n in ['x']}
    grad_fn = _jax.value_and_grad(_loss, argnums=(0, 1))

    def one_microbatch(ex, loss_target):
        ex = dict(ex)
        diff = ex.pop(TWIN_DIFF_INPUT)
        return grad_fn(weights, diff, {**shared, **ex}, loss_target)

    if N_MICROBATCH == 1:
        loss, (grad_w, grad_x) = one_microbatch(per_example, given["loss_target"])
    else:
        def body(carry, xs):
            loss_sum, grad_sum = carry
            l_k, (gw_k, gx_k) = one_microbatch(xs[0], xs[1])
            with _jax.named_scope("update"):
                return (loss_sum + l_k, _jax.tree.map(_jnp.add, grad_sum, gw_k)), gx_k

        init = (_jnp.zeros((), _jnp.float32), _jax.tree.map(_jnp.zeros_like, weights))
        (loss, grad_w), grad_x = _jax.lax.scan(body, init, (per_example, given["loss_target"]))
    with _jax.named_scope("update"):
        delta_w, new_m, new_v = {}, {}, {}
        for n in TWIN_WEIGHTS:
            delta_w[n], new_m[n], new_v[n] = _adamw(weights[n], grad_w[n], given["m_" + n], given["v_" + n])
    return (loss, grad_x, *[grad_w[n] for n in TWIN_WEIGHTS], *[delta_w[n] for n in TWIN_WEIGHTS],
            *[new_m[n] for n in TWIN_WEIGHTS], *[new_v[n] for n in TWIN_WEIGHTS])
```

```python
import functools

import jax
import jax.numpy as jnp
from jax import lax
from jax.experimental import pallas as pl
from jax.experimental.pallas import tpu as pltpu

F32 = jnp.float32
BF16 = jnp.bfloat16

D_MODEL = 2048
N_META = 16
D_ATTN = 1024
D_POOL = 512
D_CONV = 512
POOL_WINDOWS = (2, 4, 8, 16)
CONV_WIDTH = 31
D_IN_PROJ = 3 * D_ATTN + D_POOL + 2 * D_CONV
D_FF = 5632
EPS = 1e-6
BLOCK = 128
PAD = BLOCK - N_META
HALO = 32
N_SHARD = 4
N_DEV = 8
MESH = pl.DeviceIdType.MESH

ADAM_LR = 0.001
ADAM_B1 = 0.9
ADAM_B2 = 0.999
ADAM_EPS = 1e-08
ADAM_WD = 0.01
ADAM_STEP = 10

COL_Q, COL_K, COL_V = 0, D_ATTN, 2 * D_ATTN
COL_POOL = 3 * D_ATTN
COL_A = COL_POOL + D_POOL
COL_G = COL_A + D_CONV


def _call(body, *, name, out_shape, grid=None, in_specs=None, out_specs=None, scratch=(), vmem_mb=None):
    params = {}
    if grid is not None:
        params["dimension_semantics"] = ("arbitrary",) * len(grid)
    if vmem_mb is not None:
        params["vmem_limit_bytes"] = vmem_mb << 20
    kw = dict(out_shape=out_shape, name=name, compiler_params=pltpu.CompilerParams(**params))
    if grid is not None:
        kw["grid"] = grid
    if in_specs is not None:
        kw["in_specs"] = in_specs
    if out_specs is not None:
        kw["out_specs"] = out_specs
    if scratch:
        kw["scratch_shapes"] = list(scratch)
    return pl.pallas_call(body, **kw)


def _sds(shape, dtype):
    return jax.ShapeDtypeStruct(tuple(shape), dtype)


def _pick(n, candidates):
    for c in candidates:
        if n % c == 0:
            return c
    return n


def _rowwise(name, fn, rows_in, consts, outs, accs=(), tm=BLOCK, with_row0=False, vmem_mb=None):
    lp = rows_in[0].shape[0]
    n_in, n_c, n_o = len(rows_in), len(consts), len(outs)

    def body(*refs):
        vals = [r[...] for r in refs[:n_in + n_c]]
        if with_row0:
            vals = [pl.program_id(0) * tm] + vals
        res = fn(*vals)
        if not isinstance(res, (tuple, list)):
            res = (res,)
        o_refs = refs[n_in + n_c:n_in + n_c + n_o]
        a_refs = refs[n_in + n_c + n_o:]
        for r, v in zip(o_refs, res[:n_o]):
            r[...] = v.astype(r.dtype)
        if a_refs:
            @pl.when(pl.program_id(0) == 0)
            def _():
                for r in a_refs:
                    r[...] = jnp.zeros(r.shape, r.dtype)
            for r, v in zip(a_refs, res[n_o:]):
                r[...] += v.astype(r.dtype)

    in_specs = [pl.BlockSpec((tm, a.shape[1]), lambda i: (i, 0)) for a in rows_in]
    in_specs += [pl.BlockSpec(c.shape, lambda i: (0, 0)) for c in consts]
    out_specs = [pl.BlockSpec((tm, w), lambda i: (i, 0)) for (w, _) in outs]
    out_specs += [pl.BlockSpec(s, lambda i: (0, 0)) for s in accs]
    out_shape = [_sds((lp, w), dt) for (w, dt) in outs] + [_sds(s, F32) for s in accs]
    res = _call(body, name=name, out_shape=out_shape, grid=(lp // tm,), in_specs=in_specs,
                out_specs=out_specs, vmem_mb=vmem_mb)(*rows_in, *consts)
    return res


def _rms(x, g):
    return x * lax.rsqrt(jnp.mean(x * x, axis=-1, keepdims=True) + EPS) * g


def _merge(oa, op, oc, ga, gp, gc):
    return jnp.concatenate([_rms(oa, ga), _rms(op, gp), _rms(oc, gc)], axis=1)


def _colsum(x):
    return jnp.sum(x, axis=0, keepdims=True)


def _sigmoid(x):
    return 1.0 / (1.0 + jnp.exp(-x))


def _mm_tiles(lp):
    return _pick(lp, (384, 256, 128))


def _mm_nn_col(name, a, wg, layer, out_dtype):
    lp, k = a.shape
    n = wg.shape[3]
    tm = _mm_tiles(lp)

    def body(a_ref, w_ref, o_ref):
        o_ref[...] = jnp.dot(a_ref[...], w_ref[...], preferred_element_type=F32).astype(o_ref.dtype)

    return _call(body, name=name, out_shape=_sds((lp, N_SHARD * n), out_dtype), grid=(N_SHARD, lp // tm),
                 in_specs=[pl.BlockSpec((tm, k), lambda s, i: (i, 0)),
                           pl.BlockSpec((None, None, k, n), lambda s, i: (s, layer, 0, 0))],
                 out_specs=pl.BlockSpec((tm, n), lambda s, i: (i, s)), vmem_mb=48)(a, wg)


def _mm_nn_row(name, a, wg, layer):
    lp = a.shape[0]
    k, n = wg.shape[2], wg.shape[3]
    tm = _mm_tiles(lp)

    def body(a_ref, w_ref, o_ref):
        part = jnp.dot(a_ref[...], w_ref[...], preferred_element_type=F32)

        @pl.when(pl.program_id(1) == 0)
        def _():
            o_ref[...] = part

        @pl.when(pl.program_id(1) != 0)
        def _():
            o_ref[...] += part

    return _call(body, name=name, out_shape=_sds((lp, n), F32), grid=(lp // tm, N_SHARD),
                 in_specs=[pl.BlockSpec((tm, k), lambda i, s: (i, s)),
                           pl.BlockSpec((None, None, k, n), lambda i, s: (s, layer, 0, 0))],
                 out_specs=pl.BlockSpec((tm, n), lambda i, s: (i, 0)), vmem_mb=48)(a, wg)


_NT = (((1,), (1,)), ((), ()))
_TN = (((0,), (0,)), ((), ()))


def _mm_nt_col(name, dy, wg, layer):
    lp = dy.shape[0]
    k, n = wg.shape[2], wg.shape[3]
    tm = _mm_tiles(lp)

    def body(d_ref, w_ref, o_ref):
        part = lax.dot_general(d_ref[...], w_ref[...], _NT, preferred_element_type=F32)

        @pl.when(pl.program_id(1) == 0)
        def _():
            o_ref[...] = part

        @pl.when(pl.program_id(1) != 0)
        def _():
            o_ref[...] += part

    return _call(body, name=name, out_shape=_sds((lp, k), F32), grid=(lp // tm, N_SHARD),
                 in_specs=[pl.BlockSpec((tm, n), lambda i, s: (i, s)),
                           pl.BlockSpec((None, None, k, n), lambda i, s: (s, layer, 0, 0))],
                 out_specs=pl.BlockSpec((tm, k), lambda i, s: (i, 0)), vmem_mb=48)(dy, wg)


def _mm_nt_row(name, dy, wg, layer, out_dtype):
    lp = dy.shape[0]
    k, n = wg.shape[2], wg.shape[3]
    tm = _mm_tiles(lp)

    def body(d_ref, w_ref, o_ref):
        o_ref[...] = lax.dot_general(d_ref[...], w_ref[...], _NT, preferred_element_type=F32).astype(o_ref.dtype)

    return _call(body, name=name, out_shape=_sds((lp, N_SHARD * k), out_dtype), grid=(N_SHARD, lp // tm),
                 in_specs=[pl.BlockSpec((tm, n), lambda s, i: (i, 0)),
                           pl.BlockSpec((None, None, k, n), lambda s, i: (s, layer, 0, 0))],
                 out_specs=pl.BlockSpec((tm, k), lambda s, i: (i, s)), vmem_mb=48)(dy, wg)


def _mm_tn_col(name, a, dy):
    lp, k = a.shape
    n = dy.shape[1] // N_SHARD
    tm = _mm_tiles(lp)
    tk = _pick(k, (1024, 512))

    def body(a_ref, d_ref, o_ref, acc):
        @pl.when(pl.program_id(2) == 0)
        def _():
            acc[...] = jnp.zeros(acc.shape, F32)

        acc[...] += lax.dot_general(a_ref[...], d_ref[...], _TN, preferred_element_type=F32)

        @pl.when(pl.program_id(2) == pl.num_programs(2) - 1)
        def _():
            o_ref[...] = acc[...].astype(o_ref.dtype)

    return _call(body, name=name, out_shape=_sds((N_SHARD, k, n), BF16), grid=(N_SHARD, k // tk, lp // tm),
                 in_specs=[pl.BlockSpec((tm, tk), lambda s, kk, i: (i, kk)),
                           pl.BlockSpec((tm, n), lambda s, kk, i: (i, s))],
                 out_specs=pl.BlockSpec((None, tk, n), lambda s, kk, i: (s, kk, 0)),
                 scratch=[pltpu.VMEM((tk, n), F32)], vmem_mb=48)(a, dy)


def _mm_tn_row(name, a, dy):
    lp = a.shape[0]
    k = a.shape[1] // N_SHARD
    n = dy.shape[1]
    tm = _mm_tiles(lp)
    tn = _pick(n, (1024, 512))

    def body(a_ref, d_ref, o_ref, acc):
        @pl.when(pl.program_id(2) == 0)
        def _():
            acc[...] = jnp.zeros(acc.shape, F32)

        acc[...] += lax.dot_general(a_ref[...], d_ref[...], _TN, preferred_element_type=F32)

        @pl.when(pl.program_id(2) == pl.num_programs(2) - 1)
        def _():
            o_ref[...] = acc[...].astype(o_ref.dtype)

    return _call(body, name=name, out_shape=_sds((N_SHARD, k, n), BF16), grid=(N_SHARD, n // tn, lp // tm),
                 in_specs=[pl.BlockSpec((tm, k), lambda s, j, i: (i, s)),
                           pl.BlockSpec((tm, tn), lambda s, j, i: (i, j))],
                 out_specs=pl.BlockSpec((None, k, tn), lambda s, j, i: (s, 0, j)),
                 scratch=[pltpu.VMEM((k, tn), F32)], vmem_mb=48)(a, dy)


def _attn_tq(lp):
    return _pick(lp, (384, 128))


def _neg_softplus(z):
    return -(jnp.maximum(z, 0.0) + jnp.log(1.0 + jnp.exp(-jnp.abs(z))))


def _split_dot(x, tri):
    hi = x.astype(BF16)
    lo = (x - hi.astype(F32)).astype(BF16)
    return jnp.dot(hi, tri, preferred_element_type=F32) + jnp.dot(lo, tri, preferred_element_type=F32)


def _attn_fwd(proj):
    lp = proj.shape[0]
    tq, tk = _attn_tq(lp), BLOCK
    n_pair = D_ATTN // BLOCK

    def body(q_ref, k_ref, v_ref, o_ref, tot_ref):
        i = pl.program_id(1)
        m0 = lax.broadcasted_iota(jnp.int32, (1, BLOCK), 1) < (BLOCK // 2)
        q = q_ref[...]
        qh = (jnp.where(m0, q, 0.0).astype(BF16), jnp.where(m0, 0.0, q).astype(BF16))
        qpos = i * tq + lax.broadcasted_iota(jnp.int32, (tq, 1), 0) - PAD
        tri = (lax.broadcasted_iota(jnp.int32, (tk, tk), 0) >=
               lax.broadcasted_iota(jnp.int32, (tk, tk), 1)).astype(BF16)
        nkb = (i + 1) * (tq // tk)

        def step(n, carry):
            r0, r1, acc = carry
            jb = nkb - 1 - n
            row = pl.multiple_of(jb * tk, tk)
            ks = k_ref[pl.ds(row, tk), :].astype(BF16)
            vs = v_ref[pl.ds(row, tk), :]
            vh = (jnp.where(m0, vs, 0.0).astype(BF16), jnp.where(m0, 0.0, vs).astype(BF16))
            kpos = jb * tk + lax.broadcasted_iota(jnp.int32, (1, tk), 1) - PAD
            mask = (kpos < qpos) & (kpos >= 0)
            rs = [r0, r1]
            for h in range(2):
                z = lax.dot_general(qh[h], ks, _NT, preferred_element_type=F32) * 0.125
                lnb = jnp.where(mask, _neg_softplus(z), 0.0)
                c = _split_dot(lnb, tri)
                w = jnp.where(mask, jnp.exp(z + c + rs[h]), 0.0)
                acc = acc + jnp.dot(w.astype(BF16), vh[h], preferred_element_type=F32)
                rs[h] = rs[h] + jnp.sum(lnb, axis=1, keepdims=True)
            return rs[0], rs[1], acc

        zero = jnp.zeros((tq, 1), F32)
        r0, r1, acc = lax.fori_loop(0, nkb, step, (zero, zero, jnp.zeros((tq, BLOCK), F32)))
        o_ref[...] = acc
        tot_ref[...] = jnp.where(m0, r0, r1)

    cq, ck, cv = COL_Q // BLOCK, COL_K // BLOCK, COL_V // BLOCK
    return _call(body, name="attn_fwd", out_shape=[_sds((lp, D_ATTN), F32), _sds((lp, D_ATTN), F32)],
                 grid=(n_pair, lp // tq),
                 in_specs=[pl.BlockSpec((tq, BLOCK), lambda p, i: (i, cq + p)),
                           pl.BlockSpec((lp, BLOCK), lambda p, i: (0, ck + p)),
                           pl.BlockSpec((lp, BLOCK), lambda p, i: (0, cv + p))],
                 out_specs=[pl.BlockSpec((tq, BLOCK), lambda p, i: (i, p)),
                            pl.BlockSpec((tq, BLOCK), lambda p, i: (i, p))], vmem_mb=48)(proj, proj, proj)


def _attn_bwd(proj, tot, d_out):
    lp = proj.shape[0]
    tq, tk = _attn_tq(lp), BLOCK
    n_pair = D_ATTN // BLOCK

    def body(q_ref, k_ref, v_ref, tot_ref, do_ref, dq_ref, dk_ref, dv_ref, dk_acc, dv_acc):
        i = pl.program_id(1)
        m0 = lax.broadcasted_iota(jnp.int32, (1, BLOCK), 1) < (BLOCK // 2)

        @pl.when(i == 0)
        def _():
            dk_acc[...] = jnp.zeros(dk_acc.shape, F32)
            dv_acc[...] = jnp.zeros(dv_acc.shape, F32)

        q = q_ref[...]
        do = do_ref[...]
        qh = (jnp.where(m0, q, 0.0).astype(BF16), jnp.where(m0, 0.0, q).astype(BF16))
        doh = (jnp.where(m0, do, 0.0).astype(BF16), jnp.where(m0, 0.0, do).astype(BF16))
        tot_v = tot_ref[...]
        lane = lax.broadcasted_iota(jnp.int32, (1, BLOCK), 1)
        tots = (jnp.sum(jnp.where(lane == 0, tot_v, 0.0), axis=1, keepdims=True),
                jnp.sum(jnp.where(lane == BLOCK - 1, tot_v, 0.0), axis=1, keepdims=True))
        qpos = i * tq + lax.broadcasted_iota(jnp.int32, (tq, 1), 0) - PAD
        rr = lax.broadcasted_iota(jnp.int32, (tk, tk), 0)
        cc = lax.broadcasted_iota(jnp.int32, (tk, tk), 1)
        tri_ex = (rr < cc).astype(BF16)
        tri_in = (rr <= cc).astype(BF16)
        nkb = (i + 1) * (tq // tk)

        def step(jb, carry):
            a0, a1, b0, b1, dq = carry
            row = pl.multiple_of(jb * tk, tk)
            kf = k_ref[pl.ds(row, tk), :]
            vf = v_ref[pl.ds(row, tk), :]
            ks = kf.astype(BF16)
            vs = vf.astype(BF16)
            kh = (jnp.where(m0, kf, 0.0).astype(BF16), jnp.where(m0, 0.0, kf).astype(BF16))
            kpos = jb * tk + lax.broadcasted_iota(jnp.int32, (1, tk), 1) - PAD
            mask = (kpos < qpos) & (kpos >= 0)
            a_s, b_s = [a0, a1], [b0, b1]
            dk_blk = jnp.zeros((tk, BLOCK), F32)
            dv_blk = jnp.zeros((tk, BLOCK), F32)
            for h in range(2):
                z = lax.dot_general(qh[h], ks, _NT, preferred_element_type=F32) * 0.125
                lnb = jnp.where(mask, _neg_softplus(z), 0.0)
                c = tots[h] - a_s[h] - _split_dot(lnb, tri_ex)
                w = jnp.where(mask, jnp.exp(z + c), 0.0)
                dp = lax.dot_general(doh[h], vs, _NT, preferred_element_type=F32)
                g = w * dp
                g_in = _split_dot(g, tri_in) + b_s[h]
                sig = jnp.where(mask, jnp.exp(z + lnb), 0.0)
                dz = ((g - sig * g_in) * 0.125).astype(BF16)
                dq = dq + jnp.dot(dz, kh[h], preferred_element_type=F32)
                dk_blk = dk_blk + lax.dot_general(dz, qh[h], _TN, preferred_element_type=F32)
                dv_blk = dv_blk + lax.dot_general(w.astype(BF16), doh[h], _TN, preferred_element_type=F32)
                a_s[h] = a_s[h] + jnp.sum(lnb, axis=1, keepdims=True)
                b_s[h] = b_s[h] + jnp.sum(g, axis=1, keepdims=True)
            dk_acc[pl.ds(row, tk), :] += dk_blk
            dv_acc[pl.ds(row, tk), :] += dv_blk
            return a_s[0], a_s[1], b_s[0], b_s[1], dq

        zero = jnp.zeros((tq, 1), F32)
        res = lax.fori_loop(0, nkb, step, (zero, zero, zero, zero, jnp.zeros((tq, BLOCK), F32)))
        dq_ref[...] = res[4].astype(dq_ref.dtype)

        @pl.when(i == pl.num_programs(1) - 1)
        def _():
            dk_ref[...] = dk_acc[...].astype(dk_ref.dtype)
            dv_ref[...] = dv_acc[...].astype(dv_ref.dtype)

    cq, ck, cv = COL_Q // BLOCK, COL_K // BLOCK, COL_V // BLOCK
    blk = lambda p, i: (i, p)
    col = lambda p, i: (0, p)
    return _call(body, name="attn_bwd", out_shape=[_sds((lp, D_ATTN), BF16)] * 3, grid=(n_pair, lp // tq),
                 in_specs=[pl.BlockSpec((tq, BLOCK), lambda p, i: (i, cq + p)),
                           pl.BlockSpec((lp, BLOCK), lambda p, i: (0, ck + p)),
                           pl.BlockSpec((lp, BLOCK), lambda p, i: (0, cv + p)),
                           pl.BlockSpec((tq, BLOCK), blk), pl.BlockSpec((tq, BLOCK), blk)],
                 out_specs=[pl.BlockSpec((tq, BLOCK), blk), pl.BlockSpec((lp, BLOCK), col),
                            pl.BlockSpec((lp, BLOCK), col)],
                 scratch=[pltpu.VMEM((lp, BLOCK), F32), pltpu.VMEM((lp, BLOCK), F32)],
                 vmem_mb=48)(proj, proj, proj, tot, d_out)


def _shift_down(x, d):
    return x if d == 0 else pltpu.roll(x, d, 0)


def _shift_up(x, d):
    return x if d == 0 else pltpu.roll(x, x.shape[0] - d, 0)


def _pool_windows(ext, down):
    shift = _shift_down if down else _shift_up
    outs = []
    for g, w in enumerate(POOL_WINDOWS):
        s = ext[:, g * BLOCK:(g + 1) * BLOCK]
        d = 1
        while d < w:
            s = s + shift(s, d)
            d *= 2
        outs.append(s)
    return jnp.concatenate(outs, axis=1)


def _pool_counts(pos):
    cols = [jnp.broadcast_to(jnp.clip(pos + 1, 1, w).astype(F32), (pos.shape[0], BLOCK)) for w in POOL_WINDOWS]
    return jnp.concatenate(cols, axis=1)


def _group_dot(x, w_ref, transpose):
    outs = []
    for g in range(len(POOL_WINDOWS)):
        xg = x[:, g * BLOCK:(g + 1) * BLOCK].astype(BF16)
        wg = w_ref[g * BLOCK:(g + 1) * BLOCK, :]
        if transpose:
            outs.append(lax.dot_general(xg, wg, _NT, preferred_element_type=F32))
        else:
            outs.append(jnp.dot(xg, wg, preferred_element_type=F32))
    return jnp.concatenate(outs, axis=1)


def _pooled(prev, cur, r):
    rows = cur.shape[0]
    ext = jnp.concatenate([prev[rows - HALO:], cur], axis=0)
    pos = r * rows + lax.broadcasted_iota(jnp.int32, (rows, 1), 0) - PAD
    ws = _pool_windows(ext, down=True)[HALO:]
    return jnp.where(pos >= 0, ws / _pool_counts(pos) - cur, 0.0)


def _pool_fwd(proj, w_pool_bf, scale):
    lp = proj.shape[0]
    rows = BLOCK
    cb = COL_POOL // D_POOL

    def body(prev_ref, cur_ref, w_ref, s_ref, o_ref):
        pooled = _pooled(prev_ref[...], cur_ref[...], pl.program_id(0))
        o_ref[...] = _group_dot(pooled, w_ref, False) * s_ref[...]

    return _call(body, name="pool_fwd", out_shape=_sds((lp, D_POOL), F32), grid=(lp // rows,),
                 in_specs=[pl.BlockSpec((rows, D_POOL), lambda r: (jnp.maximum(r - 1, 0), cb)),
                           pl.BlockSpec((rows, D_POOL), lambda r: (r, cb)),
                           pl.BlockSpec((D_POOL, BLOCK), lambda r: (0, 0)),
                           pl.BlockSpec((1, D_POOL), lambda r: (0, 0))],
                 out_specs=pl.BlockSpec((rows, D_POOL), lambda r: (r, 0)))(proj, proj, w_pool_bf, scale)


def _pool_bwd(proj, d_out, w_pool_bf, scale):
    lp = proj.shape[0]
    rows = BLOCK
    n_chunk = lp // rows
    cb = COL_POOL // D_POOL

    def body(prev_ref, cur_ref, do_ref, don_ref, w_ref, s_ref, du_ref, dw_ref, ds_ref):
        r = pl.program_id(0)

        @pl.when(r == 0)
        def _():
            dw_ref[...] = jnp.zeros(dw_ref.shape, F32)
            ds_ref[...] = jnp.zeros(ds_ref.shape, F32)

        pooled = _pooled(prev_ref[...], cur_ref[...], r)
        d_ext = jnp.concatenate([do_ref[...], don_ref[0:HALO]], axis=0)
        pos = r * rows + lax.broadcasted_iota(jnp.int32, (rows + HALO, 1), 0) - PAD
        dmixed = jnp.where((pos >= 0) & (pos < lp - PAD), d_ext * s_ref[...], 0.0)
        dpooled = _group_dot(dmixed, w_ref, True)
        back = _pool_windows(dpooled / _pool_counts(pos), down=False)[0:rows]
        du = jnp.where(pos[0:rows] >= 0, back - dpooled[0:rows], 0.0)
        du_ref[...] = du.astype(du_ref.dtype)
        mixed = _group_dot(pooled, w_ref, False)
        ds_ref[...] += _colsum(do_ref[...] * mixed)
        pooled_bf = pooled.astype(BF16)
        dm_bf = dmixed[0:rows].astype(BF16)
        for g in range(len(POOL_WINDOWS)):
            sl = slice(g * BLOCK, (g + 1) * BLOCK)
            dw_ref[sl, :] += lax.dot_general(pooled_bf[:, sl], dm_bf[:, sl], _TN, preferred_element_type=F32)

    return _call(body, name="pool_bwd",
                 out_shape=[_sds((lp, D_POOL), BF16), _sds((D_POOL, BLOCK), F32), _sds((1, D_POOL), F32)],
                 grid=(n_chunk,),
                 in_specs=[pl.BlockSpec((rows, D_POOL), lambda r: (jnp.maximum(r - 1, 0), cb)),
                           pl.BlockSpec((rows, D_POOL), lambda r: (r, cb)),
                           pl.BlockSpec((rows, D_POOL), lambda r: (r, 0)),
                           pl.BlockSpec((rows, D_POOL), lambda r: (jnp.minimum(r + 1, n_chunk - 1), 0)),
                           pl.BlockSpec((D_POOL, BLOCK), lambda r: (0, 0)),
                           pl.BlockSpec((1, D_POOL), lambda r: (0, 0))],
                 out_specs=[pl.BlockSpec((rows, D_POOL), lambda r: (r, 0)),
                            pl.BlockSpec((D_POOL, BLOCK), lambda r: (0, 0)),
                            pl.BlockSpec((1, D_POOL), lambda r: (0, 0))])(proj, proj, d_out, d_out, w_pool_bf, scale)


def _conv_taps(u, wdw_ref):
    y = wdw_ref[CONV_WIDTH - 1:CONV_WIDTH, :] * u
    for d in range(1, CONV_WIDTH):
        y = y + wdw_ref[CONV_WIDTH - 1 - d:CONV_WIDTH - d, :] * _shift_down(u, d)
    return y


def _layernorm_stats(y):
    mu = jnp.mean(y, axis=-1, keepdims=True)
    yc = y - mu
    rstd = lax.rsqrt(jnp.mean(yc * yc, axis=-1, keepdims=True) + EPS)
    return yc * rstd, rstd


def _conv_fwd(proj, wdw, bdw, ln_g, ln_b, wpw_bf):
    lp = proj.shape[0]
    rows = BLOCK
    ca, cg = COL_A // D_CONV, COL_G // D_CONV

    def body(ap_ref, a_ref, gp_ref, g_ref, wdw_ref, b_ref, lg_ref, lb_ref, wpw_ref, o_ref):
        r = pl.program_id(0)
        a = jnp.concatenate([ap_ref[rows - HALO:rows], a_ref[...]], axis=0)
        g = jnp.concatenate([gp_ref[rows - HALO:rows], g_ref[...]], axis=0)
        u = a * _sigmoid(g)
        y = _conv_taps(u, wdw_ref)[HALO:] + b_ref[...]
        xhat, _ = _layernorm_stats(y)
        yn = xhat * lg_ref[...] + lb_ref[...]
        pos = r * rows + lax.broadcasted_iota(jnp.int32, (rows, 1), 0) - PAD
        s = jnp.where(pos >= 0, yn * _sigmoid(yn), 0.0)
        o_ref[...] = jnp.dot(s.astype(BF16), wpw_ref[...], preferred_element_type=F32)

    prev = lambda c: (lambda r: (jnp.maximum(r - 1, 0), c))
    cur = lambda c: (lambda r: (r, c))
    const = lambda shape: pl.BlockSpec(shape, lambda r: (0, 0))
    return _call(body, name="conv_fwd", out_shape=_sds((lp, D_CONV), F32), grid=(lp // rows,),
                 in_specs=[pl.BlockSpec((rows, D_CONV), prev(ca)), pl.BlockSpec((rows, D_CONV), cur(ca)),
                           pl.BlockSpec((rows, D_CONV), prev(cg)), pl.BlockSpec((rows, D_CONV), cur(cg)),
                           const((HALO, D_CONV)), const((1, D_CONV)), const((1, D_CONV)), const((1, D_CONV)),
                           const((D_CONV, D_CONV))],
                 out_specs=pl.BlockSpec((rows, D_CONV), cur(0)))(proj, proj, proj, proj, wdw, bdw, ln_g, ln_b, wpw_bf)


def _conv_bwd(proj, d_out, wdw, bdw, ln_g, ln_b, wpw_bf):
    lp = proj.shape[0]
    rows = BLOCK
    n_chunk = lp // rows
    ca, cg = COL_A // D_CONV, COL_G // D_CONV
    ext = rows + HALO

    def body(ap_ref, a_ref, an_ref, gp_ref, g_ref, gn_ref, do_ref, don_ref, wdw_ref, b_ref, lg_ref, lb_ref,
             wpw_ref, da_ref, dg_ref, dwdw_ref, db_ref, dlg_ref, dlb_ref, dwpw_ref):
        r = pl.program_id(0)

        @pl.when(r == 0)
        def _():
            for ref in (dwdw_ref, db_ref, dlg_ref, dlb_ref, dwpw_ref):
                ref[...] = jnp.zeros(ref.shape, F32)

        a3 = jnp.concatenate([ap_ref[rows - HALO:rows], a_ref[...], an_ref[0:HALO]], axis=0)
        g3 = jnp.concatenate([gp_ref[rows - HALO:rows], g_ref[...], gn_ref[0:HALO]], axis=0)
        sig3 = _sigmoid(g3)
        u3 = a3 * sig3
        y = _conv_taps(u3, wdw_ref)[HALO:] + b_ref[...]
        xhat, rstd = _layernorm_stats(y)
        yn = xhat * lg_ref[...] + lb_ref[...]
        sgm = _sigmoid(yn)
        pos = r * rows + lax.broadcasted_iota(jnp.int32, (ext, 1), 0) - PAD
        valid = (pos >= 0) & (pos < lp - PAD)
        d_ext = jnp.concatenate([do_ref[...], don_ref[0:HALO]], axis=0)
        ds = lax.dot_general(d_ext.astype(BF16), wpw_ref[...], _NT, preferred_element_type=F32)
        dyn = jnp.where(valid, ds * (sgm * (1.0 + yn * (1.0 - sgm))), 0.0)
        dxh = dyn * lg_ref[...]
        dy = rstd * (dxh - jnp.mean(dxh, axis=-1, keepdims=True)
                     - xhat * jnp.mean(dxh * xhat, axis=-1, keepdims=True))
        s_cur = jnp.where(valid[0:rows], (yn * sgm)[0:rows], 0.0)
        dwpw_ref[...] += lax.dot_general(s_cur.astype(BF16), do_ref[...].astype(BF16), _TN,
                                         preferred_element_type=F32)
        dlg_ref[...] += _colsum(dyn[0:rows] * xhat[0:rows])
        dlb_ref[...] += _colsum(dyn[0:rows])
        dy_cur = dy[0:rows]
        db_ref[...] += _colsum(dy_cur)
        du = wdw_ref[CONV_WIDTH - 1:CONV_WIDTH, :] * dy
        for d in range(CONV_WIDTH):
            k = CONV_WIDTH - 1 - d
            dwdw_ref[k:k + 1, :] += _colsum(dy_cur * _shift_down(u3, d)[HALO:HALO + rows])
            if d:
                du = du + wdw_ref[k:k + 1, :] * _shift_up(dy, d)
        du = jnp.where(pos[0:rows] >= 0, du[0:rows], 0.0)
        sig = sig3[HALO:HALO + rows]
        da_ref[...] = (du * sig).astype(da_ref.dtype)
        dg_ref[...] = (du * a_ref[...] * sig * (1.0 - sig)).astype(dg_ref.dtype)

    prev = lambda c: (lambda r: (jnp.maximum(r - 1, 0), c))
    cur = lambda c: (lambda r: (r, c))
    nxt = lambda c: (lambda r: (jnp.minimum(r + 1, n_chunk - 1), c))
    const = lambda shape: pl.BlockSpec(shape, lambda r: (0, 0))
    blk = lambda f: pl.BlockSpec((rows, D_CONV), f)
    return _call(body, name="conv_bwd",
                 out_shape=[_sds((lp, D_CONV), BF16), _sds((lp, D_CONV), BF16), _sds((HALO, D_CONV), F32),
                            _sds((1, D_CONV), F32), _sds((1, D_CONV), F32), _sds((1, D_CONV), F32),
                            _sds((D_CONV, D_CONV), F32)],
                 grid=(n_chunk,),
                 in_specs=[blk(prev(ca)), blk(cur(ca)), blk(nxt(ca)), blk(prev(cg)), blk(cur(cg)), blk(nxt(cg)),
                           blk(cur(0)), blk(nxt(0)),
                           const((HALO, D_CONV)), const((1, D_CONV)), const((1, D_CONV)), const((1, D_CONV)),
                           const((D_CONV, D_CONV))],
                 out_specs=[blk(cur(0)), blk(cur(0)), const((HALO, D_CONV)), const((1, D_CONV)),
                            const((1, D_CONV)), const((1, D_CONV)), const((D_CONV, D_CONV))],
                 vmem_mb=48)(proj, proj, proj, proj, proj, proj, d_out, d_out, wdw, bdw, ln_g, ln_b, wpw_bf)


_ANY = pl.BlockSpec(memory_space=pl.ANY)


def _place():
    return lax.axis_index("x"), lax.axis_index("y"), lax.axis_index("c")


def _xy_peers(x, y):
    return [(1 - x, y), (x, 1 - y), (1 - x, 1 - y)]


def _gather_xy(name, arrs):
    n = len(arrs)

    def body(*refs):
        ins, outs = refs[:n], refs[n:2 * n]
        send, recv, local = refs[2 * n:]
        x, y, c = _place()
        me = 2 * x + y
        copies = []
        for a in range(n):
            cp = pltpu.make_async_copy(ins[a], outs[a].at[me], local.at[a])
            cp.start()
            copies.append(cp)
        remote = []
        for a in range(n):
            for r, (px, py) in enumerate(_xy_peers(x, y)):
                cp = pltpu.make_async_remote_copy(src_ref=ins[a], dst_ref=outs[a].at[me],
                                                  send_sem=send.at[3 * a + r], recv_sem=recv.at[3 * a + r],
                                                  device_id=(px, py, c), device_id_type=MESH)
                cp.start()
                remote.append(cp)
        for cp in remote:
            cp.wait()
        for cp in copies:
            cp.wait()

    return _call(body, name=name, out_shape=[_sds((N_SHARD,) + a.shape, a.dtype) for a in arrs],
                 in_specs=[_ANY] * n, out_specs=[_ANY] * n,
                 scratch=[pltpu.SemaphoreType.DMA((3 * n,)), pltpu.SemaphoreType.DMA((3 * n,)),
                          pltpu.SemaphoreType.DMA((n,))])(*arrs)


def _scatter_xy(name, parts):
    n = len(parts)

    def body(*refs):
        ins, outs = refs[:n], refs[n:2 * n]
        send, recv, local = refs[2 * n:]
        x, y, c = _place()
        me = 2 * x + y
        copies = []
        for a in range(n):
            cp = pltpu.make_async_copy(ins[a].at[me], outs[a].at[me], local.at[a])
            cp.start()
            copies.append(cp)
        remote = []
        for a in range(n):
            for r, (px, py) in enumerate(_xy_peers(x, y)):
                cp = pltpu.make_async_remote_copy(src_ref=ins[a].at[2 * px + py], dst_ref=outs[a].at[me],
                                                  send_sem=send.at[3 * a + r], recv_sem=recv.at[3 * a + r],
                                                  device_id=(px, py, c), device_id_type=MESH)
                cp.start()
                remote.append(cp)
        for cp in remote:
            cp.wait()
        for cp in copies:
            cp.wait()

    return _call(body, name=name, out_shape=[_sds(a.shape, a.dtype) for a in parts],
                 in_specs=[_ANY] * n, out_specs=[_ANY] * n,
                 scratch=[pltpu.SemaphoreType.DMA((3 * n,)), pltpu.SemaphoreType.DMA((3 * n,)),
                          pltpu.SemaphoreType.DMA((n,))])(*parts)


def _swap_core(name, arrs):
    n = len(arrs)

    def body(*refs):
        ins, outs = refs[:n], refs[n:2 * n]
        send, recv = refs[2 * n:]
        x, y, c = _place()
        remote = []
        for a in range(n):
            cp = pltpu.make_async_remote_copy(src_ref=ins[a], dst_ref=outs[a], send_sem=send.at[a],
                                              recv_sem=recv.at[a], device_id=(x, y, 1 - c), device_id_type=MESH)
            cp.start()
            remote.append(cp)
        for cp in remote:
            cp.wait()

    return _call(body, name=name, out_shape=[_sds(a.shape, a.dtype) for a in arrs],
                 in_specs=[_ANY] * n, out_specs=[_ANY] * n,
                 scratch=[pltpu.SemaphoreType.DMA((n,)), pltpu.SemaphoreType.DMA((n,))])(*arrs)


def _gather_all(name, arr):
    flips = [(fx, fy, fc) for fx in (0, 1) for fy in (0, 1) for fc in (0, 1)][1:]

    def body(in_ref, out_ref, send, recv, local):
        x, y, c = _place()
        me = 4 * x + 2 * y + c
        own = pltpu.make_async_copy(in_ref, out_ref.at[me], local)
        own.start()
        remote = []
        for k, (fx, fy, fc) in enumerate(flips):
            peer = (1 - x if fx else x, 1 - y if fy else y, 1 - c if fc else c)
            cp = pltpu.make_async_remote_copy(src_ref=in_ref, dst_ref=out_ref.at[me], send_sem=send.at[k],
                                              recv_sem=recv.at[k], device_id=peer, device_id_type=MESH)
            cp.start()
            remote.append(cp)
        for cp in remote:
            cp.wait()
        own.wait()

    return _call(body, name=name, out_shape=_sds((N_DEV,) + arr.shape, arr.dtype), in_specs=[_ANY], out_specs=_ANY,
                 scratch=[pltpu.SemaphoreType.DMA((N_DEV - 1,)), pltpu.SemaphoreType.DMA((N_DEV - 1,)),
                          pltpu.SemaphoreType.DMA])(arr)


def _sum_slots(name, stacked, out_dtype=F32):
    s, r, c = stacked.shape
    tr = _pick(r, (256, 128, 64, 8))

    def body(in_ref, o_ref):
        acc = in_ref[0].astype(F32)
        for k in range(1, s):
            acc = acc + in_ref[k].astype(F32)
        o_ref[...] = acc.astype(o_ref.dtype)

    return _call(body, name=name, out_shape=_sds((r, c), out_dtype), grid=(r // tr,),
                 in_specs=[pl.BlockSpec((s, tr, c), lambda i: (0, i, 0))],
                 out_specs=pl.BlockSpec((tr, c), lambda i: (i, 0)))(stacked)


def _sum_slots_layers(name, r0, r1):
    s, r, c = r0.shape
    tr = _pick(r, (256, 128))

    def body(a_ref, b_ref, o_ref):
        def total(ref):
            acc = ref[0].astype(F32)
            for k in range(1, s):
                acc = acc + ref[k].astype(F32)
            return acc

        @pl.when(pl.program_id(0) == 0)
        def _():
            o_ref[...] = total(a_ref)

        @pl.when(pl.program_id(0) == 1)
        def _():
            o_ref[...] = total(b_ref)

    return _call(body, name=name, out_shape=_sds((2, r, c), F32), grid=(2, r // tr),
                 in_specs=[pl.BlockSpec((s, tr, c), lambda l, i: (0, i * (1 - l), 0)),
                           pl.BlockSpec((s, tr, c), lambda l, i: (0, i * l, 0))],
                 out_specs=pl.BlockSpec((None, tr, c), lambda l, i: (l, i, 0)))(r0, r1)


def _adamw_math(w, g, m, v):
    m = ADAM_B1 * m + (1.0 - ADAM_B1) * g
    v = ADAM_B2 * v + (1.0 - ADAM_B2) * (g * g)
    m_hat = m / (1.0 - ADAM_B1 ** ADAM_STEP)
    v_hat = v / (1.0 - ADAM_B2 ** ADAM_STEP)
    delta = -ADAM_LR * (m_hat / (jnp.sqrt(v_hat) + ADAM_EPS) + ADAM_WD * w)
    return delta, m, v


def _adamw(name, w, m, v, g_mine, g_other):
    l, r, c = w.shape
    tr = _pick(r, (128, 64, 8))

    def body(w_ref, m_ref, v_ref, ga_ref, gb_ref, g_ref, d_ref, nm_ref, nv_ref):
        g = ga_ref[...] + gb_ref[...]
        delta, nm, nv = _adamw_math(w_ref[...], g, m_ref[...], v_ref[...])
        g_ref[...] = g
        d_ref[...] = delta
        nm_ref[...] = nm
        nv_ref[...] = nv

    spec = pl.BlockSpec((None, tr, c), lambda li, i: (li, i, 0))
    return _call(body, name=name, out_shape=[_sds(w.shape, F32)] * 4, grid=(l, r // tr),
                 in_specs=[spec] * 5, out_specs=[spec] * 4, vmem_mb=48)(w, m, v, g_mine, g_other)


def _adamw_flat(name, w, m, v, g):
    r, c = w.shape
    tr = _pick(r, (256, 128, 64, 8))

    def body(w_ref, m_ref, v_ref, g_ref, d_ref, nm_ref, nv_ref):
        delta, nm, nv = _adamw_math(w_ref[...], g_ref[...], m_ref[...], v_ref[...])
        d_ref[...] = delta
        nm_ref[...] = nm
        nv_ref[...] = nv

    spec = pl.BlockSpec((tr, c), lambda i: (i, 0))
    return _call(body, name=name, out_shape=[_sds(w.shape, F32)] * 3, grid=(r // tr,),
                 in_specs=[spec] * 4, out_specs=[spec] * 3)(w, m, v, g)


def _pack(arrs, row_multiple=256):
    flat = jnp.concatenate([a.reshape(-1).astype(F32) for a in arrs])
    per = BLOCK * row_multiple
    total = -(-flat.shape[0] // per) * per
    return jnp.pad(flat, (0, total - flat.shape[0])).reshape(total // BLOCK, BLOCK)


def _unpack(buf, shapes):
    flat = buf.reshape(-1)
    outs, off = [], 0
    for s in shapes:
        size = 1
        for d in s:
            size *= d
        outs.append(flat[off:off + size].reshape(s))
        off += size
    return outs


def kernel(x, meta_tokens, pre_mix_g, w_in, w_pool, pool_scale, w_dw, b_dw, conv_ln_g, conv_ln_b, w_pw, mix_out_g, w_out, post_mix_g, pre_ffn_g, w_gate, w_up, w_down, post_ffn_g, loss_target, m_meta_tokens, m_pre_mix_g, m_w_in, m_w_pool, m_pool_scale, m_w_dw, m_b_dw, m_conv_ln_g, m_conv_ln_b, m_w_pw, m_mix_out_g, m_w_out, m_post_mix_g, m_pre_ffn_g, m_w_gate, m_w_up, m_w_down, m_post_ffn_g, v_meta_tokens, v_pre_mix_g, v_w_in, v_w_pool, v_pool_scale, v_w_dw, v_b_dw, v_conv_ln_g, v_conv_ln_b, v_w_pw, v_mix_out_g, v_w_out, v_post_mix_g, v_pre_ffn_g, v_w_gate, v_w_up, v_w_down, v_post_ffn_g):
    seq = x.shape[1]
    lp = PAD + N_META + seq
    depth = w_in.shape[0]
    xy = 2 * lax.axis_index("x") + lax.axis_index("y")

    small_shapes = [meta_tokens.shape, w_dw.shape, w_pw.shape]
    small_local = _pack([meta_tokens, w_dw, w_pw], row_multiple=8)
    big_local = [w.astype(BF16) for w in (w_in, w_out, w_gate, w_up, w_down)]
    gathered = _gather_xy("gather_weights", big_local + [small_local])
    wg_in, wg_out, wg_gate, wg_up, wg_down, small_all = gathered
    metas, wdws, wpws = [], [], []
    for s in range(N_SHARD):
        mt, wd, wp = _unpack(small_all[s], small_shapes)
        metas.append(mt)
        wdws.append(wd)
        wpws.append(wp)
    meta_full = jnp.concatenate(metas, axis=1)
    wdw_full = jnp.concatenate(wdws, axis=2)
    wpw_full = jnp.concatenate(wpws, axis=1)
    wdw_pad = jnp.pad(wdw_full, ((0, 0), (0, HALO - CONV_WIDTH), (0, 0)))
    wpw_bf = wpw_full.astype(BF16)
    wpool_bf = w_pool.reshape(depth, D_POOL, BLOCK).astype(BF16)

    row = lambda a, i: a[i][None, :]

    h = jnp.concatenate([jnp.zeros((PAD, D_MODEL), F32), meta_full, x[0]], axis=0)
    target = jnp.pad(loss_target[0], ((PAD + N_META, 0), (0, 0)))
    u = _rowwise("pre_mix_norm0", lambda hh, g: _rms(hh, g), [h], [row(pre_mix_g, 0)], [(D_MODEL, BF16)])[0]
    saved = []
    for i in range(depth):
        proj = _mm_nn_col("in_proj%d" % i, u, wg_in, i, F32)
        o_attn, tot = _attn_fwd(proj)
        o_pool = _pool_fwd(proj, wpool_bf[i], row(pool_scale, i))
        o_conv = _conv_fwd(proj, wdw_pad[i], row(b_dw, i), row(conv_ln_g, i), row(conv_ln_b, i), wpw_bf[i])

        mix_gains = [row(mix_out_g, i)[:, :D_ATTN], row(mix_out_g, i)[:, D_ATTN:D_ATTN + D_POOL],
                     row(mix_out_g, i)[:, D_ATTN + D_POOL:]]
        merged = _rowwise("merge%d" % i, _merge, [o_attn, o_pool, o_conv], mix_gains, [(D_MODEL, BF16)])[0]
        mix = _mm_nn_row("out_proj%d" % i, merged, wg_out, i)

        def post_mix(hh, mx, g1, g2):
            h1 = hh + _rms(mx, g1)
            return h1, _rms(h1, g2)

        h1, u2 = _rowwise("post_mix%d" % i, post_mix, [h, mix], [row(post_mix_g, i), row(pre_ffn_g, i)],
                          [(D_MODEL, F32), (D_MODEL, BF16)])
        gate = _mm_nn_col("ffn_gate%d" % i, u2, wg_gate, i, F32)
        up = _mm_nn_col("ffn_up%d" % i, u2, wg_up, i, F32)
        act = _rowwise("swiglu%d" % i, lambda gt, p: gt * _sigmoid(gt) * p, [gate, up], [], [(D_FF, BF16)],
                       vmem_mb=48)[0]
        ff = _mm_nn_row("ffn_down%d" % i, act, wg_down, i)
        rec = dict(h=h, u=u, proj=proj, tot=tot, o_attn=o_attn, o_pool=o_pool, o_conv=o_conv, merged=merged,
                   mix=mix, h1=h1, u2=u2, gate=gate, up=up, act=act, ff=ff)
        saved.append(rec)
        if i + 1 < depth:
            def post_ffn(hh, f, g1, g2):
                h2 = hh + _rms(f, g1)
                return h2, _rms(h2, g2)

            h, u = _rowwise("post_ffn%d" % i, post_ffn, [h1, ff], [row(post_ffn_g, i), row(pre_mix_g, i + 1)],
                            [(D_MODEL, F32), (D_MODEL, BF16)])
        else:
            def head(row0, hh, f, tgt, g1):
                y = hh + _rms(f, g1)
                rid = row0 + lax.broadcasted_iota(jnp.int32, (y.shape[0], 1), 0)
                err = jnp.where(rid >= PAD + N_META, y - tgt, 0.0)
                part = 0.5 * jnp.sum(jnp.mean(err * err, axis=-1, keepdims=True), axis=0, keepdims=True)
                return err * (1.0 / D_MODEL), jnp.broadcast_to(part, (8, BLOCK))

            dh, loss_part = _rowwise("loss_head", head, [h1, ff, target], [row(post_ffn_g, i)],
                                     [(D_MODEL, F32)], accs=[(8, BLOCK)], with_row0=True)

    loss = lax.psum(loss_part[0, 0], ("x", "y", "c"))

    small_grads = {}
    big_parts = {}
    for i in reversed(range(depth)):
        rec = saved[i]

        def post_ffn_b(f, d, g):
            _, vjp = jax.vjp(_rms, f, g)
            df, dg = vjp(d)
            return df, dg

        dff, g_post_ffn = _rowwise("post_ffn_b%d" % i, post_ffn_b, [rec["ff"], dh], [row(post_ffn_g, i)],
                                   [(D_MODEL, BF16)], accs=[(1, D_MODEL)])
        big_parts[("w_down", i)] = _mm_tn_row("dw_down%d" % i, rec["act"], dff)
        dact = _mm_nt_row("d_act%d" % i, dff, wg_down, i, F32)

        def swiglu_b(gt, p, d):
            sg = _sigmoid(gt)
            return d * p * (sg * (1.0 + gt * (1.0 - sg))), d * (gt * sg)

        dgate, dup = _rowwise("swiglu_b%d" % i, swiglu_b, [rec["gate"], rec["up"], dact], [],
                              [(D_FF, BF16), (D_FF, BF16)], vmem_mb=56)
        big_parts[("w_gate", i)] = _mm_tn_col("dw_gate%d" % i, rec["u2"], dgate)
        big_parts[("w_up", i)] = _mm_tn_col("dw_up%d" % i, rec["u2"], dup)
        du2a = _mm_nt_col("d_u2_gate%d" % i, dgate, wg_gate, i)
        du2b = _mm_nt_col("d_u2_up%d" % i, dup, wg_up, i)

        def post_mix_b(h1v, mx, d, da, db, g1, g2):
            _, vjp2 = jax.vjp(_rms, h1v, g2)
            dh1, dg2 = vjp2(da + db)
            dmid = d + dh1
            _, vjp1 = jax.vjp(_rms, mx, g1)
            dmx, dg1 = vjp1(dmid)
            return dmid, dmx, dg1, dg2

        dmid, dmix, g_post_mix, g_pre_ffn = _rowwise(
            "post_mix_b%d" % i, post_mix_b, [rec["h1"], rec["mix"], dh, du2a, du2b],
            [row(post_mix_g, i), row(pre_ffn_g, i)], [(D_MODEL, F32), (D_MODEL, BF16)],
            accs=[(1, D_MODEL), (1, D_MODEL)], vmem_mb=48)
        big_parts[("w_out", i)] = _mm_tn_row("dw_out%d" % i, rec["merged"], dmix)
        dmerged = _mm_nt_row("d_merged%d" % i, dmix, wg_out, i, F32)

        def merge_b(oa, op, oc, d, ga, gp, gc):
            _, vjp = jax.vjp(_merge, oa, op, oc, ga, gp, gc)
            return vjp(d)

        mix_gains = [row(mix_out_g, i)[:, :D_ATTN], row(mix_out_g, i)[:, D_ATTN:D_ATTN + D_POOL],
                     row(mix_out_g, i)[:, D_ATTN + D_POOL:]]
        do_attn, do_pool, do_conv, g_mo_a, g_mo_p, g_mo_c = _rowwise(
            "merge_b%d" % i, merge_b, [rec["o_attn"], rec["o_pool"], rec["o_conv"], dmerged], mix_gains,
            [(D_ATTN, F32), (D_POOL, F32), (D_CONV, F32)], accs=[(1, D_ATTN), (1, D_POOL), (1, D_CONV)])
        g_mix_out = jnp.concatenate([g_mo_a, g_mo_p, g_mo_c], axis=1)
        dq, dk, dv = _attn_bwd(rec["proj"], rec["tot"], do_attn)
        du_pool, g_w_pool, g_pool_scale = _pool_bwd(rec["proj"], do_pool, wpool_bf[i], row(pool_scale, i))
        da, dgt, g_w_dw, g_b_dw, g_ln_g, g_ln_b, g_w_pw = _conv_bwd(
            rec["proj"], do_conv, wdw_pad[i], row(b_dw, i), row(conv_ln_g, i), row(conv_ln_b, i), wpw_bf[i])
        dproj = jnp.concatenate([dq, dk, dv, du_pool, da, dgt], axis=1)
        big_parts[("w_in", i)] = _mm_tn_col("dw_in%d" % i, rec["u"], dproj)
        du = _mm_nt_col("d_u%d" % i, dproj, wg_in, i)

        def pre_mix_b(hv, d, dd, g):
            _, vjp = jax.vjp(_rms, hv, g)
            dhh, dg = vjp(dd)
            return d + dhh, dg

        dh, g_pre_mix = _rowwise("pre_mix_b%d" % i, pre_mix_b, [rec["h"], dmid, du], [row(pre_mix_g, i)],
                                 [(D_MODEL, F32)], accs=[(1, D_MODEL)])
        small_grads[i] = dict(pre_mix_g=g_pre_mix[0], w_pool=g_w_pool, pool_scale=g_pool_scale[0],
                              w_dw=g_w_dw[:CONV_WIDTH], b_dw=g_b_dw[0], conv_ln_g=g_ln_g[0], conv_ln_b=g_ln_b[0],
                              w_pw=g_w_pw, mix_out_g=g_mix_out[0], post_mix_g=g_post_mix[0],
                              pre_ffn_g=g_pre_ffn[0], post_ffn_g=g_post_ffn[0])

    grad_x = dh[PAD + N_META:][None]
    g_meta_part = dh[PAD:PAD + N_META]

    rep_names = ["pre_mix_g", "pool_scale", "b_dw", "conv_ln_g", "conv_ln_b", "mix_out_g", "post_mix_g",
                 "pre_ffn_g", "post_ffn_g", "w_pool"]
    stack2 = lambda nme: jnp.stack([small_grads[l][nme] for l in range(depth)])
    small_list = [stack2(nme) for nme in rep_names] + [g_meta_part, stack2("w_dw"), stack2("w_pw")]
    small_list[rep_names.index("w_pool")] = small_list[rep_names.index("w_pool")].reshape(w_pool.shape)
    full_shapes = [a.shape for a in small_list]
    packed = _pack(small_list)
    summed = _sum_slots("sum_small", _gather_all("gather_small_grads", packed))
    full = _unpack(summed, full_shapes)
    rep_grads = dict(zip(rep_names, full[:len(rep_names)]))
    g_meta = lax.dynamic_slice_in_dim(full[-3], xy * meta_tokens.shape[1], meta_tokens.shape[1], axis=1)
    g_w_dw = lax.dynamic_slice_in_dim(full[-2], xy * w_dw.shape[2], w_dw.shape[2], axis=2)
    g_w_pw = lax.dynamic_slice_in_dim(full[-1], xy * w_pw.shape[1], w_pw.shape[1], axis=1)

    rep_w = dict(pre_mix_g=pre_mix_g, pool_scale=pool_scale, b_dw=b_dw, conv_ln_g=conv_ln_g, conv_ln_b=conv_ln_b,
                 mix_out_g=mix_out_g, post_mix_g=post_mix_g, pre_ffn_g=pre_ffn_g, post_ffn_g=post_ffn_g,
                 w_pool=w_pool)
    rep_m = dict(pre_mix_g=m_pre_mix_g, pool_scale=m_pool_scale, b_dw=m_b_dw, conv_ln_g=m_conv_ln_g,
                 conv_ln_b=m_conv_ln_b, mix_out_g=m_mix_out_g, post_mix_g=m_post_mix_g, pre_ffn_g=m_pre_ffn_g,
                 post_ffn_g=m_post_ffn_g, w_pool=m_w_pool)
    rep_v = dict(pre_mix_g=v_pre_mix_g, pool_scale=v_pool_scale, b_dw=v_b_dw, conv_ln_g=v_conv_ln_g,
                 conv_ln_b=v_conv_ln_b, mix_out_g=v_mix_out_g, post_mix_g=v_post_mix_g, pre_ffn_g=v_pre_ffn_g,
                 post_ffn_g=v_post_ffn_g, w_pool=v_w_pool)
    sm_names = rep_names + ["meta_tokens", "w_dw", "w_pw"]
    sm_w = [rep_w[k] for k in rep_names] + [meta_tokens, w_dw, w_pw]
    sm_m = [rep_m[k] for k in rep_names] + [m_meta_tokens, m_w_dw, m_w_pw]
    sm_v = [rep_v[k] for k in rep_names] + [v_meta_tokens, v_w_dw, v_w_pw]
    sm_g = [rep_grads[k] for k in rep_names] + [g_meta, g_w_dw, g_w_pw]
    sm_shapes = [a.shape for a in sm_w]
    sm_delta, sm_nm, sm_nv = _adamw_flat("adamw_small", _pack(sm_w), _pack(sm_m), _pack(sm_v), _pack(sm_g))
    small_out = {}
    for k, g, d, nm, nv in zip(sm_names, sm_g, _unpack(sm_delta, sm_shapes), _unpack(sm_nm, sm_shapes),
                               _unpack(sm_nv, sm_shapes)):
        small_out[k] = (g, d, nm, nv)

    big_names = ["w_in", "w_out", "w_gate", "w_up", "w_down"]
    part_list = [big_parts[(k, l)] for k in big_names for l in range(depth)]
    received = _scatter_xy("scatter_weight_grads", part_list)
    plane_sums = [_sum_slots_layers("sum_%s" % k, received[depth * j], received[depth * j + 1])
                  for j, k in enumerate(big_names)]
    other_sums = _swap_core("swap_core_sums", plane_sums)
    big_w = dict(w_in=(w_in, m_w_in, v_w_in), w_out=(w_out, m_w_out, v_w_out), w_gate=(w_gate, m_w_gate, v_w_gate),
                 w_up=(w_up, m_w_up, v_w_up), w_down=(w_down, m_w_down, v_w_down))
    big_out = {}
    for j, k in enumerate(big_names):
        w, m, v = big_w[k]
        big_out[k] = _adamw("adamw_%s" % k, w, m, v, plane_sums[j], other_sums[j])

    order = ["meta_tokens", "pre_mix_g", "w_in", "w_pool", "pool_scale", "w_dw", "b_dw", "conv_ln_g", "conv_ln_b",
             "w_pw", "mix_out_g", "w_out", "post_mix_g", "pre_ffn_g", "w_gate", "w_up", "w_down", "post_ffn_g"]
    res = lambda k: big_out[k] if k in big_out else small_out[k]
    outs = [loss, grad_x]
    for part in range(4):
        outs += [res(k)[part] for k in order]
    return tuple(outs)
```

```python
import functools

import jax
import jax.numpy as jnp
from jax import lax
from jax.experimental import pallas as pl
from jax.experimental.pallas import tpu as pltpu

F32 = jnp.float32
BF16 = jnp.bfloat16

D_MODEL = 2048
N_META = 16
D_ATTN = 1024
D_POOL = 512
D_CONV = 512
POOL_WINDOWS = (2, 4, 8, 16)
CONV_WIDTH = 31
D_IN_PROJ = 3 * D_ATTN + D_POOL + 2 * D_CONV
D_FF = 5632
EPS = 1e-6
BLOCK = 128
PAD = BLOCK - N_META
HALO = 32
N_SHARD = 4
N_DEV = 8
MESH = pl.DeviceIdType.MESH

ADAM_LR = 0.001
ADAM_B1 = 0.9
ADAM_B2 = 0.999
ADAM_EPS = 1e-08
ADAM_WD = 0.01
ADAM_STEP = 10

COL_Q, COL_K, COL_V = 0, D_ATTN, 2 * D_ATTN
COL_POOL = 3 * D_ATTN
COL_A = COL_POOL + D_POOL
COL_G = COL_A + D_CONV


def _call(body, *, name, out_shape, grid=None, in_specs=None, out_specs=None, scratch=(), vmem_mb=None):
    params = {}
    if grid is not None:
        params["dimension_semantics"] = ("arbitrary",) * len(grid)
    if vmem_mb is not None:
        params["vmem_limit_bytes"] = vmem_mb << 20
    kw = dict(out_shape=out_shape, name=name, compiler_params=pltpu.CompilerParams(**params))
    if grid is not None:
        kw["grid"] = grid
    if in_specs is not None:
        kw["in_specs"] = in_specs
    if out_specs is not None:
        kw["out_specs"] = out_specs
    if scratch:
        kw["scratch_shapes"] = list(scratch)
    return pl.pallas_call(body, **kw)


def _sds(shape, dtype):
    return jax.ShapeDtypeStruct(tuple(shape), dtype)


def _pick(n, candidates):
    for c in candidates:
        if n % c == 0:
            return c
    return n


def _rowwise(name, fn, rows_in, consts, outs, accs=(), tm=BLOCK, with_row0=False, vmem_mb=None):
    lp = rows_in[0].shape[0]
    n_in, n_c, n_o = len(rows_in), len(consts), len(outs)

    def body(*refs):
        vals = [r[...] for r in refs[:n_in + n_c]]
        if with_row0:
            vals = [pl.program_id(0) * tm] + vals
        res = fn(*vals)
        if not isinstance(res, (tuple, list)):
            res = (res,)
        o_refs = refs[n_in + n_c:n_in + n_c + n_o]
        a_refs = refs[n_in + n_c + n_o:]
        for r, v in zip(o_refs, res[:n_o]):
            r[...] = v.astype(r.dtype)
        if a_refs:
            @pl.when(pl.program_id(0) == 0)
            def _():
                for r in a_refs:
                    r[...] = jnp.zeros(r.shape, r.dtype)
            for r, v in zip(a_refs, res[n_o:]):
                r[...] += v.astype(r.dtype)

    in_specs = [pl.BlockSpec((tm, a.shape[1]), lambda i: (i, 0)) for a in rows_in]
    in_specs += [pl.BlockSpec(c.shape, lambda i: (0, 0)) for c in consts]
    out_specs = [pl.BlockSpec((tm, w), lambda i: (i, 0)) for (w, _) in outs]
    out_specs += [pl.BlockSpec(s, lambda i: (0, 0)) for s in accs]
    out_shape = [_sds((lp, w), dt) for (w, dt) in outs] + [_sds(s, F32) for s in accs]
    res = _call(body, name=name, out_shape=out_shape, grid=(lp // tm,), in_specs=in_specs,
                out_specs=out_specs, vmem_mb=vmem_mb)(*rows_in, *consts)
    return res


def _rms(x, g):
    return x * lax.rsqrt(jnp.mean(x * x, axis=-1, keepdims=True) + EPS) * g


def _merge(oa, op, oc, ga, gp, gc):
    return jnp.concatenate([_rms(oa, ga), _rms(op, gp), _rms(oc, gc)], axis=1)


def _colsum(x):
    return jnp.sum(x, axis=0, keepdims=True)


def _sigmoid(x):
    return 1.0 / (1.0 + jnp.exp(-x))


def _mm_tiles(lp):
    return _pick(lp, (384, 256, 128))


def _mm_nn_col(name, a, wg, layer, out_dtype):
    lp, k = a.shape
    n = wg.shape[3]
    tm = _mm_tiles(lp)

    def body(a_ref, w_ref, o_ref):
        o_ref[...] = jnp.dot(a_ref[...], w_ref[...], preferred_element_type=F32).astype(o_ref.dtype)

    return _call(body, name=name, out_shape=_sds((lp, N_SHARD * n), out_dtype), grid=(N_SHARD, lp // tm),
                 in_specs=[pl.BlockSpec((tm, k), lambda s, i: (i, 0)),
                           pl.BlockSpec((None, None, k, n), lambda s, i: (s, layer, 0, 0))],
                 out_specs=pl.BlockSpec((tm, n), lambda s, i: (i, s)), vmem_mb=48)(a, wg)


def _mm_nn_row(name, a, wg, layer):
    lp = a.shape[0]
    k, n = wg.shape[2], wg.shape[3]
    tm = _mm_tiles(lp)

    def body(a_ref, w_ref, o_ref):
        part = jnp.dot(a_ref[...], w_ref[...], preferred_element_type=F32)

        @pl.when(pl.program_id(1) == 0)
        def _():
            o_ref[...] = part

        @pl.when(pl.program_id(1) != 0)
        def _():
            o_ref[...] += part

    return _call(body, name=name, out_shape=_sds((lp, n), F32), grid=(lp // tm, N_SHARD),
                 in_specs=[pl.BlockSpec((tm, k), lambda i, s: (i, s)),
                           pl.BlockSpec((None, None, k, n), lambda i, s: (s, layer, 0, 0))],
                 out_specs=pl.BlockSpec((tm, n), lambda i, s: (i, 0)), vmem_mb=48)(a, wg)


_NT = (((1,), (1,)), ((), ()))
_TN = (((0,), (0,)), ((), ()))


def _mm_nt_col(name, dy, wg, layer):
    lp = dy.shape[0]
    k, n = wg.shape[2], wg.shape[3]
    tm = _mm_tiles(lp)

    def body(d_ref, w_ref, o_ref):
        part = lax.dot_general(d_ref[...], w_ref[...], _NT, preferred_element_type=F32)

        @pl.when(pl.program_id(1) == 0)
        def _():
            o_ref[...] = part

        @pl.when(pl.program_id(1) != 0)
        def _():
            o_ref[...] += part

    return _call(body, name=name, out_shape=_sds((lp, k), F32), grid=(lp // tm, N_SHARD),
                 in_specs=[pl.BlockSpec((tm, n), lambda i, s: (i, s)),
                           pl.BlockSpec((None, None, k, n), lambda i, s: (s, layer, 0, 0))],
                 out_specs=pl.BlockSpec((tm, k), lambda i, s: (i, 0)), vmem_mb=48)(dy, wg)


def _mm_nt_row(name, dy, wg, layer, out_dtype):
    lp = dy.shape[0]
    k, n = wg.shape[2], wg.shape[3]
    tm = _mm_tiles(lp)

    def body(d_ref, w_ref, o_ref):
        o_ref[...] = lax.dot_general(d_ref[...], w_ref[...], _NT, preferred_element_type=F32).astype(o_ref.dtype)

    return _call(body, name=name, out_shape=_sds((lp, N_SHARD * k), out_dtype), grid=(N_SHARD, lp // tm),
                 in_specs=[pl.BlockSpec((tm, n), lambda s, i: (i, 0)),
                           pl.BlockSpec((None, None, k, n), lambda s, i: (s, layer, 0, 0))],
                 out_specs=pl.BlockSpec((tm, k), lambda s, i: (i, s)), vmem_mb=48)(dy, wg)


def _mm_tn_col(name, a, dy):
    lp, k = a.shape
    n = dy.shape[1] // N_SHARD
    tm = _mm_tiles(lp)
    tk = _pick(k, (1024, 512))

    def body(a_ref, d_ref, o_ref, acc):
        @pl.when(pl.program_id(2) == 0)
        def _():
            acc[...] = jnp.zeros(acc.shape, F32)

        acc[...] += lax.dot_general(a_ref[...], d_ref[...], _TN, preferred_element_type=F32)

        @pl.when(pl.program_id(2) == pl.num_programs(2) - 1)
        def _():
            o_ref[...] = acc[...].astype(o_ref.dtype)

    return _call(body, name=name, out_shape=_sds((N_SHARD, k, n), BF16), grid=(N_SHARD, k // tk, lp // tm),
                 in_specs=[pl.BlockSpec((tm, tk), lambda s, kk, i: (i, kk)),
                           pl.BlockSpec((tm, n), lambda s, kk, i: (i, s))],
                 out_specs=pl.BlockSpec((None, tk, n), lambda s, kk, i: (s, kk, 0)),
                 scratch=[pltpu.VMEM((tk, n), F32)], vmem_mb=48)(a, dy)


def _mm_tn_row(name, a, dy):
    lp = a.shape[0]
    k = a.shape[1] // N_SHARD
    n = dy.shape[1]
    tm = _mm_tiles(lp)
    tn = _pick(n, (1024, 512))

    def body(a_ref, d_ref, o_ref, acc):
        @pl.when(pl.program_id(2) == 0)
        def _():
            acc[...] = jnp.zeros(acc.shape, F32)

        acc[...] += lax.dot_general(a_ref[...], d_ref[...], _TN, preferred_element_type=F32)

        @pl.when(pl.program_id(2) == pl.num_programs(2) - 1)
        def _():
            o_ref[...] = acc[...].astype(o_ref.dtype)

    return _call(body, name=name, out_shape=_sds((N_SHARD, k, n), BF16), grid=(N_SHARD, n // tn, lp // tm),
                 in_specs=[pl.BlockSpec((tm, k), lambda s, j, i: (i, s)),
                           pl.BlockSpec((tm, tn), lambda s, j, i: (i, j))],
                 out_specs=pl.BlockSpec((None, k, tn), lambda s, j, i: (s, 0, j)),
                 scratch=[pltpu.VMEM((k, tn), F32)], vmem_mb=48)(a, dy)


SUB = 16
WIDE = 2 * BLOCK
Z_CLAMP = 20.0


def _log1m_sigmoid(z):
    return -jnp.where(z > Z_CLAMP, z, jnp.log(1.0 + jnp.exp(jnp.minimum(z, Z_CLAMP))))


def _tri2(tk, kind):
    r = lax.broadcasted_iota(jnp.int32, (2 * tk, tk), 0)
    r = jnp.where(r >= tk, r - tk, r)
    c = lax.broadcasted_iota(jnp.int32, (2 * tk, tk), 1)
    t = {"ge": r >= c, "le": r <= c}[kind]
    return jnp.where(t, 1.0, 0.0).astype(BF16)


def _strip_mask(kind, s, tk):
    if kind == "none":
        return None
    col = lax.broadcasted_iota(jnp.int32, (SUB, tk), 1)
    row = lax.broadcasted_iota(jnp.int32, (SUB, tk), 0)
    causal = (col - row) < s * SUB
    if kind == "diag":
        return causal
    if kind == "pad":
        return col >= PAD
    return causal & (col >= PAD)


def _attn_blocks(lp):
    nb = lp // BLOCK
    assert nb % 2 == 1, "sequence must be a 128-row block plus whole 256-row blocks"
    return nb, (nb + 1) // 2


class _Exchange:
    def __init__(self, kind, arrs):
        self.kind, self.arrs, self.n = kind, list(arrs), len(arrs)

    def out_shape(self):
        if self.kind == "gather":
            return [_sds((N_SHARD,) + a.shape, a.dtype) for a in self.arrs]
        return [_sds(a.shape, a.dtype) for a in self.arrs]

    def scratch(self):
        return [pltpu.SemaphoreType.DMA((3 * self.n,)), pltpu.SemaphoreType.DMA((3 * self.n,)),
                pltpu.SemaphoreType.DMA((self.n,))]

    def copies(self, ins, outs, send, recv, local):
        x, y, c = _place()
        me = 2 * x + y
        out = []
        for a in range(self.n):
            own = ins[a] if self.kind == "gather" else ins[a].at[me]
            out.append(pltpu.make_async_copy(own, outs[a].at[me], local.at[a]))
            for r, (px, py) in enumerate(_xy_peers(x, y)):
                src = ins[a] if self.kind == "gather" else ins[a].at[2 * px + py]
                out.append(pltpu.make_async_remote_copy(
                    src_ref=src, dst_ref=outs[a].at[me], send_sem=send.at[3 * a + r], recv_sem=recv.at[3 * a + r],
                    device_id=(px, py, c), device_id_type=MESH))
        return out


def _attn_fwd(proj, exchange=None):
    lp = proj.shape[0]
    nb, nq = _attn_blocks(lp)
    n_pair = D_ATTN // BLOCK
    n_x = exchange.n if exchange else 0

    def body(*refs):
        q_ref, k_ref, v_ref = refs[:3]
        x_in = refs[3:3 + n_x]
        o_ref, tot_ref = refs[3 + n_x:5 + n_x]
        x_out = refs[5 + n_x:5 + 2 * n_x]
        qs, kb, vh, tri_l, tri_s, z_s, hl_s, c_s, w_s, r_s, acc_s = refs[5 + 2 * n_x:16 + 2 * n_x]
        x_sem = refs[16 + 2 * n_x:]
        p, i = pl.program_id(0), pl.program_id(1)
        m0 = lax.broadcasted_iota(jnp.int32, (1, BLOCK), 1) < (BLOCK // 2)

        if exchange:
            @pl.when((p == 0) & (i == 0))
            def _():
                for cp in exchange.copies(x_in, x_out, *x_sem):
                    cp.start()

        @pl.when(i == 0)
        def _():
            tri_l[...] = _tri2(WIDE, "ge")
            tri_s[...] = _tri2(BLOCK, "ge")

            def prep(b, carry):
                rows = pl.ds(pl.multiple_of(b * BLOCK, BLOCK), BLOCK)
                q = q_ref[rows, :] * 0.125
                v = v_ref[rows, :]
                qs[0, rows, :] = jnp.where(m0, q, 0.0).astype(BF16)
                qs[1, rows, :] = jnp.where(m0, 0.0, q).astype(BF16)
                kb[rows, :] = k_ref[rows, :].astype(BF16)
                vh[0, rows, :] = jnp.where(m0, v, 0.0).astype(BF16)
                vh[1, rows, :] = jnp.where(m0, 0.0, v).astype(BF16)
                return carry

            lax.fori_loop(0, nb, prep, 0)

        def tiles(q0, tq, specs):
            heads = [(t, h) for t in range(len(specs)) for h in range(2)]
            strips = [slice(s * SUB, (s + 1) * SUB) for s in range(tq // SUB)]
            for t, h in heads:
                k0, tk, _ = specs[t]
                z_s[t, h, 0:tq, 0:tk] = lax.dot_general(qs[h, pl.ds(q0, tq), :], kb[pl.ds(k0, tk), :], _NT,
                                                        preferred_element_type=F32)
            for t, h in heads:
                _, tk, kind = specs[t]
                for s, rows in enumerate(strips):
                    lnb = _log1m_sigmoid(z_s[t, h, rows, 0:tk])
                    m = _strip_mask(kind, s, tk)
                    if m is not None:
                        lnb = jnp.where(m, lnb, 0.0)
                    hi = lnb.astype(BF16)
                    hl_s[t, h, rows, 0:tk] = hi
                    hl_s[t, h, rows, tk:2 * tk] = (lnb - hi.astype(F32)).astype(BF16)
            for t, h in heads:
                _, tk, _ = specs[t]
                tri = tri_l if tk == WIDE else tri_s
                c_s[t, h, 0:tq, 0:tk] = jnp.dot(hl_s[t, h, 0:tq, 0:2 * tk], tri[...], preferred_element_type=F32)
            for t, h in heads:
                _, tk, kind = specs[t]
                for s, rows in enumerate(strips):
                    r = r_s[h, rows, :]
                    c = c_s[t, h, rows, 0:tk]
                    rr = r if tk == BLOCK else jnp.concatenate([r, r], axis=1)
                    w = jnp.exp(z_s[t, h, rows, 0:tk] + c + rr)
                    m = _strip_mask(kind, s, tk)
                    if m is not None:
                        w = jnp.where(m, w, 0.0)
                    w_s[t, h, rows, 0:tk] = w.astype(BF16)
                    r_s[h, rows, :] = r + jnp.broadcast_to(c[:, 0:1], (SUB, BLOCK))
            upd = None
            for t, h in heads:
                k0, tk, _ = specs[t]
                d = jnp.dot(w_s[t, h, 0:tq, 0:tk], vh[h, pl.ds(k0, tk), :], preferred_element_type=F32)
                upd = d if upd is None else upd + d
            acc_s[0:tq, :] += upd

        def finish(q0, tq):
            o_ref[pl.ds(q0, tq), :] = acc_s[0:tq, :]
            tot_ref[pl.ds(q0, tq), :] = jnp.where(m0, r_s[0, 0:tq, :], r_s[1, 0:tq, :])

        r_s[...] = jnp.zeros(r_s.shape, F32)
        acc_s[...] = jnp.zeros(acc_s.shape, F32)

        @pl.when(i == 0)
        def _():
            tiles(0, BLOCK, [(0, BLOCK, "first")])
            finish(0, BLOCK)

        @pl.when(i > 0)
        def _():
            q0 = pl.multiple_of(i * WIDE - BLOCK, BLOCK)
            key0 = lambda j: pl.multiple_of(j * WIDE - BLOCK, BLOCK)
            tiles(q0, WIDE, [(q0, WIDE, "diag")])

            def inner(n, carry):
                j = i - 1 - 2 * n
                tiles(q0, WIDE, [(key0(j), WIDE, "none"), (key0(j - 1), WIDE, "none")])
                return carry

            lax.fori_loop(0, (i - 1) // 2, inner, 0)

            @pl.when((i - 1) % 2 == 1)
            def _():
                tiles(q0, WIDE, [(key0(1), WIDE, "none"), (0, BLOCK, "pad")])

            @pl.when((i - 1) % 2 == 0)
            def _():
                tiles(q0, WIDE, [(0, BLOCK, "pad")])

            finish(q0, WIDE)

        if exchange:
            @pl.when((p == n_pair - 1) & (i == nq - 1))
            def _():
                for cp in exchange.copies(x_in, x_out, *x_sem):
                    cp.wait()

    cq, ck, cv = COL_Q // BLOCK, COL_K // BLOCK, COL_V // BLOCK
    col = lambda c0: (lambda p, i: (0, c0 + p))
    scratch = [pltpu.VMEM((2, lp, BLOCK), BF16), pltpu.VMEM((lp, BLOCK), BF16), pltpu.VMEM((2, lp, BLOCK), BF16),
               pltpu.VMEM((2 * WIDE, WIDE), BF16), pltpu.VMEM((2 * BLOCK, BLOCK), BF16),
               pltpu.VMEM((2, 2, WIDE, WIDE), F32), pltpu.VMEM((2, 2, WIDE, 2 * WIDE), BF16),
               pltpu.VMEM((2, 2, WIDE, WIDE), F32), pltpu.VMEM((2, 2, WIDE, WIDE), BF16),
               pltpu.VMEM((2, WIDE, BLOCK), F32), pltpu.VMEM((WIDE, BLOCK), F32)]
    res = _call(body, name="attn_fwd",
                out_shape=[_sds((lp, D_ATTN), F32), _sds((lp, D_ATTN), F32)] + (exchange.out_shape() if exchange else []),
                grid=(n_pair, nq),
                in_specs=[pl.BlockSpec((lp, BLOCK), col(cq)), pl.BlockSpec((lp, BLOCK), col(ck)),
                          pl.BlockSpec((lp, BLOCK), col(cv))] + [_ANY] * n_x,
                out_specs=[pl.BlockSpec((lp, BLOCK), col(0)), pl.BlockSpec((lp, BLOCK), col(0))] + [_ANY] * n_x,
                scratch=scratch + (exchange.scratch() if exchange else []),
                vmem_mb=56)(proj, proj, proj, *(exchange.arrs if exchange else []))
    return res[0], res[1], list(res[2:])


def _attn_bwd(proj, tot, d_out, exchange=None):
    lp = proj.shape[0]
    nb, nq = _attn_blocks(lp)
    n_pair = D_ATTN // BLOCK
    n_x = exchange.n if exchange else 0
    n_s = 24

    def body(*refs):
        q_ref, k_ref, v_ref, tot_ref, do_ref = refs[:5]
        x_in = refs[5:5 + n_x]
        dq_ref, dk_ref, dv_ref = refs[5 + n_x:8 + n_x]
        x_out = refs[8 + n_x:8 + 2 * n_x]
        (qs, kb, kh, vb, doh, tge_l, tge_s, tle_l, tle_s, z_s, g_s, lnb_s, hl_s, c_s, gl_s, gc_s, w_s, dz_s,
         tot_s, a_s, b_s, dq_acc, dk_acc, dv_acc) = refs[8 + 2 * n_x:8 + 2 * n_x + n_s]
        x_sem = refs[8 + 2 * n_x + n_s:]
        p, i = pl.program_id(0), pl.program_id(1)
        m0 = lax.broadcasted_iota(jnp.int32, (1, BLOCK), 1) < (BLOCK // 2)

        if exchange:
            @pl.when((p == 0) & (i == 0))
            def _():
                for cp in exchange.copies(x_in, x_out, *x_sem):
                    cp.start()

        @pl.when(i == 0)
        def _():
            tge_l[...] = _tri2(WIDE, "ge")
            tge_s[...] = _tri2(BLOCK, "ge")
            tle_l[...] = _tri2(WIDE, "le")[0:WIDE]
            tle_s[...] = _tri2(BLOCK, "le")[0:BLOCK]

            def prep(b, carry):
                rows = pl.ds(pl.multiple_of(b * BLOCK, BLOCK), BLOCK)
                q = q_ref[rows, :] * 0.125
                k = k_ref[rows, :]
                do = do_ref[rows, :]
                qs[0, rows, :] = jnp.where(m0, q, 0.0).astype(BF16)
                qs[1, rows, :] = jnp.where(m0, 0.0, q).astype(BF16)
                kb[rows, :] = k.astype(BF16)
                kh[0, rows, :] = jnp.where(m0, k, 0.0).astype(BF16)
                kh[1, rows, :] = jnp.where(m0, 0.0, k).astype(BF16)
                vb[rows, :] = v_ref[rows, :].astype(BF16)
                doh[0, rows, :] = jnp.where(m0, do, 0.0).astype(BF16)
                doh[1, rows, :] = jnp.where(m0, 0.0, do).astype(BF16)
                dk_acc[rows, :] = jnp.zeros((BLOCK, BLOCK), F32)
                dv_acc[rows, :] = jnp.zeros((BLOCK, BLOCK), F32)
                return carry

            lax.fori_loop(0, nb, prep, 0)

        def wide(x, tk):
            return x if tk == BLOCK else jnp.concatenate([x, x], axis=1)

        def tiles(q0, tq, specs):
            heads = [(t, h) for t in range(len(specs)) for h in range(2)]
            strips = [slice(s * SUB, (s + 1) * SUB) for s in range(tq // SUB)]
            for t, h in heads:
                k0, tk, _ = specs[t]
                z_s[t, h, 0:tq, 0:tk] = lax.dot_general(qs[h, pl.ds(q0, tq), :], kb[pl.ds(k0, tk), :], _NT,
                                                        preferred_element_type=F32)
                g_s[t, h, 0:tq, 0:tk] = lax.dot_general(doh[h, pl.ds(q0, tq), :], vb[pl.ds(k0, tk), :], _NT,
                                                        preferred_element_type=F32)
            for t, h in heads:
                _, tk, kind = specs[t]
                for s, rows in enumerate(strips):
                    lnb = _log1m_sigmoid(z_s[t, h, rows, 0:tk])
                    m = _strip_mask(kind, s, tk)
                    if m is not None:
                        lnb = jnp.where(m, lnb, 0.0)
                    lnb_s[t, h, rows, 0:tk] = lnb
                    hi = lnb.astype(BF16)
                    hl_s[t, h, rows, 0:tk] = hi
                    hl_s[t, h, rows, tk:2 * tk] = (lnb - hi.astype(F32)).astype(BF16)
            for t, h in heads:
                _, tk, _ = specs[t]
                tri = tge_l if tk == WIDE else tge_s
                c_s[t, h, 0:tq, 0:tk] = jnp.dot(hl_s[t, h, 0:tq, 0:2 * tk], tri[...], preferred_element_type=F32)
            for t, h in heads:
                _, tk, kind = specs[t]
                for s, rows in enumerate(strips):
                    c = c_s[t, h, rows, 0:tk]
                    a_next = a_s[h, rows, :] + jnp.broadcast_to(c[:, 0:1], (SUB, BLOCK))
                    a_s[h, rows, :] = a_next
                    w = jnp.exp(z_s[t, h, rows, 0:tk] + c + wide(tot_s[h, rows, :] - a_next, tk))
                    m = _strip_mask(kind, s, tk)
                    if m is not None:
                        w = jnp.where(m, w, 0.0)
                    g = w * g_s[t, h, rows, 0:tk]
                    g_s[t, h, rows, 0:tk] = g
                    gl_s[t, h, rows, 0:tk] = g.astype(BF16)
                    w_s[t, h, rows, 0:tk] = w.astype(BF16)
            for t, h in heads:
                _, tk, _ = specs[t]
                tri = tle_l if tk == WIDE else tle_s
                gc_s[t, h, 0:tq, 0:tk] = jnp.dot(gl_s[t, h, 0:tq, 0:tk], tri[...], preferred_element_type=F32)
            for t, h in heads:
                _, tk, kind = specs[t]
                for s, rows in enumerate(strips):
                    gc = gc_s[t, h, rows, 0:tk]
                    b = b_s[h, rows, :]
                    sig = jnp.exp(z_s[t, h, rows, 0:tk] + lnb_s[t, h, rows, 0:tk])
                    dz = g_s[t, h, rows, 0:tk] - sig * (gc + wide(b, tk))
                    m = _strip_mask(kind, s, tk)
                    if m is not None:
                        dz = jnp.where(m, dz, 0.0)
                    dz_s[t, h, rows, 0:tk] = dz.astype(BF16)
                    b_s[h, rows, :] = b + jnp.broadcast_to(gc[:, tk - 1:tk], (SUB, BLOCK))
            upd = None
            for t, h in heads:
                k0, tk, _ = specs[t]
                d = jnp.dot(dz_s[t, h, 0:tq, 0:tk], kh[h, pl.ds(k0, tk), :], preferred_element_type=F32)
                upd = d if upd is None else upd + d
            dq_acc[0:tq, :] += upd
            for t, (k0, tk, _) in enumerate(specs):
                dk_acc[pl.ds(k0, tk), :] += (
                    lax.dot_general(dz_s[t, 0, 0:tq, 0:tk], qs[0, pl.ds(q0, tq), :], _TN, preferred_element_type=F32) +
                    lax.dot_general(dz_s[t, 1, 0:tq, 0:tk], qs[1, pl.ds(q0, tq), :], _TN, preferred_element_type=F32))
                dv_acc[pl.ds(k0, tk), :] += (
                    lax.dot_general(w_s[t, 0, 0:tq, 0:tk], doh[0, pl.ds(q0, tq), :], _TN, preferred_element_type=F32) +
                    lax.dot_general(w_s[t, 1, 0:tq, 0:tk], doh[1, pl.ds(q0, tq), :], _TN, preferred_element_type=F32))

        def start(q0, tq):
            tv = tot_ref[pl.ds(q0, tq), :]
            tot_s[0, 0:tq, :] = jnp.broadcast_to(tv[:, 0:1], (tq, BLOCK))
            tot_s[1, 0:tq, :] = jnp.broadcast_to(tv[:, BLOCK - 1:BLOCK], (tq, BLOCK))
            a_s[...] = jnp.zeros(a_s.shape, F32)
            b_s[...] = jnp.zeros(b_s.shape, F32)
            dq_acc[...] = jnp.zeros(dq_acc.shape, F32)

        def finish(q0, tq):
            dq_ref[pl.ds(q0, tq), :] = (dq_acc[0:tq, :] * 0.125).astype(dq_ref.dtype)

        @pl.when(i == 0)
        def _():
            start(0, BLOCK)
            tiles(0, BLOCK, [(0, BLOCK, "first")])
            finish(0, BLOCK)

        @pl.when(i > 0)
        def _():
            q0 = pl.multiple_of(i * WIDE - BLOCK, BLOCK)
            key0 = lambda j: pl.multiple_of(j * WIDE - BLOCK, BLOCK)
            odd = (i - 1) % 2
            start(q0, WIDE)

            @pl.when(odd == 1)
            def _():
                tiles(q0, WIDE, [(0, BLOCK, "pad"), (key0(1), WIDE, "none")])

            @pl.when(odd == 0)
            def _():
                tiles(q0, WIDE, [(0, BLOCK, "pad")])

            def inner(n, carry):
                j = 1 + odd + 2 * n
                tiles(q0, WIDE, [(key0(j), WIDE, "none"), (key0(j + 1), WIDE, "none")])
                return carry

            lax.fori_loop(0, (i - 1) // 2, inner, 0)
            tiles(q0, WIDE, [(q0, WIDE, "diag")])
            finish(q0, WIDE)

        @pl.when(i == nq - 1)
        def _():
            dk_ref[...] = dk_acc[...].astype(dk_ref.dtype)
            dv_ref[...] = dv_acc[...].astype(dv_ref.dtype)

        if exchange:
            @pl.when((p == n_pair - 1) & (i == nq - 1))
            def _():
                for cp in exchange.copies(x_in, x_out, *x_sem):
                    cp.wait()

    cq, ck, cv = COL_Q // BLOCK, COL_K // BLOCK, COL_V // BLOCK
    col = lambda c0: (lambda p, i: (0, c0 + p))
    whole = lambda c0: pl.BlockSpec((lp, BLOCK), col(c0))
    tile4 = lambda w, dt: pltpu.VMEM((2, 2, WIDE, w), dt)
    scratch = [pltpu.VMEM((2, lp, BLOCK), BF16), pltpu.VMEM((lp, BLOCK), BF16), pltpu.VMEM((2, lp, BLOCK), BF16),
               pltpu.VMEM((lp, BLOCK), BF16), pltpu.VMEM((2, lp, BLOCK), BF16),
               pltpu.VMEM((2 * WIDE, WIDE), BF16), pltpu.VMEM((2 * BLOCK, BLOCK), BF16),
               pltpu.VMEM((WIDE, WIDE), BF16), pltpu.VMEM((BLOCK, BLOCK), BF16),
               tile4(WIDE, F32), tile4(WIDE, F32), tile4(WIDE, F32), tile4(2 * WIDE, BF16), tile4(WIDE, F32),
               tile4(WIDE, BF16), tile4(WIDE, F32), tile4(WIDE, BF16), tile4(WIDE, BF16),
               pltpu.VMEM((2, WIDE, BLOCK), F32), pltpu.VMEM((2, WIDE, BLOCK), F32), pltpu.VMEM((2, WIDE, BLOCK), F32),
               pltpu.VMEM((WIDE, BLOCK), F32), pltpu.VMEM((lp, BLOCK), F32), pltpu.VMEM((lp, BLOCK), F32)]
    res = _call(body, name="attn_bwd",
                out_shape=[_sds((lp, D_ATTN), BF16)] * 3 + (exchange.out_shape() if exchange else []),
                grid=(n_pair, nq),
                in_specs=[whole(cq), whole(ck), whole(cv), whole(0), whole(0)] + [_ANY] * n_x,
                out_specs=[whole(0), whole(0), whole(0)] + [_ANY] * n_x,
                scratch=scratch + (exchange.scratch() if exchange else []),
                vmem_mb=60)(proj, proj, proj, tot, d_out, *(exchange.arrs if exchange else []))
    return res[0], res[1], res[2], list(res[3:])


def _shift_down(x, d):
    return x if d == 0 else pltpu.roll(x, d, 0)


def _shift_up(x, d):
    return x if d == 0 else pltpu.roll(x, x.shape[0] - d, 0)


def _pool_windows(ext, down):
    shift = _shift_down if down else _shift_up
    outs = []
    for g, w in enumerate(POOL_WINDOWS):
        s = ext[:, g * BLOCK:(g + 1) * BLOCK]
        d = 1
        while d < w:
            s = s + shift(s, d)
            d *= 2
        outs.append(s)
    return jnp.concatenate(outs, axis=1)


def _pool_counts(pos):
    cols = [jnp.broadcast_to(jnp.clip(pos + 1, 1, w).astype(F32), (pos.shape[0], BLOCK)) for w in POOL_WINDOWS]
    return jnp.concatenate(cols, axis=1)


def _group_dot(x, w_ref, transpose):
    outs = []
    for g in range(len(POOL_WINDOWS)):
        xg = x[:, g * BLOCK:(g + 1) * BLOCK].astype(BF16)
        wg = w_ref[g * BLOCK:(g + 1) * BLOCK, :]
        if transpose:
            outs.append(lax.dot_general(xg, wg, _NT, preferred_element_type=F32))
        else:
            outs.append(jnp.dot(xg, wg, preferred_element_type=F32))
    return jnp.concatenate(outs, axis=1)


def _pooled(prev, cur, r):
    rows = cur.shape[0]
    ext = jnp.concatenate([prev[rows - HALO:], cur], axis=0)
    pos = r * rows + lax.broadcasted_iota(jnp.int32, (rows, 1), 0) - PAD
    ws = _pool_windows(ext, down=True)[HALO:]
    return jnp.where(pos >= 0, ws / _pool_counts(pos) - cur, 0.0)


def _pool_fwd(proj, w_pool_bf, scale):
    lp = proj.shape[0]
    rows = BLOCK
    cb = COL_POOL // D_POOL

    def body(prev_ref, cur_ref, w_ref, s_ref, o_ref):
        pooled = _pooled(prev_ref[...], cur_ref[...], pl.program_id(0))
        o_ref[...] = _group_dot(pooled, w_ref, False) * s_ref[...]

    return _call(body, name="pool_fwd", out_shape=_sds((lp, D_POOL), F32), grid=(lp // rows,),
                 in_specs=[pl.BlockSpec((rows, D_POOL), lambda r: (jnp.maximum(r - 1, 0), cb)),
                           pl.BlockSpec((rows, D_POOL), lambda r: (r, cb)),
                           pl.BlockSpec((D_POOL, BLOCK), lambda r: (0, 0)),
                           pl.BlockSpec((1, D_POOL), lambda r: (0, 0))],
                 out_specs=pl.BlockSpec((rows, D_POOL), lambda r: (r, 0)))(proj, proj, w_pool_bf, scale)


def _pool_bwd(proj, d_out, w_pool_bf, scale):
    lp = proj.shape[0]
    rows = BLOCK
    n_chunk = lp // rows
    cb = COL_POOL // D_POOL

    def body(prev_ref, cur_ref, do_ref, don_ref, w_ref, s_ref, du_ref, dw_ref, ds_ref):
        r = pl.program_id(0)

        @pl.when(r == 0)
        def _():
            dw_ref[...] = jnp.zeros(dw_ref.shape, F32)
            ds_ref[...] = jnp.zeros(ds_ref.shape, F32)

        pooled = _pooled(prev_ref[...], cur_ref[...], r)
        d_ext = jnp.concatenate([do_ref[...], don_ref[0:HALO]], axis=0)
        pos = r * rows + lax.broadcasted_iota(jnp.int32, (rows + HALO, 1), 0) - PAD
        dmixed = jnp.where((pos >= 0) & (pos < lp - PAD), d_ext * s_ref[...], 0.0)
        dpooled = _group_dot(dmixed, w_ref, True)
        back = _pool_windows(dpooled / _pool_counts(pos), down=False)[0:rows]
        du = jnp.where(pos[0:rows] >= 0, back - dpooled[0:rows], 0.0)
        du_ref[...] = du.astype(du_ref.dtype)
        mixed = _group_dot(pooled, w_ref, False)
        ds_ref[...] += _colsum(do_ref[...] * mixed)
        pooled_bf = pooled.astype(BF16)
        dm_bf = dmixed[0:rows].astype(BF16)
        for g in range(len(POOL_WINDOWS)):
            sl = slice(g * BLOCK, (g + 1) * BLOCK)
            dw_ref[sl, :] += lax.dot_general(pooled_bf[:, sl], dm_bf[:, sl], _TN, preferred_element_type=F32)

    return _call(body, name="pool_bwd",
                 out_shape=[_sds((lp, D_POOL), BF16), _sds((D_POOL, BLOCK), F32), _sds((1, D_POOL), F32)],
                 grid=(n_chunk,),
                 in_specs=[pl.BlockSpec((rows, D_POOL), lambda r: (jnp.maximum(r - 1, 0), cb)),
                           pl.BlockSpec((rows, D_POOL), lambda r: (r, cb)),
                           pl.BlockSpec((rows, D_POOL), lambda r: (r, 0)),
                           pl.BlockSpec((rows, D_POOL), lambda r: (jnp.minimum(r + 1, n_chunk - 1), 0)),
                           pl.BlockSpec((D_POOL, BLOCK), lambda r: (0, 0)),
                           pl.BlockSpec((1, D_POOL), lambda r: (0, 0))],
                 out_specs=[pl.BlockSpec((rows, D_POOL), lambda r: (r, 0)),
                            pl.BlockSpec((D_POOL, BLOCK), lambda r: (0, 0)),
                            pl.BlockSpec((1, D_POOL), lambda r: (0, 0))])(proj, proj, d_out, d_out, w_pool_bf, scale)


def _conv_taps(u, wdw_ref):
    y = wdw_ref[CONV_WIDTH - 1:CONV_WIDTH, :] * u
    for d in range(1, CONV_WIDTH):
        y = y + wdw_ref[CONV_WIDTH - 1 - d:CONV_WIDTH - d, :] * _shift_down(u, d)
    return y


def _layernorm_stats(y):
    mu = jnp.mean(y, axis=-1, keepdims=True)
    yc = y - mu
    rstd = lax.rsqrt(jnp.mean(yc * yc, axis=-1, keepdims=True) + EPS)
    return yc * rstd, rstd


def _conv_fwd(proj, wdw, bdw, ln_g, ln_b, wpw_bf):
    lp = proj.shape[0]
    rows = BLOCK
    ca, cg = COL_A // D_CONV, COL_G // D_CONV

    def body(ap_ref, a_ref, gp_ref, g_ref, wdw_ref, b_ref, lg_ref, lb_ref, wpw_ref, o_ref):
        r = pl.program_id(0)
        a = jnp.concatenate([ap_ref[rows - HALO:rows], a_ref[...]], axis=0)
        g = jnp.concatenate([gp_ref[rows - HALO:rows], g_ref[...]], axis=0)
        u = a * _sigmoid(g)
        y = _conv_taps(u, wdw_ref)[HALO:] + b_ref[...]
        xhat, _ = _layernorm_stats(y)
        yn = xhat * lg_ref[...] + lb_ref[...]
        pos = r * rows + lax.broadcasted_iota(jnp.int32, (rows, 1), 0) - PAD
        s = jnp.where(pos >= 0, yn * _sigmoid(yn), 0.0)
        o_ref[...] = jnp.dot(s.astype(BF16), wpw_ref[...], preferred_element_type=F32)

    prev = lambda c: (lambda r: (jnp.maximum(r - 1, 0), c))
    cur = lambda c: (lambda r: (r, c))
    const = lambda shape: pl.BlockSpec(shape, lambda r: (0, 0))
    return _call(body, name="conv_fwd", out_shape=_sds((lp, D_CONV), F32), grid=(lp // rows,),
                 in_specs=[pl.BlockSpec((rows, D_CONV), prev(ca)), pl.BlockSpec((rows, D_CONV), cur(ca)),
                           pl.BlockSpec((rows, D_CONV), prev(cg)), pl.BlockSpec((rows, D_CONV), cur(cg)),
                           const((HALO, D_CONV)), const((1, D_CONV)), const((1, D_CONV)), const((1, D_CONV)),
                           const((D_CONV, D_CONV))],
                 out_specs=pl.BlockSpec((rows, D_CONV), cur(0)))(proj, proj, proj, proj, wdw, bdw, ln_g, ln_b, wpw_bf)


def _conv_bwd(proj, d_out, wdw, bdw, ln_g, ln_b, wpw_bf):
    lp = proj.shape[0]
    rows = BLOCK
    n_chunk = lp // rows
    ca, cg = COL_A // D_CONV, COL_G // D_CONV
    ext = rows + HALO

    def body(ap_ref, a_ref, an_ref, gp_ref, g_ref, gn_ref, do_ref, don_ref, wdw_ref, b_ref, lg_ref, lb_ref,
             wpw_ref, da_ref, dg_ref, dwdw_ref, db_ref, dlg_ref, dlb_ref, dwpw_ref):
        r = pl.program_id(0)

        @pl.when(r == 0)
        def _():
            for ref in (dwdw_ref, db_ref, dlg_ref, dlb_ref, dwpw_ref):
                ref[...] = jnp.zeros(ref.shape, F32)

        a3 = jnp.concatenate([ap_ref[rows - HALO:rows], a_ref[...], an_ref[0:HALO]], axis=0)
        g3 = jnp.concatenate([gp_ref[rows - HALO:rows], g_ref[...], gn_ref[0:HALO]], axis=0)
        sig3 = _sigmoid(g3)
        u3 = a3 * sig3
        y = _conv_taps(u3, wdw_ref)[HALO:] + b_ref[...]
        xhat, rstd = _layernorm_stats(y)
        yn = xhat * lg_ref[...] + lb_ref[...]
        sgm = _sigmoid(yn)
        pos = r * rows + lax.broadcasted_iota(jnp.int32, (ext, 1), 0) - PAD
        valid = (pos >= 0) & (pos < lp - PAD)
        d_ext = jnp.concatenate([do_ref[...], don_ref[0:HALO]], axis=0)
        ds = lax.dot_general(d_ext.astype(BF16), wpw_ref[...], _NT, preferred_element_type=F32)
        dyn = jnp.where(valid, ds * (sgm * (1.0 + yn * (1.0 - sgm))), 0.0)
        dxh = dyn * lg_ref[...]
        dy = rstd * (dxh - jnp.mean(dxh, axis=-1, keepdims=True)
                     - xhat * jnp.mean(dxh * xhat, axis=-1, keepdims=True))
        s_cur = jnp.where(valid[0:rows], (yn * sgm)[0:rows], 0.0)
        dwpw_ref[...] += lax.dot_general(s_cur.astype(BF16), do_ref[...].astype(BF16), _TN,
                                         preferred_element_type=F32)
        dlg_ref[...] += _colsum(dyn[0:rows] * xhat[0:rows])
        dlb_ref[...] += _colsum(dyn[0:rows])
        dy_cur = dy[0:rows]
        db_ref[...] += _colsum(dy_cur)
        du = wdw_ref[CONV_WIDTH - 1:CONV_WIDTH, :] * dy
        for d in range(CONV_WIDTH):
            k = CONV_WIDTH - 1 - d
            dwdw_ref[k:k + 1, :] += _colsum(dy_cur * _shift_down(u3, d)[HALO:HALO + rows])
            if d:
                du = du + wdw_ref[k:k + 1, :] * _shift_up(dy, d)
        du = jnp.where(pos[0:rows] >= 0, du[0:rows], 0.0)
        sig = sig3[HALO:HALO + rows]
        da_ref[...] = (du * sig).astype(da_ref.dtype)
        dg_ref[...] = (du * a_ref[...] * sig * (1.0 - sig)).astype(dg_ref.dtype)

    prev = lambda c: (lambda r: (jnp.maximum(r - 1, 0), c))
    cur = lambda c: (lambda r: (r, c))
    nxt = lambda c: (lambda r: (jnp.minimum(r + 1, n_chunk - 1), c))
    const = lambda shape: pl.BlockSpec(shape, lambda r: (0, 0))
    blk = lambda f: pl.BlockSpec((rows, D_CONV), f)
    return _call(body, name="conv_bwd",
                 out_shape=[_sds((lp, D_CONV), BF16), _sds((lp, D_CONV), BF16), _sds((HALO, D_CONV), F32),
                            _sds((1, D_CONV), F32), _sds((1, D_CONV), F32), _sds((1, D_CONV), F32),
                            _sds((D_CONV, D_CONV), F32)],
                 grid=(n_chunk,),
                 in_specs=[blk(prev(ca)), blk(cur(ca)), blk(nxt(ca)), blk(prev(cg)), blk(cur(cg)), blk(nxt(cg)),
                           blk(cur(0)), blk(nxt(0)),
                           const((HALO, D_CONV)), const((1, D_CONV)), const((1, D_CONV)), const((1, D_CONV)),
                           const((D_CONV, D_CONV))],
                 out_specs=[blk(cur(0)), blk(cur(0)), const((HALO, D_CONV)), const((1, D_CONV)),
                            const((1, D_CONV)), const((1, D_CONV)), const((D_CONV, D_CONV))],
                 vmem_mb=48)(proj, proj, proj, proj, proj, proj, d_out, d_out, wdw, bdw, ln_g, ln_b, wpw_bf)


_ANY = pl.BlockSpec(memory_space=pl.ANY)


def _place():
    return lax.axis_index("x"), lax.axis_index("y"), lax.axis_index("c")


def _xy_peers(x, y):
    return [(1 - x, y), (x, 1 - y), (1 - x, 1 - y)]


def _gather_xy(name, arrs):
    n = len(arrs)

    def body(*refs):
        ins, outs = refs[:n], refs[n:2 * n]
        send, recv, local = refs[2 * n:]
        x, y, c = _place()
        me = 2 * x + y
        copies = []
        for a in range(n):
            cp = pltpu.make_async_copy(ins[a], outs[a].at[me], local.at[a])
            cp.start()
            copies.append(cp)
        remote = []
        for a in range(n):
            for r, (px, py) in enumerate(_xy_peers(x, y)):
                cp = pltpu.make_async_remote_copy(src_ref=ins[a], dst_ref=outs[a].at[me],
                                                  send_sem=send.at[3 * a + r], recv_sem=recv.at[3 * a + r],
                                                  device_id=(px, py, c), device_id_type=MESH)
                cp.start()
                remote.append(cp)
        for cp in remote:
            cp.wait()
        for cp in copies:
            cp.wait()

    return _call(body, name=name, out_shape=[_sds((N_SHARD,) + a.shape, a.dtype) for a in arrs],
                 in_specs=[_ANY] * n, out_specs=[_ANY] * n,
                 scratch=[pltpu.SemaphoreType.DMA((3 * n,)), pltpu.SemaphoreType.DMA((3 * n,)),
                          pltpu.SemaphoreType.DMA((n,))])(*arrs)


def _scatter_xy(name, parts):
    n = len(parts)

    def body(*refs):
        ins, outs = refs[:n], refs[n:2 * n]
        send, recv, local = refs[2 * n:]
        x, y, c = _place()
        me = 2 * x + y
        copies = []
        for a in range(n):
            cp = pltpu.make_async_copy(ins[a].at[me], outs[a].at[me], local.at[a])
            cp.start()
            copies.append(cp)
        remote = []
        for a in range(n):
            for r, (px, py) in enumerate(_xy_peers(x, y)):
                cp = pltpu.make_async_remote_copy(src_ref=ins[a].at[2 * px + py], dst_ref=outs[a].at[me],
                                                  send_sem=send.at[3 * a + r], recv_sem=recv.at[3 * a + r],
                                                  device_id=(px, py, c), device_id_type=MESH)
                cp.start()
                remote.append(cp)
        for cp in remote:
            cp.wait()
        for cp in copies:
            cp.wait()

    return _call(body, name=name, out_shape=[_sds(a.shape, a.dtype) for a in parts],
                 in_specs=[_ANY] * n, out_specs=[_ANY] * n,
                 scratch=[pltpu.SemaphoreType.DMA((3 * n,)), pltpu.SemaphoreType.DMA((3 * n,)),
                          pltpu.SemaphoreType.DMA((n,))])(*parts)


def _swap_core(name, arrs):
    n = len(arrs)

    def body(*refs):
        ins, outs = refs[:n], refs[n:2 * n]
        send, recv = refs[2 * n:]
        x, y, c = _place()
        remote = []
        for a in range(n):
            cp = pltpu.make_async_remote_copy(src_ref=ins[a], dst_ref=outs[a], send_sem=send.at[a],
                                              recv_sem=recv.at[a], device_id=(x, y, 1 - c), device_id_type=MESH)
            cp.start()
            remote.append(cp)
        for cp in remote:
            cp.wait()

    return _call(body, name=name, out_shape=[_sds(a.shape, a.dtype) for a in arrs],
                 in_specs=[_ANY] * n, out_specs=[_ANY] * n,
                 scratch=[pltpu.SemaphoreType.DMA((n,)), pltpu.SemaphoreType.DMA((n,))])(*arrs)


def _gather_all(name, arr):
    flips = [(fx, fy, fc) for fx in (0, 1) for fy in (0, 1) for fc in (0, 1)][1:]

    def body(in_ref, out_ref, send, recv, local):
        x, y, c = _place()
        me = 4 * x + 2 * y + c
        own = pltpu.make_async_copy(in_ref, out_ref.at[me], local)
        own.start()
        remote = []
        for k, (fx, fy, fc) in enumerate(flips):
            peer = (1 - x if fx else x, 1 - y if fy else y, 1 - c if fc else c)
            cp = pltpu.make_async_remote_copy(src_ref=in_ref, dst_ref=out_ref.at[me], send_sem=send.at[k],
                                              recv_sem=recv.at[k], device_id=peer, device_id_type=MESH)
            cp.start()
            remote.append(cp)
        for cp in remote:
            cp.wait()
        own.wait()

    return _call(body, name=name, out_shape=_sds((N_DEV,) + arr.shape, arr.dtype), in_specs=[_ANY], out_specs=_ANY,
                 scratch=[pltpu.SemaphoreType.DMA((N_DEV - 1,)), pltpu.SemaphoreType.DMA((N_DEV - 1,)),
                          pltpu.SemaphoreType.DMA])(arr)


def _sum_slots(name, stacked, out_dtype=F32):
    s, r, c = stacked.shape
    tr = _pick(r, (256, 128, 64, 8))

    def body(in_ref, o_ref):
        acc = in_ref[0].astype(F32)
        for k in range(1, s):
            acc = acc + in_ref[k].astype(F32)
        o_ref[...] = acc.astype(o_ref.dtype)

    return _call(body, name=name, out_shape=_sds((r, c), out_dtype), grid=(r // tr,),
                 in_specs=[pl.BlockSpec((s, tr, c), lambda i: (0, i, 0))],
                 out_specs=pl.BlockSpec((tr, c), lambda i: (i, 0)))(stacked)


def _sum_slots_layers(name, r0, r1):
    s, r, c = r0.shape
    tr = _pick(r, (256, 128))

    def body(a_ref, b_ref, o_ref):
        def total(ref):
            acc = ref[0].astype(F32)
            for k in range(1, s):
                acc = acc + ref[k].astype(F32)
            return acc

        @pl.when(pl.program_id(0) == 0)
        def _():
            o_ref[...] = total(a_ref)

        @pl.when(pl.program_id(0) == 1)
        def _():
            o_ref[...] = total(b_ref)

    return _call(body, name=name, out_shape=_sds((2, r, c), F32), grid=(2, r // tr),
                 in_specs=[pl.BlockSpec((s, tr, c), lambda l, i: (0, i * (1 - l), 0)),
                           pl.BlockSpec((s, tr, c), lambda l, i: (0, i * l, 0))],
                 out_specs=pl.BlockSpec((None, tr, c), lambda l, i: (l, i, 0)))(r0, r1)


def _adamw_math(w, g, m, v):
    m = ADAM_B1 * m + (1.0 - ADAM_B1) * g
    v = ADAM_B2 * v + (1.0 - ADAM_B2) * (g * g)
    m_hat = m / (1.0 - ADAM_B1 ** ADAM_STEP)
    v_hat = v / (1.0 - ADAM_B2 ** ADAM_STEP)
    delta = -ADAM_LR * (m_hat / (jnp.sqrt(v_hat) + ADAM_EPS) + ADAM_WD * w)
    return delta, m, v


def _adamw(name, w, m, v, g_mine, g_other):
    l, r, c = w.shape
    tr = _pick(r, (128, 64, 8))

    def body(w_ref, m_ref, v_ref, ga_ref, gb_ref, g_ref, d_ref, nm_ref, nv_ref):
        g = ga_ref[...] + gb_ref[...]
        delta, nm, nv = _adamw_math(w_ref[...], g, m_ref[...], v_ref[...])
        g_ref[...] = g
        d_ref[...] = delta
        nm_ref[...] = nm
        nv_ref[...] = nv

    spec = pl.BlockSpec((None, tr, c), lambda li, i: (li, i, 0))
    return _call(body, name=name, out_shape=[_sds(w.shape, F32)] * 4, grid=(l, r // tr),
                 in_specs=[spec] * 5, out_specs=[spec] * 4, vmem_mb=48)(w, m, v, g_mine, g_other)


def _adamw_flat(name, w, m, v, g):
    r, c = w.shape
    tr = _pick(r, (256, 128, 64, 8))

    def body(w_ref, m_ref, v_ref, g_ref, d_ref, nm_ref, nv_ref):
        delta, nm, nv = _adamw_math(w_ref[...], g_ref[...], m_ref[...], v_ref[...])
        d_ref[...] = delta
        nm_ref[...] = nm
        nv_ref[...] = nv

    spec = pl.BlockSpec((tr, c), lambda i: (i, 0))
    return _call(body, name=name, out_shape=[_sds(w.shape, F32)] * 3, grid=(r // tr,),
                 in_specs=[spec] * 4, out_specs=[spec] * 3)(w, m, v, g)


def _pack(arrs, row_multiple=256):
    flat = jnp.concatenate([a.reshape(-1).astype(F32) for a in arrs])
    per = BLOCK * row_multiple
    total = -(-flat.shape[0] // per) * per
    return jnp.pad(flat, (0, total - flat.shape[0])).reshape(total // BLOCK, BLOCK)


def _unpack(buf, shapes):
    flat = buf.reshape(-1)
    outs, off = [], 0
    for s in shapes:
        size = 1
        for d in s:
            size *= d
        outs.append(flat[off:off + size].reshape(s))
        off += size
    return outs


def kernel(x, meta_tokens, pre_mix_g, w_in, w_pool, pool_scale, w_dw, b_dw, conv_ln_g, conv_ln_b, w_pw, mix_out_g, w_out, post_mix_g, pre_ffn_g, w_gate, w_up, w_down, post_ffn_g, loss_target, m_meta_tokens, m_pre_mix_g, m_w_in, m_w_pool, m_pool_scale, m_w_dw, m_b_dw, m_conv_ln_g, m_conv_ln_b, m_w_pw, m_mix_out_g, m_w_out, m_post_mix_g, m_pre_ffn_g, m_w_gate, m_w_up, m_w_down, m_post_ffn_g, v_meta_tokens, v_pre_mix_g, v_w_in, v_w_pool, v_pool_scale, v_w_dw, v_b_dw, v_conv_ln_g, v_conv_ln_b, v_w_pw, v_mix_out_g, v_w_out, v_post_mix_g, v_pre_ffn_g, v_w_gate, v_w_up, v_w_down, v_post_ffn_g):
    seq = x.shape[1]
    lp = PAD + N_META + seq
    depth = w_in.shape[0]
    xy = 2 * lax.axis_index("x") + lax.axis_index("y")

    small_shapes = [meta_tokens.shape, w_dw.shape, w_pw.shape]
    small_local = _pack([meta_tokens, w_dw, w_pw], row_multiple=8)
    big_names = ["w_in", "w_out", "w_gate", "w_up", "w_down"]
    big_local = {(k, l): w[l:l + 1].astype(BF16)
                 for k, w in zip(big_names, (w_in, w_out, w_gate, w_up, w_down)) for l in range(depth)}
    wg = {}
    wg[("w_in", 0)], small_all = _gather_xy("gather_first", [big_local[("w_in", 0)], small_local])
    metas, wdws, wpws = [], [], []
    for s in range(N_SHARD):
        mt, wd, wp = _unpack(small_all[s], small_shapes)
        metas.append(mt)
        wdws.append(wd)
        wpws.append(wp)
    meta_full = jnp.concatenate(metas, axis=1)
    wdw_full = jnp.concatenate(wdws, axis=2)
    wpw_full = jnp.concatenate(wpws, axis=1)
    wdw_pad = jnp.pad(wdw_full, ((0, 0), (0, HALO - CONV_WIDTH), (0, 0)))
    wpw_bf = wpw_full.astype(BF16)
    wpool_bf = w_pool.reshape(depth, D_POOL, BLOCK).astype(BF16)

    row = lambda a, i: a[i][None, :]

    h = jnp.concatenate([jnp.zeros((PAD, D_MODEL), F32), meta_full, x[0]], axis=0)
    target = jnp.pad(loss_target[0], ((PAD + N_META, 0), (0, 0)))
    u = _rowwise("pre_mix_norm0", lambda hh, g: _rms(hh, g), [h], [row(pre_mix_g, 0)], [(D_MODEL, BF16)])[0]
    saved = []
    for i in range(depth):
        proj = _mm_nn_col("in_proj%d" % i, u, wg[("w_in", i)], 0, F32)
        ride = [(k, i) for k in big_names[1:]] + ([("w_in", i + 1)] if i + 1 < depth else [])
        o_attn, tot, got = _attn_fwd(proj, _Exchange("gather", [big_local[k] for k in ride]))
        wg.update(zip(ride, got))
        o_pool = _pool_fwd(proj, wpool_bf[i], row(pool_scale, i))
        o_conv = _conv_fwd(proj, wdw_pad[i], row(b_dw, i), row(conv_ln_g, i), row(conv_ln_b, i), wpw_bf[i])

        mix_gains = [row(mix_out_g, i)[:, :D_ATTN], row(mix_out_g, i)[:, D_ATTN:D_ATTN + D_POOL],
                     row(mix_out_g, i)[:, D_ATTN + D_POOL:]]
        merged = _rowwise("merge%d" % i, _merge, [o_attn, o_pool, o_conv], mix_gains, [(D_MODEL, BF16)])[0]
        mix = _mm_nn_row("out_proj%d" % i, merged, wg[("w_out", i)], 0)

        def post_mix(hh, mx, g1, g2):
            h1 = hh + _rms(mx, g1)
            return h1, _rms(h1, g2)

        h1, u2 = _rowwise("post_mix%d" % i, post_mix, [h, mix], [row(post_mix_g, i), row(pre_ffn_g, i)],
                          [(D_MODEL, F32), (D_MODEL, BF16)])
        gate = _mm_nn_col("ffn_gate%d" % i, u2, wg[("w_gate", i)], 0, F32)
        up = _mm_nn_col("ffn_up%d" % i, u2, wg[("w_up", i)], 0, F32)
        act = _rowwise("swiglu%d" % i, lambda gt, p: gt * _sigmoid(gt) * p, [gate, up], [], [(D_FF, BF16)],
                       vmem_mb=48)[0]
        ff = _mm_nn_row("ffn_down%d" % i, act, wg[("w_down", i)], 0)
        rec = dict(h=h, u=u, proj=proj, tot=tot, o_attn=o_attn, o_pool=o_pool, o_conv=o_conv, merged=merged,
                   mix=mix, h1=h1, u2=u2, gate=gate, up=up, act=act, ff=ff)
        saved.append(rec)
        if i + 1 < depth:
            def post_ffn(hh, f, g1, g2):
                h2 = hh + _rms(f, g1)
                return h2, _rms(h2, g2)

            h, u = _rowwise("post_ffn%d" % i, post_ffn, [h1, ff], [row(post_ffn_g, i), row(pre_mix_g, i + 1)],
                            [(D_MODEL, F32), (D_MODEL, BF16)])
        else:
            def head(row0, hh, f, tgt, g1):
                y = hh + _rms(f, g1)
                rid = row0 + lax.broadcasted_iota(jnp.int32, (y.shape[0], 1), 0)
                err = jnp.where(rid >= PAD + N_META, y - tgt, 0.0)
                part = 0.5 * jnp.sum(jnp.mean(err * err, axis=-1, keepdims=True), axis=0, keepdims=True)
                return err * (1.0 / D_MODEL), jnp.broadcast_to(part, (8, BLOCK))

            dh, loss_part = _rowwise("loss_head", head, [h1, ff, target], [row(post_ffn_g, i)],
                                     [(D_MODEL, F32)], accs=[(8, BLOCK)], with_row0=True)

    loss = lax.psum(loss_part[0, 0], ("x", "y", "c"))

    small_grads = {}
    big_parts = {}
    received = {}
    for i in reversed(range(depth)):
        rec = saved[i]

        def post_ffn_b(f, d, g):
            _, vjp = jax.vjp(_rms, f, g)
            df, dg = vjp(d)
            return df, dg

        dff, g_post_ffn = _rowwise("post_ffn_b%d" % i, post_ffn_b, [rec["ff"], dh], [row(post_ffn_g, i)],
                                   [(D_MODEL, BF16)], accs=[(1, D_MODEL)])
        big_parts[("w_down", i)] = _mm_tn_row("dw_down%d" % i, rec["act"], dff)
        dact = _mm_nt_row("d_act%d" % i, dff, wg[("w_down", i)], 0, F32)

        def swiglu_b(gt, p, d):
            sg = _sigmoid(gt)
            return d * p * (sg * (1.0 + gt * (1.0 - sg))), d * (gt * sg)

        dgate, dup = _rowwise("swiglu_b%d" % i, swiglu_b, [rec["gate"], rec["up"], dact], [],
                              [(D_FF, BF16), (D_FF, BF16)], vmem_mb=56)
        big_parts[("w_gate", i)] = _mm_tn_col("dw_gate%d" % i, rec["u2"], dgate)
        big_parts[("w_up", i)] = _mm_tn_col("dw_up%d" % i, rec["u2"], dup)
        du2a = _mm_nt_col("d_u2_gate%d" % i, dgate, wg[("w_gate", i)], 0)
        du2b = _mm_nt_col("d_u2_up%d" % i, dup, wg[("w_up", i)], 0)

        def post_mix_b(h1v, mx, d, da, db, g1, g2):
            _, vjp2 = jax.vjp(_rms, h1v, g2)
            dh1, dg2 = vjp2(da + db)
            dmid = d + dh1
            _, vjp1 = jax.vjp(_rms, mx, g1)
            dmx, dg1 = vjp1(dmid)
            return dmid, dmx, dg1, dg2

        dmid, dmix, g_post_mix, g_pre_ffn = _rowwise(
            "post_mix_b%d" % i, post_mix_b, [rec["h1"], rec["mix"], dh, du2a, du2b],
            [row(post_mix_g, i), row(pre_ffn_g, i)], [(D_MODEL, F32), (D_MODEL, BF16)],
            accs=[(1, D_MODEL), (1, D_MODEL)], vmem_mb=48)
        big_parts[("w_out", i)] = _mm_tn_row("dw_out%d" % i, rec["merged"], dmix)
        dmerged = _mm_nt_row("d_merged%d" % i, dmix, wg[("w_out", i)], 0, F32)

        def merge_b(oa, op, oc, d, ga, gp, gc):
            _, vjp = jax.vjp(_merge, oa, op, oc, ga, gp, gc)
            return vjp(d)

        mix_gains = [row(mix_out_g, i)[:, :D_ATTN], row(mix_out_g, i)[:, D_ATTN:D_ATTN + D_POOL],
                     row(mix_out_g, i)[:, D_ATTN + D_POOL:]]
        do_attn, do_pool, do_conv, g_mo_a, g_mo_p, g_mo_c = _rowwise(
            "merge_b%d" % i, merge_b, [rec["o_attn"], rec["o_pool"], rec["o_conv"], dmerged], mix_gains,
            [(D_ATTN, F32), (D_POOL, F32), (D_CONV, F32)], accs=[(1, D_ATTN), (1, D_POOL), (1, D_CONV)])
        g_mix_out = jnp.concatenate([g_mo_a, g_mo_p, g_mo_c], axis=1)
        ride = [(k, i) for k in big_names[1:]] + ([("w_in", i + 1)] if i + 1 < depth else [])
        dq, dk, dv, got = _attn_bwd(rec["proj"], rec["tot"], do_attn,
                                    _Exchange("scatter", [big_parts[k] for k in ride]))
        received.update(zip(ride, got))
        du_pool, g_w_pool, g_pool_scale = _pool_bwd(rec["proj"], do_pool, wpool_bf[i], row(pool_scale, i))
        da, dgt, g_w_dw, g_b_dw, g_ln_g, g_ln_b, g_w_pw = _conv_bwd(
            rec["proj"], do_conv, wdw_pad[i], row(b_dw, i), row(conv_ln_g, i), row(conv_ln_b, i), wpw_bf[i])
        dproj = jnp.concatenate([dq, dk, dv, du_pool, da, dgt], axis=1)
        big_parts[("w_in", i)] = _mm_tn_col("dw_in%d" % i, rec["u"], dproj)
        du = _mm_nt_col("d_u%d" % i, dproj, wg[("w_in", i)], 0)

        def pre_mix_b(hv, d, dd, g):
            _, vjp = jax.vjp(_rms, hv, g)
            dhh, dg = vjp(dd)
            return d + dhh, dg

        dh, g_pre_mix = _rowwise("pre_mix_b%d" % i, pre_mix_b, [rec["h"], dmid, du], [row(pre_mix_g, i)],
                                 [(D_MODEL, F32)], accs=[(1, D_MODEL)])
        small_grads[i] = dict(pre_mix_g=g_pre_mix[0], w_pool=g_w_pool, pool_scale=g_pool_scale[0],
                              w_dw=g_w_dw[:CONV_WIDTH], b_dw=g_b_dw[0], conv_ln_g=g_ln_g[0], conv_ln_b=g_ln_b[0],
                              w_pw=g_w_pw, mix_out_g=g_mix_out[0], post_mix_g=g_post_mix[0],
                              pre_ffn_g=g_pre_ffn[0], post_ffn_g=g_post_ffn[0])

    grad_x = dh[PAD + N_META:][None]
    g_meta_part = dh[PAD:PAD + N_META]

    rep_names = ["pre_mix_g", "pool_scale", "b_dw", "conv_ln_g", "conv_ln_b", "mix_out_g", "post_mix_g",
                 "pre_ffn_g", "post_ffn_g", "w_pool"]
    stack2 = lambda nme: jnp.stack([small_grads[l][nme] for l in range(depth)])
    small_list = [stack2(nme) for nme in rep_names] + [g_meta_part, stack2("w_dw"), stack2("w_pw")]
    small_list[rep_names.index("w_pool")] = small_list[rep_names.index("w_pool")].reshape(w_pool.shape)
    full_shapes = [a.shape for a in small_list]
    packed = _pack(small_list)
    summed = _sum_slots("sum_small", _gather_all("gather_small_grads", packed))
    full = _unpack(summed, full_shapes)
    rep_grads = dict(zip(rep_names, full[:len(rep_names)]))
    g_meta = lax.dynamic_slice_in_dim(full[-3], xy * meta_tokens.shape[1], meta_tokens.shape[1], axis=1)
    g_w_dw = lax.dynamic_slice_in_dim(full[-2], xy * w_dw.shape[2], w_dw.shape[2], axis=2)
    g_w_pw = lax.dynamic_slice_in_dim(full[-1], xy * w_pw.shape[1], w_pw.shape[1], axis=1)

    rep_w = dict(pre_mix_g=pre_mix_g, pool_scale=pool_scale, b_dw=b_dw, conv_ln_g=conv_ln_g, conv_ln_b=conv_ln_b,
                 mix_out_g=mix_out_g, post_mix_g=post_mix_g, pre_ffn_g=pre_ffn_g, post_ffn_g=post_ffn_g,
                 w_pool=w_pool)
    rep_m = dict(pre_mix_g=m_pre_mix_g, pool_scale=m_pool_scale, b_dw=m_b_dw, conv_ln_g=m_conv_ln_g,
                 conv_ln_b=m_conv_ln_b, mix_out_g=m_mix_out_g, post_mix_g=m_post_mix_g, pre_ffn_g=m_pre_ffn_g,
                 post_ffn_g=m_post_ffn_g, w_pool=m_w_pool)
    rep_v = dict(pre_mix_g=v_pre_mix_g, pool_scale=v_pool_scale, b_dw=v_b_dw, conv_ln_g=v_conv_ln_g,
                 conv_ln_b=v_conv_ln_b, mix_out_g=v_mix_out_g, post_mix_g=v_post_mix_g, pre_ffn_g=v_pre_ffn_g,
                 post_ffn_g=v_post_ffn_g, w_pool=v_w_pool)
    sm_names = rep_names + ["meta_tokens", "w_dw", "w_pw"]
    sm_w = [rep_w[k] for k in rep_names] + [meta_tokens, w_dw, w_pw]
    sm_m = [rep_m[k] for k in rep_names] + [m_meta_tokens, m_w_dw, m_w_pw]
    sm_v = [rep_v[k] for k in rep_names] + [v_meta_tokens, v_w_dw, v_w_pw]
    sm_g = [rep_grads[k] for k in rep_names] + [g_meta, g_w_dw, g_w_pw]
    sm_shapes = [a.shape for a in sm_w]
    sm_delta, sm_nm, sm_nv = _adamw_flat("adamw_small", _pack(sm_w), _pack(sm_m), _pack(sm_v), _pack(sm_g))
    small_out = {}
    for k, g, d, nm, nv in zip(sm_names, sm_g, _unpack(sm_delta, sm_shapes), _unpack(sm_nm, sm_shapes),
                               _unpack(sm_nv, sm_shapes)):
        small_out[k] = (g, d, nm, nv)

    assert depth == 2
    received[("w_in", 0)] = _scatter_xy("scatter_last", [big_parts[("w_in", 0)]])[0]
    plane_sums = [_sum_slots_layers("sum_%s" % k, received[(k, 0)], received[(k, 1)]) for k in big_names]
    other_sums = _swap_core("swap_core_sums", plane_sums)
    big_w = dict(w_in=(w_in, m_w_in, v_w_in), w_out=(w_out, m_w_out, v_w_out), w_gate=(w_gate, m_w_gate, v_w_gate),
                 w_up=(w_up, m_w_up, v_w_up), w_down=(w_down, m_w_down, v_w_down))
    big_out = {}
    for j, k in enumerate(big_names):
        w, m, v = big_w[k]
        big_out[k] = _adamw("adamw_%s" % k, w, m, v, plane_sums[j], other_sums[j])

    order = ["meta_tokens", "pre_mix_g", "w_in", "w_pool", "pool_scale", "w_dw", "b_dw", "conv_ln_g", "conv_ln_b",
             "w_pw", "mix_out_g", "w_out", "post_mix_g", "pre_ffn_g", "w_gate", "w_up", "w_down", "post_ffn_g"]
    res = lambda k: big_out[k] if k in big_out else small_out[k]
    outs = [loss, grad_x]
    for part in range(4):
        outs += [res(k)[part] for k in order]
    return tuple(outs)
```

```python
import functools

import jax
import jax.numpy as jnp
from jax import lax
from jax.experimental import pallas as pl
from jax.experimental.pallas import tpu as pltpu

F32 = jnp.float32
BF16 = jnp.bfloat16

D_MODEL = 2048
N_META = 16
D_ATTN = 1024
D_POOL = 512
D_CONV = 512
POOL_WINDOWS = (2, 4, 8, 16)
CONV_WIDTH = 31
D_IN_PROJ = 3 * D_ATTN + D_POOL + 2 * D_CONV
D_FF = 5632
EPS = 1e-6
BLOCK = 128
PAD = BLOCK - N_META
HALO = 32
N_SHARD = 4
N_DEV = 8
MESH = pl.DeviceIdType.MESH

ADAM_LR = 0.001
ADAM_B1 = 0.9
ADAM_B2 = 0.999
ADAM_EPS = 1e-08
ADAM_WD = 0.01
ADAM_STEP = 10

COL_Q, COL_K, COL_V = 0, D_ATTN, 2 * D_ATTN
COL_POOL = 3 * D_ATTN
COL_A = COL_POOL + D_POOL
COL_G = COL_A + D_CONV


def _call(body, *, name, out_shape, grid=None, in_specs=None, out_specs=None, scratch=(), vmem_mb=None,
          aliases=None):
    params = {}
    if grid is not None:
        params["dimension_semantics"] = ("arbitrary",) * len(grid)
    if vmem_mb is not None:
        params["vmem_limit_bytes"] = vmem_mb << 20
    kw = dict(out_shape=out_shape, name=name, compiler_params=pltpu.CompilerParams(**params))
    if grid is not None:
        kw["grid"] = grid
    if in_specs is not None:
        kw["in_specs"] = in_specs
    if out_specs is not None:
        kw["out_specs"] = out_specs
    if scratch:
        kw["scratch_shapes"] = list(scratch)
    if aliases:
        kw["input_output_aliases"] = dict(aliases)
    return pl.pallas_call(body, **kw)


def _sds(shape, dtype):
    return jax.ShapeDtypeStruct(tuple(shape), dtype)


def _pick(n, candidates):
    for c in candidates:
        if n % c == 0:
            return c
    return n


def _rowwise(name, fn, rows_in, consts, outs, accs=(), tm=BLOCK, with_row0=False, vmem_mb=None):
    lp = rows_in[0].shape[0]
    n_in, n_c, n_o = len(rows_in), len(consts), len(outs)

    def body(*refs):
        vals = [r[...] for r in refs[:n_in + n_c]]
        if with_row0:
            vals = [pl.program_id(0) * tm] + vals
        res = fn(*vals)
        if not isinstance(res, (tuple, list)):
            res = (res,)
        o_refs = refs[n_in + n_c:n_in + n_c + n_o]
        a_refs = refs[n_in + n_c + n_o:]
        for r, v in zip(o_refs, res[:n_o]):
            r[...] = v.astype(r.dtype)
        if a_refs:
            @pl.when(pl.program_id(0) == 0)
            def _():
                for r in a_refs:
                    r[...] = jnp.zeros(r.shape, r.dtype)
            for r, v in zip(a_refs, res[n_o:]):
                r[...] += v.astype(r.dtype)

    in_specs = [pl.BlockSpec((tm, a.shape[1]), lambda i: (i, 0)) for a in rows_in]
    in_specs += [pl.BlockSpec(c.shape, lambda i: (0, 0)) for c in consts]
    out_specs = [pl.BlockSpec((tm, w), lambda i: (i, 0)) for (w, _) in outs]
    out_specs += [pl.BlockSpec(s, lambda i: (0, 0)) for s in accs]
    out_shape = [_sds((lp, w), dt) for (w, dt) in outs] + [_sds(s, F32) for s in accs]
    res = _call(body, name=name, out_shape=out_shape, grid=(lp // tm,), in_specs=in_specs,
                out_specs=out_specs, vmem_mb=vmem_mb)(*rows_in, *consts)
    return res


def _rms(x, g):
    return x * lax.rsqrt(jnp.mean(x * x, axis=-1, keepdims=True) + EPS) * g


def _merge(oa, op, oc, ga, gp, gc):
    return jnp.concatenate([_rms(oa, ga), _rms(op, gp), _rms(oc, gc)], axis=1)


def _colsum(x):
    return jnp.sum(x, axis=0, keepdims=True)


def _sigmoid(x):
    return 1.0 / (1.0 + jnp.exp(-x))


MM_VMEM_MB = 56


def _mm_tiles(lp):
    return _pick(lp, (1408, 384, 256, 128))


def _mm_nn_col(name, a, wg, layer, out_dtype):
    lp, k = a.shape
    n = wg.shape[3]
    tm = _mm_tiles(lp)

    def body(a_ref, w_ref, o_ref):
        o_ref[...] = jnp.dot(a_ref[...], w_ref[...], preferred_element_type=F32).astype(o_ref.dtype)

    return _call(body, name=name, out_shape=_sds((lp, N_SHARD * n), out_dtype), grid=(N_SHARD, lp // tm),
                 in_specs=[pl.BlockSpec((tm, k), lambda s, i: (i, 0)),
                           pl.BlockSpec((None, None, k, n), lambda s, i: (s, layer, 0, 0))],
                 out_specs=pl.BlockSpec((tm, n), lambda s, i: (i, s)), vmem_mb=MM_VMEM_MB)(a, wg)


def _mm_nn_row(name, a, wg, layer):
    lp = a.shape[0]
    k, n = wg.shape[2], wg.shape[3]
    tm = _mm_tiles(lp)

    def body(a_ref, w_ref, o_ref):
        part = jnp.dot(a_ref[...], w_ref[...], preferred_element_type=F32)

        @pl.when(pl.program_id(1) == 0)
        def _():
            o_ref[...] = part

        @pl.when(pl.program_id(1) != 0)
        def _():
            o_ref[...] += part

    return _call(body, name=name, out_shape=_sds((lp, n), F32), grid=(lp // tm, N_SHARD),
                 in_specs=[pl.BlockSpec((tm, k), lambda i, s: (i, s)),
                           pl.BlockSpec((None, None, k, n), lambda i, s: (s, layer, 0, 0))],
                 out_specs=pl.BlockSpec((tm, n), lambda i, s: (i, 0)), vmem_mb=MM_VMEM_MB)(a, wg)


_NT = (((1,), (1,)), ((), ()))
_TN = (((0,), (0,)), ((), ()))


def _mm_nt_col(name, dy, wg, layer):
    lp = dy.shape[0]
    k, n = wg.shape[2], wg.shape[3]
    tm = _mm_tiles(lp)

    def body(d_ref, w_ref, o_ref):
        part = lax.dot_general(d_ref[...], w_ref[...], _NT, preferred_element_type=F32)

        @pl.when(pl.program_id(1) == 0)
        def _():
            o_ref[...] = part

        @pl.when(pl.program_id(1) != 0)
        def _():
            o_ref[...] += part

    return _call(body, name=name, out_shape=_sds((lp, k), F32), grid=(lp // tm, N_SHARD),
                 in_specs=[pl.BlockSpec((tm, n), lambda i, s: (i, s)),
                           pl.BlockSpec((None, None, k, n), lambda i, s: (s, layer, 0, 0))],
                 out_specs=pl.BlockSpec((tm, k), lambda i, s: (i, 0)), vmem_mb=MM_VMEM_MB)(dy, wg)


def _mm_nt_row(name, dy, wg, layer, out_dtype):
    lp = dy.shape[0]
    k, n = wg.shape[2], wg.shape[3]
    tm = _mm_tiles(lp)

    def body(d_ref, w_ref, o_ref):
        o_ref[...] = lax.dot_general(d_ref[...], w_ref[...], _NT, preferred_element_type=F32).astype(o_ref.dtype)

    return _call(body, name=name, out_shape=_sds((lp, N_SHARD * k), out_dtype), grid=(N_SHARD, lp // tm),
                 in_specs=[pl.BlockSpec((tm, n), lambda s, i: (i, 0)),
                           pl.BlockSpec((None, None, k, n), lambda s, i: (s, layer, 0, 0))],
                 out_specs=pl.BlockSpec((tm, k), lambda s, i: (i, s)), vmem_mb=MM_VMEM_MB)(dy, wg)


def _mm_tn_col(name, a, dy):
    lp, k = a.shape
    n = dy.shape[1] // N_SHARD
    tm = _mm_tiles(lp)
    tk = _pick(k, (1024, 512))

    def body(a_ref, d_ref, o_ref, acc):
        @pl.when(pl.program_id(2) == 0)
        def _():
            acc[...] = jnp.zeros(acc.shape, F32)

        acc[...] += lax.dot_general(a_ref[...], d_ref[...], _TN, preferred_element_type=F32)

        @pl.when(pl.program_id(2) == pl.num_programs(2) - 1)
        def _():
            o_ref[...] = acc[...].astype(o_ref.dtype)

    return _call(body, name=name, out_shape=_sds((N_SHARD, k, n), BF16), grid=(N_SHARD, k // tk, lp // tm),
                 in_specs=[pl.BlockSpec((tm, tk), lambda s, kk, i: (i, kk)),
                           pl.BlockSpec((tm, n), lambda s, kk, i: (i, s))],
                 out_specs=pl.BlockSpec((None, tk, n), lambda s, kk, i: (s, kk, 0)),
                 scratch=[pltpu.VMEM((tk, n), F32)], vmem_mb=MM_VMEM_MB)(a, dy)


def _mm_tn_row(name, a, dy):
    lp = a.shape[0]
    k = a.shape[1] // N_SHARD
    n = dy.shape[1]
    tm = _mm_tiles(lp)
    tn = _pick(n, (1024, 512))

    def body(a_ref, d_ref, o_ref, acc):
        @pl.when(pl.program_id(2) == 0)
        def _():
            acc[...] = jnp.zeros(acc.shape, F32)

        acc[...] += lax.dot_general(a_ref[...], d_ref[...], _TN, preferred_element_type=F32)

        @pl.when(pl.program_id(2) == pl.num_programs(2) - 1)
        def _():
            o_ref[...] = acc[...].astype(o_ref.dtype)

    return _call(body, name=name, out_shape=_sds((N_SHARD, k, n), BF16), grid=(N_SHARD, n // tn, lp // tm),
                 in_specs=[pl.BlockSpec((tm, k), lambda s, j, i: (i, s)),
                           pl.BlockSpec((tm, tn), lambda s, j, i: (i, j))],
                 out_specs=pl.BlockSpec((None, k, tn), lambda s, j, i: (s, 0, j)),
                 scratch=[pltpu.VMEM((k, tn), F32)], vmem_mb=MM_VMEM_MB)(a, dy)


SUB = 16
WIDE = 2 * BLOCK
Z_CLAMP = 20.0


def _log1m_sigmoid(z):
    return -jnp.where(z > Z_CLAMP, z, jnp.log(1.0 + jnp.exp(jnp.minimum(z, Z_CLAMP))))


def _tri2(tk, kind):
    r = lax.broadcasted_iota(jnp.int32, (2 * tk, tk), 0)
    r = jnp.where(r >= tk, r - tk, r)
    c = lax.broadcasted_iota(jnp.int32, (2 * tk, tk), 1)
    t = {"ge": r >= c, "le": r <= c}[kind]
    return jnp.where(t, 1.0, 0.0).astype(BF16)


def _strip_mask(kind, s, tk):
    if kind == "none":
        return None
    col = lax.broadcasted_iota(jnp.int32, (SUB, tk), 1)
    row = lax.broadcasted_iota(jnp.int32, (SUB, tk), 0)
    causal = (col - row) < s * SUB
    if kind == "diag":
        return causal
    if kind == "pad":
        return col >= PAD
    return causal & (col >= PAD)


def _attn_blocks(lp):
    nb = lp // BLOCK
    assert nb % 2 == 1, "sequence must be a 128-row block plus whole 256-row blocks"
    return nb, (nb + 1) // 2


class _Exchange:
    def __init__(self, kind, arrs, split=False):
        self.kind, self.arrs, self.n, self.split = kind, list(arrs), len(arrs), split

    def out_shape(self):
        if self.kind == "gather":
            return [_sds((N_SHARD,) + a.shape, a.dtype) for a in self.arrs]
        return [_sds(a.shape, a.dtype) for a in self.arrs]

    def scratch(self):
        return [pltpu.SemaphoreType.DMA((3 * self.n,)), pltpu.SemaphoreType.DMA((3 * self.n,)),
                pltpu.SemaphoreType.DMA((self.n,))]

    def copies(self, a, ins, outs, send, recv, local):
        x, y, c = _place()
        me = 2 * x + y
        own = ins[a] if self.kind == "gather" else ins[a].at[me]
        out = [pltpu.make_async_copy(own, outs[a].at[me], local.at[a])]
        for r, (px, py) in enumerate(_xy_peers(x, y)):
            src = ins[a] if self.kind == "gather" else ins[a].at[2 * px + py]
            out.append(pltpu.make_async_remote_copy(
                src_ref=src, dst_ref=outs[a].at[me], send_sem=send.at[3 * a + r], recv_sem=recv.at[3 * a + r],
                device_id=(px, py, c), device_id_type=MESH))
        return out

    def run(self, wait, ins, outs, send, recv, local):
        for a in range(self.n):
            def go(a=a):
                for cp in self.copies(a, ins, outs, send, recv, local):
                    if wait:
                        cp.wait()
                    else:
                        cp.start()
            if self.split:
                pl.when(lax.axis_index("c") == a % 2)(go)
            else:
                go()


def _attn_fwd(proj, exchange=None):
    lp = proj.shape[0]
    nb, nq = _attn_blocks(lp)
    n_pair = D_ATTN // BLOCK
    n_x = exchange.n if exchange else 0

    def body(*refs):
        q_ref, k_ref, v_ref = refs[:3]
        x_in = refs[3:3 + n_x]
        o_ref, tot_ref = refs[3 + n_x:5 + n_x]
        x_out = refs[5 + n_x:5 + 2 * n_x]
        qs, kb, vh, tri_l, tri_s, z_s, hl_s, c_s, w_s, r_s, acc_s = refs[5 + 2 * n_x:16 + 2 * n_x]
        x_sem = refs[16 + 2 * n_x:]
        p, i = pl.program_id(0), pl.program_id(1)
        m0 = lax.broadcasted_iota(jnp.int32, (1, BLOCK), 1) < (BLOCK // 2)

        if exchange:
            @pl.when((p == 0) & (i == 0))
            def _():
                exchange.run(False, x_in, x_out, *x_sem)

        @pl.when(i == 0)
        def _():
            tri_l[...] = _tri2(WIDE, "ge")
            tri_s[...] = _tri2(BLOCK, "ge")

            def prep(b, carry):
                rows = pl.ds(pl.multiple_of(b * BLOCK, BLOCK), BLOCK)
                q = q_ref[rows, :] * 0.125
                v = v_ref[rows, :]
                qs[0, rows, :] = jnp.where(m0, q, 0.0).astype(BF16)
                qs[1, rows, :] = jnp.where(m0, 0.0, q).astype(BF16)
                kb[rows, :] = k_ref[rows, :].astype(BF16)
                vh[0, rows, :] = jnp.where(m0, v, 0.0).astype(BF16)
                vh[1, rows, :] = jnp.where(m0, 0.0, v).astype(BF16)
                return carry

            lax.fori_loop(0, nb, prep, 0)

        def tiles(q0, tq, specs):
            heads = [(t, h) for t in range(len(specs)) for h in range(2)]
            strips = [slice(s * SUB, (s + 1) * SUB) for s in range(tq // SUB)]
            for t, h in heads:
                k0, tk, _ = specs[t]
                z_s[t, h, 0:tq, 0:tk] = lax.dot_general(qs[h, pl.ds(q0, tq), :], kb[pl.ds(k0, tk), :], _NT,
                                                        preferred_element_type=F32)
            for t, h in heads:
                _, tk, kind = specs[t]
                for s, rows in enumerate(strips):
                    lnb = _log1m_sigmoid(z_s[t, h, rows, 0:tk])
                    m = _strip_mask(kind, s, tk)
                    if m is not None:
                        lnb = jnp.where(m, lnb, 0.0)
                    hi = lnb.astype(BF16)
                    hl_s[t, h, rows, 0:tk] = hi
                    hl_s[t, h, rows, tk:2 * tk] = (lnb - hi.astype(F32)).astype(BF16)
            for t, h in heads:
                _, tk, _ = specs[t]
                tri = tri_l if tk == WIDE else tri_s
                c_s[t, h, 0:tq, 0:tk] = jnp.dot(hl_s[t, h, 0:tq, 0:2 * tk], tri[...], preferred_element_type=F32)
            for t, h in heads:
                _, tk, kind = specs[t]
                for s, rows in enumerate(strips):
                    r = r_s[h, rows, :]
                    c = c_s[t, h, rows, 0:tk]
                    rr = r if tk == BLOCK else jnp.concatenate([r, r], axis=1)
                    w = jnp.exp(z_s[t, h, rows, 0:tk] + c + rr)
                    m = _strip_mask(kind, s, tk)
                    if m is not None:
                        w = jnp.where(m, w, 0.0)
                    w_s[t, h, rows, 0:tk] = w.astype(BF16)
                    r_s[h, rows, :] = r + jnp.broadcast_to(c[:, 0:1], (SUB, BLOCK))
            upd = None
            for t, h in heads:
                k0, tk, _ = specs[t]
                d = jnp.dot(w_s[t, h, 0:tq, 0:tk], vh[h, pl.ds(k0, tk), :], preferred_element_type=F32)
                upd = d if upd is None else upd + d
            acc_s[0:tq, :] += upd

        def finish(q0, tq):
            o_ref[pl.ds(q0, tq), :] = acc_s[0:tq, :]
            tot_ref[pl.ds(q0, tq), :] = jnp.where(m0, r_s[0, 0:tq, :], r_s[1, 0:tq, :])

        r_s[...] = jnp.zeros(r_s.shape, F32)
        acc_s[...] = jnp.zeros(acc_s.shape, F32)

        @pl.when(i == 0)
        def _():
            tiles(0, BLOCK, [(0, BLOCK, "first")])
            finish(0, BLOCK)

        @pl.when(i > 0)
        def _():
            q0 = pl.multiple_of(i * WIDE - BLOCK, BLOCK)
            key0 = lambda j: pl.multiple_of(j * WIDE - BLOCK, BLOCK)
            tiles(q0, WIDE, [(q0, WIDE, "diag")])

            def inner(n, carry):
                j = i - 1 - 2 * n
                tiles(q0, WIDE, [(key0(j), WIDE, "none"), (key0(j - 1), WIDE, "none")])
                return carry

            lax.fori_loop(0, (i - 1) // 2, inner, 0)

            @pl.when((i - 1) % 2 == 1)
            def _():
                tiles(q0, WIDE, [(key0(1), WIDE, "none"), (0, BLOCK, "pad")])

            @pl.when((i - 1) % 2 == 0)
            def _():
                tiles(q0, WIDE, [(0, BLOCK, "pad")])

            finish(q0, WIDE)

        if exchange:
            @pl.when((p == n_pair - 1) & (i == nq - 1))
            def _():
                exchange.run(True, x_in, x_out, *x_sem)

    cq, ck, cv = COL_Q // BLOCK, COL_K // BLOCK, COL_V // BLOCK
    col = lambda c0: (lambda p, i: (0, c0 + p))
    scratch = [pltpu.VMEM((2, lp, BLOCK), BF16), pltpu.VMEM((lp, BLOCK), BF16), pltpu.VMEM((2, lp, BLOCK), BF16),
               pltpu.VMEM((2 * WIDE, WIDE), BF16), pltpu.VMEM((2 * BLOCK, BLOCK), BF16),
               pltpu.VMEM((2, 2, WIDE, WIDE), F32), pltpu.VMEM((2, 2, WIDE, 2 * WIDE), BF16),
               pltpu.VMEM((2, 2, WIDE, WIDE), F32), pltpu.VMEM((2, 2, WIDE, WIDE), BF16),
               pltpu.VMEM((2, WIDE, BLOCK), F32), pltpu.VMEM((WIDE, BLOCK), F32)]
    res = _call(body, name="attn_fwd",
                out_shape=[_sds((lp, D_ATTN), F32), _sds((lp, D_ATTN), F32)] + (exchange.out_shape() if exchange else []),
                grid=(n_pair, nq),
                in_specs=[pl.BlockSpec((lp, BLOCK), col(cq)), pl.BlockSpec((lp, BLOCK), col(ck)),
                          pl.BlockSpec((lp, BLOCK), col(cv))] + [_ANY] * n_x,
                out_specs=[pl.BlockSpec((lp, BLOCK), col(0)), pl.BlockSpec((lp, BLOCK), col(0))] + [_ANY] * n_x,
                scratch=scratch + (exchange.scratch() if exchange else []),
                vmem_mb=56)(proj, proj, proj, *(exchange.arrs if exchange else []))
    return res[0], res[1], list(res[2:])


def _attn_bwd(proj, tot, d_out, exchange=None):
    lp = proj.shape[0]
    nb, nq = _attn_blocks(lp)
    n_pair = D_ATTN // BLOCK
    n_x = exchange.n if exchange else 0
    n_s = 24

    def body(*refs):
        q_ref, k_ref, v_ref, tot_ref, do_ref = refs[:5]
        x_in = refs[5:5 + n_x]
        dq_ref, dk_ref, dv_ref = refs[5 + n_x:8 + n_x]
        x_out = refs[8 + n_x:8 + 2 * n_x]
        (qs, kb, kh, vb, doh, tge_l, tge_s, tle_l, tle_s, z_s, g_s, lnb_s, hl_s, c_s, gl_s, gc_s, w_s, dz_s,
         tot_s, a_s, b_s, dq_acc, dk_acc, dv_acc) = refs[8 + 2 * n_x:8 + 2 * n_x + n_s]
        x_sem = refs[8 + 2 * n_x + n_s:]
        p, i = pl.program_id(0), pl.program_id(1)
        m0 = lax.broadcasted_iota(jnp.int32, (1, BLOCK), 1) < (BLOCK // 2)

        if exchange:
            @pl.when((p == 0) & (i == 0))
            def _():
                exchange.run(False, x_in, x_out, *x_sem)

        @pl.when(i == 0)
        def _():
            tge_l[...] = _tri2(WIDE, "ge")
            tge_s[...] = _tri2(BLOCK, "ge")
            tle_l[...] = _tri2(WIDE, "le")[0:WIDE]
            tle_s[...] = _tri2(BLOCK, "le")[0:BLOCK]

            def prep(b, carry):
                rows = pl.ds(pl.multiple_of(b * BLOCK, BLOCK), BLOCK)
                q = q_ref[rows, :] * 0.125
                k = k_ref[rows, :]
                do = do_ref[rows, :]
                qs[0, rows, :] = jnp.where(m0, q, 0.0).astype(BF16)
                qs[1, rows, :] = jnp.where(m0, 0.0, q).astype(BF16)
                kb[rows, :] = k.astype(BF16)
                kh[0, rows, :] = jnp.where(m0, k, 0.0).astype(BF16)
                kh[1, rows, :] = jnp.where(m0, 0.0, k).astype(BF16)
                vb[rows, :] = v_ref[rows, :].astype(BF16)
                doh[0, rows, :] = jnp.where(m0, do, 0.0).astype(BF16)
                doh[1, rows, :] = jnp.where(m0, 0.0, do).astype(BF16)
                dk_acc[rows, :] = jnp.zeros((BLOCK, BLOCK), F32)
                dv_acc[rows, :] = jnp.zeros((BLOCK, BLOCK), F32)
                return carry

            lax.fori_loop(0, nb, prep, 0)

        def wide(x, tk):
            return x if tk == BLOCK else jnp.concatenate([x, x], axis=1)

        def tiles(q0, tq, specs):
            heads = [(t, h) for t in range(len(specs)) for h in range(2)]
            strips = [slice(s * SUB, (s + 1) * SUB) for s in range(tq // SUB)]
            for t, h in heads:
                k0, tk, _ = specs[t]
                z_s[t, h, 0:tq, 0:tk] = lax.dot_general(qs[h, pl.ds(q0, tq), :], kb[pl.ds(k0, tk), :], _NT,
                                                        preferred_element_type=F32)
                g_s[t, h, 0:tq, 0:tk] = lax.dot_general(doh[h, pl.ds(q0, tq), :], vb[pl.ds(k0, tk), :], _NT,
                                                        preferred_element_type=F32)
            for t, h in heads:
                _, tk, kind = specs[t]
                for s, rows in enumerate(strips):
                    lnb = _log1m_sigmoid(z_s[t, h, rows, 0:tk])
                    m = _strip_mask(kind, s, tk)
                    if m is not None:
                        lnb = jnp.where(m, lnb, 0.0)
                    lnb_s[t, h, rows, 0:tk] = lnb
                    hi = lnb.astype(BF16)
                    hl_s[t, h, rows, 0:tk] = hi
                    hl_s[t, h, rows, tk:2 * tk] = (lnb - hi.astype(F32)).astype(BF16)
            for t, h in heads:
                _, tk, _ = specs[t]
                tri = tge_l if tk == WIDE else tge_s
                c_s[t, h, 0:tq, 0:tk] = jnp.dot(hl_s[t, h, 0:tq, 0:2 * tk], tri[...], preferred_element_type=F32)
            for t, h in heads:
                _, tk, kind = specs[t]
                for s, rows in enumerate(strips):
                    c = c_s[t, h, rows, 0:tk]
                    a_next = a_s[h, rows, :] + jnp.broadcast_to(c[:, 0:1], (SUB, BLOCK))
                    a_s[h, rows, :] = a_next
                    w = jnp.exp(z_s[t, h, rows, 0:tk] + c + wide(tot_s[h, rows, :] - a_next, tk))
                    m = _strip_mask(kind, s, tk)
                    if m is not None:
                        w = jnp.where(m, w, 0.0)
                    g = w * g_s[t, h, rows, 0:tk]
                    g_s[t, h, rows, 0:tk] = g
                    gl_s[t, h, rows, 0:tk] = g.astype(BF16)
                    w_s[t, h, rows, 0:tk] = w.astype(BF16)
            for t, h in heads:
                _, tk, _ = specs[t]
                tri = tle_l if tk == WIDE else tle_s
                gc_s[t, h, 0:tq, 0:tk] = jnp.dot(gl_s[t, h, 0:tq, 0:tk], tri[...], preferred_element_type=F32)
            for t, h in heads:
                _, tk, kind = specs[t]
                for s, rows in enumerate(strips):
                    gc = gc_s[t, h, rows, 0:tk]
                    b = b_s[h, rows, :]
                    sig = jnp.exp(z_s[t, h, rows, 0:tk] + lnb_s[t, h, rows, 0:tk])
                    dz = g_s[t, h, rows, 0:tk] - sig * (gc + wide(b, tk))
                    m = _strip_mask(kind, s, tk)
                    if m is not None:
                        dz = jnp.where(m, dz, 0.0)
                    dz_s[t, h, rows, 0:tk] = dz.astype(BF16)
                    b_s[h, rows, :] = b + jnp.broadcast_to(gc[:, tk - 1:tk], (SUB, BLOCK))
            upd = None
            for t, h in heads:
                k0, tk, _ = specs[t]
                d = jnp.dot(dz_s[t, h, 0:tq, 0:tk], kh[h, pl.ds(k0, tk), :], preferred_element_type=F32)
                upd = d if upd is None else upd + d
            dq_acc[0:tq, :] += upd
            for t, (k0, tk, _) in enumerate(specs):
                dk_acc[pl.ds(k0, tk), :] += (
                    lax.dot_general(dz_s[t, 0, 0:tq, 0:tk], qs[0, pl.ds(q0, tq), :], _TN, preferred_element_type=F32) +
                    lax.dot_general(dz_s[t, 1, 0:tq, 0:tk], qs[1, pl.ds(q0, tq), :], _TN, preferred_element_type=F32))
                dv_acc[pl.ds(k0, tk), :] += (
                    lax.dot_general(w_s[t, 0, 0:tq, 0:tk], doh[0, pl.ds(q0, tq), :], _TN, preferred_element_type=F32) +
                    lax.dot_general(w_s[t, 1, 0:tq, 0:tk], doh[1, pl.ds(q0, tq), :], _TN, preferred_element_type=F32))

        def start(q0, tq):
            tv = tot_ref[pl.ds(q0, tq), :]
            tot_s[0, 0:tq, :] = jnp.broadcast_to(tv[:, 0:1], (tq, BLOCK))
            tot_s[1, 0:tq, :] = jnp.broadcast_to(tv[:, BLOCK - 1:BLOCK], (tq, BLOCK))
            a_s[...] = jnp.zeros(a_s.shape, F32)
            b_s[...] = jnp.zeros(b_s.shape, F32)
            dq_acc[...] = jnp.zeros(dq_acc.shape, F32)

        def finish(q0, tq):
            dq_ref[pl.ds(q0, tq), :] = (dq_acc[0:tq, :] * 0.125).astype(dq_ref.dtype)

        @pl.when(i == 0)
        def _():
            start(0, BLOCK)
            tiles(0, BLOCK, [(0, BLOCK, "first")])
            finish(0, BLOCK)

        @pl.when(i > 0)
        def _():
            q0 = pl.multiple_of(i * WIDE - BLOCK, BLOCK)
            key0 = lambda j: pl.multiple_of(j * WIDE - BLOCK, BLOCK)
            odd = (i - 1) % 2
            start(q0, WIDE)

            @pl.when(odd == 1)
            def _():
                tiles(q0, WIDE, [(0, BLOCK, "pad"), (key0(1), WIDE, "none")])

            @pl.when(odd == 0)
            def _():
                tiles(q0, WIDE, [(0, BLOCK, "pad")])

            def inner(n, carry):
                j = 1 + odd + 2 * n
                tiles(q0, WIDE, [(key0(j), WIDE, "none"), (key0(j + 1), WIDE, "none")])
                return carry

            lax.fori_loop(0, (i - 1) // 2, inner, 0)
            tiles(q0, WIDE, [(q0, WIDE, "diag")])
            finish(q0, WIDE)

        @pl.when(i == nq - 1)
        def _():
            dk_ref[...] = dk_acc[...].astype(dk_ref.dtype)
            dv_ref[...] = dv_acc[...].astype(dv_ref.dtype)

        if exchange:
            @pl.when((p == n_pair - 1) & (i == nq - 1))
            def _():
                exchange.run(True, x_in, x_out, *x_sem)

    cq, ck, cv = COL_Q // BLOCK, COL_K // BLOCK, COL_V // BLOCK
    col = lambda c0: (lambda p, i: (0, c0 + p))
    whole = lambda c0: pl.BlockSpec((lp, BLOCK), col(c0))
    tile4 = lambda w, dt: pltpu.VMEM((2, 2, WIDE, w), dt)
    scratch = [pltpu.VMEM((2, lp, BLOCK), BF16), pltpu.VMEM((lp, BLOCK), BF16), pltpu.VMEM((2, lp, BLOCK), BF16),
               pltpu.VMEM((lp, BLOCK), BF16), pltpu.VMEM((2, lp, BLOCK), BF16),
               pltpu.VMEM((2 * WIDE, WIDE), BF16), pltpu.VMEM((2 * BLOCK, BLOCK), BF16),
               pltpu.VMEM((WIDE, WIDE), BF16), pltpu.VMEM((BLOCK, BLOCK), BF16),
               tile4(WIDE, F32), tile4(WIDE, F32), tile4(WIDE, F32), tile4(2 * WIDE, BF16), tile4(WIDE, F32),
               tile4(WIDE, BF16), tile4(WIDE, F32), tile4(WIDE, BF16), tile4(WIDE, BF16),
               pltpu.VMEM((2, WIDE, BLOCK), F32), pltpu.VMEM((2, WIDE, BLOCK), F32), pltpu.VMEM((2, WIDE, BLOCK), F32),
               pltpu.VMEM((WIDE, BLOCK), F32), pltpu.VMEM((lp, BLOCK), F32), pltpu.VMEM((lp, BLOCK), F32)]
    res = _call(body, name="attn_bwd",
                out_shape=[_sds((lp, D_ATTN), BF16)] * 3 + (exchange.out_shape() if exchange else []),
                grid=(n_pair, nq),
                in_specs=[whole(cq), whole(ck), whole(cv), whole(0), whole(0)] + [_ANY] * n_x,
                out_specs=[whole(0), whole(0), whole(0)] + [_ANY] * n_x,
                scratch=scratch + (exchange.scratch() if exchange else []),
                vmem_mb=60)(proj, proj, proj, tot, d_out, *(exchange.arrs if exchange else []))
    return res[0], res[1], res[2], list(res[3:])


def _shift_down(x, d):
    return x if d == 0 else pltpu.roll(x, d, 0)


def _shift_up(x, d):
    return x if d == 0 else pltpu.roll(x, x.shape[0] - d, 0)


def _pool_windows(ext, down):
    shift = _shift_down if down else _shift_up
    outs = []
    for g, w in enumerate(POOL_WINDOWS):
        s = ext[:, g * BLOCK:(g + 1) * BLOCK]
        d = 1
        while d < w:
            s = s + shift(s, d)
            d *= 2
        outs.append(s)
    return jnp.concatenate(outs, axis=1)


def _pool_counts(pos):
    cols = [jnp.broadcast_to(jnp.clip(pos + 1, 1, w).astype(F32), (pos.shape[0], BLOCK)) for w in POOL_WINDOWS]
    return jnp.concatenate(cols, axis=1)


def _group_dot(x, w_ref, transpose):
    outs = []
    for g in range(len(POOL_WINDOWS)):
        xg = x[:, g * BLOCK:(g + 1) * BLOCK].astype(BF16)
        wg = w_ref[g * BLOCK:(g + 1) * BLOCK, :]
        if transpose:
            outs.append(lax.dot_general(xg, wg, _NT, preferred_element_type=F32))
        else:
            outs.append(jnp.dot(xg, wg, preferred_element_type=F32))
    return jnp.concatenate(outs, axis=1)


def _pooled(prev, cur, r):
    rows = cur.shape[0]
    ext = jnp.concatenate([prev[rows - HALO:], cur], axis=0)
    pos = r * rows + lax.broadcasted_iota(jnp.int32, (rows, 1), 0) - PAD
    ws = _pool_windows(ext, down=True)[HALO:]
    return jnp.where(pos >= 0, ws / _pool_counts(pos) - cur, 0.0)


def _pool_fwd(proj, w_pool_bf, scale):
    lp = proj.shape[0]
    rows = BLOCK
    cb = COL_POOL // D_POOL

    def body(prev_ref, cur_ref, w_ref, s_ref, o_ref):
        pooled = _pooled(prev_ref[...], cur_ref[...], pl.program_id(0))
        o_ref[...] = _group_dot(pooled, w_ref, False) * s_ref[...]

    return _call(body, name="pool_fwd", out_shape=_sds((lp, D_POOL), F32), grid=(lp // rows,),
                 in_specs=[pl.BlockSpec((rows, D_POOL), lambda r: (jnp.maximum(r - 1, 0), cb)),
                           pl.BlockSpec((rows, D_POOL), lambda r: (r, cb)),
                           pl.BlockSpec((D_POOL, BLOCK), lambda r: (0, 0)),
                           pl.BlockSpec((1, D_POOL), lambda r: (0, 0))],
                 out_specs=pl.BlockSpec((rows, D_POOL), lambda r: (r, 0)))(proj, proj, w_pool_bf, scale)


def _pool_bwd(proj, d_out, w_pool_bf, scale):
    lp = proj.shape[0]
    rows = BLOCK
    n_chunk = lp // rows
    cb = COL_POOL // D_POOL

    def body(prev_ref, cur_ref, do_ref, don_ref, w_ref, s_ref, du_ref, dw_ref, ds_ref):
        r = pl.program_id(0)

        @pl.when(r == 0)
        def _():
            dw_ref[...] = jnp.zeros(dw_ref.shape, F32)
            ds_ref[...] = jnp.zeros(ds_ref.shape, F32)

        pooled = _pooled(prev_ref[...], cur_ref[...], r)
        d_ext = jnp.concatenate([do_ref[...], don_ref[0:HALO]], axis=0)
        pos = r * rows + lax.broadcasted_iota(jnp.int32, (rows + HALO, 1), 0) - PAD
        dmixed = jnp.where((pos >= 0) & (pos < lp - PAD), d_ext * s_ref[...], 0.0)
        dpooled = _group_dot(dmixed, w_ref, True)
        back = _pool_windows(dpooled / _pool_counts(pos), down=False)[0:rows]
        du = jnp.where(pos[0:rows] >= 0, back - dpooled[0:rows], 0.0)
        du_ref[...] = du.astype(du_ref.dtype)
        mixed = _group_dot(pooled, w_ref, False)
        ds_ref[...] += _colsum(do_ref[...] * mixed)
        pooled_bf = pooled.astype(BF16)
        dm_bf = dmixed[0:rows].astype(BF16)
        for g in range(len(POOL_WINDOWS)):
            sl = slice(g * BLOCK, (g + 1) * BLOCK)
            dw_ref[sl, :] += lax.dot_general(pooled_bf[:, sl], dm_bf[:, sl], _TN, preferred_element_type=F32)

    return _call(body, name="pool_bwd",
                 out_shape=[_sds((lp, D_POOL), BF16), _sds((D_POOL, BLOCK), F32), _sds((1, D_POOL), F32)],
                 grid=(n_chunk,),
                 in_specs=[pl.BlockSpec((rows, D_POOL), lambda r: (jnp.maximum(r - 1, 0), cb)),
                           pl.BlockSpec((rows, D_POOL), lambda r: (r, cb)),
                           pl.BlockSpec((rows, D_POOL), lambda r: (r, 0)),
                           pl.BlockSpec((rows, D_POOL), lambda r: (jnp.minimum(r + 1, n_chunk - 1), 0)),
                           pl.BlockSpec((D_POOL, BLOCK), lambda r: (0, 0)),
                           pl.BlockSpec((1, D_POOL), lambda r: (0, 0))],
                 out_specs=[pl.BlockSpec((rows, D_POOL), lambda r: (r, 0)),
                            pl.BlockSpec((D_POOL, BLOCK), lambda r: (0, 0)),
                            pl.BlockSpec((1, D_POOL), lambda r: (0, 0))])(proj, proj, d_out, d_out, w_pool_bf, scale)


def _conv_taps(u, wdw_ref):
    y = wdw_ref[CONV_WIDTH - 1:CONV_WIDTH, :] * u
    for d in range(1, CONV_WIDTH):
        y = y + wdw_ref[CONV_WIDTH - 1 - d:CONV_WIDTH - d, :] * _shift_down(u, d)
    return y


def _layernorm_stats(y):
    mu = jnp.mean(y, axis=-1, keepdims=True)
    yc = y - mu
    rstd = lax.rsqrt(jnp.mean(yc * yc, axis=-1, keepdims=True) + EPS)
    return yc * rstd, rstd


def _conv_fwd(proj, wdw, bdw, ln_g, ln_b, wpw_bf):
    lp = proj.shape[0]
    rows = BLOCK
    ca, cg = COL_A // D_CONV, COL_G // D_CONV

    def body(ap_ref, a_ref, gp_ref, g_ref, wdw_ref, b_ref, lg_ref, lb_ref, wpw_ref, o_ref):
        r = pl.program_id(0)
        a = jnp.concatenate([ap_ref[rows - HALO:rows], a_ref[...]], axis=0)
        g = jnp.concatenate([gp_ref[rows - HALO:rows], g_ref[...]], axis=0)
        u = a * _sigmoid(g)
        y = _conv_taps(u, wdw_ref)[HALO:] + b_ref[...]
        xhat, _ = _layernorm_stats(y)
        yn = xhat * lg_ref[...] + lb_ref[...]
        pos = r * rows + lax.broadcasted_iota(jnp.int32, (rows, 1), 0) - PAD
        s = jnp.where(pos >= 0, yn * _sigmoid(yn), 0.0)
        o_ref[...] = jnp.dot(s.astype(BF16), wpw_ref[...], preferred_element_type=F32)

    prev = lambda c: (lambda r: (jnp.maximum(r - 1, 0), c))
    cur = lambda c: (lambda r: (r, c))
    const = lambda shape: pl.BlockSpec(shape, lambda r: (0, 0))
    return _call(body, name="conv_fwd", out_shape=_sds((lp, D_CONV), F32), grid=(lp // rows,),
                 in_specs=[pl.BlockSpec((rows, D_CONV), prev(ca)), pl.BlockSpec((rows, D_CONV), cur(ca)),
                           pl.BlockSpec((rows, D_CONV), prev(cg)), pl.BlockSpec((rows, D_CONV), cur(cg)),
                           const((HALO, D_CONV)), const((1, D_CONV)), const((1, D_CONV)), const((1, D_CONV)),
                           const((D_CONV, D_CONV))],
                 out_specs=pl.BlockSpec((rows, D_CONV), cur(0)))(proj, proj, proj, proj, wdw, bdw, ln_g, ln_b, wpw_bf)


def _conv_bwd(proj, d_out, wdw, bdw, ln_g, ln_b, wpw_bf):
    lp = proj.shape[0]
    rows = BLOCK
    n_chunk = lp // rows
    ca, cg = COL_A // D_CONV, COL_G // D_CONV
    ext = rows + HALO

    def body(ap_ref, a_ref, an_ref, gp_ref, g_ref, gn_ref, do_ref, don_ref, wdw_ref, b_ref, lg_ref, lb_ref,
             wpw_ref, da_ref, dg_ref, dwdw_ref, db_ref, dlg_ref, dlb_ref, dwpw_ref):
        r = pl.program_id(0)

        @pl.when(r == 0)
        def _():
            for ref in (dwdw_ref, db_ref, dlg_ref, dlb_ref, dwpw_ref):
                ref[...] = jnp.zeros(ref.shape, F32)

        a3 = jnp.concatenate([ap_ref[rows - HALO:rows], a_ref[...], an_ref[0:HALO]], axis=0)
        g3 = jnp.concatenate([gp_ref[rows - HALO:rows], g_ref[...], gn_ref[0:HALO]], axis=0)
        sig3 = _sigmoid(g3)
        u3 = a3 * sig3
        y = _conv_taps(u3, wdw_ref)[HALO:] + b_ref[...]
        xhat, rstd = _layernorm_stats(y)
        yn = xhat * lg_ref[...] + lb_ref[...]
        sgm = _sigmoid(yn)
        pos = r * rows + lax.broadcasted_iota(jnp.int32, (ext, 1), 0) - PAD
        valid = (pos >= 0) & (pos < lp - PAD)
        d_ext = jnp.concatenate([do_ref[...], don_ref[0:HALO]], axis=0)
        ds = lax.dot_general(d_ext.astype(BF16), wpw_ref[...], _NT, preferred_element_type=F32)
        dyn = jnp.where(valid, ds * (sgm * (1.0 + yn * (1.0 - sgm))), 0.0)
        dxh = dyn * lg_ref[...]
        dy = rstd * (dxh - jnp.mean(dxh, axis=-1, keepdims=True)
                     - xhat * jnp.mean(dxh * xhat, axis=-1, keepdims=True))
        s_cur = jnp.where(valid[0:rows], (yn * sgm)[0:rows], 0.0)
        dwpw_ref[...] += lax.dot_general(s_cur.astype(BF16), do_ref[...].astype(BF16), _TN,
                                         preferred_element_type=F32)
        dlg_ref[...] += _colsum(dyn[0:rows] * xhat[0:rows])
        dlb_ref[...] += _colsum(dyn[0:rows])
        dy_cur = dy[0:rows]
        db_ref[...] += _colsum(dy_cur)
        du = wdw_ref[CONV_WIDTH - 1:CONV_WIDTH, :] * dy
        for d in range(CONV_WIDTH):
            k = CONV_WIDTH - 1 - d
            dwdw_ref[k:k + 1, :] += _colsum(dy_cur * _shift_down(u3, d)[HALO:HALO + rows])
            if d:
                du = du + wdw_ref[k:k + 1, :] * _shift_up(dy, d)
        du = jnp.where(pos[0:rows] >= 0, du[0:rows], 0.0)
        sig = sig3[HALO:HALO + rows]
        da_ref[...] = (du * sig).astype(da_ref.dtype)
        dg_ref[...] = (du * a_ref[...] * sig * (1.0 - sig)).astype(dg_ref.dtype)

    prev = lambda c: (lambda r: (jnp.maximum(r - 1, 0), c))
    cur = lambda c: (lambda r: (r, c))
    nxt = lambda c: (lambda r: (jnp.minimum(r + 1, n_chunk - 1), c))
    const = lambda shape: pl.BlockSpec(shape, lambda r: (0, 0))
    blk = lambda f: pl.BlockSpec((rows, D_CONV), f)
    return _call(body, name="conv_bwd",
                 out_shape=[_sds((lp, D_CONV), BF16), _sds((lp, D_CONV), BF16), _sds((HALO, D_CONV), F32),
                            _sds((1, D_CONV), F32), _sds((1, D_CONV), F32), _sds((1, D_CONV), F32),
                            _sds((D_CONV, D_CONV), F32)],
                 grid=(n_chunk,),
                 in_specs=[blk(prev(ca)), blk(cur(ca)), blk(nxt(ca)), blk(prev(cg)), blk(cur(cg)), blk(nxt(cg)),
                           blk(cur(0)), blk(nxt(0)),
                           const((HALO, D_CONV)), const((1, D_CONV)), const((1, D_CONV)), const((1, D_CONV)),
                           const((D_CONV, D_CONV))],
                 out_specs=[blk(cur(0)), blk(cur(0)), const((HALO, D_CONV)), const((1, D_CONV)),
                            const((1, D_CONV)), const((1, D_CONV)), const((D_CONV, D_CONV))],
                 vmem_mb=48)(proj, proj, proj, proj, proj, proj, d_out, d_out, wdw, bdw, ln_g, ln_b, wpw_bf)


_ANY = pl.BlockSpec(memory_space=pl.ANY)


def _place():
    return lax.axis_index("x"), lax.axis_index("y"), lax.axis_index("c")


def _xy_peers(x, y):
    return [(1 - x, y), (x, 1 - y), (1 - x, 1 - y)]


def _gather_xy(name, arrs):
    n = len(arrs)

    def body(*refs):
        ins, outs = refs[:n], refs[n:2 * n]
        send, recv, local = refs[2 * n:]
        x, y, c = _place()
        me = 2 * x + y
        copies = []
        for a in range(n):
            cp = pltpu.make_async_copy(ins[a], outs[a].at[me], local.at[a])
            cp.start()
            copies.append(cp)
        remote = []
        for a in range(n):
            for r, (px, py) in enumerate(_xy_peers(x, y)):
                cp = pltpu.make_async_remote_copy(src_ref=ins[a], dst_ref=outs[a].at[me],
                                                  send_sem=send.at[3 * a + r], recv_sem=recv.at[3 * a + r],
                                                  device_id=(px, py, c), device_id_type=MESH)
                cp.start()
                remote.append(cp)
        for cp in remote:
            cp.wait()
        for cp in copies:
            cp.wait()

    return _call(body, name=name, out_shape=[_sds((N_SHARD,) + a.shape, a.dtype) for a in arrs],
                 in_specs=[_ANY] * n, out_specs=[_ANY] * n,
                 scratch=[pltpu.SemaphoreType.DMA((3 * n,)), pltpu.SemaphoreType.DMA((3 * n,)),
                          pltpu.SemaphoreType.DMA((n,))])(*arrs)


def _scatter_xy(name, parts):
    n = len(parts)

    def body(*refs):
        ins, outs = refs[:n], refs[n:2 * n]
        send, recv, local = refs[2 * n:]
        x, y, c = _place()
        me = 2 * x + y
        copies = []
        for a in range(n):
            cp = pltpu.make_async_copy(ins[a].at[me], outs[a].at[me], local.at[a])
            cp.start()
            copies.append(cp)
        remote = []
        for a in range(n):
            for r, (px, py) in enumerate(_xy_peers(x, y)):
                cp = pltpu.make_async_remote_copy(src_ref=ins[a].at[2 * px + py], dst_ref=outs[a].at[me],
                                                  send_sem=send.at[3 * a + r], recv_sem=recv.at[3 * a + r],
                                                  device_id=(px, py, c), device_id_type=MESH)
                cp.start()
                remote.append(cp)
        for cp in remote:
            cp.wait()
        for cp in copies:
            cp.wait()

    return _call(body, name=name, out_shape=[_sds(a.shape, a.dtype) for a in parts],
                 in_specs=[_ANY] * n, out_specs=[_ANY] * n,
                 scratch=[pltpu.SemaphoreType.DMA((3 * n,)), pltpu.SemaphoreType.DMA((3 * n,)),
                          pltpu.SemaphoreType.DMA((n,))])(*parts)


def _forward_core(name, arrs):
    n = len(arrs)

    def body(*refs):
        ins, outs = refs[:n], refs[n:2 * n]
        send, recv = refs[2 * n:]
        x, y, c = _place()
        for a in range(n):
            cp = pltpu.make_async_remote_copy(src_ref=ins[a], dst_ref=outs[a], send_sem=send.at[a],
                                              recv_sem=recv.at[a], device_id=(x, y, 1 - c), device_id_type=MESH)

            @pl.when(c == a % 2)
            def _(cp=cp):
                cp.start()
                cp.wait_send()

            @pl.when(c != a % 2)
            def _(cp=cp):
                cp.wait_recv()

    return _call(body, name=name, out_shape=[_sds(a.shape, a.dtype) for a in arrs],
                 in_specs=[_ANY] * n, out_specs=[_ANY] * n, aliases={a: a for a in range(n)},
                 scratch=[pltpu.SemaphoreType.DMA((n,)), pltpu.SemaphoreType.DMA((n,))])(*arrs)


def _swap_core(name, arrs):
    n = len(arrs)

    def body(*refs):
        ins, outs = refs[:n], refs[n:2 * n]
        send, recv = refs[2 * n:]
        x, y, c = _place()
        remote = []
        for a in range(n):
            cp = pltpu.make_async_remote_copy(src_ref=ins[a], dst_ref=outs[a], send_sem=send.at[a],
                                              recv_sem=recv.at[a], device_id=(x, y, 1 - c), device_id_type=MESH)
            cp.start()
            remote.append(cp)
        for cp in remote:
            cp.wait()

    return _call(body, name=name, out_shape=[_sds(a.shape, a.dtype) for a in arrs],
                 in_specs=[_ANY] * n, out_specs=[_ANY] * n,
                 scratch=[pltpu.SemaphoreType.DMA((n,)), pltpu.SemaphoreType.DMA((n,))])(*arrs)


def _gather_all(name, arr):
    flips = [(fx, fy, fc) for fx in (0, 1) for fy in (0, 1) for fc in (0, 1)][1:]

    def body(in_ref, out_ref, send, recv, local):
        x, y, c = _place()
        me = 4 * x + 2 * y + c
        own = pltpu.make_async_copy(in_ref, out_ref.at[me], local)
        own.start()
        remote = []
        for k, (fx, fy, fc) in enumerate(flips):
            peer = (1 - x if fx else x, 1 - y if fy else y, 1 - c if fc else c)
            cp = pltpu.make_async_remote_copy(src_ref=in_ref, dst_ref=out_ref.at[me], send_sem=send.at[k],
                                              recv_sem=recv.at[k], device_id=peer, device_id_type=MESH)
            cp.start()
            remote.append(cp)
        for cp in remote:
            cp.wait()
        own.wait()

    return _call(body, name=name, out_shape=_sds((N_DEV,) + arr.shape, arr.dtype), in_specs=[_ANY], out_specs=_ANY,
                 scratch=[pltpu.SemaphoreType.DMA((N_DEV - 1,)), pltpu.SemaphoreType.DMA((N_DEV - 1,)),
                          pltpu.SemaphoreType.DMA])(arr)


def _sum_slots(name, stacked, out_dtype=F32):
    s, r, c = stacked.shape
    tr = _pick(r, (256, 128, 64, 8))

    def body(in_ref, o_ref):
        acc = in_ref[0].astype(F32)
        for k in range(1, s):
            acc = acc + in_ref[k].astype(F32)
        o_ref[...] = acc.astype(o_ref.dtype)

    return _call(body, name=name, out_shape=_sds((r, c), out_dtype), grid=(r // tr,),
                 in_specs=[pl.BlockSpec((s, tr, c), lambda i: (0, i, 0))],
                 out_specs=pl.BlockSpec((tr, c), lambda i: (i, 0)))(stacked)


def _sum_slots_layers(name, r0, r1):
    s, r, c = r0.shape
    tr = _pick(r, (256, 128))

    def body(a_ref, b_ref, o_ref):
        def total(ref):
            acc = ref[0].astype(F32)
            for k in range(1, s):
                acc = acc + ref[k].astype(F32)
            return acc

        @pl.when(pl.program_id(0) == 0)
        def _():
            o_ref[...] = total(a_ref)

        @pl.when(pl.program_id(0) == 1)
        def _():
            o_ref[...] = total(b_ref)

    return _call(body, name=name, out_shape=_sds((2, r, c), F32), grid=(2, r // tr),
                 in_specs=[pl.BlockSpec((s, tr, c), lambda l, i: (0, i * (1 - l), 0)),
                           pl.BlockSpec((s, tr, c), lambda l, i: (0, i * l, 0))],
                 out_specs=pl.BlockSpec((None, tr, c), lambda l, i: (l, i, 0)))(r0, r1)


def _adamw_math(w, g, m, v):
    m = ADAM_B1 * m + (1.0 - ADAM_B1) * g
    v = ADAM_B2 * v + (1.0 - ADAM_B2) * (g * g)
    m_hat = m / (1.0 - ADAM_B1 ** ADAM_STEP)
    v_hat = v / (1.0 - ADAM_B2 ** ADAM_STEP)
    delta = -ADAM_LR * (m_hat / (jnp.sqrt(v_hat) + ADAM_EPS) + ADAM_WD * w)
    return delta, m, v


def _adamw(name, w, m, v, g_mine, g_other):
    l, r, c = w.shape
    tr = _pick(r, (128, 64, 8))

    def body(w_ref, m_ref, v_ref, ga_ref, gb_ref, g_ref, d_ref, nm_ref, nv_ref):
        g = ga_ref[...] + gb_ref[...]
        delta, nm, nv = _adamw_math(w_ref[...], g, m_ref[...], v_ref[...])
        g_ref[...] = g
        d_ref[...] = delta
        nm_ref[...] = nm
        nv_ref[...] = nv

    spec = pl.BlockSpec((None, tr, c), lambda li, i: (li, i, 0))
    return _call(body, name=name, out_shape=[_sds(w.shape, F32)] * 4, grid=(l, r // tr),
                 in_specs=[spec] * 5, out_specs=[spec] * 4, vmem_mb=48)(w, m, v, g_mine, g_other)


def _adamw_flat(name, w, m, v, g):
    r, c = w.shape
    tr = _pick(r, (256, 128, 64, 8))

    def body(w_ref, m_ref, v_ref, g_ref, d_ref, nm_ref, nv_ref):
        delta, nm, nv = _adamw_math(w_ref[...], g_ref[...], m_ref[...], v_ref[...])
        d_ref[...] = delta
        nm_ref[...] = nm
        nv_ref[...] = nv

    spec = pl.BlockSpec((tr, c), lambda i: (i, 0))
    return _call(body, name=name, out_shape=[_sds(w.shape, F32)] * 3, grid=(r // tr,),
                 in_specs=[spec] * 4, out_specs=[spec] * 3)(w, m, v, g)


def _pack(arrs, row_multiple=256):
    flat = jnp.concatenate([a.reshape(-1).astype(F32) for a in arrs])
    per = BLOCK * row_multiple
    total = -(-flat.shape[0] // per) * per
    return jnp.pad(flat, (0, total - flat.shape[0])).reshape(total // BLOCK, BLOCK)


def _unpack(buf, shapes):
    flat = buf.reshape(-1)
    outs, off = [], 0
    for s in shapes:
        size = 1
        for d in s:
            size *= d
        outs.append(flat[off:off + size].reshape(s))
        off += size
    return outs


def kernel(x, meta_tokens, pre_mix_g, w_in, w_pool, pool_scale, w_dw, b_dw, conv_ln_g, conv_ln_b, w_pw, mix_out_g, w_out, post_mix_g, pre_ffn_g, w_gate, w_up, w_down, post_ffn_g, loss_target, m_meta_tokens, m_pre_mix_g, m_w_in, m_w_pool, m_pool_scale, m_w_dw, m_b_dw, m_conv_ln_g, m_conv_ln_b, m_w_pw, m_mix_out_g, m_w_out, m_post_mix_g, m_pre_ffn_g, m_w_gate, m_w_up, m_w_down, m_post_ffn_g, v_meta_tokens, v_pre_mix_g, v_w_in, v_w_pool, v_pool_scale, v_w_dw, v_b_dw, v_conv_ln_g, v_conv_ln_b, v_w_pw, v_mix_out_g, v_w_out, v_post_mix_g, v_pre_ffn_g, v_w_gate, v_w_up, v_w_down, v_post_ffn_g):
    seq = x.shape[1]
    lp = PAD + N_META + seq
    depth = w_in.shape[0]
    xy = 2 * lax.axis_index("x") + lax.axis_index("y")

    small_shapes = [meta_tokens.shape, w_dw.shape, w_pw.shape]
    small_local = _pack([meta_tokens, w_dw, w_pw], row_multiple=8)
    big_names = ["w_in", "w_out", "w_gate", "w_up", "w_down"]
    big_local = {(k, l): w[l:l + 1].astype(BF16)
                 for k, w in zip(big_names, (w_in, w_out, w_gate, w_up, w_down)) for l in range(depth)}
    wg = {}
    wg[("w_in", 0)], small_all = _gather_xy("gather_first", [big_local[("w_in", 0)], small_local])
    metas, wdws, wpws = [], [], []
    for s in range(N_SHARD):
        mt, wd, wp = _unpack(small_all[s], small_shapes)
        metas.append(mt)
        wdws.append(wd)
        wpws.append(wp)
    meta_full = jnp.concatenate(metas, axis=1)
    wdw_full = jnp.concatenate(wdws, axis=2)
    wpw_full = jnp.concatenate(wpws, axis=1)
    wdw_pad = jnp.pad(wdw_full, ((0, 0), (0, HALO - CONV_WIDTH), (0, 0)))
    wpw_bf = wpw_full.astype(BF16)
    wpool_bf = w_pool.reshape(depth, D_POOL, BLOCK).astype(BF16)

    row = lambda a, i: a[i][None, :]

    h = jnp.concatenate([jnp.zeros((PAD, D_MODEL), F32), meta_full, x[0]], axis=0)
    target = jnp.pad(loss_target[0], ((PAD + N_META, 0), (0, 0)))
    u = _rowwise("pre_mix_norm0", lambda hh, g: _rms(hh, g), [h], [row(pre_mix_g, 0)], [(D_MODEL, BF16)])[0]
    saved = []
    for i in range(depth):
        proj = _mm_nn_col("in_proj%d" % i, u, wg[("w_in", i)], 0, F32)
        ride = [(k, i) for k in big_names[1:]] + ([("w_in", i + 1)] if i + 1 < depth else [])
        o_attn, tot, got = _attn_fwd(proj, _Exchange("gather", [big_local[k] for k in ride], split=True))
        wg.update(zip(ride, _forward_core("forward_weights%d" % i, got)))
        o_pool = _pool_fwd(proj, wpool_bf[i], row(pool_scale, i))
        o_conv = _conv_fwd(proj, wdw_pad[i], row(b_dw, i), row(conv_ln_g, i), row(conv_ln_b, i), wpw_bf[i])

        mix_gains = [row(mix_out_g, i)[:, :D_ATTN], row(mix_out_g, i)[:, D_ATTN:D_ATTN + D_POOL],
                     row(mix_out_g, i)[:, D_ATTN + D_POOL:]]
        merged = _rowwise("merge%d" % i, _merge, [o_attn, o_pool, o_conv], mix_gains, [(D_MODEL, BF16)])[0]
        mix = _mm_nn_row("out_proj%d" % i, merged, wg[("w_out", i)], 0)

        def post_mix(hh, mx, g1, g2):
            h1 = hh + _rms(mx, g1)
            return h1, _rms(h1, g2)

        h1, u2 = _rowwise("post_mix%d" % i, post_mix, [h, mix], [row(post_mix_g, i), row(pre_ffn_g, i)],
                          [(D_MODEL, F32), (D_MODEL, BF16)])
        gate = _mm_nn_col("ffn_gate%d" % i, u2, wg[("w_gate", i)], 0, F32)
        up = _mm_nn_col("ffn_up%d" % i, u2, wg[("w_up", i)], 0, F32)
        act = _rowwise("swiglu%d" % i, lambda gt, p: gt * _sigmoid(gt) * p, [gate, up], [], [(D_FF, BF16)],
                       vmem_mb=48)[0]
        ff = _mm_nn_row("ffn_down%d" % i, act, wg[("w_down", i)], 0)
        rec = dict(h=h, u=u, proj=proj, tot=tot, o_attn=o_attn, o_pool=o_pool, o_conv=o_conv, merged=merged,
                   mix=mix, h1=h1, u2=u2, gate=gate, up=up, act=act, ff=ff)
        saved.append(rec)
        if i + 1 < depth:
            def post_ffn(hh, f, g1, g2):
                h2 = hh + _rms(f, g1)
                return h2, _rms(h2, g2)

            h, u = _rowwise("post_ffn%d" % i, post_ffn, [h1, ff], [row(post_ffn_g, i), row(pre_mix_g, i + 1)],
                            [(D_MODEL, F32), (D_MODEL, BF16)])
        else:
            def head(row0, hh, f, tgt, g1):
                y = hh + _rms(f, g1)
                rid = row0 + lax.broadcasted_iota(jnp.int32, (y.shape[0], 1), 0)
                err = jnp.where(rid >= PAD + N_META, y - tgt, 0.0)
                part = 0.5 * jnp.sum(jnp.mean(err * err, axis=-1, keepdims=True), axis=0, keepdims=True)
                return err * (1.0 / D_MODEL), jnp.broadcast_to(part, (8, BLOCK))

            dh, loss_part = _rowwise("loss_head", head, [h1, ff, target], [row(post_ffn_g, i)],
                                     [(D_MODEL, F32)], accs=[(8, BLOCK)], with_row0=True)

    loss = lax.psum(loss_part[0, 0], ("x", "y", "c"))

    small_grads = {}
    big_parts = {}
    received = {}
    for i in reversed(range(depth)):
        rec = saved[i]

        def post_ffn_b(f, d, g):
            _, vjp = jax.vjp(_rms, f, g)
            df, dg = vjp(d)
            return df, dg

        dff, g_post_ffn = _rowwise("post_ffn_b%d" % i, post_ffn_b, [rec["ff"], dh], [row(post_ffn_g, i)],
                                   [(D_MODEL, BF16)], accs=[(1, D_MODEL)])
        big_parts[("w_down", i)] = _mm_tn_row("dw_down%d" % i, rec["act"], dff)
        dact = _mm_nt_row("d_act%d" % i, dff, wg[("w_down", i)], 0, F32)

        def swiglu_b(gt, p, d):
            sg = _sigmoid(gt)
            return d * p * (sg * (1.0 + gt * (1.0 - sg))), d * (gt * sg)

        dgate, dup = _rowwise("swiglu_b%d" % i, swiglu_b, [rec["gate"], rec["up"], dact], [],
                              [(D_FF, BF16), (D_FF, BF16)], vmem_mb=56)
        big_parts[("w_gate", i)] = _mm_tn_col("dw_gate%d" % i, rec["u2"], dgate)
        big_parts[("w_up", i)] = _mm_tn_col("dw_up%d" % i, rec["u2"], dup)
        du2a = _mm_nt_col("d_u2_gate%d" % i, dgate, wg[("w_gate", i)], 0)
        du2b = _mm_nt_col("d_u2_up%d" % i, dup, wg[("w_up", i)], 0)

        def post_mix_b(h1v, mx, d, da, db, g1, g2):
            _, vjp2 = jax.vjp(_rms, h1v, g2)
            dh1, dg2 = vjp2(da + db)
            dmid = d + dh1
            _, vjp1 = jax.vjp(_rms, mx, g1)
            dmx, dg1 = vjp1(dmid)
            return dmid, dmx, dg1, dg2

        dmid, dmix, g_post_mix, g_pre_ffn = _rowwise(
            "post_mix_b%d" % i, post_mix_b, [rec["h1"], rec["mix"], dh, du2a, du2b],
            [row(post_mix_g, i), row(pre_ffn_g, i)], [(D_MODEL, F32), (D_MODEL, BF16)],
            accs=[(1, D_MODEL), (1, D_MODEL)], vmem_mb=48)
        big_parts[("w_out", i)] = _mm_tn_row("dw_out%d" % i, rec["merged"], dmix)
        dmerged = _mm_nt_row("d_merged%d" % i, dmix, wg[("w_out", i)], 0, F32)

        def merge_b(oa, op, oc, d, ga, gp, gc):
            _, vjp = jax.vjp(_merge, oa, op, oc, ga, gp, gc)
            return vjp(d)

        mix_gains = [row(mix_out_g, i)[:, :D_ATTN], row(mix_out_g, i)[:, D_ATTN:D_ATTN + D_POOL],
                     row(mix_out_g, i)[:, D_ATTN + D_POOL:]]
        do_attn, do_pool, do_conv, g_mo_a, g_mo_p, g_mo_c = _rowwise(
            "merge_b%d" % i, merge_b, [rec["o_attn"], rec["o_pool"], rec["o_conv"], dmerged], mix_gains,
            [(D_ATTN, F32), (D_POOL, F32), (D_CONV, F32)], accs=[(1, D_ATTN), (1, D_POOL), (1, D_CONV)])
        g_mix_out = jnp.concatenate([g_mo_a, g_mo_p, g_mo_c], axis=1)
        ride = [(k, i) for k in big_names[1:]] + ([("w_in", i + 1)] if i + 1 < depth else [])
        dq, dk, dv, got = _attn_bwd(rec["proj"], rec["tot"], do_attn,
                                    _Exchange("scatter", [big_parts[k] for k in ride]))
        received.update(zip(ride, got))
        du_pool, g_w_pool, g_pool_scale = _pool_bwd(rec["proj"], do_pool, wpool_bf[i], row(pool_scale, i))
        da, dgt, g_w_dw, g_b_dw, g_ln_g, g_ln_b, g_w_pw = _conv_bwd(
            rec["proj"], do_conv, wdw_pad[i], row(b_dw, i), row(conv_ln_g, i), row(conv_ln_b, i), wpw_bf[i])
        dproj = jnp.concatenate([dq, dk, dv, du_pool, da, dgt], axis=1)
        big_parts[("w_in", i)] = _mm_tn_col("dw_in%d" % i, rec["u"], dproj)
        du = _mm_nt_col("d_u%d" % i, dproj, wg[("w_in", i)], 0)

        def pre_mix_b(hv, d, dd, g):
            _, vjp = jax.vjp(_rms, hv, g)
            dhh, dg = vjp(dd)
            return d + dhh, dg

        dh, g_pre_mix = _rowwise("pre_mix_b%d" % i, pre_mix_b, [rec["h"], dmid, du], [row(pre_mix_g, i)],
                                 [(D_MODEL, F32)], accs=[(1, D_MODEL)])
        small_grads[i] = dict(pre_mix_g=g_pre_mix[0], w_pool=g_w_pool, pool_scale=g_pool_scale[0],
                              w_dw=g_w_dw[:CONV_WIDTH], b_dw=g_b_dw[0], conv_ln_g=g_ln_g[0], conv_ln_b=g_ln_b[0],
                              w_pw=g_w_pw, mix_out_g=g_mix_out[0], post_mix_g=g_post_mix[0],
                              pre_ffn_g=g_pre_ffn[0], post_ffn_g=g_post_ffn[0])

    grad_x = dh[PAD + N_META:][None]
    g_meta_part = dh[PAD:PAD + N_META]

    rep_names = ["pre_mix_g", "pool_scale", "b_dw", "conv_ln_g", "conv_ln_b", "mix_out_g", "post_mix_g",
                 "pre_ffn_g", "post_ffn_g", "w_pool"]
    stack2 = lambda nme: jnp.stack([small_grads[l][nme] for l in range(depth)])
    small_list = [stack2(nme) for nme in rep_names] + [g_meta_part, stack2("w_dw"), stack2("w_pw")]
    small_list[rep_names.index("w_pool")] = small_list[rep_names.index("w_pool")].reshape(w_pool.shape)
    full_shapes = [a.shape for a in small_list]
    packed = _pack(small_list)
    summed = _sum_slots("sum_small", _gather_all("gather_small_grads", packed))
    full = _unpack(summed, full_shapes)
    rep_grads = dict(zip(rep_names, full[:len(rep_names)]))
    g_meta = lax.dynamic_slice_in_dim(full[-3], xy * meta_tokens.shape[1], meta_tokens.shape[1], axis=1)
    g_w_dw = lax.dynamic_slice_in_dim(full[-2], xy * w_dw.shape[2], w_dw.shape[2], axis=2)
    g_w_pw = lax.dynamic_slice_in_dim(full[-1], xy * w_pw.shape[1], w_pw.shape[1], axis=1)

    rep_w = dict(pre_mix_g=pre_mix_g, pool_scale=pool_scale, b_dw=b_dw, conv_ln_g=conv_ln_g, conv_ln_b=conv_ln_b,
                 mix_out_g=mix_out_g, post_mix_g=post_mix_g, pre_ffn_g=pre_ffn_g, post_ffn_g=post_ffn_g,
                 w_pool=w_pool)
    rep_m = dict(pre_mix_g=m_pre_mix_g, pool_scale=m_pool_scale, b_dw=m_b_dw, conv_ln_g=m_conv_ln_g,
                 conv_ln_b=m_conv_ln_b, mix_out_g=m_mix_out_g, post_mix_g=m_post_mix_g, pre_ffn_g=m_pre_ffn_g,
                 post_ffn_g=m_post_ffn_g, w_pool=m_w_pool)
    rep_v = dict(pre_mix_g=v_pre_mix_g, pool_scale=v_pool_scale, b_dw=v_b_dw, conv_ln_g=v_conv_ln_g,
                 conv_ln_b=v_conv_ln_b, mix_out_g=v_mix_out_g, post_mix_g=v_post_mix_g, pre_ffn_g=v_pre_ffn_g,
                 post_ffn_g=v_post_ffn_g, w_pool=v_w_pool)
    sm_names = rep_names + ["meta_tokens", "w_dw", "w_pw"]
    sm_w = [rep_w[k] for k in rep_names] + [meta_tokens, w_dw, w_pw]
    sm_m = [rep_m[k] for k in rep_names] + [m_meta_tokens, m_w_dw, m_w_pw]
    sm_v = [rep_v[k] for k in rep_names] + [v_meta_tokens, v_w_dw, v_w_pw]
    sm_g = [rep_grads[k] for k in rep_names] + [g_meta, g_w_dw, g_w_pw]
    sm_shapes = [a.shape for a in sm_w]
    sm_delta, sm_nm, sm_nv = _adamw_flat("adamw_small", _pack(sm_w), _pack(sm_m), _pack(sm_v), _pack(sm_g))
    small_out = {}
    for k, g, d, nm, nv in zip(sm_names, sm_g, _unpack(sm_delta, sm_shapes), _unpack(sm_nm, sm_shapes),
                               _unpack(sm_nv, sm_shapes)):
        small_out[k] = (g, d, nm, nv)

    assert depth == 2
    received[("w_in", 0)] = _scatter_xy("scatter_last", [big_parts[("w_in", 0)]])[0]
    plane_sums = [_sum_slots_layers("sum_%s" % k, received[(k, 0)], received[(k, 1)]) for k in big_names]
    other_sums = _swap_core("swap_core_sums", plane_sums)
    big_w = dict(w_in=(w_in, m_w_in, v_w_in), w_out=(w_out, m_w_out, v_w_out), w_gate=(w_gate, m_w_gate, v_w_gate),
                 w_up=(w_up, m_w_up, v_w_up), w_down=(w_down, m_w_down, v_w_down))
    big_out = {}
    for j, k in enumerate(big_names):
        w, m, v = big_w[k]
        big_out[k] = _adamw("adamw_%s" % k, w, m, v, plane_sums[j], other_sums[j])

    order = ["meta_tokens", "pre_mix_g", "w_in", "w_pool", "pool_scale", "w_dw", "b_dw", "conv_ln_g", "conv_ln_b",
             "w_pw", "mix_out_g", "w_out", "post_mix_g", "pre_ffn_g", "w_gate", "w_up", "w_down", "post_ffn_g"]
    res = lambda k: big_out[k] if k in big_out else small_out[k]
    outs = [loss, grad_x]
    for part in range(4):
        outs += [res(k)[part] for k in order]
    return tuple(outs)
```

```python
import functools

import jax
import jax.numpy as jnp
from jax import lax
from jax.experimental import pallas as pl
from jax.experimental.pallas import tpu as pltpu

F32 = jnp.float32
BF16 = jnp.bfloat16

D_MODEL = 2048
N_META = 16
D_ATTN = 1024
D_POOL = 512
D_CONV = 512
POOL_WINDOWS = (2, 4, 8, 16)
CONV_WIDTH = 31
D_IN_PROJ = 3 * D_ATTN + D_POOL + 2 * D_CONV
D_FF = 5632
EPS = 1e-6
BLOCK = 128
PAD = BLOCK - N_META
HALO = 32
N_SHARD = 4
N_DEV = 8
MESH = pl.DeviceIdType.MESH

ADAM_LR = 0.001
ADAM_B1 = 0.9
ADAM_B2 = 0.999
ADAM_EPS = 1e-08
ADAM_WD = 0.01
ADAM_STEP = 10

COL_Q, COL_K, COL_V = 0, D_ATTN, 2 * D_ATTN
COL_POOL = 3 * D_ATTN
COL_A = COL_POOL + D_POOL
COL_G = COL_A + D_CONV


def _call(body, *, name, out_shape, grid=None, in_specs=None, out_specs=None, scratch=(), vmem_mb=None,
          aliases=None):
    params = {}
    if grid is not None:
        params["dimension_semantics"] = ("arbitrary",) * len(grid)
    if vmem_mb is not None:
        params["vmem_limit_bytes"] = vmem_mb << 20
    kw = dict(out_shape=out_shape, name=name, compiler_params=pltpu.CompilerParams(**params))
    if grid is not None:
        kw["grid"] = grid
    if in_specs is not None:
        kw["in_specs"] = in_specs
    if out_specs is not None:
        kw["out_specs"] = out_specs
    if scratch:
        kw["scratch_shapes"] = list(scratch)
    if aliases:
        kw["input_output_aliases"] = dict(aliases)
    return pl.pallas_call(body, **kw)


def _sds(shape, dtype):
    return jax.ShapeDtypeStruct(tuple(shape), dtype)


def _pick(n, candidates):
    for c in candidates:
        if n % c == 0:
            return c
    return n


def _rowwise(name, fn, rows_in, consts, outs, accs=(), tm=BLOCK, with_row0=False, vmem_mb=None):
    lp = rows_in[0].shape[0]
    n_in, n_c, n_o = len(rows_in), len(consts), len(outs)

    def body(*refs):
        vals = [r[...] for r in refs[:n_in + n_c]]
        if with_row0:
            vals = [pl.program_id(0) * tm] + vals
        res = fn(*vals)
        if not isinstance(res, (tuple, list)):
            res = (res,)
        o_refs = refs[n_in + n_c:n_in + n_c + n_o]
        a_refs = refs[n_in + n_c + n_o:]
        for r, v in zip(o_refs, res[:n_o]):
            r[...] = v.astype(r.dtype)
        if a_refs:
            @pl.when(pl.program_id(0) == 0)
            def _():
                for r in a_refs:
                    r[...] = jnp.zeros(r.shape, r.dtype)
            for r, v in zip(a_refs, res[n_o:]):
                r[...] += v.astype(r.dtype)

    in_specs = [pl.BlockSpec((tm, a.shape[1]), lambda i: (i, 0)) for a in rows_in]
    in_specs += [pl.BlockSpec(c.shape, lambda i: (0, 0)) for c in consts]
    out_specs = [pl.BlockSpec((tm, w), lambda i: (i, 0)) for (w, _) in outs]
    out_specs += [pl.BlockSpec(s, lambda i: (0, 0)) for s in accs]
    out_shape = [_sds((lp, w), dt) for (w, dt) in outs] + [_sds(s, F32) for s in accs]
    res = _call(body, name=name, out_shape=out_shape, grid=(lp // tm,), in_specs=in_specs,
                out_specs=out_specs, vmem_mb=vmem_mb)(*rows_in, *consts)
    return res


def _rms(x, g):
    return x * lax.rsqrt(jnp.mean(x * x, axis=-1, keepdims=True) + EPS) * g


def _merge(oa, op, oc, ga, gp, gc):
    return jnp.concatenate([_rms(oa, ga), _rms(op, gp), _rms(oc, gc)], axis=1)


def _colsum(x):
    return jnp.sum(x, axis=0, keepdims=True)


def _sigmoid(x):
    return 1.0 / (1.0 + jnp.exp(-x))


MM_VMEM_MB = 56


def _mm_tiles(lp):
    return _pick(lp, (1408, 384, 256, 128))


def _mm_nn_col(name, a, wg, layer, out_dtype):
    lp, k = a.shape
    n = wg.shape[3]
    tm = _mm_tiles(lp)

    def body(a_ref, w_ref, o_ref):
        o_ref[...] = jnp.dot(a_ref[...], w_ref[...], preferred_element_type=F32).astype(o_ref.dtype)

    return _call(body, name=name, out_shape=_sds((lp, N_SHARD * n), out_dtype), grid=(N_SHARD, lp // tm),
                 in_specs=[pl.BlockSpec((tm, k), lambda s, i: (i, 0)),
                           pl.BlockSpec((None, None, k, n), lambda s, i: (s, layer, 0, 0))],
                 out_specs=pl.BlockSpec((tm, n), lambda s, i: (i, s)), vmem_mb=MM_VMEM_MB)(a, wg)


def _mm_nn_row(name, a, wg, layer):
    lp = a.shape[0]
    k, n = wg.shape[2], wg.shape[3]
    tm = _mm_tiles(lp)

    def body(a_ref, w_ref, o_ref):
        part = jnp.dot(a_ref[...], w_ref[...], preferred_element_type=F32)

        @pl.when(pl.program_id(1) == 0)
        def _():
            o_ref[...] = part

        @pl.when(pl.program_id(1) != 0)
        def _():
            o_ref[...] += part

    return _call(body, name=name, out_shape=_sds((lp, n), F32), grid=(lp // tm, N_SHARD),
                 in_specs=[pl.BlockSpec((tm, k), lambda i, s: (i, s)),
                           pl.BlockSpec((None, None, k, n), lambda i, s: (s, layer, 0, 0))],
                 out_specs=pl.BlockSpec((tm, n), lambda i, s: (i, 0)), vmem_mb=MM_VMEM_MB)(a, wg)


_NT = (((1,), (1,)), ((), ()))
_TN = (((0,), (0,)), ((), ()))


def _mm_nt_col(name, dy, wg, layer):
    lp = dy.shape[0]
    k, n = wg.shape[2], wg.shape[3]
    tm = _mm_tiles(lp)

    def body(d_ref, w_ref, o_ref):
        part = lax.dot_general(d_ref[...], w_ref[...], _NT, preferred_element_type=F32)

        @pl.when(pl.program_id(1) == 0)
        def _():
            o_ref[...] = part

        @pl.when(pl.program_id(1) != 0)
        def _():
            o_ref[...] += part

    return _call(body, name=name, out_shape=_sds((lp, k), F32), grid=(lp // tm, N_SHARD),
                 in_specs=[pl.BlockSpec((tm, n), lambda i, s: (i, s)),
                           pl.BlockSpec((None, None, k, n), lambda i, s: (s, layer, 0, 0))],
                 out_specs=pl.BlockSpec((tm, k), lambda i, s: (i, 0)), vmem_mb=MM_VMEM_MB)(dy, wg)


def _mm_nt_row(name, dy, wg, layer, out_dtype):
    lp = dy.shape[0]
    k, n = wg.shape[2], wg.shape[3]
    tm = _mm_tiles(lp)

    def body(d_ref, w_ref, o_ref):
        o_ref[...] = lax.dot_general(d_ref[...], w_ref[...], _NT, preferred_element_type=F32).astype(o_ref.dtype)

    return _call(body, name=name, out_shape=_sds((lp, N_SHARD * k), out_dtype), grid=(N_SHARD, lp // tm),
                 in_specs=[pl.BlockSpec((tm, n), lambda s, i: (i, 0)),
                           pl.BlockSpec((None, None, k, n), lambda s, i: (s, layer, 0, 0))],
                 out_specs=pl.BlockSpec((tm, k), lambda s, i: (i, s)), vmem_mb=MM_VMEM_MB)(dy, wg)


def _mm_tn_col(name, a, dy):
    lp, k = a.shape
    n = dy.shape[1] // N_SHARD
    tm = _mm_tiles(lp)
    tk = _pick(k, (1024, 512))

    def body(a_ref, d_ref, o_ref, acc):
        @pl.when(pl.program_id(2) == 0)
        def _():
            acc[...] = jnp.zeros(acc.shape, F32)

        acc[...] += lax.dot_general(a_ref[...], d_ref[...], _TN, preferred_element_type=F32)

        @pl.when(pl.program_id(2) == pl.num_programs(2) - 1)
        def _():
            o_ref[...] = acc[...].astype(o_ref.dtype)

    return _call(body, name=name, out_shape=_sds((N_SHARD, k, n), BF16), grid=(N_SHARD, k // tk, lp // tm),
                 in_specs=[pl.BlockSpec((tm, tk), lambda s, kk, i: (i, kk)),
                           pl.BlockSpec((tm, n), lambda s, kk, i: (i, s))],
                 out_specs=pl.BlockSpec((None, tk, n), lambda s, kk, i: (s, kk, 0)),
                 scratch=[pltpu.VMEM((tk, n), F32)], vmem_mb=MM_VMEM_MB)(a, dy)


def _mm_tn_row(name, a, dy):
    lp = a.shape[0]
    k = a.shape[1] // N_SHARD
    n = dy.shape[1]
    tm = _mm_tiles(lp)
    tn = _pick(n, (1024, 512))

    def body(a_ref, d_ref, o_ref, acc):
        @pl.when(pl.program_id(2) == 0)
        def _():
            acc[...] = jnp.zeros(acc.shape, F32)

        acc[...] += lax.dot_general(a_ref[...], d_ref[...], _TN, preferred_element_type=F32)

        @pl.when(pl.program_id(2) == pl.num_programs(2) - 1)
        def _():
            o_ref[...] = acc[...].astype(o_ref.dtype)

    return _call(body, name=name, out_shape=_sds((N_SHARD, k, n), BF16), grid=(N_SHARD, n // tn, lp // tm),
                 in_specs=[pl.BlockSpec((tm, k), lambda s, j, i: (i, s)),
                           pl.BlockSpec((tm, tn), lambda s, j, i: (i, j))],
                 out_specs=pl.BlockSpec((None, k, tn), lambda s, j, i: (s, 0, j)),
                 scratch=[pltpu.VMEM((k, tn), F32)], vmem_mb=MM_VMEM_MB)(a, dy)


SUB = 16
WIDE = 2 * BLOCK
Z_CLAMP = 20.0


def _log1m_sigmoid(z):
    return -jnp.where(z > Z_CLAMP, z, jnp.log(1.0 + jnp.exp(jnp.minimum(z, Z_CLAMP))))


def _tri(tk, kind):
    r = lax.broadcasted_iota(jnp.int32, (tk, tk), 0)
    c = lax.broadcasted_iota(jnp.int32, (tk, tk), 1)
    t = {"gt": r > c, "le": r <= c}[kind]
    return jnp.where(t, 1.0, 0.0).astype(BF16)


def _strip_mask(kind, s, tk):
    if kind == "none":
        return None
    col = lax.broadcasted_iota(jnp.int32, (SUB, tk), 1)
    row = lax.broadcasted_iota(jnp.int32, (SUB, tk), 0)
    causal = (col - row) < s * SUB
    if kind == "diag":
        return causal
    if kind == "pad":
        return col >= PAD
    return causal & (col >= PAD)


def _attn_blocks(lp):
    nb = lp // BLOCK
    assert nb % 2 == 1, "sequence must be a 128-row block plus whole 256-row blocks"
    return nb, (nb + 1) // 2


class _Exchange:
    def __init__(self, kind, arrs, split=False):
        self.kind, self.arrs, self.n, self.split = kind, list(arrs), len(arrs), split

    def out_shape(self):
        if self.kind == "gather":
            return [_sds((N_SHARD,) + a.shape, a.dtype) for a in self.arrs]
        return [_sds(a.shape, a.dtype) for a in self.arrs]

    def scratch(self):
        sems = [pltpu.SemaphoreType.DMA((3 * self.n,)), pltpu.SemaphoreType.DMA((3 * self.n,)),
                pltpu.SemaphoreType.DMA((self.n,))]
        if self.split:
            sems += [pltpu.SemaphoreType.DMA((self.n,)), pltpu.SemaphoreType.DMA((self.n,))]
        return sems

    def forward(self, wait, outs, fsend, frecv):
        x, y, c = _place()
        for a in range(self.n):
            cp = pltpu.make_async_remote_copy(src_ref=outs[a], dst_ref=outs[a], send_sem=fsend.at[a],
                                              recv_sem=frecv.at[a], device_id=(x, y, 1 - c), device_id_type=MESH)
            if wait:
                pl.when(c == a % 2)(cp.wait_send)
                pl.when(c != a % 2)(cp.wait_recv)
            else:
                pl.when(c == a % 2)(cp.start)

    def copies(self, a, ins, outs, send, recv, local):
        x, y, c = _place()
        me = 2 * x + y
        own = ins[a] if self.kind == "gather" else ins[a].at[me]
        out = [pltpu.make_async_copy(own, outs[a].at[me], local.at[a])]
        for r, (px, py) in enumerate(_xy_peers(x, y)):
            src = ins[a] if self.kind == "gather" else ins[a].at[2 * px + py]
            out.append(pltpu.make_async_remote_copy(
                src_ref=src, dst_ref=outs[a].at[me], send_sem=send.at[3 * a + r], recv_sem=recv.at[3 * a + r],
                device_id=(px, py, c), device_id_type=MESH))
        return out

    def run(self, wait, ins, outs, send, recv, local):
        for a in range(self.n):
            def go(a=a):
                for cp in self.copies(a, ins, outs, send, recv, local):
                    if wait:
                        cp.wait()
                    else:
                        cp.start()
            if self.split:
                pl.when(lax.axis_index("c") == a % 2)(go)
            else:
                go()


def _attn_fwd(proj, exchange=None):
    lp = proj.shape[0]
    nb, nq = _attn_blocks(lp)
    n_pair = D_ATTN // BLOCK
    n_x = exchange.n if exchange else 0

    def body(*refs):
        q_ref, k_ref, v_ref = refs[:3]
        x_in = refs[3:3 + n_x]
        o_ref, tot_ref = refs[3 + n_x:5 + n_x]
        x_out = refs[5 + n_x:5 + 2 * n_x]
        qs, kb, vh, tri_l, tri_s, z_s, hl_s, c_s, w_s, r_s, acc_s = refs[5 + 2 * n_x:16 + 2 * n_x]
        x_sem = refs[16 + 2 * n_x:]
        p, i = pl.program_id(0), pl.program_id(1)
        m0 = lax.broadcasted_iota(jnp.int32, (1, BLOCK), 1) < (BLOCK // 2)

        if exchange:
            @pl.when((p == 0) & (i == 0))
            def _():
                exchange.run(False, x_in, x_out, *x_sem[:3])

            if exchange.split:
                @pl.when((p == n_pair - 2) & (i == 0))
                def _():
                    exchange.run(True, x_in, x_out, *x_sem[:3])
                    exchange.forward(False, x_out, *x_sem[3:])

        @pl.when(i == 0)
        def _():
            tri_l[...] = _tri(WIDE, "gt")
            tri_s[...] = _tri(BLOCK, "gt")

            def prep(b, carry):
                rows = pl.ds(pl.multiple_of(b * BLOCK, BLOCK), BLOCK)
                q = q_ref[rows, :] * 0.125
                v = v_ref[rows, :]
                qs[0, rows, :] = jnp.where(m0, q, 0.0).astype(BF16)
                qs[1, rows, :] = jnp.where(m0, 0.0, q).astype(BF16)
                kb[rows, :] = k_ref[rows, :].astype(BF16)
                vh[0, rows, :] = jnp.where(m0, v, 0.0).astype(BF16)
                vh[1, rows, :] = jnp.where(m0, 0.0, v).astype(BF16)
                return carry

            lax.fori_loop(0, nb, prep, 0)

        def tiles(q0, tq, specs):
            heads = [(t, h) for t in range(len(specs)) for h in range(2)]
            strips = [slice(s * SUB, (s + 1) * SUB) for s in range(tq // SUB)]
            for t, h in heads:
                k0, tk, _ = specs[t]
                z_s[t, h, 0:tq, 0:tk] = lax.dot_general(qs[h, pl.ds(q0, tq), :], kb[pl.ds(k0, tk), :], _NT,
                                                        preferred_element_type=F32)
            for t, h in heads:
                _, tk, kind = specs[t]
                for s, rows in enumerate(strips):
                    z = z_s[t, h, rows, 0:tk]
                    lnb = _log1m_sigmoid(z)
                    m = _strip_mask(kind, s, tk)
                    if m is not None:
                        lnb = jnp.where(m, lnb, 0.0)
                    z_s[t, h, rows, 0:tk] = z + lnb
                    hl_s[t, h, rows, 0:tk] = lnb.astype(BF16)
            for t, h in heads:
                _, tk, _ = specs[t]
                tri = tri_l if tk == WIDE else tri_s
                c_s[t, h, 0:tq, 0:tk] = jnp.dot(hl_s[t, h, 0:tq, 0:tk], tri[...], preferred_element_type=F32)
            for t, h in heads:
                _, tk, kind = specs[t]
                for s, rows in enumerate(strips):
                    r = r_s[h, rows, :]
                    c = c_s[t, h, rows, 0:tk]
                    rr = r if tk == BLOCK else jnp.concatenate([r, r], axis=1)
                    w = jnp.exp(z_s[t, h, rows, 0:tk] + c + rr)
                    m = _strip_mask(kind, s, tk)
                    if m is not None:
                        w = jnp.where(m, w, 0.0)
                    w_s[t, h, rows, 0:tk] = w.astype(BF16)
                    total = c[:, 0:1] + hl_s[t, h, rows, 0:BLOCK].astype(F32)[:, 0:1]
                    r_s[h, rows, :] = r + jnp.broadcast_to(total, (SUB, BLOCK))
            upd = None
            for t, h in heads:
                k0, tk, _ = specs[t]
                d = jnp.dot(w_s[t, h, 0:tq, 0:tk], vh[h, pl.ds(k0, tk), :], preferred_element_type=F32)
                upd = d if upd is None else upd + d
            acc_s[0:tq, :] += upd

        def finish(q0, tq):
            o_ref[pl.ds(q0, tq), :] = acc_s[0:tq, :]
            tot_ref[pl.ds(q0, tq), :] = jnp.where(m0, r_s[0, 0:tq, :], r_s[1, 0:tq, :])

        r_s[...] = jnp.zeros(r_s.shape, F32)
        acc_s[...] = jnp.zeros(acc_s.shape, F32)

        @pl.when(i == 0)
        def _():
            tiles(0, BLOCK, [(0, BLOCK, "first")])
            finish(0, BLOCK)

        @pl.when(i > 0)
        def _():
            q0 = pl.multiple_of(i * WIDE - BLOCK, BLOCK)
            key0 = lambda j: pl.multiple_of(j * WIDE - BLOCK, BLOCK)
            tiles(q0, WIDE, [(q0, WIDE, "diag")])

            def inner(n, carry):
                j = i - 1 - 2 * n
                tiles(q0, WIDE, [(key0(j), WIDE, "none"), (key0(j - 1), WIDE, "none")])
                return carry

            lax.fori_loop(0, (i - 1) // 2, inner, 0)

            @pl.when((i - 1) % 2 == 1)
            def _():
                tiles(q0, WIDE, [(key0(1), WIDE, "none"), (0, BLOCK, "pad")])

            @pl.when((i - 1) % 2 == 0)
            def _():
                tiles(q0, WIDE, [(0, BLOCK, "pad")])

            finish(q0, WIDE)

        if exchange:
            @pl.when((p == n_pair - 1) & (i == nq - 1))
            def _():
                if exchange.split:
                    exchange.forward(True, x_out, *x_sem[3:])
                else:
                    exchange.run(True, x_in, x_out, *x_sem)

    cq, ck, cv = COL_Q // BLOCK, COL_K // BLOCK, COL_V // BLOCK
    col = lambda c0: (lambda p, i: (0, c0 + p))
    scratch = [pltpu.VMEM((2, lp, BLOCK), BF16), pltpu.VMEM((lp, BLOCK), BF16), pltpu.VMEM((2, lp, BLOCK), BF16),
               pltpu.VMEM((WIDE, WIDE), BF16), pltpu.VMEM((BLOCK, BLOCK), BF16),
               pltpu.VMEM((2, 2, WIDE, WIDE), F32), pltpu.VMEM((2, 2, WIDE, WIDE), BF16),
               pltpu.VMEM((2, 2, WIDE, WIDE), F32), pltpu.VMEM((2, 2, WIDE, WIDE), BF16),
               pltpu.VMEM((2, WIDE, BLOCK), F32), pltpu.VMEM((WIDE, BLOCK), F32)]
    res = _call(body, name="attn_fwd",
                out_shape=[_sds((lp, D_ATTN), F32), _sds((lp, D_ATTN), F32)] + (exchange.out_shape() if exchange else []),
                grid=(n_pair, nq),
                in_specs=[pl.BlockSpec((lp, BLOCK), col(cq)), pl.BlockSpec((lp, BLOCK), col(ck)),
                          pl.BlockSpec((lp, BLOCK), col(cv))] + [_ANY] * n_x,
                out_specs=[pl.BlockSpec((lp, BLOCK), col(0)), pl.BlockSpec((lp, BLOCK), col(0))] + [_ANY] * n_x,
                scratch=scratch + (exchange.scratch() if exchange else []),
                vmem_mb=56)(proj, proj, proj, *(exchange.arrs if exchange else []))
    return res[0], res[1], list(res[2:])


def _attn_bwd(proj, tot, d_out, exchange=None):
    lp = proj.shape[0]
    nb, nq = _attn_blocks(lp)
    n_pair = D_ATTN // BLOCK
    n_x = exchange.n if exchange else 0
    n_s = 23

    def body(*refs):
        q_ref, k_ref, v_ref, tot_ref, do_ref = refs[:5]
        x_in = refs[5:5 + n_x]
        dq_ref, dk_ref, dv_ref = refs[5 + n_x:8 + n_x]
        x_out = refs[8 + n_x:8 + 2 * n_x]
        (qs, kb, kh, vb, doh, tge_l, tge_s, tle_l, tle_s, z_s, g_s, hl_s, c_s, gl_s, gc_s, w_s, dz_s,
         tot_s, a_s, b_s, dq_acc, dk_acc, dv_acc) = refs[8 + 2 * n_x:8 + 2 * n_x + n_s]
        x_sem = refs[8 + 2 * n_x + n_s:]
        p, i = pl.program_id(0), pl.program_id(1)
        m0 = lax.broadcasted_iota(jnp.int32, (1, BLOCK), 1) < (BLOCK // 2)

        if exchange:
            @pl.when((p == 0) & (i == 0))
            def _():
                exchange.run(False, x_in, x_out, *x_sem)

        @pl.when(i == 0)
        def _():
            tge_l[...] = _tri(WIDE, "gt")
            tge_s[...] = _tri(BLOCK, "gt")
            tle_l[...] = _tri(WIDE, "le")
            tle_s[...] = _tri(BLOCK, "le")

            def prep(b, carry):
                rows = pl.ds(pl.multiple_of(b * BLOCK, BLOCK), BLOCK)
                q = q_ref[rows, :] * 0.125
                k = k_ref[rows, :]
                do = do_ref[rows, :]
                qs[0, rows, :] = jnp.where(m0, q, 0.0).astype(BF16)
                qs[1, rows, :] = jnp.where(m0, 0.0, q).astype(BF16)
                kb[rows, :] = k.astype(BF16)
                kh[0, rows, :] = jnp.where(m0, k, 0.0).astype(BF16)
                kh[1, rows, :] = jnp.where(m0, 0.0, k).astype(BF16)
                vb[rows, :] = v_ref[rows, :].astype(BF16)
                doh[0, rows, :] = jnp.where(m0, do, 0.0).astype(BF16)
                doh[1, rows, :] = jnp.where(m0, 0.0, do).astype(BF16)
                dk_acc[rows, :] = jnp.zeros((BLOCK, BLOCK), F32)
                dv_acc[rows, :] = jnp.zeros((BLOCK, BLOCK), F32)
                return carry

            lax.fori_loop(0, nb, prep, 0)

        def wide(x, tk):
            return x if tk == BLOCK else jnp.concatenate([x, x], axis=1)

        def tiles(q0, tq, specs):
            heads = [(t, h) for t in range(len(specs)) for h in range(2)]
            strips = [slice(s * SUB, (s + 1) * SUB) for s in range(tq // SUB)]
            for t, h in heads:
                k0, tk, _ = specs[t]
                z_s[t, h, 0:tq, 0:tk] = lax.dot_general(qs[h, pl.ds(q0, tq), :], kb[pl.ds(k0, tk), :], _NT,
                                                        preferred_element_type=F32)
                g_s[t, h, 0:tq, 0:tk] = lax.dot_general(doh[h, pl.ds(q0, tq), :], vb[pl.ds(k0, tk), :], _NT,
                                                        preferred_element_type=F32)
            for t, h in heads:
                _, tk, kind = specs[t]
                for s, rows in enumerate(strips):
                    z = z_s[t, h, rows, 0:tk]
                    lnb = _log1m_sigmoid(z)
                    m = _strip_mask(kind, s, tk)
                    if m is not None:
                        lnb = jnp.where(m, lnb, 0.0)
                    z_s[t, h, rows, 0:tk] = z + lnb
                    hl_s[t, h, rows, 0:tk] = lnb.astype(BF16)
            for t, h in heads:
                _, tk, _ = specs[t]
                tri = tge_l if tk == WIDE else tge_s
                c_s[t, h, 0:tq, 0:tk] = jnp.dot(hl_s[t, h, 0:tq, 0:tk], tri[...], preferred_element_type=F32)
            for t, h in heads:
                _, tk, kind = specs[t]
                for s, rows in enumerate(strips):
                    c = c_s[t, h, rows, 0:tk]
                    total = c[:, 0:1] + hl_s[t, h, rows, 0:BLOCK].astype(F32)[:, 0:1]
                    a_next = a_s[h, rows, :] + jnp.broadcast_to(total, (SUB, BLOCK))
                    a_s[h, rows, :] = a_next
                    w = jnp.exp(z_s[t, h, rows, 0:tk] + c + wide(tot_s[h, rows, :] - a_next, tk))
                    m = _strip_mask(kind, s, tk)
                    if m is not None:
                        w = jnp.where(m, w, 0.0)
                    g = w * g_s[t, h, rows, 0:tk]
                    g_s[t, h, rows, 0:tk] = g
                    gl_s[t, h, rows, 0:tk] = g.astype(BF16)
                    w_s[t, h, rows, 0:tk] = w.astype(BF16)
            for t, h in heads:
                _, tk, _ = specs[t]
                tri = tle_l if tk == WIDE else tle_s
                gc_s[t, h, 0:tq, 0:tk] = jnp.dot(gl_s[t, h, 0:tq, 0:tk], tri[...], preferred_element_type=F32)
            for t, h in heads:
                _, tk, kind = specs[t]
                for s, rows in enumerate(strips):
                    gc = gc_s[t, h, rows, 0:tk]
                    b = b_s[h, rows, :]
                    sig = jnp.exp(z_s[t, h, rows, 0:tk])
                    dz = g_s[t, h, rows, 0:tk] - sig * (gc + wide(b, tk))
                    m = _strip_mask(kind, s, tk)
                    if m is not None:
                        dz = jnp.where(m, dz, 0.0)
                    dz_s[t, h, rows, 0:tk] = dz.astype(BF16)
                    b_s[h, rows, :] = b + jnp.broadcast_to(gc[:, tk - 1:tk], (SUB, BLOCK))
            upd = None
            for t, h in heads:
                k0, tk, _ = specs[t]
                d = jnp.dot(dz_s[t, h, 0:tq, 0:tk], kh[h, pl.ds(k0, tk), :], preferred_element_type=F32)
                upd = d if upd is None else upd + d
            dq_acc[0:tq, :] += upd
            for t, (k0, tk, _) in enumerate(specs):
                dk_acc[pl.ds(k0, tk), :] += (
                    lax.dot_general(dz_s[t, 0, 0:tq, 0:tk], qs[0, pl.ds(q0, tq), :], _TN, preferred_element_type=F32) +
                    lax.dot_general(dz_s[t, 1, 0:tq, 0:tk], qs[1, pl.ds(q0, tq), :], _TN, preferred_element_type=F32))
                dv_acc[pl.ds(k0, tk), :] += (
                    lax.dot_general(w_s[t, 0, 0:tq, 0:tk], doh[0, pl.ds(q0, tq), :], _TN, preferred_element_type=F32) +
                    lax.dot_general(w_s[t, 1, 0:tq, 0:tk], doh[1, pl.ds(q0, tq), :], _TN, preferred_element_type=F32))

        def start(q0, tq):
            tv = tot_ref[pl.ds(q0, tq), :]
            tot_s[0, 0:tq, :] = jnp.broadcast_to(tv[:, 0:1], (tq, BLOCK))
            tot_s[1, 0:tq, :] = jnp.broadcast_to(tv[:, BLOCK - 1:BLOCK], (tq, BLOCK))
            a_s[...] = jnp.zeros(a_s.shape, F32)
            b_s[...] = jnp.zeros(b_s.shape, F32)
            dq_acc[...] = jnp.zeros(dq_acc.shape, F32)

        def finish(q0, tq):
            dq_ref[pl.ds(q0, tq), :] = (dq_acc[0:tq, :] * 0.125).astype(dq_ref.dtype)

        @pl.when(i == 0)
        def _():
            start(0, BLOCK)
            tiles(0, BLOCK, [(0, BLOCK, "first")])
            finish(0, BLOCK)

        @pl.when(i > 0)
        def _():
            q0 = pl.multiple_of(i * WIDE - BLOCK, BLOCK)
            key0 = lambda j: pl.multiple_of(j * WIDE - BLOCK, BLOCK)
            odd = (i - 1) % 2
            start(q0, WIDE)

            @pl.when(odd == 1)
            def _():
                tiles(q0, WIDE, [(0, BLOCK, "pad"), (key0(1), WIDE, "none")])

            @pl.when(odd == 0)
            def _():
                tiles(q0, WIDE, [(0, BLOCK, "pad")])

            def inner(n, carry):
                j = 1 + odd + 2 * n
                tiles(q0, WIDE, [(key0(j), WIDE, "none"), (key0(j + 1), WIDE, "none")])
                return carry

            lax.fori_loop(0, (i - 1) // 2, inner, 0)
            tiles(q0, WIDE, [(q0, WIDE, "diag")])
            finish(q0, WIDE)

        @pl.when(i == nq - 1)
        def _():
            dk_ref[...] = dk_acc[...].astype(dk_ref.dtype)
            dv_ref[...] = dv_acc[...].astype(dv_ref.dtype)

        if exchange:
            @pl.when((p == n_pair - 1) & (i == nq - 1))
            def _():
                exchange.run(True, x_in, x_out, *x_sem)

    cq, ck, cv = COL_Q // BLOCK, COL_K // BLOCK, COL_V // BLOCK
    col = lambda c0: (lambda p, i: (0, c0 + p))
    whole = lambda c0: pl.BlockSpec((lp, BLOCK), col(c0))
    tile4 = lambda w, dt: pltpu.VMEM((2, 2, WIDE, w), dt)
    scratch = [pltpu.VMEM((2, lp, BLOCK), BF16), pltpu.VMEM((lp, BLOCK), BF16), pltpu.VMEM((2, lp, BLOCK), BF16),
               pltpu.VMEM((lp, BLOCK), BF16), pltpu.VMEM((2, lp, BLOCK), BF16),
               pltpu.VMEM((WIDE, WIDE), BF16), pltpu.VMEM((BLOCK, BLOCK), BF16),
               pltpu.VMEM((WIDE, WIDE), BF16), pltpu.VMEM((BLOCK, BLOCK), BF16),
               tile4(WIDE, F32), tile4(WIDE, F32), tile4(WIDE, BF16), tile4(WIDE, F32),
               tile4(WIDE, BF16), tile4(WIDE, F32), tile4(WIDE, BF16), tile4(WIDE, BF16),
               pltpu.VMEM((2, WIDE, BLOCK), F32), pltpu.VMEM((2, WIDE, BLOCK), F32), pltpu.VMEM((2, WIDE, BLOCK), F32),
               pltpu.VMEM((WIDE, BLOCK), F32), pltpu.VMEM((lp, BLOCK), F32), pltpu.VMEM((lp, BLOCK), F32)]
    res = _call(body, name="attn_bwd",
                out_shape=[_sds((lp, D_ATTN), BF16)] * 3 + (exchange.out_shape() if exchange else []),
                grid=(n_pair, nq),
                in_specs=[whole(cq), whole(ck), whole(cv), whole(0), whole(0)] + [_ANY] * n_x,
                out_specs=[whole(0), whole(0), whole(0)] + [_ANY] * n_x,
                scratch=scratch + (exchange.scratch() if exchange else []),
                vmem_mb=60)(proj, proj, proj, tot, d_out, *(exchange.arrs if exchange else []))
    return res[0], res[1], res[2], list(res[3:])


def _shift_down(x, d):
    return x if d == 0 else pltpu.roll(x, d, 0)


def _shift_up(x, d):
    return x if d == 0 else pltpu.roll(x, x.shape[0] - d, 0)


def _pool_windows(ext, down):
    shift = _shift_down if down else _shift_up
    outs = []
    for g, w in enumerate(POOL_WINDOWS):
        s = ext[:, g * BLOCK:(g + 1) * BLOCK]
        d = 1
        while d < w:
            s = s + shift(s, d)
            d *= 2
        outs.append(s)
    return jnp.concatenate(outs, axis=1)


def _pool_counts(pos):
    cols = [jnp.broadcast_to(jnp.clip(pos + 1, 1, w).astype(F32), (pos.shape[0], BLOCK)) for w in POOL_WINDOWS]
    return jnp.concatenate(cols, axis=1)


def _group_dot(x, w_ref, transpose):
    outs = []
    for g in range(len(POOL_WINDOWS)):
        xg = x[:, g * BLOCK:(g + 1) * BLOCK].astype(BF16)
        wg = w_ref[g * BLOCK:(g + 1) * BLOCK, :]
        if transpose:
            outs.append(lax.dot_general(xg, wg, _NT, preferred_element_type=F32))
        else:
            outs.append(jnp.dot(xg, wg, preferred_element_type=F32))
    return jnp.concatenate(outs, axis=1)


def _pooled(prev, cur, r):
    rows = cur.shape[0]
    ext = jnp.concatenate([prev[rows - HALO:], cur], axis=0)
    pos = r * rows + lax.broadcasted_iota(jnp.int32, (rows, 1), 0) - PAD
    ws = _pool_windows(ext, down=True)[HALO:]
    return jnp.where(pos >= 0, ws / _pool_counts(pos) - cur, 0.0)


def _pool_fwd(proj, w_pool_bf, scale):
    lp = proj.shape[0]
    rows = BLOCK
    cb = COL_POOL // D_POOL

    def body(prev_ref, cur_ref, w_ref, s_ref, o_ref):
        pooled = _pooled(prev_ref[...], cur_ref[...], pl.program_id(0))
        o_ref[...] = _group_dot(pooled, w_ref, False) * s_ref[...]

    return _call(body, name="pool_fwd", out_shape=_sds((lp, D_POOL), F32), grid=(lp // rows,),
                 in_specs=[pl.BlockSpec((rows, D_POOL), lambda r: (jnp.maximum(r - 1, 0), cb)),
                           pl.BlockSpec((rows, D_POOL), lambda r: (r, cb)),
                           pl.BlockSpec((D_POOL, BLOCK), lambda r: (0, 0)),
                           pl.BlockSpec((1, D_POOL), lambda r: (0, 0))],
                 out_specs=pl.BlockSpec((rows, D_POOL), lambda r: (r, 0)))(proj, proj, w_pool_bf, scale)


def _pool_bwd(proj, d_out, w_pool_bf, scale):
    lp = proj.shape[0]
    rows = BLOCK
    n_chunk = lp // rows
    cb = COL_POOL // D_POOL

    def body(prev_ref, cur_ref, do_ref, don_ref, w_ref, s_ref, du_ref, dw_ref, ds_ref):
        r = pl.program_id(0)

        @pl.when(r == 0)
        def _():
            dw_ref[...] = jnp.zeros(dw_ref.shape, F32)
            ds_ref[...] = jnp.zeros(ds_ref.shape, F32)

        pooled = _pooled(prev_ref[...], cur_ref[...], r)
        d_ext = jnp.concatenate([do_ref[...], don_ref[0:HALO]], axis=0)
        pos = r * rows + lax.broadcasted_iota(jnp.int32, (rows + HALO, 1), 0) - PAD
        dmixed = jnp.where((pos >= 0) & (pos < lp - PAD), d_ext * s_ref[...], 0.0)
        dpooled = _group_dot(dmixed, w_ref, True)
        back = _pool_windows(dpooled / _pool_counts(pos), down=False)[0:rows]
        du = jnp.where(pos[0:rows] >= 0, back - dpooled[0:rows], 0.0)
        du_ref[...] = du.astype(du_ref.dtype)
        mixed = _group_dot(pooled, w_ref, False)
        ds_ref[...] += _colsum(do_ref[...] * mixed)
        pooled_bf = pooled.astype(BF16)
        dm_bf = dmixed[0:rows].astype(BF16)
        for g in range(len(POOL_WINDOWS)):
            sl = slice(g * BLOCK, (g + 1) * BLOCK)
            dw_ref[sl, :] += lax.dot_general(pooled_bf[:, sl], dm_bf[:, sl], _TN, preferred_element_type=F32)

    return _call(body, name="pool_bwd",
                 out_shape=[_sds((lp, D_POOL), BF16), _sds((D_POOL, BLOCK), F32), _sds((1, D_POOL), F32)],
                 grid=(n_chunk,),
                 in_specs=[pl.BlockSpec((rows, D_POOL), lambda r: (jnp.maximum(r - 1, 0), cb)),
                           pl.BlockSpec((rows, D_POOL), lambda r: (r, cb)),
                           pl.BlockSpec((rows, D_POOL), lambda r: (r, 0)),
                           pl.BlockSpec((rows, D_POOL), lambda r: (jnp.minimum(r + 1, n_chunk - 1), 0)),
                           pl.BlockSpec((D_POOL, BLOCK), lambda r: (0, 0)),
                           pl.BlockSpec((1, D_POOL), lambda r: (0, 0))],
                 out_specs=[pl.BlockSpec((rows, D_POOL), lambda r: (r, 0)),
                            pl.BlockSpec((D_POOL, BLOCK), lambda r: (0, 0)),
                            pl.BlockSpec((1, D_POOL), lambda r: (0, 0))])(proj, proj, d_out, d_out, w_pool_bf, scale)


def _conv_taps(u, wdw_ref):
    y = wdw_ref[CONV_WIDTH - 1:CONV_WIDTH, :] * u
    for d in range(1, CONV_WIDTH):
        y = y + wdw_ref[CONV_WIDTH - 1 - d:CONV_WIDTH - d, :] * _shift_down(u, d)
    return y


def _layernorm_stats(y):
    mu = jnp.mean(y, axis=-1, keepdims=True)
    yc = y - mu
    rstd = lax.rsqrt(jnp.mean(yc * yc, axis=-1, keepdims=True) + EPS)
    return yc * rstd, rstd


def _conv_fwd(proj, wdw, bdw, ln_g, ln_b, wpw_bf):
    lp = proj.shape[0]
    rows = BLOCK
    ca, cg = COL_A // D_CONV, COL_G // D_CONV

    def body(ap_ref, a_ref, gp_ref, g_ref, wdw_ref, b_ref, lg_ref, lb_ref, wpw_ref, o_ref):
        r = pl.program_id(0)
        a = jnp.concatenate([ap_ref[rows - HALO:rows], a_ref[...]], axis=0)
        g = jnp.concatenate([gp_ref[rows - HALO:rows], g_ref[...]], axis=0)
        u = a * _sigmoid(g)
        y = _conv_taps(u, wdw_ref)[HALO:] + b_ref[...]
        xhat, _ = _layernorm_stats(y)
        yn = xhat * lg_ref[...] + lb_ref[...]
        pos = r * rows + lax.broadcasted_iota(jnp.int32, (rows, 1), 0) - PAD
        s = jnp.where(pos >= 0, yn * _sigmoid(yn), 0.0)
        o_ref[...] = jnp.dot(s.astype(BF16), wpw_ref[...], preferred_element_type=F32)

    prev = lambda c: (lambda r: (jnp.maximum(r - 1, 0), c))
    cur = lambda c: (lambda r: (r, c))
    const = lambda shape: pl.BlockSpec(shape, lambda r: (0, 0))
    return _call(body, name="conv_fwd", out_shape=_sds((lp, D_CONV), F32), grid=(lp // rows,),
                 in_specs=[pl.BlockSpec((rows, D_CONV), prev(ca)), pl.BlockSpec((rows, D_CONV), cur(ca)),
                           pl.BlockSpec((rows, D_CONV), prev(cg)), pl.BlockSpec((rows, D_CONV), cur(cg)),
                           const((HALO, D_CONV)), const((1, D_CONV)), const((1, D_CONV)), const((1, D_CONV)),
                           const((D_CONV, D_CONV))],
                 out_specs=pl.BlockSpec((rows, D_CONV), cur(0)))(proj, proj, proj, proj, wdw, bdw, ln_g, ln_b, wpw_bf)


def _conv_bwd(proj, d_out, wdw, bdw, ln_g, ln_b, wpw_bf):
    lp = proj.shape[0]
    rows = BLOCK
    n_chunk = lp // rows
    ca, cg = COL_A // D_CONV, COL_G // D_CONV
    ext = rows + HALO

    def body(ap_ref, a_ref, an_ref, gp_ref, g_ref, gn_ref, do_ref, don_ref, wdw_ref, b_ref, lg_ref, lb_ref,
             wpw_ref, da_ref, dg_ref, dwdw_ref, db_ref, dlg_ref, dlb_ref, dwpw_ref):
        r = pl.program_id(0)

        @pl.when(r == 0)
        def _():
            for ref in (dwdw_ref, db_ref, dlg_ref, dlb_ref, dwpw_ref):
                ref[...] = jnp.zeros(ref.shape, F32)

        a3 = jnp.concatenate([ap_ref[rows - HALO:rows], a_ref[...], an_ref[0:HALO]], axis=0)
        g3 = jnp.concatenate([gp_ref[rows - HALO:rows], g_ref[...], gn_ref[0:HALO]], axis=0)
        sig3 = _sigmoid(g3)
        u3 = a3 * sig3
        y = _conv_taps(u3, wdw_ref)[HALO:] + b_ref[...]
        xhat, rstd = _layernorm_stats(y)
        yn = xhat * lg_ref[...] + lb_ref[...]
        sgm = _sigmoid(yn)
        pos = r * rows + lax.broadcasted_iota(jnp.int32, (ext, 1), 0) - PAD
        valid = (pos >= 0) & (pos < lp - PAD)
        d_ext = jnp.concatenate([do_ref[...], don_ref[0:HALO]], axis=0)
        ds = lax.dot_general(d_ext.astype(BF16), wpw_ref[...], _NT, preferred_element_type=F32)
        dyn = jnp.where(valid, ds * (sgm * (1.0 + yn * (1.0 - sgm))), 0.0)
        dxh = dyn * lg_ref[...]
        dy = rstd * (dxh - jnp.mean(dxh, axis=-1, keepdims=True)
                     - xhat * jnp.mean(dxh * xhat, axis=-1, keepdims=True))
        s_cur = jnp.where(valid[0:rows], (yn * sgm)[0:rows], 0.0)
        dwpw_ref[...] += lax.dot_general(s_cur.astype(BF16), do_ref[...].astype(BF16), _TN,
                                         preferred_element_type=F32)
        dlg_ref[...] += _colsum(dyn[0:rows] * xhat[0:rows])
        dlb_ref[...] += _colsum(dyn[0:rows])
        dy_cur = dy[0:rows]
        db_ref[...] += _colsum(dy_cur)
        du = wdw_ref[CONV_WIDTH - 1:CONV_WIDTH, :] * dy
        for d in range(CONV_WIDTH):
            k = CONV_WIDTH - 1 - d
            dwdw_ref[k:k + 1, :] += _colsum(dy_cur * _shift_down(u3, d)[HALO:HALO + rows])
            if d:
                du = du + wdw_ref[k:k + 1, :] * _shift_up(dy, d)
        du = jnp.where(pos[0:rows] >= 0, du[0:rows], 0.0)
        sig = sig3[HALO:HALO + rows]
        da_ref[...] = (du * sig).astype(da_ref.dtype)
        dg_ref[...] = (du * a_ref[...] * sig * (1.0 - sig)).astype(dg_ref.dtype)

    prev = lambda c: (lambda r: (jnp.maximum(r - 1, 0), c))
    cur = lambda c: (lambda r: (r, c))
    nxt = lambda c: (lambda r: (jnp.minimum(r + 1, n_chunk - 1), c))
    const = lambda shape: pl.BlockSpec(shape, lambda r: (0, 0))
    blk = lambda f: pl.BlockSpec((rows, D_CONV), f)
    return _call(body, name="conv_bwd",
                 out_shape=[_sds((lp, D_CONV), BF16), _sds((lp, D_CONV), BF16), _sds((HALO, D_CONV), F32),
                            _sds((1, D_CONV), F32), _sds((1, D_CONV), F32), _sds((1, D_CONV), F32),
                            _sds((D_CONV, D_CONV), F32)],
                 grid=(n_chunk,),
                 in_specs=[blk(prev(ca)), blk(cur(ca)), blk(nxt(ca)), blk(prev(cg)), blk(cur(cg)), blk(nxt(cg)),
                           blk(cur(0)), blk(nxt(0)),
                           const((HALO, D_CONV)), const((1, D_CONV)), const((1, D_CONV)), const((1, D_CONV)),
                           const((D_CONV, D_CONV))],
                 out_specs=[blk(cur(0)), blk(cur(0)), const((HALO, D_CONV)), const((1, D_CONV)),
                            const((1, D_CONV)), const((1, D_CONV)), const((D_CONV, D_CONV))],
                 vmem_mb=48)(proj, proj, proj, proj, proj, proj, d_out, d_out, wdw, bdw, ln_g, ln_b, wpw_bf)


_ANY = pl.BlockSpec(memory_space=pl.ANY)


def _place():
    return lax.axis_index("x"), lax.axis_index("y"), lax.axis_index("c")


def _xy_peers(x, y):
    return [(1 - x, y), (x, 1 - y), (1 - x, 1 - y)]


def _gather_xy(name, arrs):
    n = len(arrs)

    def body(*refs):
        ins, outs = refs[:n], refs[n:2 * n]
        send, recv, local = refs[2 * n:]
        x, y, c = _place()
        me = 2 * x + y
        copies = []
        for a in range(n):
            cp = pltpu.make_async_copy(ins[a], outs[a].at[me], local.at[a])
            cp.start()
            copies.append(cp)
        remote = []
        for a in range(n):
            for r, (px, py) in enumerate(_xy_peers(x, y)):
                cp = pltpu.make_async_remote_copy(src_ref=ins[a], dst_ref=outs[a].at[me],
                                                  send_sem=send.at[3 * a + r], recv_sem=recv.at[3 * a + r],
                                                  device_id=(px, py, c), device_id_type=MESH)
                cp.start()
                remote.append(cp)
        for cp in remote:
            cp.wait()
        for cp in copies:
            cp.wait()

    return _call(body, name=name, out_shape=[_sds((N_SHARD,) + a.shape, a.dtype) for a in arrs],
                 in_specs=[_ANY] * n, out_specs=[_ANY] * n,
                 scratch=[pltpu.SemaphoreType.DMA((3 * n,)), pltpu.SemaphoreType.DMA((3 * n,)),
                          pltpu.SemaphoreType.DMA((n,))])(*arrs)


def _scatter_xy(name, parts):
    n = len(parts)

    def body(*refs):
        ins, outs = refs[:n], refs[n:2 * n]
        send, recv, local = refs[2 * n:]
        x, y, c = _place()
        me = 2 * x + y
        copies = []
        for a in range(n):
            cp = pltpu.make_async_copy(ins[a].at[me], outs[a].at[me], local.at[a])
            cp.start()
            copies.append(cp)
        remote = []
        for a in range(n):
            for r, (px, py) in enumerate(_xy_peers(x, y)):
                cp = pltpu.make_async_remote_copy(src_ref=ins[a].at[2 * px + py], dst_ref=outs[a].at[me],
                                                  send_sem=send.at[3 * a + r], recv_sem=recv.at[3 * a + r],
                                                  device_id=(px, py, c), device_id_type=MESH)
                cp.start()
                remote.append(cp)
        for cp in remote:
            cp.wait()
        for cp in copies:
            cp.wait()

    return _call(body, name=name, out_shape=[_sds(a.shape, a.dtype) for a in parts],
                 in_specs=[_ANY] * n, out_specs=[_ANY] * n,
                 scratch=[pltpu.SemaphoreType.DMA((3 * n,)), pltpu.SemaphoreType.DMA((3 * n,)),
                          pltpu.SemaphoreType.DMA((n,))])(*parts)


def _scatter_and_swap(name, parts, arrs):
    ex = _Exchange("scatter", parts)
    n, m = ex.n, len(arrs)

    def body(*refs):
        p_in, a_in = refs[:n], refs[n:n + m]
        p_out, a_out = refs[n + m:2 * n + m], refs[2 * n + m:2 * (n + m)]
        send, recv, local, s_send, s_recv = refs[2 * (n + m):]
        x, y, c = _place()
        ex.run(False, p_in, p_out, send, recv, local)
        swaps = [pltpu.make_async_remote_copy(src_ref=a_in[a], dst_ref=a_out[a], send_sem=s_send.at[a],
                                              recv_sem=s_recv.at[a], device_id=(x, y, 1 - c), device_id_type=MESH)
                 for a in range(m)]
        for cp in swaps:
            cp.start()
        for cp in swaps:
            cp.wait()
        ex.run(True, p_in, p_out, send, recv, local)

    res = _call(body, name=name, out_shape=ex.out_shape() + [_sds(a.shape, a.dtype) for a in arrs],
                in_specs=[_ANY] * (n + m), out_specs=[_ANY] * (n + m),
                scratch=ex.scratch() + [pltpu.SemaphoreType.DMA((m,)), pltpu.SemaphoreType.DMA((m,))])(*parts, *arrs)
    return list(res[:n]), list(res[n:])


def _swap_core(name, arrs):
    n = len(arrs)

    def body(*refs):
        ins, outs = refs[:n], refs[n:2 * n]
        send, recv = refs[2 * n:]
        x, y, c = _place()
        remote = []
        for a in range(n):
            cp = pltpu.make_async_remote_copy(src_ref=ins[a], dst_ref=outs[a], send_sem=send.at[a],
                                              recv_sem=recv.at[a], device_id=(x, y, 1 - c), device_id_type=MESH)
            cp.start()
            remote.append(cp)
        for cp in remote:
            cp.wait()

    return _call(body, name=name, out_shape=[_sds(a.shape, a.dtype) for a in arrs],
                 in_specs=[_ANY] * n, out_specs=[_ANY] * n,
                 scratch=[pltpu.SemaphoreType.DMA((n,)), pltpu.SemaphoreType.DMA((n,))])(*arrs)


def _gather_all(name, arr):
    flips = [(fx, fy, fc) for fx in (0, 1) for fy in (0, 1) for fc in (0, 1)][1:]

    def body(in_ref, out_ref, send, recv, local):
        x, y, c = _place()
        me = 4 * x + 2 * y + c
        own = pltpu.make_async_copy(in_ref, out_ref.at[me], local)
        own.start()
        remote = []
        for k, (fx, fy, fc) in enumerate(flips):
            peer = (1 - x if fx else x, 1 - y if fy else y, 1 - c if fc else c)
            cp = pltpu.make_async_remote_copy(src_ref=in_ref, dst_ref=out_ref.at[me], send_sem=send.at[k],
                                              recv_sem=recv.at[k], device_id=peer, device_id_type=MESH)
            cp.start()
            remote.append(cp)
        for cp in remote:
            cp.wait()
        own.wait()

    return _call(body, name=name, out_shape=_sds((N_DEV,) + arr.shape, arr.dtype), in_specs=[_ANY], out_specs=_ANY,
                 scratch=[pltpu.SemaphoreType.DMA((N_DEV - 1,)), pltpu.SemaphoreType.DMA((N_DEV - 1,)),
                          pltpu.SemaphoreType.DMA])(arr)


def _sum_slots(name, stacked, out_dtype=F32):
    s, r, c = stacked.shape
    tr = _pick(r, (256, 128, 64, 8))

    def body(in_ref, o_ref):
        acc = in_ref[0].astype(F32)
        for k in range(1, s):
            acc = acc + in_ref[k].astype(F32)
        o_ref[...] = acc.astype(o_ref.dtype)

    return _call(body, name=name, out_shape=_sds((r, c), out_dtype), grid=(r // tr,),
                 in_specs=[pl.BlockSpec((s, tr, c), lambda i: (0, i, 0))],
                 out_specs=pl.BlockSpec((tr, c), lambda i: (i, 0)))(stacked)


def _sum_slots_layers(name, r0, r1):
    s, r, c = r0.shape
    tr = _pick(r, (256, 128))

    def body(a_ref, b_ref, o_ref):
        def total(ref):
            acc = ref[0].astype(F32)
            for k in range(1, s):
                acc = acc + ref[k].astype(F32)
            return acc

        @pl.when(pl.program_id(0) == 0)
        def _():
            o_ref[...] = total(a_ref)

        @pl.when(pl.program_id(0) == 1)
        def _():
            o_ref[...] = total(b_ref)

    return _call(body, name=name, out_shape=_sds((2, r, c), F32), grid=(2, r // tr),
                 in_specs=[pl.BlockSpec((s, tr, c), lambda l, i: (0, i * (1 - l), 0)),
                           pl.BlockSpec((s, tr, c), lambda l, i: (0, i * l, 0))],
                 out_specs=pl.BlockSpec((None, tr, c), lambda l, i: (l, i, 0)))(r0, r1)


def _adamw_math(w, g, m, v):
    m = ADAM_B1 * m + (1.0 - ADAM_B1) * g
    v = ADAM_B2 * v + (1.0 - ADAM_B2) * (g * g)
    m_hat = m / (1.0 - ADAM_B1 ** ADAM_STEP)
    v_hat = v / (1.0 - ADAM_B2 ** ADAM_STEP)
    delta = -ADAM_LR * (m_hat / (jnp.sqrt(v_hat) + ADAM_EPS) + ADAM_WD * w)
    return delta, m, v


def _adamw(name, w, m, v, g_mine, g_other):
    l, r, c = w.shape
    tr = _pick(r, (128, 64, 8))

    def body(w_ref, m_ref, v_ref, ga_ref, gb_ref, g_ref, d_ref, nm_ref, nv_ref):
        g = ga_ref[...] + gb_ref[...]
        delta, nm, nv = _adamw_math(w_ref[...], g, m_ref[...], v_ref[...])
        g_ref[...] = g
        d_ref[...] = delta
        nm_ref[...] = nm
        nv_ref[...] = nv

    spec = pl.BlockSpec((None, tr, c), lambda li, i: (li, i, 0))
    return _call(body, name=name, out_shape=[_sds(w.shape, F32)] * 4, grid=(l, r // tr),
                 in_specs=[spec] * 5, out_specs=[spec] * 4, vmem_mb=48)(w, m, v, g_mine, g_other)


def _adamw_flat(name, w, m, v, g):
    r, c = w.shape
    tr = _pick(r, (256, 128, 64, 8))

    def body(w_ref, m_ref, v_ref, g_ref, d_ref, nm_ref, nv_ref):
        delta, nm, nv = _adamw_math(w_ref[...], g_ref[...], m_ref[...], v_ref[...])
        d_ref[...] = delta
        nm_ref[...] = nm
        nv_ref[...] = nv

    spec = pl.BlockSpec((tr, c), lambda i: (i, 0))
    return _call(body, name=name, out_shape=[_sds(w.shape, F32)] * 3, grid=(r // tr,),
                 in_specs=[spec] * 4, out_specs=[spec] * 3)(w, m, v, g)


def _pack(arrs, row_multiple=256):
    flat = jnp.concatenate([a.reshape(-1).astype(F32) for a in arrs])
    per = BLOCK * row_multiple
    total = -(-flat.shape[0] // per) * per
    return jnp.pad(flat, (0, total - flat.shape[0])).reshape(total // BLOCK, BLOCK)


def _unpack(buf, shapes):
    flat = buf.reshape(-1)
    outs, off = [], 0
    for s in shapes:
        size = 1
        for d in s:
            size *= d
        outs.append(flat[off:off + size].reshape(s))
        off += size
    return outs


def kernel(x, meta_tokens, pre_mix_g, w_in, w_pool, pool_scale, w_dw, b_dw, conv_ln_g, conv_ln_b, w_pw, mix_out_g, w_out, post_mix_g, pre_ffn_g, w_gate, w_up, w_down, post_ffn_g, loss_target, m_meta_tokens, m_pre_mix_g, m_w_in, m_w_pool, m_pool_scale, m_w_dw, m_b_dw, m_conv_ln_g, m_conv_ln_b, m_w_pw, m_mix_out_g, m_w_out, m_post_mix_g, m_pre_ffn_g, m_w_gate, m_w_up, m_w_down, m_post_ffn_g, v_meta_tokens, v_pre_mix_g, v_w_in, v_w_pool, v_pool_scale, v_w_dw, v_b_dw, v_conv_ln_g, v_conv_ln_b, v_w_pw, v_mix_out_g, v_w_out, v_post_mix_g, v_pre_ffn_g, v_w_gate, v_w_up, v_w_down, v_post_ffn_g):
    seq = x.shape[1]
    lp = PAD + N_META + seq
    depth = w_in.shape[0]
    xy = 2 * lax.axis_index("x") + lax.axis_index("y")

    small_shapes = [meta_tokens.shape, w_dw.shape, w_pw.shape]
    small_local = _pack([meta_tokens, w_dw, w_pw], row_multiple=8)
    big_names = ["w_in", "w_out", "w_gate", "w_up", "w_down"]
    big_local = {(k, l): w[l:l + 1].astype(BF16)
                 for k, w in zip(big_names, (w_in, w_out, w_gate, w_up, w_down)) for l in range(depth)}
    wg = {}
    wg[("w_in", 0)], small_all = _gather_xy("gather_first", [big_local[("w_in", 0)], small_local])
    metas, wdws, wpws = [], [], []
    for s in range(N_SHARD):
        mt, wd, wp = _unpack(small_all[s], small_shapes)
        metas.append(mt)
        wdws.append(wd)
        wpws.append(wp)
    meta_full = jnp.concatenate(metas, axis=1)
    wdw_full = jnp.concatenate(wdws, axis=2)
    wpw_full = jnp.concatenate(wpws, axis=1)
    wdw_pad = jnp.pad(wdw_full, ((0, 0), (0, HALO - CONV_WIDTH), (0, 0)))
    wpw_bf = wpw_full.astype(BF16)
    wpool_bf = w_pool.reshape(depth, D_POOL, BLOCK).astype(BF16)

    row = lambda a, i: a[i][None, :]

    h = jnp.concatenate([jnp.zeros((PAD, D_MODEL), F32), meta_full, x[0]], axis=0)
    target = jnp.pad(loss_target[0], ((PAD + N_META, 0), (0, 0)))
    u = _rowwise("pre_mix_norm0", lambda hh, g: _rms(hh, g), [h], [row(pre_mix_g, 0)], [(D_MODEL, BF16)])[0]
    saved = []
    for i in range(depth):
        proj = _mm_nn_col("in_proj%d" % i, u, wg[("w_in", i)], 0, F32)
        ride = [(k, i) for k in big_names[1:]] + ([("w_in", i + 1)] if i + 1 < depth else [])
        o_attn, tot, got = _attn_fwd(proj, _Exchange("gather", [big_local[k] for k in ride], split=True))
        wg.update(zip(ride, got))
        o_pool = _pool_fwd(proj, wpool_bf[i], row(pool_scale, i))
        o_conv = _conv_fwd(proj, wdw_pad[i], row(b_dw, i), row(conv_ln_g, i), row(conv_ln_b, i), wpw_bf[i])

        mix_gains = [row(mix_out_g, i)[:, :D_ATTN], row(mix_out_g, i)[:, D_ATTN:D_ATTN + D_POOL],
                     row(mix_out_g, i)[:, D_ATTN + D_POOL:]]
        merged = _rowwise("merge%d" % i, _merge, [o_attn, o_pool, o_conv], mix_gains, [(D_MODEL, BF16)])[0]
        mix = _mm_nn_row("out_proj%d" % i, merged, wg[("w_out", i)], 0)

        def post_mix(hh, mx, g1, g2):
            h1 = hh + _rms(mx, g1)
            return h1, _rms(h1, g2)

        h1, u2 = _rowwise("post_mix%d" % i, post_mix, [h, mix], [row(post_mix_g, i), row(pre_ffn_g, i)],
                          [(D_MODEL, F32), (D_MODEL, BF16)])
        gate = _mm_nn_col("ffn_gate%d" % i, u2, wg[("w_gate", i)], 0, F32)
        up = _mm_nn_col("ffn_up%d" % i, u2, wg[("w_up", i)], 0, F32)
        act = _rowwise("swiglu%d" % i, lambda gt, p: gt * _sigmoid(gt) * p, [gate, up], [], [(D_FF, BF16)],
                       vmem_mb=48)[0]
        ff = _mm_nn_row("ffn_down%d" % i, act, wg[("w_down", i)], 0)
        rec = dict(h=h, u=u, proj=proj, tot=tot, o_attn=o_attn, o_pool=o_pool, o_conv=o_conv, merged=merged,
                   mix=mix, h1=h1, u2=u2, gate=gate, up=up, act=act, ff=ff)
        saved.append(rec)
        if i + 1 < depth:
            def post_ffn(hh, f, g1, g2):
                h2 = hh + _rms(f, g1)
                return h2, _rms(h2, g2)

            h, u = _rowwise("post_ffn%d" % i, post_ffn, [h1, ff], [row(post_ffn_g, i), row(pre_mix_g, i + 1)],
                            [(D_MODEL, F32), (D_MODEL, BF16)])
        else:
            def head(row0, hh, f, tgt, g1):
                y = hh + _rms(f, g1)
                rid = row0 + lax.broadcasted_iota(jnp.int32, (y.shape[0], 1), 0)
                err = jnp.where(rid >= PAD + N_META, y - tgt, 0.0)
                part = 0.5 * jnp.sum(jnp.mean(err * err, axis=-1, keepdims=True), axis=0, keepdims=True)
                return err * (1.0 / D_MODEL), jnp.broadcast_to(part, (8, BLOCK))

            dh, loss_part = _rowwise("loss_head", head, [h1, ff, target], [row(post_ffn_g, i)],
                                     [(D_MODEL, F32)], accs=[(8, BLOCK)], with_row0=True)

    loss = lax.psum(loss_part[0, 0], ("x", "y", "c"))

    small_grads = {}
    big_parts = {}
    received = {}
    for i in reversed(range(depth)):
        rec = saved[i]

        def post_ffn_b(f, d, g):
            _, vjp = jax.vjp(_rms, f, g)
            df, dg = vjp(d)
            return df, dg

        dff, g_post_ffn = _rowwise("post_ffn_b%d" % i, post_ffn_b, [rec["ff"], dh], [row(post_ffn_g, i)],
                                   [(D_MODEL, BF16)], accs=[(1, D_MODEL)])
        big_parts[("w_down", i)] = _mm_tn_row("dw_down%d" % i, rec["act"], dff)
        dact = _mm_nt_row("d_act%d" % i, dff, wg[("w_down", i)], 0, F32)

        def swiglu_b(gt, p, d):
            sg = _sigmoid(gt)
            return d * p * (sg * (1.0 + gt * (1.0 - sg))), d * (gt * sg)

        dgate, dup = _rowwise("swiglu_b%d" % i, swiglu_b, [rec["gate"], rec["up"], dact], [],
                              [(D_FF, BF16), (D_FF, BF16)], vmem_mb=56)
        big_parts[("w_gate", i)] = _mm_tn_col("dw_gate%d" % i, rec["u2"], dgate)
        big_parts[("w_up", i)] = _mm_tn_col("dw_up%d" % i, rec["u2"], dup)
        du2a = _mm_nt_col("d_u2_gate%d" % i, dgate, wg[("w_gate", i)], 0)
        du2b = _mm_nt_col("d_u2_up%d" % i, dup, wg[("w_up", i)], 0)

        def post_mix_b(h1v, mx, d, da, db, g1, g2):
            _, vjp2 = jax.vjp(_rms, h1v, g2)
            dh1, dg2 = vjp2(da + db)
            dmid = d + dh1
            _, vjp1 = jax.vjp(_rms, mx, g1)
            dmx, dg1 = vjp1(dmid)
            return dmid, dmx, dg1, dg2

        dmid, dmix, g_post_mix, g_pre_ffn = _rowwise(
            "post_mix_b%d" % i, post_mix_b, [rec["h1"], rec["mix"], dh, du2a, du2b],
            [row(post_mix_g, i), row(pre_ffn_g, i)], [(D_MODEL, F32), (D_MODEL, BF16)],
            accs=[(1, D_MODEL), (1, D_MODEL)], vmem_mb=48)
        big_parts[("w_out", i)] = _mm_tn_row("dw_out%d" % i, rec["merged"], dmix)
        dmerged = _mm_nt_row("d_merged%d" % i, dmix, wg[("w_out", i)], 0, F32)

        def merge_b(oa, op, oc, d, ga, gp, gc):
            _, vjp = jax.vjp(_merge, oa, op, oc, ga, gp, gc)
            return vjp(d)

        mix_gains = [row(mix_out_g, i)[:, :D_ATTN], row(mix_out_g, i)[:, D_ATTN:D_ATTN + D_POOL],
                     row(mix_out_g, i)[:, D_ATTN + D_POOL:]]
        do_attn, do_pool, do_conv, g_mo_a, g_mo_p, g_mo_c = _rowwise(
            "merge_b%d" % i, merge_b, [rec["o_attn"], rec["o_pool"], rec["o_conv"], dmerged], mix_gains,
            [(D_ATTN, F32), (D_POOL, F32), (D_CONV, F32)], accs=[(1, D_ATTN), (1, D_POOL), (1, D_CONV)])
        g_mix_out = jnp.concatenate([g_mo_a, g_mo_p, g_mo_c], axis=1)
        du_pool, g_w_pool, g_pool_scale = _pool_bwd(rec["proj"], do_pool, wpool_bf[i], row(pool_scale, i))
        da, dgt, g_w_dw, g_b_dw, g_ln_g, g_ln_b, g_w_pw = _conv_bwd(
            rec["proj"], do_conv, wdw_pad[i], row(b_dw, i), row(conv_ln_g, i), row(conv_ln_b, i), wpw_bf[i])
        big_parts[("w_pw", i)] = g_w_pw.reshape(N_SHARD, D_CONV // N_SHARD, D_CONV).astype(BF16)
        ride = [(k, i) for k in big_names[1:] + ["w_pw"]] + ([("w_in", i + 1)] if i + 1 < depth else [])
        dq, dk, dv, got = _attn_bwd(rec["proj"], rec["tot"], do_attn,
                                    _Exchange("scatter", [big_parts[k] for k in ride]))
        received.update(zip(ride, got))
        dproj = jnp.concatenate([dq, dk, dv, du_pool, da, dgt], axis=1)
        big_parts[("w_in", i)] = _mm_tn_col("dw_in%d" % i, rec["u"], dproj)
        du = _mm_nt_col("d_u%d" % i, dproj, wg[("w_in", i)], 0)

        def pre_mix_b(hv, d, dd, g):
            _, vjp = jax.vjp(_rms, hv, g)
            dhh, dg = vjp(dd)
            return d + dhh, dg

        dh, g_pre_mix = _rowwise("pre_mix_b%d" % i, pre_mix_b, [rec["h"], dmid, du], [row(pre_mix_g, i)],
                                 [(D_MODEL, F32)], accs=[(1, D_MODEL)])
        small_grads[i] = dict(pre_mix_g=g_pre_mix[0], w_pool=g_w_pool, pool_scale=g_pool_scale[0],
                              w_dw=g_w_dw[:CONV_WIDTH], b_dw=g_b_dw[0], conv_ln_g=g_ln_g[0], conv_ln_b=g_ln_b[0],
                              mix_out_g=g_mix_out[0], post_mix_g=g_post_mix[0],
                              pre_ffn_g=g_pre_ffn[0], post_ffn_g=g_post_ffn[0])

    grad_x = dh[PAD + N_META:][None]
    g_meta_part = dh[PAD:PAD + N_META]

    rep_names = ["pre_mix_g", "pool_scale", "b_dw", "conv_ln_g", "conv_ln_b", "mix_out_g", "post_mix_g",
                 "pre_ffn_g", "post_ffn_g", "w_pool"]
    stack2 = lambda nme: jnp.stack([small_grads[l][nme] for l in range(depth)])
    small_list = [stack2(nme) for nme in rep_names] + [g_meta_part, stack2("w_dw")]
    small_list[rep_names.index("w_pool")] = small_list[rep_names.index("w_pool")].reshape(w_pool.shape)
    full_shapes = [a.shape for a in small_list]
    packed = _pack(small_list)
    summed = _sum_slots("sum_small", _gather_all("gather_small_grads", packed))
    full = _unpack(summed, full_shapes)
    rep_grads = dict(zip(rep_names, full[:len(rep_names)]))
    g_meta = lax.dynamic_slice_in_dim(full[-2], xy * meta_tokens.shape[1], meta_tokens.shape[1], axis=1)
    g_w_dw = lax.dynamic_slice_in_dim(full[-1], xy * w_dw.shape[2], w_dw.shape[2], axis=2)

    rep_w = dict(pre_mix_g=pre_mix_g, pool_scale=pool_scale, b_dw=b_dw, conv_ln_g=conv_ln_g, conv_ln_b=conv_ln_b,
                 mix_out_g=mix_out_g, post_mix_g=post_mix_g, pre_ffn_g=pre_ffn_g, post_ffn_g=post_ffn_g,
                 w_pool=w_pool)
    rep_m = dict(pre_mix_g=m_pre_mix_g, pool_scale=m_pool_scale, b_dw=m_b_dw, conv_ln_g=m_conv_ln_g,
                 conv_ln_b=m_conv_ln_b, mix_out_g=m_mix_out_g, post_mix_g=m_post_mix_g, pre_ffn_g=m_pre_ffn_g,
                 post_ffn_g=m_post_ffn_g, w_pool=m_w_pool)
    rep_v = dict(pre_mix_g=v_pre_mix_g, pool_scale=v_pool_scale, b_dw=v_b_dw, conv_ln_g=v_conv_ln_g,
                 conv_ln_b=v_conv_ln_b, mix_out_g=v_mix_out_g, post_mix_g=v_post_mix_g, pre_ffn_g=v_pre_ffn_g,
                 post_ffn_g=v_post_ffn_g, w_pool=v_w_pool)
    sm_names = rep_names + ["meta_tokens", "w_dw"]
    sm_w = [rep_w[k] for k in rep_names] + [meta_tokens, w_dw]
    sm_m = [rep_m[k] for k in rep_names] + [m_meta_tokens, m_w_dw]
    sm_v = [rep_v[k] for k in rep_names] + [v_meta_tokens, v_w_dw]
    sm_g = [rep_grads[k] for k in rep_names] + [g_meta, g_w_dw]
    sm_shapes = [a.shape for a in sm_w]
    sm_delta, sm_nm, sm_nv = _adamw_flat("adamw_small", _pack(sm_w), _pack(sm_m), _pack(sm_v), _pack(sm_g))
    small_out = {}
    for k, g, d, nm, nv in zip(sm_names, sm_g, _unpack(sm_delta, sm_shapes), _unpack(sm_nm, sm_shapes),
                               _unpack(sm_nv, sm_shapes)):
        small_out[k] = (g, d, nm, nv)

    assert depth == 2
    early = big_names[1:] + ["w_pw"]
    plane_sums = {k: _sum_slots_layers("sum_%s" % k, received[(k, 0)], received[(k, 1)]) for k in early}
    last, swapped = _scatter_and_swap("scatter_last", [big_parts[("w_in", 0)]], [plane_sums[k] for k in early])
    other_sums = dict(zip(early, swapped))
    received[("w_in", 0)] = last[0]
    plane_sums["w_in"] = _sum_slots_layers("sum_w_in", received[("w_in", 0)], received[("w_in", 1)])
    other_sums["w_in"] = _swap_core("swap_core_w_in", [plane_sums["w_in"]])[0]
    big_w = dict(w_in=(w_in, m_w_in, v_w_in), w_out=(w_out, m_w_out, v_w_out), w_gate=(w_gate, m_w_gate, v_w_gate),
                 w_up=(w_up, m_w_up, v_w_up), w_down=(w_down, m_w_down, v_w_down), w_pw=(w_pw, m_w_pw, v_w_pw))
    big_out = {}
    for k in early + ["w_in"]:
        w, m, v = big_w[k]
        big_out[k] = _adamw("adamw_%s" % k, w, m, v, plane_sums[k], other_sums[k])

    order = ["meta_tokens", "pre_mix_g", "w_in", "w_pool", "pool_scale", "w_dw", "b_dw", "conv_ln_g", "conv_ln_b",
             "w_pw", "mix_out_g", "w_out", "post_mix_g", "pre_ffn_g", "w_gate", "w_up", "w_down", "post_ffn_g"]
    res = lambda k: big_out[k] if k in big_out else small_out[k]
    outs = [loss, grad_x]
    for part in range(4):
        outs += [res(k)[part] for k in order]
    return tuple(outs)
```

```python
import functools

import jax
import jax.numpy as jnp
from jax import lax
from jax.experimental import pallas as pl
from jax.experimental.pallas import tpu as pltpu

F32 = jnp.float32
BF16 = jnp.bfloat16

D_MODEL = 2048
N_META = 16
D_ATTN = 1024
D_POOL = 512
D_CONV = 512
POOL_WINDOWS = (2, 4, 8, 16)
CONV_WIDTH = 31
D_IN_PROJ = 3 * D_ATTN + D_POOL + 2 * D_CONV
D_FF = 5632
EPS = 1e-6
BLOCK = 128
PAD = BLOCK - N_META
HALO = 32
N_SHARD = 4
N_DEV = 8
MESH = pl.DeviceIdType.MESH

ADAM_LR = 0.001
ADAM_B1 = 0.9
ADAM_B2 = 0.999
ADAM_EPS = 1e-08
ADAM_WD = 0.01
ADAM_STEP = 10

COL_Q, COL_K, COL_V = 0, D_ATTN, 2 * D_ATTN
COL_POOL = 3 * D_ATTN
COL_A = COL_POOL + D_POOL
COL_G = COL_A + D_CONV


def _call(body, *, name, out_shape, grid=None, in_specs=None, out_specs=None, scratch=(), vmem_mb=None,
          aliases=None):
    params = {}
    if grid is not None:
        params["dimension_semantics"] = ("arbitrary",) * len(grid)
    if vmem_mb is not None:
        params["vmem_limit_bytes"] = vmem_mb << 20
    kw = dict(out_shape=out_shape, name=name, compiler_params=pltpu.CompilerParams(**params))
    if grid is not None:
        kw["grid"] = grid
    if in_specs is not None:
        kw["in_specs"] = in_specs
    if out_specs is not None:
        kw["out_specs"] = out_specs
    if scratch:
        kw["scratch_shapes"] = list(scratch)
    if aliases:
        kw["input_output_aliases"] = dict(aliases)
    return pl.pallas_call(body, **kw)


def _sds(shape, dtype):
    return jax.ShapeDtypeStruct(tuple(shape), dtype)


def _pick(n, candidates):
    for c in candidates:
        if n % c == 0:
            return c
    return n


def _rowwise(name, fn, rows_in, consts, outs, accs=(), tm=BLOCK, with_row0=False, vmem_mb=None):
    lp = rows_in[0].shape[0]
    n_in, n_c, n_o = len(rows_in), len(consts), len(outs)

    def body(*refs):
        vals = [r[...] for r in refs[:n_in + n_c]]
        if with_row0:
            vals = [pl.program_id(0) * tm] + vals
        res = fn(*vals)
        if not isinstance(res, (tuple, list)):
            res = (res,)
        o_refs = refs[n_in + n_c:n_in + n_c + n_o]
        a_refs = refs[n_in + n_c + n_o:]
        for r, v in zip(o_refs, res[:n_o]):
            r[...] = v.astype(r.dtype)
        if a_refs:
            @pl.when(pl.program_id(0) == 0)
            def _():
                for r in a_refs:
                    r[...] = jnp.zeros(r.shape, r.dtype)
            for r, v in zip(a_refs, res[n_o:]):
                r[...] += v.astype(r.dtype)

    in_specs = [pl.BlockSpec((tm, a.shape[1]), lambda i: (i, 0)) for a in rows_in]
    in_specs += [pl.BlockSpec(c.shape, lambda i: (0, 0)) for c in consts]
    out_specs = [pl.BlockSpec((tm, w), lambda i: (i, 0)) for (w, _) in outs]
    out_specs += [pl.BlockSpec(s, lambda i: (0, 0)) for s in accs]
    out_shape = [_sds((lp, w), dt) for (w, dt) in outs] + [_sds(s, F32) for s in accs]
    res = _call(body, name=name, out_shape=out_shape, grid=(lp // tm,), in_specs=in_specs,
                out_specs=out_specs, vmem_mb=vmem_mb)(*rows_in, *consts)
    return res


def _rms(x, g):
    return x * lax.rsqrt(jnp.mean(x * x, axis=-1, keepdims=True) + EPS) * g


def _merge(oa, op, oc, ga, gp, gc):
    return jnp.concatenate([_rms(oa, ga), _rms(op, gp), _rms(oc, gc)], axis=1)


def _colsum(x):
    return jnp.sum(x, axis=0, keepdims=True)


def _sigmoid(x):
    return 1.0 / (1.0 + jnp.exp(-x))


MM_VMEM_MB = 56


def _mm_tiles(lp):
    return _pick(lp, (1408, 384, 256, 128))


def _mm_nn_col(name, a, wg, layer, out_dtype):
    lp, k = a.shape
    n = wg.shape[3]
    tm = _mm_tiles(lp)

    def body(a_ref, w_ref, o_ref):
        o_ref[...] = jnp.dot(a_ref[...], w_ref[...], preferred_element_type=F32).astype(o_ref.dtype)

    return _call(body, name=name, out_shape=_sds((lp, N_SHARD * n), out_dtype), grid=(N_SHARD, lp // tm),
                 in_specs=[pl.BlockSpec((tm, k), lambda s, i: (i, 0)),
                           pl.BlockSpec((None, None, k, n), lambda s, i: (s, layer, 0, 0))],
                 out_specs=pl.BlockSpec((tm, n), lambda s, i: (i, s)), vmem_mb=MM_VMEM_MB)(a, wg)


def _mm_nn_row(name, a, wg, layer):
    lp = a.shape[0]
    k, n = wg.shape[2], wg.shape[3]
    tm = _mm_tiles(lp)

    def body(a_ref, w_ref, o_ref):
        part = jnp.dot(a_ref[...], w_ref[...], preferred_element_type=F32)

        @pl.when(pl.program_id(1) == 0)
        def _():
            o_ref[...] = part

        @pl.when(pl.program_id(1) != 0)
        def _():
            o_ref[...] += part

    return _call(body, name=name, out_shape=_sds((lp, n), F32), grid=(lp // tm, N_SHARD),
                 in_specs=[pl.BlockSpec((tm, k), lambda i, s: (i, s)),
                           pl.BlockSpec((None, None, k, n), lambda i, s: (s, layer, 0, 0))],
                 out_specs=pl.BlockSpec((tm, n), lambda i, s: (i, 0)), vmem_mb=MM_VMEM_MB)(a, wg)


_NT = (((1,), (1,)), ((), ()))
_TN = (((0,), (0,)), ((), ()))


def _mm_nt_col(name, dy, wg, layer):
    lp = dy.shape[0]
    k, n = wg.shape[2], wg.shape[3]
    tm = _mm_tiles(lp)

    def body(d_ref, w_ref, o_ref):
        part = lax.dot_general(d_ref[...], w_ref[...], _NT, preferred_element_type=F32)

        @pl.when(pl.program_id(1) == 0)
        def _():
            o_ref[...] = part

        @pl.when(pl.program_id(1) != 0)
        def _():
            o_ref[...] += part

    return _call(body, name=name, out_shape=_sds((lp, k), F32), grid=(lp // tm, N_SHARD),
                 in_specs=[pl.BlockSpec((tm, n), lambda i, s: (i, s)),
                           pl.BlockSpec((None, None, k, n), lambda i, s: (s, layer, 0, 0))],
                 out_specs=pl.BlockSpec((tm, k), lambda i, s: (i, 0)), vmem_mb=MM_VMEM_MB)(dy, wg)


def _mm_nt_row(name, dy, wg, layer, out_dtype):
    lp = dy.shape[0]
    k, n = wg.shape[2], wg.shape[3]
    tm = _mm_tiles(lp)

    def body(d_ref, w_ref, o_ref):
        o_ref[...] = lax.dot_general(d_ref[...], w_ref[...], _NT, preferred_element_type=F32).astype(o_ref.dtype)

    return _call(body, name=name, out_shape=_sds((lp, N_SHARD * k), out_dtype), grid=(N_SHARD, lp // tm),
                 in_specs=[pl.BlockSpec((tm, n), lambda s, i: (i, 0)),
                           pl.BlockSpec((None, None, k, n), lambda s, i: (s, layer, 0, 0))],
                 out_specs=pl.BlockSpec((tm, k), lambda s, i: (i, s)), vmem_mb=MM_VMEM_MB)(dy, wg)


def _mm_nt_row_swiglu(name, dy, wg, layer, gate, up):
    lp = dy.shape[0]
    k, n = wg.shape[2], wg.shape[3]
    tm = _pick(lp, (384, 128))

    def body(d_ref, w_ref, g_ref, u_ref, dg_ref, du_ref):
        dact = lax.dot_general(d_ref[...], w_ref[...], _NT, preferred_element_type=F32)
        g = g_ref[...]
        sg = _sigmoid(g)
        dg_ref[...] = (dact * u_ref[...] * (sg * (1.0 + g * (1.0 - sg)))).astype(dg_ref.dtype)
        du_ref[...] = (dact * (g * sg)).astype(du_ref.dtype)

    blk = pl.BlockSpec((tm, k), lambda s, i: (i, s))
    return _call(body, name=name, out_shape=[_sds((lp, N_SHARD * k), BF16)] * 2, grid=(N_SHARD, lp // tm),
                 in_specs=[pl.BlockSpec((tm, n), lambda s, i: (i, 0)),
                           pl.BlockSpec((None, None, k, n), lambda s, i: (s, layer, 0, 0)), blk, blk],
                 out_specs=[blk, blk], vmem_mb=MM_VMEM_MB)(dy, wg, gate, up)


def _mm_tn_col(name, a, dy):
    lp, k = a.shape
    n = dy.shape[1] // N_SHARD
    tm = _mm_tiles(lp)
    tk = _pick(k, (1024, 512))

    def body(a_ref, d_ref, o_ref, acc):
        @pl.when(pl.program_id(2) == 0)
        def _():
            acc[...] = jnp.zeros(acc.shape, F32)

        acc[...] += lax.dot_general(a_ref[...], d_ref[...], _TN, preferred_element_type=F32)

        @pl.when(pl.program_id(2) == pl.num_programs(2) - 1)
        def _():
            o_ref[...] = acc[...].astype(o_ref.dtype)

    return _call(body, name=name, out_shape=_sds((N_SHARD, k, n), BF16), grid=(N_SHARD, k // tk, lp // tm),
                 in_specs=[pl.BlockSpec((tm, tk), lambda s, kk, i: (i, kk)),
                           pl.BlockSpec((tm, n), lambda s, kk, i: (i, s))],
                 out_specs=pl.BlockSpec((None, tk, n), lambda s, kk, i: (s, kk, 0)),
                 scratch=[pltpu.VMEM((tk, n), F32)], vmem_mb=MM_VMEM_MB)(a, dy)


def _mm_tn_row(name, a, dy):
    lp = a.shape[0]
    k = a.shape[1] // N_SHARD
    n = dy.shape[1]
    tm = _mm_tiles(lp)
    tn = _pick(n, (1024, 512))

    def body(a_ref, d_ref, o_ref, acc):
        @pl.when(pl.program_id(2) == 0)
        def _():
            acc[...] = jnp.zeros(acc.shape, F32)

        acc[...] += lax.dot_general(a_ref[...], d_ref[...], _TN, preferred_element_type=F32)

        @pl.when(pl.program_id(2) == pl.num_programs(2) - 1)
        def _():
            o_ref[...] = acc[...].astype(o_ref.dtype)

    return _call(body, name=name, out_shape=_sds((N_SHARD, k, n), BF16), grid=(N_SHARD, n // tn, lp // tm),
                 in_specs=[pl.BlockSpec((tm, k), lambda s, j, i: (i, s)),
                           pl.BlockSpec((tm, tn), lambda s, j, i: (i, j))],
                 out_specs=pl.BlockSpec((None, k, tn), lambda s, j, i: (s, 0, j)),
                 scratch=[pltpu.VMEM((k, tn), F32)], vmem_mb=MM_VMEM_MB)(a, dy)


SUB = 16
WIDE = 2 * BLOCK
Z_CLAMP = 20.0


def _log1m_sigmoid(z):
    return -jnp.where(z > Z_CLAMP, z, jnp.log(1.0 + jnp.exp(jnp.minimum(z, Z_CLAMP))))


def _tri(tk, kind):
    r = lax.broadcasted_iota(jnp.int32, (tk, tk), 0)
    c = lax.broadcasted_iota(jnp.int32, (tk, tk), 1)
    t = {"gt": r > c, "le": r <= c}[kind]
    return jnp.where(t, 1.0, 0.0).astype(BF16)


def _strip_mask(kind, s, tk):
    if kind == "none":
        return None
    col = lax.broadcasted_iota(jnp.int32, (SUB, tk), 1)
    row = lax.broadcasted_iota(jnp.int32, (SUB, tk), 0)
    causal = (col - row) < s * SUB
    if kind == "diag":
        return causal
    if kind == "pad":
        return col >= PAD
    return causal & (col >= PAD)


def _attn_blocks(lp):
    nb = lp // BLOCK
    assert nb % 2 == 1, "sequence must be a 128-row block plus whole 256-row blocks"
    return nb, (nb + 1) // 2


class _Exchange:
    def __init__(self, kind, arrs, split=False):
        self.kind, self.arrs, self.n, self.split = kind, list(arrs), len(arrs), split

    def out_shape(self):
        if self.kind == "gather":
            return [_sds((N_SHARD,) + a.shape, a.dtype) for a in self.arrs]
        return [_sds(a.shape, a.dtype) for a in self.arrs]

    def scratch(self):
        sems = [pltpu.SemaphoreType.DMA((3 * self.n,)), pltpu.SemaphoreType.DMA((3 * self.n,)),
                pltpu.SemaphoreType.DMA((self.n,))]
        if self.split:
            sems += [pltpu.SemaphoreType.DMA((self.n,)), pltpu.SemaphoreType.DMA((self.n,))]
        return sems

    def forward(self, wait, outs, fsend, frecv):
        x, y, c = _place()
        for a in range(self.n):
            cp = pltpu.make_async_remote_copy(src_ref=outs[a], dst_ref=outs[a], send_sem=fsend.at[a],
                                              recv_sem=frecv.at[a], device_id=(x, y, 1 - c), device_id_type=MESH)
            if wait:
                pl.when(c == a % 2)(cp.wait_send)
                pl.when(c != a % 2)(cp.wait_recv)
            else:
                pl.when(c == a % 2)(cp.start)

    def copies(self, a, ins, outs, send, recv, local):
        x, y, c = _place()
        me = 2 * x + y
        own = ins[a] if self.kind == "gather" else ins[a].at[me]
        out = [pltpu.make_async_copy(own, outs[a].at[me], local.at[a])]
        for r, (px, py) in enumerate(_xy_peers(x, y)):
            src = ins[a] if self.kind == "gather" else ins[a].at[2 * px + py]
            out.append(pltpu.make_async_remote_copy(
                src_ref=src, dst_ref=outs[a].at[me], send_sem=send.at[3 * a + r], recv_sem=recv.at[3 * a + r],
                device_id=(px, py, c), device_id_type=MESH))
        return out

    def run(self, wait, ins, outs, send, recv, local):
        for a in range(self.n):
            def go(a=a):
                for cp in self.copies(a, ins, outs, send, recv, local):
                    if wait:
                        cp.wait()
                    else:
                        cp.start()
            if self.split:
                pl.when(lax.axis_index("c") == a % 2)(go)
            else:
                go()


def _attn_fwd(proj, exchange=None):
    lp = proj.shape[0]
    nb, nq = _attn_blocks(lp)
    n_pair = D_ATTN // BLOCK
    n_x = exchange.n if exchange else 0

    def body(*refs):
        q_ref, k_ref, v_ref = refs[:3]
        x_in = refs[3:3 + n_x]
        o_ref, tot_ref = refs[3 + n_x:5 + n_x]
        x_out = refs[5 + n_x:5 + 2 * n_x]
        qs, kb, vh, tri_l, tri_s, z_s, hl_s, c_s, w_s, r_s, acc_s = refs[5 + 2 * n_x:16 + 2 * n_x]
        x_sem = refs[16 + 2 * n_x:]
        p, i = pl.program_id(0), pl.program_id(1)
        m0 = lax.broadcasted_iota(jnp.int32, (1, BLOCK), 1) < (BLOCK // 2)

        if exchange:
            @pl.when((p == 0) & (i == 0))
            def _():
                exchange.run(False, x_in, x_out, *x_sem[:3])

            if exchange.split:
                @pl.when((p == n_pair - 2) & (i == 0))
                def _():
                    exchange.run(True, x_in, x_out, *x_sem[:3])
                    exchange.forward(False, x_out, *x_sem[3:])

        @pl.when(i == 0)
        def _():
            tri_l[...] = _tri(WIDE, "gt")
            tri_s[...] = _tri(BLOCK, "gt")

            def prep(b, carry):
                rows = pl.ds(pl.multiple_of(b * BLOCK, BLOCK), BLOCK)
                q = q_ref[rows, :] * 0.125
                v = v_ref[rows, :]
                qs[0, rows, :] = jnp.where(m0, q, 0.0).astype(BF16)
                qs[1, rows, :] = jnp.where(m0, 0.0, q).astype(BF16)
                kb[rows, :] = k_ref[rows, :].astype(BF16)
                vh[0, rows, :] = jnp.where(m0, v, 0.0).astype(BF16)
                vh[1, rows, :] = jnp.where(m0, 0.0, v).astype(BF16)
                return carry

            lax.fori_loop(0, nb, prep, 0)

        def tiles(q0, tq, specs):
            heads = [(t, h) for t in range(len(specs)) for h in range(2)]
            strips = [slice(s * SUB, (s + 1) * SUB) for s in range(tq // SUB)]
            for t, h in heads:
                k0, tk, _ = specs[t]
                z_s[t, h, 0:tq, 0:tk] = lax.dot_general(qs[h, pl.ds(q0, tq), :], kb[pl.ds(k0, tk), :], _NT,
                                                        preferred_element_type=F32)
            for t, h in heads:
                _, tk, kind = specs[t]
                for s, rows in enumerate(strips):
                    z = z_s[t, h, rows, 0:tk]
                    lnb = _log1m_sigmoid(z)
                    m = _strip_mask(kind, s, tk)
                    if m is not None:
                        lnb = jnp.where(m, lnb, 0.0)
                    z_s[t, h, rows, 0:tk] = z + lnb
                    hl_s[t, h, rows, 0:tk] = lnb.astype(BF16)
            for t, h in heads:
                _, tk, _ = specs[t]
                tri = tri_l if tk == WIDE else tri_s
                c_s[t, h, 0:tq, 0:tk] = jnp.dot(hl_s[t, h, 0:tq, 0:tk], tri[...], preferred_element_type=F32)
            for t, h in heads:
                _, tk, kind = specs[t]
                for s, rows in enumerate(strips):
                    r = r_s[h, rows, :]
                    c = c_s[t, h, rows, 0:tk]
                    rr = r if tk == BLOCK else jnp.concatenate([r, r], axis=1)
                    w = jnp.exp(z_s[t, h, rows, 0:tk] + c + rr)
                    m = _strip_mask(kind, s, tk)
                    if m is not None:
                        w = jnp.where(m, w, 0.0)
                    w_s[t, h, rows, 0:tk] = w.astype(BF16)
                    total = c[:, 0:1] + hl_s[t, h, rows, 0:BLOCK].astype(F32)[:, 0:1]
                    r_s[h, rows, :] = r + jnp.broadcast_to(total, (SUB, BLOCK))
            upd = None
            for t, h in heads:
                k0, tk, _ = specs[t]
                d = jnp.dot(w_s[t, h, 0:tq, 0:tk], vh[h, pl.ds(k0, tk), :], preferred_element_type=F32)
                upd = d if upd is None else upd + d
            acc_s[0:tq, :] += upd

        def finish(q0, tq):
            o_ref[pl.ds(q0, tq), :] = acc_s[0:tq, :]
            tot_ref[pl.ds(q0, tq), :] = jnp.where(m0, r_s[0, 0:tq, :], r_s[1, 0:tq, :])

        r_s[...] = jnp.zeros(r_s.shape, F32)
        acc_s[...] = jnp.zeros(acc_s.shape, F32)

        @pl.when(i == 0)
        def _():
            tiles(0, BLOCK, [(0, BLOCK, "first")])
            finish(0, BLOCK)

        @pl.when(i > 0)
        def _():
            q0 = pl.multiple_of(i * WIDE - BLOCK, BLOCK)
            full = lambda j: (pl.multiple_of(j * WIDE - BLOCK, BLOCK), WIDE, "none")
            diag, meta = (q0, WIDE, "diag"), (0, BLOCK, "pad")

            @pl.when(i == 1)
            def _():
                tiles(q0, WIDE, [diag, meta])

            @pl.when(i >= 2)
            def _():
                tiles(q0, WIDE, [diag, full(i - 1)])

                def inner(n, carry):
                    j = i - 2 - 2 * n
                    tiles(q0, WIDE, [full(j), full(j - 1)])
                    return carry

                lax.fori_loop(0, (i - 2) // 2, inner, 0)

                @pl.when(i % 2 == 1)
                def _():
                    tiles(q0, WIDE, [full(1), meta])

                @pl.when(i % 2 == 0)
                def _():
                    tiles(q0, WIDE, [meta])

            finish(q0, WIDE)

        if exchange:
            @pl.when((p == n_pair - 1) & (i == nq - 1))
            def _():
                if exchange.split:
                    exchange.forward(True, x_out, *x_sem[3:])
                else:
                    exchange.run(True, x_in, x_out, *x_sem)

    cq, ck, cv = COL_Q // BLOCK, COL_K // BLOCK, COL_V // BLOCK
    col = lambda c0: (lambda p, i: (0, c0 + p))
    scratch = [pltpu.VMEM((2, lp, BLOCK), BF16), pltpu.VMEM((lp, BLOCK), BF16), pltpu.VMEM((2, lp, BLOCK), BF16),
               pltpu.VMEM((WIDE, WIDE), BF16), pltpu.VMEM((BLOCK, BLOCK), BF16),
               pltpu.VMEM((2, 2, WIDE, WIDE), F32), pltpu.VMEM((2, 2, WIDE, WIDE), BF16),
               pltpu.VMEM((2, 2, WIDE, WIDE), F32), pltpu.VMEM((2, 2, WIDE, WIDE), BF16),
               pltpu.VMEM((2, WIDE, BLOCK), F32), pltpu.VMEM((WIDE, BLOCK), F32)]
    res = _call(body, name="attn_fwd",
                out_shape=[_sds((lp, D_ATTN), F32), _sds((lp, D_ATTN), F32)] + (exchange.out_shape() if exchange else []),
                grid=(n_pair, nq),
                in_specs=[pl.BlockSpec((lp, BLOCK), col(cq)), pl.BlockSpec((lp, BLOCK), col(ck)),
                          pl.BlockSpec((lp, BLOCK), col(cv))] + [_ANY] * n_x,
                out_specs=[pl.BlockSpec((lp, BLOCK), col(0)), pl.BlockSpec((lp, BLOCK), col(0))] + [_ANY] * n_x,
                scratch=scratch + (exchange.scratch() if exchange else []),
                vmem_mb=56)(proj, proj, proj, *(exchange.arrs if exchange else []))
    return res[0], res[1], list(res[2:])


def _attn_bwd(proj, tot, d_out, exchange=None):
    lp = proj.shape[0]
    nb, nq = _attn_blocks(lp)
    n_pair = D_ATTN // BLOCK
    n_x = exchange.n if exchange else 0
    n_s = 23

    def body(*refs):
        q_ref, k_ref, v_ref, tot_ref, do_ref = refs[:5]
        x_in = refs[5:5 + n_x]
        dq_ref, dk_ref, dv_ref = refs[5 + n_x:8 + n_x]
        x_out = refs[8 + n_x:8 + 2 * n_x]
        (qs, kb, kh, vb, doh, tge_l, tge_s, tle_l, tle_s, z_s, g_s, hl_s, c_s, gl_s, gc_s, w_s, dz_s,
         tot_s, a_s, b_s, dq_acc, dk_acc, dv_acc) = refs[8 + 2 * n_x:8 + 2 * n_x + n_s]
        x_sem = refs[8 + 2 * n_x + n_s:]
        p, i = pl.program_id(0), pl.program_id(1)
        m0 = lax.broadcasted_iota(jnp.int32, (1, BLOCK), 1) < (BLOCK // 2)

        if exchange:
            @pl.when((p == 0) & (i == 0))
            def _():
                exchange.run(False, x_in, x_out, *x_sem)

        @pl.when(i == 0)
        def _():
            tge_l[...] = _tri(WIDE, "gt")
            tge_s[...] = _tri(BLOCK, "gt")
            tle_l[...] = _tri(WIDE, "le")
            tle_s[...] = _tri(BLOCK, "le")

            def prep(b, carry):
                rows = pl.ds(pl.multiple_of(b * BLOCK, BLOCK), BLOCK)
                q = q_ref[rows, :] * 0.125
                k = k_ref[rows, :]
                do = do_ref[rows, :]
                qs[0, rows, :] = jnp.where(m0, q, 0.0).astype(BF16)
                qs[1, rows, :] = jnp.where(m0, 0.0, q).astype(BF16)
                kb[rows, :] = k.astype(BF16)
                kh[0, rows, :] = jnp.where(m0, k, 0.0).astype(BF16)
                kh[1, rows, :] = jnp.where(m0, 0.0, k).astype(BF16)
                vb[rows, :] = v_ref[rows, :].astype(BF16)
                doh[0, rows, :] = jnp.where(m0, do, 0.0).astype(BF16)
                doh[1, rows, :] = jnp.where(m0, 0.0, do).astype(BF16)
                dk_acc[rows, :] = jnp.zeros((BLOCK, BLOCK), F32)
                dv_acc[rows, :] = jnp.zeros((BLOCK, BLOCK), F32)
                return carry

            lax.fori_loop(0, nb, prep, 0)

        def wide(x, tk):
            return x if tk == BLOCK else jnp.concatenate([x, x], axis=1)

        def tiles(q0, tq, specs):
            heads = [(t, h) for t in range(len(specs)) for h in range(2)]
            strips = [slice(s * SUB, (s + 1) * SUB) for s in range(tq // SUB)]
            for t, h in heads:
                k0, tk, _ = specs[t]
                z_s[t, h, 0:tq, 0:tk] = lax.dot_general(qs[h, pl.ds(q0, tq), :], kb[pl.ds(k0, tk), :], _NT,
                                                        preferred_element_type=F32)
                g_s[t, h, 0:tq, 0:tk] = lax.dot_general(doh[h, pl.ds(q0, tq), :], vb[pl.ds(k0, tk), :], _NT,
                                                        preferred_element_type=F32)
            for t, h in heads:
                _, tk, kind = specs[t]
                for s, rows in enumerate(strips):
                    z = z_s[t, h, rows, 0:tk]
                    lnb = _log1m_sigmoid(z)
                    m = _strip_mask(kind, s, tk)
                    if m is not None:
                        lnb = jnp.where(m, lnb, 0.0)
                    z_s[t, h, rows, 0:tk] = z + lnb
                    hl_s[t, h, rows, 0:tk] = lnb.astype(BF16)
            for t, h in heads:
                _, tk, _ = specs[t]
                tri = tge_l if tk == WIDE else tge_s
                c_s[t, h, 0:tq, 0:tk] = jnp.dot(hl_s[t, h, 0:tq, 0:tk], tri[...], preferred_element_type=F32)
            for t, h in heads:
                _, tk, kind = specs[t]
                for s, rows in enumerate(strips):
                    c = c_s[t, h, rows, 0:tk]
                    total = c[:, 0:1] + hl_s[t, h, rows, 0:BLOCK].astype(F32)[:, 0:1]
                    a_next = a_s[h, rows, :] + jnp.broadcast_to(total, (SUB, BLOCK))
                    a_s[h, rows, :] = a_next
                    w = jnp.exp(z_s[t, h, rows, 0:tk] + c + wide(tot_s[h, rows, :] - a_next, tk))
                    m = _strip_mask(kind, s, tk)
                    if m is not None:
                        w = jnp.where(m, w, 0.0)
                    g = w * g_s[t, h, rows, 0:tk]
                    g_s[t, h, rows, 0:tk] = g
                    gl_s[t, h, rows, 0:tk] = g.astype(BF16)
                    w_s[t, h, rows, 0:tk] = w.astype(BF16)
            for t, h in heads:
                _, tk, _ = specs[t]
                tri = tle_l if tk == WIDE else tle_s
                gc_s[t, h, 0:tq, 0:tk] = jnp.dot(gl_s[t, h, 0:tq, 0:tk], tri[...], preferred_element_type=F32)
            for t, h in heads:
                _, tk, kind = specs[t]
                for s, rows in enumerate(strips):
                    gc = gc_s[t, h, rows, 0:tk]
                    b = b_s[h, rows, :]
                    sig = jnp.exp(z_s[t, h, rows, 0:tk])
                    dz = g_s[t, h, rows, 0:tk] - sig * (gc + wide(b, tk))
                    m = _strip_mask(kind, s, tk)
                    if m is not None:
                        dz = jnp.where(m, dz, 0.0)
                    dz_s[t, h, rows, 0:tk] = dz.astype(BF16)
                    b_s[h, rows, :] = b + jnp.broadcast_to(gc[:, tk - 1:tk], (SUB, BLOCK))
            upd = None
            for t, h in heads:
                k0, tk, _ = specs[t]
                d = jnp.dot(dz_s[t, h, 0:tq, 0:tk], kh[h, pl.ds(k0, tk), :], preferred_element_type=F32)
                upd = d if upd is None else upd + d
            dq_acc[0:tq, :] += upd
            for t, (k0, tk, _) in enumerate(specs):
                dk_acc[pl.ds(k0, tk), :] += (
                    lax.dot_general(dz_s[t, 0, 0:tq, 0:tk], qs[0, pl.ds(q0, tq), :], _TN, preferred_element_type=F32) +
                    lax.dot_general(dz_s[t, 1, 0:tq, 0:tk], qs[1, pl.ds(q0, tq), :], _TN, preferred_element_type=F32))
                dv_acc[pl.ds(k0, tk), :] += (
                    lax.dot_general(w_s[t, 0, 0:tq, 0:tk], doh[0, pl.ds(q0, tq), :], _TN, preferred_element_type=F32) +
                    lax.dot_general(w_s[t, 1, 0:tq, 0:tk], doh[1, pl.ds(q0, tq), :], _TN, preferred_element_type=F32))

        def start(q0, tq):
            tv = tot_ref[pl.ds(q0, tq), :]
            tot_s[0, 0:tq, :] = jnp.broadcast_to(tv[:, 0:1], (tq, BLOCK))
            tot_s[1, 0:tq, :] = jnp.broadcast_to(tv[:, BLOCK - 1:BLOCK], (tq, BLOCK))
            a_s[...] = jnp.zeros(a_s.shape, F32)
            b_s[...] = jnp.zeros(b_s.shape, F32)
            dq_acc[...] = jnp.zeros(dq_acc.shape, F32)

        def finish(q0, tq):
            dq_ref[pl.ds(q0, tq), :] = (dq_acc[0:tq, :] * 0.125).astype(dq_ref.dtype)

        @pl.when(i == 0)
        def _():
            start(0, BLOCK)
            tiles(0, BLOCK, [(0, BLOCK, "first")])
            finish(0, BLOCK)

        @pl.when(i > 0)
        def _():
            q0 = pl.multiple_of(i * WIDE - BLOCK, BLOCK)
            full = lambda j: (pl.multiple_of(j * WIDE - BLOCK, BLOCK), WIDE, "none")
            diag, meta = (q0, WIDE, "diag"), (0, BLOCK, "pad")
            start(q0, WIDE)

            @pl.when(i == 1)
            def _():
                tiles(q0, WIDE, [meta, diag])

            @pl.when(i >= 2)
            def _():
                odd = i % 2

                @pl.when(odd == 1)
                def _():
                    tiles(q0, WIDE, [meta, full(1)])

                @pl.when(odd == 0)
                def _():
                    tiles(q0, WIDE, [meta])

                def inner(n, carry):
                    j = 1 + odd + 2 * n
                    tiles(q0, WIDE, [full(j), full(j + 1)])
                    return carry

                lax.fori_loop(0, (i - 2) // 2, inner, 0)
                tiles(q0, WIDE, [full(i - 1), diag])

            finish(q0, WIDE)

        @pl.when(i == nq - 1)
        def _():
            dk_ref[...] = dk_acc[...].astype(dk_ref.dtype)
            dv_ref[...] = dv_acc[...].astype(dv_ref.dtype)

        if exchange:
            @pl.when((p == n_pair - 1) & (i == nq - 1))
            def _():
                exchange.run(True, x_in, x_out, *x_sem)

    cq, ck, cv = COL_Q // BLOCK, COL_K // BLOCK, COL_V // BLOCK
    col = lambda c0: (lambda p, i: (0, c0 + p))
    whole = lambda c0: pl.BlockSpec((lp, BLOCK), col(c0))
    tile4 = lambda w, dt: pltpu.VMEM((2, 2, WIDE, w), dt)
    scratch = [pltpu.VMEM((2, lp, BLOCK), BF16), pltpu.VMEM((lp, BLOCK), BF16), pltpu.VMEM((2, lp, BLOCK), BF16),
               pltpu.VMEM((lp, BLOCK), BF16), pltpu.VMEM((2, lp, BLOCK), BF16),
               pltpu.VMEM((WIDE, WIDE), BF16), pltpu.VMEM((BLOCK, BLOCK), BF16),
               pltpu.VMEM((WIDE, WIDE), BF16), pltpu.VMEM((BLOCK, BLOCK), BF16),
               tile4(WIDE, F32), tile4(WIDE, F32), tile4(WIDE, BF16), tile4(WIDE, F32),
               tile4(WIDE, BF16), tile4(WIDE, F32), tile4(WIDE, BF16), tile4(WIDE, BF16),
               pltpu.VMEM((2, WIDE, BLOCK), F32), pltpu.VMEM((2, WIDE, BLOCK), F32), pltpu.VMEM((2, WIDE, BLOCK), F32),
               pltpu.VMEM((WIDE, BLOCK), F32), pltpu.VMEM((lp, BLOCK), F32), pltpu.VMEM((lp, BLOCK), F32)]
    res = _call(body, name="attn_bwd",
                out_shape=[_sds((lp, D_ATTN), BF16)] * 3 + (exchange.out_shape() if exchange else []),
                grid=(n_pair, nq),
                in_specs=[whole(cq), whole(ck), whole(cv), whole(0), whole(0)] + [_ANY] * n_x,
                out_specs=[whole(0), whole(0), whole(0)] + [_ANY] * n_x,
                scratch=scratch + (exchange.scratch() if exchange else []),
                vmem_mb=60)(proj, proj, proj, tot, d_out, *(exchange.arrs if exchange else []))
    return res[0], res[1], res[2], list(res[3:])


def _shift_down(x, d):
    return x if d == 0 else pltpu.roll(x, d, 0)


def _shift_up(x, d):
    return x if d == 0 else pltpu.roll(x, x.shape[0] - d, 0)


def _pool_windows(ext, down):
    shift = _shift_down if down else _shift_up
    outs = []
    for g, w in enumerate(POOL_WINDOWS):
        s = ext[:, g * BLOCK:(g + 1) * BLOCK]
        d = 1
        while d < w:
            s = s + shift(s, d)
            d *= 2
        outs.append(s)
    return jnp.concatenate(outs, axis=1)


def _pool_counts(pos):
    cols = [jnp.broadcast_to(jnp.clip(pos + 1, 1, w).astype(F32), (pos.shape[0], BLOCK)) for w in POOL_WINDOWS]
    return jnp.concatenate(cols, axis=1)


def _group_dot(x, w_ref, transpose):
    outs = []
    for g in range(len(POOL_WINDOWS)):
        xg = x[:, g * BLOCK:(g + 1) * BLOCK].astype(BF16)
        wg = w_ref[g * BLOCK:(g + 1) * BLOCK, :]
        if transpose:
            outs.append(lax.dot_general(xg, wg, _NT, preferred_element_type=F32))
        else:
            outs.append(jnp.dot(xg, wg, preferred_element_type=F32))
    return jnp.concatenate(outs, axis=1)


def _pooled(prev, cur, r):
    rows = cur.shape[0]
    ext = jnp.concatenate([prev[rows - HALO:], cur], axis=0)
    pos = r * rows + lax.broadcasted_iota(jnp.int32, (rows, 1), 0) - PAD
    ws = _pool_windows(ext, down=True)[HALO:]
    return jnp.where(pos >= 0, ws / _pool_counts(pos) - cur, 0.0)


def _pool_fwd(proj, w_pool_bf, scale):
    lp = proj.shape[0]
    rows = BLOCK
    cb = COL_POOL // D_POOL

    def body(prev_ref, cur_ref, w_ref, s_ref, o_ref):
        pooled = _pooled(prev_ref[...], cur_ref[...], pl.program_id(0))
        o_ref[...] = _group_dot(pooled, w_ref, False) * s_ref[...]

    return _call(body, name="pool_fwd", out_shape=_sds((lp, D_POOL), F32), grid=(lp // rows,),
                 in_specs=[pl.BlockSpec((rows, D_POOL), lambda r: (jnp.maximum(r - 1, 0), cb)),
                           pl.BlockSpec((rows, D_POOL), lambda r: (r, cb)),
                           pl.BlockSpec((D_POOL, BLOCK), lambda r: (0, 0)),
                           pl.BlockSpec((1, D_POOL), lambda r: (0, 0))],
                 out_specs=pl.BlockSpec((rows, D_POOL), lambda r: (r, 0)))(proj, proj, w_pool_bf, scale)


def _pool_bwd(proj, d_out, w_pool_bf, scale):
    lp = proj.shape[0]
    rows = BLOCK
    n_chunk = lp // rows
    cb = COL_POOL // D_POOL

    def body(prev_ref, cur_ref, do_ref, don_ref, w_ref, s_ref, du_ref, dw_ref, ds_ref):
        r = pl.program_id(0)

        @pl.when(r == 0)
        def _():
            dw_ref[...] = jnp.zeros(dw_ref.shape, F32)
            ds_ref[...] = jnp.zeros(ds_ref.shape, F32)

        pooled = _pooled(prev_ref[...], cur_ref[...], r)
        d_ext = jnp.concatenate([do_ref[...], don_ref[0:HALO]], axis=0)
        pos = r * rows + lax.broadcasted_iota(jnp.int32, (rows + HALO, 1), 0) - PAD
        dmixed = jnp.where((pos >= 0) & (pos < lp - PAD), d_ext * s_ref[...], 0.0)
        dpooled = _group_dot(dmixed, w_ref, True)
        back = _pool_windows(dpooled / _pool_counts(pos), down=False)[0:rows]
        du = jnp.where(pos[0:rows] >= 0, back - dpooled[0:rows], 0.0)
        du_ref[...] = du.astype(du_ref.dtype)
        mixed = _group_dot(pooled, w_ref, False)
        ds_ref[...] += _colsum(do_ref[...] * mixed)
        pooled_bf = pooled.astype(BF16)
        dm_bf = dmixed[0:rows].astype(BF16)
        for g in range(len(POOL_WINDOWS)):
            sl = slice(g * BLOCK, (g + 1) * BLOCK)
            dw_ref[sl, :] += lax.dot_general(pooled_bf[:, sl], dm_bf[:, sl], _TN, preferred_element_type=F32)

    return _call(body, name="pool_bwd",
                 out_shape=[_sds((lp, D_POOL), BF16), _sds((D_POOL, BLOCK), F32), _sds((1, D_POOL), F32)],
                 grid=(n_chunk,),
                 in_specs=[pl.BlockSpec((rows, D_POOL), lambda r: (jnp.maximum(r - 1, 0), cb)),
                           pl.BlockSpec((rows, D_POOL), lambda r: (r, cb)),
                           pl.BlockSpec((rows, D_POOL), lambda r: (r, 0)),
                           pl.BlockSpec((rows, D_POOL), lambda r: (jnp.minimum(r + 1, n_chunk - 1), 0)),
                           pl.BlockSpec((D_POOL, BLOCK), lambda r: (0, 0)),
                           pl.BlockSpec((1, D_POOL), lambda r: (0, 0))],
                 out_specs=[pl.BlockSpec((rows, D_POOL), lambda r: (r, 0)),
                            pl.BlockSpec((D_POOL, BLOCK), lambda r: (0, 0)),
                            pl.BlockSpec((1, D_POOL), lambda r: (0, 0))])(proj, proj, d_out, d_out, w_pool_bf, scale)


SUBLANES = 8


def _sub_shifts(x, down):
    shift = _shift_down if down else _shift_up
    return [shift(x, b) for b in range(SUBLANES)]


def _lagged(shifts, d, lo, n, down):
    a, b = divmod(d, SUBLANES)
    start = lo - SUBLANES * a if down else lo + SUBLANES * a
    return shifts[b][start:start + n]


def _conv_taps(u_shifts, wdw_ref, lo, n):
    y = None
    for d in range(CONV_WIDTH):
        term = wdw_ref[CONV_WIDTH - 1 - d:CONV_WIDTH - d, :] * _lagged(u_shifts, d, lo, n, True)
        y = term if y is None else y + term
    return y


def _layernorm_stats(y):
    mu = jnp.mean(y, axis=-1, keepdims=True)
    yc = y - mu
    rstd = lax.rsqrt(jnp.mean(yc * yc, axis=-1, keepdims=True) + EPS)
    return yc * rstd, rstd


def _conv_fwd(proj, wdw, bdw, ln_g, ln_b, wpw_bf):
    lp = proj.shape[0]
    rows = BLOCK
    ca, cg = COL_A // D_CONV, COL_G // D_CONV

    def body(ap_ref, a_ref, gp_ref, g_ref, wdw_ref, b_ref, lg_ref, lb_ref, wpw_ref, o_ref):
        r = pl.program_id(0)
        a = jnp.concatenate([ap_ref[rows - HALO:rows], a_ref[...]], axis=0)
        g = jnp.concatenate([gp_ref[rows - HALO:rows], g_ref[...]], axis=0)
        u = a * _sigmoid(g)
        y = _conv_taps(_sub_shifts(u, True), wdw_ref, HALO, rows) + b_ref[...]
        xhat, _ = _layernorm_stats(y)
        yn = xhat * lg_ref[...] + lb_ref[...]
        pos = r * rows + lax.broadcasted_iota(jnp.int32, (rows, 1), 0) - PAD
        s = jnp.where(pos >= 0, yn * _sigmoid(yn), 0.0)
        o_ref[...] = jnp.dot(s.astype(BF16), wpw_ref[...], preferred_element_type=F32)

    prev = lambda c: (lambda r: (jnp.maximum(r - 1, 0), c))
    cur = lambda c: (lambda r: (r, c))
    const = lambda shape: pl.BlockSpec(shape, lambda r: (0, 0))
    return _call(body, name="conv_fwd", out_shape=_sds((lp, D_CONV), F32), grid=(lp // rows,),
                 in_specs=[pl.BlockSpec((rows, D_CONV), prev(ca)), pl.BlockSpec((rows, D_CONV), cur(ca)),
                           pl.BlockSpec((rows, D_CONV), prev(cg)), pl.BlockSpec((rows, D_CONV), cur(cg)),
                           const((HALO, D_CONV)), const((1, D_CONV)), const((1, D_CONV)), const((1, D_CONV)),
                           const((D_CONV, D_CONV))],
                 out_specs=pl.BlockSpec((rows, D_CONV), cur(0)))(proj, proj, proj, proj, wdw, bdw, ln_g, ln_b, wpw_bf)


def _conv_bwd(proj, d_out, wdw, bdw, ln_g, ln_b, wpw_bf):
    lp = proj.shape[0]
    rows = BLOCK
    n_chunk = lp // rows
    ca, cg = COL_A // D_CONV, COL_G // D_CONV
    ext = rows + HALO

    def body(ap_ref, a_ref, an_ref, gp_ref, g_ref, gn_ref, do_ref, don_ref, wdw_ref, b_ref, lg_ref, lb_ref,
             wpw_ref, da_ref, dg_ref, dwdw_ref, db_ref, dlg_ref, dlb_ref, dwpw_ref):
        r = pl.program_id(0)

        @pl.when(r == 0)
        def _():
            for ref in (dwdw_ref, db_ref, dlg_ref, dlb_ref, dwpw_ref):
                ref[...] = jnp.zeros(ref.shape, F32)

        a3 = jnp.concatenate([ap_ref[rows - HALO:rows], a_ref[...], an_ref[0:HALO]], axis=0)
        g3 = jnp.concatenate([gp_ref[rows - HALO:rows], g_ref[...], gn_ref[0:HALO]], axis=0)
        sig3 = _sigmoid(g3)
        u3 = a3 * sig3
        u_shifts = _sub_shifts(u3, True)
        y = _conv_taps(u_shifts, wdw_ref, HALO, ext) + b_ref[...]
        xhat, rstd = _layernorm_stats(y)
        yn = xhat * lg_ref[...] + lb_ref[...]
        sgm = _sigmoid(yn)
        pos = r * rows + lax.broadcasted_iota(jnp.int32, (ext, 1), 0) - PAD
        valid = (pos >= 0) & (pos < lp - PAD)
        d_ext = jnp.concatenate([do_ref[...], don_ref[0:HALO]], axis=0)
        ds = lax.dot_general(d_ext.astype(BF16), wpw_ref[...], _NT, preferred_element_type=F32)
        dyn = jnp.where(valid, ds * (sgm * (1.0 + yn * (1.0 - sgm))), 0.0)
        dxh = dyn * lg_ref[...]
        dy = rstd * (dxh - jnp.mean(dxh, axis=-1, keepdims=True)
                     - xhat * jnp.mean(dxh * xhat, axis=-1, keepdims=True))
        s_cur = jnp.where(valid[0:rows], (yn * sgm)[0:rows], 0.0)
        dwpw_ref[...] += lax.dot_general(s_cur.astype(BF16), do_ref[...].astype(BF16), _TN,
                                         preferred_element_type=F32)
        dlg_ref[...] += _colsum(dyn[0:rows] * xhat[0:rows])
        dlb_ref[...] += _colsum(dyn[0:rows])
        dy_cur = dy[0:rows]
        db_ref[...] += _colsum(dy_cur)
        dy_shifts = _sub_shifts(dy, False)
        du = None
        for d in range(CONV_WIDTH):
            k = CONV_WIDTH - 1 - d
            dwdw_ref[k:k + 1, :] += _colsum(dy_cur * _lagged(u_shifts, d, HALO, rows, True))
            term = wdw_ref[k:k + 1, :] * _lagged(dy_shifts, d, 0, rows, False)
            du = term if du is None else du + term
        du = jnp.where(pos[0:rows] >= 0, du, 0.0)
        sig = sig3[HALO:HALO + rows]
        da_ref[...] = (du * sig).astype(da_ref.dtype)
        dg_ref[...] = (du * a_ref[...] * sig * (1.0 - sig)).astype(dg_ref.dtype)

    prev = lambda c: (lambda r: (jnp.maximum(r - 1, 0), c))
    cur = lambda c: (lambda r: (r, c))
    nxt = lambda c: (lambda r: (jnp.minimum(r + 1, n_chunk - 1), c))
    const = lambda shape: pl.BlockSpec(shape, lambda r: (0, 0))
    blk = lambda f: pl.BlockSpec((rows, D_CONV), f)
    return _call(body, name="conv_bwd",
                 out_shape=[_sds((lp, D_CONV), BF16), _sds((lp, D_CONV), BF16), _sds((HALO, D_CONV), F32),
                            _sds((1, D_CONV), F32), _sds((1, D_CONV), F32), _sds((1, D_CONV), F32),
                            _sds((D_CONV, D_CONV), F32)],
                 grid=(n_chunk,),
                 in_specs=[blk(prev(ca)), blk(cur(ca)), blk(nxt(ca)), blk(prev(cg)), blk(cur(cg)), blk(nxt(cg)),
                           blk(cur(0)), blk(nxt(0)),
                           const((HALO, D_CONV)), const((1, D_CONV)), const((1, D_CONV)), const((1, D_CONV)),
                           const((D_CONV, D_CONV))],
                 out_specs=[blk(cur(0)), blk(cur(0)), const((HALO, D_CONV)), const((1, D_CONV)),
                            const((1, D_CONV)), const((1, D_CONV)), const((D_CONV, D_CONV))],
                 vmem_mb=48)(proj, proj, proj, proj, proj, proj, d_out, d_out, wdw, bdw, ln_g, ln_b, wpw_bf)


_ANY = pl.BlockSpec(memory_space=pl.ANY)


def _place():
    return lax.axis_index("x"), lax.axis_index("y"), lax.axis_index("c")


def _xy_peers(x, y):
    return [(1 - x, y), (x, 1 - y), (1 - x, 1 - y)]


def _exchange_now(name, ex):
    n = ex.n

    def body(*refs):
        ins, outs, sems = refs[:n], refs[n:2 * n], refs[2 * n:]
        ex.run(False, ins, outs, *sems[:3])
        ex.run(True, ins, outs, *sems[:3])
        if ex.split:
            ex.forward(False, outs, *sems[3:])
            ex.forward(True, outs, *sems[3:])

    return _call(body, name=name, out_shape=ex.out_shape(), in_specs=[_ANY] * n, out_specs=[_ANY] * n,
                 scratch=ex.scratch())(*ex.arrs)


def _scatter_and_swap(name, parts, arrs):
    ex = _Exchange("scatter", parts)
    n, m = ex.n, len(arrs)

    def body(*refs):
        p_in, a_in = refs[:n], refs[n:n + m]
        p_out, a_out = refs[n + m:2 * n + m], refs[2 * n + m:2 * (n + m)]
        send, recv, local, s_send, s_recv = refs[2 * (n + m):]
        x, y, c = _place()
        ex.run(False, p_in, p_out, send, recv, local)
        swaps = [pltpu.make_async_remote_copy(src_ref=a_in[a], dst_ref=a_out[a], send_sem=s_send.at[a],
                                              recv_sem=s_recv.at[a], device_id=(x, y, 1 - c), device_id_type=MESH)
                 for a in range(m)]
        for cp in swaps:
            cp.start()
        for cp in swaps:
            cp.wait()
        ex.run(True, p_in, p_out, send, recv, local)

    res = _call(body, name=name, out_shape=ex.out_shape() + [_sds(a.shape, a.dtype) for a in arrs],
                in_specs=[_ANY] * (n + m), out_specs=[_ANY] * (n + m),
                scratch=ex.scratch() + [pltpu.SemaphoreType.DMA((m,)), pltpu.SemaphoreType.DMA((m,))])(*parts, *arrs)
    return list(res[:n]), list(res[n:])


def _swap_core(name, arrs):
    n = len(arrs)

    def body(*refs):
        ins, outs = refs[:n], refs[n:2 * n]
        send, recv = refs[2 * n:]
        x, y, c = _place()
        remote = []
        for a in range(n):
            cp = pltpu.make_async_remote_copy(src_ref=ins[a], dst_ref=outs[a], send_sem=send.at[a],
                                              recv_sem=recv.at[a], device_id=(x, y, 1 - c), device_id_type=MESH)
            cp.start()
            remote.append(cp)
        for cp in remote:
            cp.wait()

    return _call(body, name=name, out_shape=[_sds(a.shape, a.dtype) for a in arrs],
                 in_specs=[_ANY] * n, out_specs=[_ANY] * n,
                 scratch=[pltpu.SemaphoreType.DMA((n,)), pltpu.SemaphoreType.DMA((n,))])(*arrs)


def _gather_all(name, arr):
    flips = [(fx, fy, fc) for fx in (0, 1) for fy in (0, 1) for fc in (0, 1)][1:]

    def body(in_ref, out_ref, send, recv, local):
        x, y, c = _place()
        me = 4 * x + 2 * y + c
        own = pltpu.make_async_copy(in_ref, out_ref.at[me], local)
        own.start()
        remote = []
        for k, (fx, fy, fc) in enumerate(flips):
            peer = (1 - x if fx else x, 1 - y if fy else y, 1 - c if fc else c)
            cp = pltpu.make_async_remote_copy(src_ref=in_ref, dst_ref=out_ref.at[me], send_sem=send.at[k],
                                              recv_sem=recv.at[k], device_id=peer, device_id_type=MESH)
            cp.start()
            remote.append(cp)
        for cp in remote:
            cp.wait()
        own.wait()

    return _call(body, name=name, out_shape=_sds((N_DEV,) + arr.shape, arr.dtype), in_specs=[_ANY], out_specs=_ANY,
                 scratch=[pltpu.SemaphoreType.DMA((N_DEV - 1,)), pltpu.SemaphoreType.DMA((N_DEV - 1,)),
                          pltpu.SemaphoreType.DMA])(arr)


def _sum_slots(name, stacked, out_dtype=F32):
    s, r, c = stacked.shape
    tr = _pick(r, (256, 128, 64, 8))

    def body(in_ref, o_ref):
        acc = in_ref[0].astype(F32)
        for k in range(1, s):
            acc = acc + in_ref[k].astype(F32)
        o_ref[...] = acc.astype(o_ref.dtype)

    return _call(body, name=name, out_shape=_sds((r, c), out_dtype), grid=(r // tr,),
                 in_specs=[pl.BlockSpec((s, tr, c), lambda i: (0, i, 0))],
                 out_specs=pl.BlockSpec((tr, c), lambda i: (i, 0)))(stacked)


def _sum_slots_layers(name, r0, r1):
    s, r, c = r0.shape
    tr = _pick(r, (256, 128))

    def body(a_ref, b_ref, o_ref):
        def total(ref):
            acc = ref[0].astype(F32)
            for k in range(1, s):
                acc = acc + ref[k].astype(F32)
            return acc

        @pl.when(pl.program_id(0) == 0)
        def _():
            o_ref[...] = total(a_ref)

        @pl.when(pl.program_id(0) == 1)
        def _():
            o_ref[...] = total(b_ref)

    return _call(body, name=name, out_shape=_sds((2, r, c), F32), grid=(2, r // tr),
                 in_specs=[pl.BlockSpec((s, tr, c), lambda l, i: (0, i * (1 - l), 0)),
                           pl.BlockSpec((s, tr, c), lambda l, i: (0, i * l, 0))],
                 out_specs=pl.BlockSpec((None, tr, c), lambda l, i: (l, i, 0)))(r0, r1)


def _adamw_math(w, g, m, v):
    m = ADAM_B1 * m + (1.0 - ADAM_B1) * g
    v = ADAM_B2 * v + (1.0 - ADAM_B2) * (g * g)
    m_hat = m / (1.0 - ADAM_B1 ** ADAM_STEP)
    v_hat = v / (1.0 - ADAM_B2 ** ADAM_STEP)
    delta = -ADAM_LR * (m_hat / (jnp.sqrt(v_hat) + ADAM_EPS) + ADAM_WD * w)
    return delta, m, v


def _adamw(name, w, m, v, g_mine, g_other):
    l, r, c = w.shape
    tr = _pick(r, (128, 64, 8))

    def body(w_ref, m_ref, v_ref, ga_ref, gb_ref, g_ref, d_ref, nm_ref, nv_ref):
        g = ga_ref[...] + gb_ref[...]
        delta, nm, nv = _adamw_math(w_ref[...], g, m_ref[...], v_ref[...])
        g_ref[...] = g
        d_ref[...] = delta
        nm_ref[...] = nm
        nv_ref[...] = nv

    spec = pl.BlockSpec((None, tr, c), lambda li, i: (li, i, 0))
    return _call(body, name=name, out_shape=[_sds(w.shape, F32)] * 4, grid=(l, r // tr),
                 in_specs=[spec] * 5, out_specs=[spec] * 4, vmem_mb=48)(w, m, v, g_mine, g_other)


def _adamw_flat(name, w, m, v, g):
    r, c = w.shape
    tr = _pick(r, (256, 128, 64, 8))

    def body(w_ref, m_ref, v_ref, g_ref, d_ref, nm_ref, nv_ref):
        delta, nm, nv = _adamw_math(w_ref[...], g_ref[...], m_ref[...], v_ref[...])
        d_ref[...] = delta
        nm_ref[...] = nm
        nv_ref[...] = nv

    spec = pl.BlockSpec((tr, c), lambda i: (i, 0))
    return _call(body, name=name, out_shape=[_sds(w.shape, F32)] * 3, grid=(r // tr,),
                 in_specs=[spec] * 4, out_specs=[spec] * 3)(w, m, v, g)


def _pack(arrs, row_multiple=256):
    flat = jnp.concatenate([a.reshape(-1).astype(F32) for a in arrs])
    per = BLOCK * row_multiple
    total = -(-flat.shape[0] // per) * per
    return jnp.pad(flat, (0, total - flat.shape[0])).reshape(total // BLOCK, BLOCK)


def _unpack(buf, shapes):
    flat = buf.reshape(-1)
    outs, off = [], 0
    for s in shapes:
        size = 1
        for d in s:
            size *= d
        outs.append(flat[off:off + size].reshape(s))
        off += size
    return outs


def kernel(x, meta_tokens, pre_mix_g, w_in, w_pool, pool_scale, w_dw, b_dw, conv_ln_g, conv_ln_b, w_pw, mix_out_g, w_out, post_mix_g, pre_ffn_g, w_gate, w_up, w_down, post_ffn_g, loss_target, m_meta_tokens, m_pre_mix_g, m_w_in, m_w_pool, m_pool_scale, m_w_dw, m_b_dw, m_conv_ln_g, m_conv_ln_b, m_w_pw, m_mix_out_g, m_w_out, m_post_mix_g, m_pre_ffn_g, m_w_gate, m_w_up, m_w_down, m_post_ffn_g, v_meta_tokens, v_pre_mix_g, v_w_in, v_w_pool, v_pool_scale, v_w_dw, v_b_dw, v_conv_ln_g, v_conv_ln_b, v_w_pw, v_mix_out_g, v_w_out, v_post_mix_g, v_pre_ffn_g, v_w_gate, v_w_up, v_w_down, v_post_ffn_g):
    seq = x.shape[1]
    lp = PAD + N_META + seq
    depth = w_in.shape[0]
    xy = 2 * lax.axis_index("x") + lax.axis_index("y")

    small_shapes = [meta_tokens.shape, w_dw.shape, w_pw.shape]
    small_local = _pack([meta_tokens, w_dw, w_pw], row_multiple=8)
    big_names = ["w_in", "w_out", "w_gate", "w_up", "w_down"]
    big_local = {(k, l): w[l:l + 1].astype(BF16)
                 for k, w in zip(big_names, (w_in, w_out, w_gate, w_up, w_down)) for l in range(depth)}
    wg = {}
    half = w_in.shape[2] // 2
    first = big_local[("w_in", 0)]
    lo_half, hi_half, small_all = _exchange_now(
        "gather_first", _Exchange("gather", [first[:, :, :half], first[:, :, half:], small_local], split=True))
    wg[("w_in", 0)] = jnp.concatenate([lo_half, hi_half], axis=3)
    metas, wdws, wpws = [], [], []
    for s in range(N_SHARD):
        mt, wd, wp = _unpack(small_all[s], small_shapes)
        metas.append(mt)
        wdws.append(wd)
        wpws.append(wp)
    meta_full = jnp.concatenate(metas, axis=1)
    wdw_full = jnp.concatenate(wdws, axis=2)
    wpw_full = jnp.concatenate(wpws, axis=1)
    wdw_pad = jnp.pad(wdw_full, ((0, 0), (0, HALO - CONV_WIDTH), (0, 0)))
    wpw_bf = wpw_full.astype(BF16)
    wpool_bf = w_pool.reshape(depth, D_POOL, BLOCK).astype(BF16)

    row = lambda a, i: a[i][None, :]

    h = jnp.concatenate([jnp.zeros((PAD, D_MODEL), F32), meta_full, x[0]], axis=0)
    target = jnp.pad(loss_target[0], ((PAD + N_META, 0), (0, 0)))
    u = _rowwise("pre_mix_norm0", lambda hh, g: _rms(hh, g), [h], [row(pre_mix_g, 0)], [(D_MODEL, BF16)])[0]
    saved = []
    for i in range(depth):
        proj = _mm_nn_col("in_proj%d" % i, u, wg[("w_in", i)], 0, F32)
        ride = [(k, i) for k in big_names[1:]] + ([("w_in", i + 1)] if i + 1 < depth else [])
        o_attn, tot, got = _attn_fwd(proj, _Exchange("gather", [big_local[k] for k in ride], split=True))
        wg.update(zip(ride, got))
        o_pool = _pool_fwd(proj, wpool_bf[i], row(pool_scale, i))
        o_conv = _conv_fwd(proj, wdw_pad[i], row(b_dw, i), row(conv_ln_g, i), row(conv_ln_b, i), wpw_bf[i])

        mix_gains = [row(mix_out_g, i)[:, :D_ATTN], row(mix_out_g, i)[:, D_ATTN:D_ATTN + D_POOL],
                     row(mix_out_g, i)[:, D_ATTN + D_POOL:]]
        merged = _rowwise("merge%d" % i, _merge, [o_attn, o_pool, o_conv], mix_gains, [(D_MODEL, BF16)])[0]
        mix = _mm_nn_row("out_proj%d" % i, merged, wg[("w_out", i)], 0)

        def post_mix(hh, mx, g1, g2):
            h1 = hh + _rms(mx, g1)
            return h1, _rms(h1, g2)

        h1, u2 = _rowwise("post_mix%d" % i, post_mix, [h, mix], [row(post_mix_g, i), row(pre_ffn_g, i)],
                          [(D_MODEL, F32), (D_MODEL, BF16)])
        gate = _mm_nn_col("ffn_gate%d" % i, u2, wg[("w_gate", i)], 0, F32)
        up = _mm_nn_col("ffn_up%d" % i, u2, wg[("w_up", i)], 0, F32)
        act = _rowwise("swiglu%d" % i, lambda gt, p: gt * _sigmoid(gt) * p, [gate, up], [], [(D_FF, BF16)],
                       vmem_mb=48)[0]
        ff = _mm_nn_row("ffn_down%d" % i, act, wg[("w_down", i)], 0)
        rec = dict(h=h, u=u, proj=proj, tot=tot, o_attn=o_attn, o_pool=o_pool, o_conv=o_conv, merged=merged,
                   mix=mix, h1=h1, u2=u2, gate=gate, up=up, act=act, ff=ff)
        saved.append(rec)
        if i + 1 < depth:
            def post_ffn(hh, f, g1, g2):
                h2 = hh + _rms(f, g1)
                return h2, _rms(h2, g2)

            h, u = _rowwise("post_ffn%d" % i, post_ffn, [h1, ff], [row(post_ffn_g, i), row(pre_mix_g, i + 1)],
                            [(D_MODEL, F32), (D_MODEL, BF16)])
        else:
            def head(row0, hh, f, tgt, g1):
                y = hh + _rms(f, g1)
                rid = row0 + lax.broadcasted_iota(jnp.int32, (y.shape[0], 1), 0)
                err = jnp.where(rid >= PAD + N_META, y - tgt, 0.0)
                part = 0.5 * jnp.sum(jnp.mean(err * err, axis=-1, keepdims=True), axis=0, keepdims=True)
                return err * (1.0 / D_MODEL), jnp.broadcast_to(part, (8, BLOCK))

            dh, loss_part = _rowwise("loss_head", head, [h1, ff, target], [row(post_ffn_g, i)],
                                     [(D_MODEL, F32)], accs=[(8, BLOCK)], with_row0=True)

    loss = lax.psum(loss_part[0, 0], ("x", "y", "c"))

    small_grads = {}
    big_parts = {}
    received = {}
    for i in reversed(range(depth)):
        rec = saved[i]

        def post_ffn_b(f, d, g):
            _, vjp = jax.vjp(_rms, f, g)
            df, dg = vjp(d)
            return df, dg

        dff, g_post_ffn = _rowwise("post_ffn_b%d" % i, post_ffn_b, [rec["ff"], dh], [row(post_ffn_g, i)],
                                   [(D_MODEL, BF16)], accs=[(1, D_MODEL)])
        big_parts[("w_down", i)] = _mm_tn_row("dw_down%d" % i, rec["act"], dff)
        dgate, dup = _mm_nt_row_swiglu("d_gate_up%d" % i, dff, wg[("w_down", i)], 0, rec["gate"], rec["up"])
        big_parts[("w_gate", i)] = _mm_tn_col("dw_gate%d" % i, rec["u2"], dgate)
        big_parts[("w_up", i)] = _mm_tn_col("dw_up%d" % i, rec["u2"], dup)
        du2a = _mm_nt_col("d_u2_gate%d" % i, dgate, wg[("w_gate", i)], 0)
        du2b = _mm_nt_col("d_u2_up%d" % i, dup, wg[("w_up", i)], 0)

        def post_mix_b(h1v, mx, d, da, db, g1, g2):
            _, vjp2 = jax.vjp(_rms, h1v, g2)
            dh1, dg2 = vjp2(da + db)
            dmid = d + dh1
            _, vjp1 = jax.vjp(_rms, mx, g1)
            dmx, dg1 = vjp1(dmid)
            return dmid, dmx, dg1, dg2

        dmid, dmix, g_post_mix, g_pre_ffn = _rowwise(
            "post_mix_b%d" % i, post_mix_b, [rec["h1"], rec["mix"], dh, du2a, du2b],
            [row(post_mix_g, i), row(pre_ffn_g, i)], [(D_MODEL, F32), (D_MODEL, BF16)],
            accs=[(1, D_MODEL), (1, D_MODEL)], vmem_mb=48)
        big_parts[("w_out", i)] = _mm_tn_row("dw_out%d" % i, rec["merged"], dmix)
        dmerged = _mm_nt_row("d_merged%d" % i, dmix, wg[("w_out", i)], 0, F32)

        def merge_b(oa, op, oc, d, ga, gp, gc):
            _, vjp = jax.vjp(_merge, oa, op, oc, ga, gp, gc)
            return vjp(d)

        mix_gains = [row(mix_out_g, i)[:, :D_ATTN], row(mix_out_g, i)[:, D_ATTN:D_ATTN + D_POOL],
                     row(mix_out_g, i)[:, D_ATTN + D_POOL:]]
        do_attn, do_pool, do_conv, g_mo_a, g_mo_p, g_mo_c = _rowwise(
            "merge_b%d" % i, merge_b, [rec["o_attn"], rec["o_pool"], rec["o_conv"], dmerged], mix_gains,
            [(D_ATTN, F32), (D_POOL, F32), (D_CONV, F32)], accs=[(1, D_ATTN), (1, D_POOL), (1, D_CONV)])
        g_mix_out = jnp.concatenate([g_mo_a, g_mo_p, g_mo_c], axis=1)
        du_pool, g_w_pool, g_pool_scale = _pool_bwd(rec["proj"], do_pool, wpool_bf[i], row(pool_scale, i))
        da, dgt, g_w_dw, g_b_dw, g_ln_g, g_ln_b, g_w_pw = _conv_bwd(
            rec["proj"], do_conv, wdw_pad[i], row(b_dw, i), row(conv_ln_g, i), row(conv_ln_b, i), wpw_bf[i])
        big_parts[("w_pw", i)] = g_w_pw.reshape(N_SHARD, D_CONV // N_SHARD, D_CONV).astype(BF16)
        ride = [(k, i) for k in big_names[1:] + ["w_pw"]] + ([("w_in", i + 1)] if i + 1 < depth else [])
        dq, dk, dv, got = _attn_bwd(rec["proj"], rec["tot"], do_attn,
                                    _Exchange("scatter", [big_parts[k] for k in ride]))
        received.update(zip(ride, got))
        dproj = jnp.concatenate([dq, dk, dv, du_pool, da, dgt], axis=1)
        big_parts[("w_in", i)] = _mm_tn_col("dw_in%d" % i, rec["u"], dproj)
        du = _mm_nt_col("d_u%d" % i, dproj, wg[("w_in", i)], 0)

        def pre_mix_b(hv, d, dd, g):
            _, vjp = jax.vjp(_rms, hv, g)
            dhh, dg = vjp(dd)
            return d + dhh, dg

        dh, g_pre_mix = _rowwise("pre_mix_b%d" % i, pre_mix_b, [rec["h"], dmid, du], [row(pre_mix_g, i)],
                                 [(D_MODEL, F32)], accs=[(1, D_MODEL)])
        small_grads[i] = dict(pre_mix_g=g_pre_mix[0], w_pool=g_w_pool, pool_scale=g_pool_scale[0],
                              w_dw=g_w_dw[:CONV_WIDTH], b_dw=g_b_dw[0], conv_ln_g=g_ln_g[0], conv_ln_b=g_ln_b[0],
                              mix_out_g=g_mix_out[0], post_mix_g=g_post_mix[0],
                              pre_ffn_g=g_pre_ffn[0], post_ffn_g=g_post_ffn[0])

    grad_x = dh[PAD + N_META:][None]
    g_meta_part = dh[PAD:PAD + N_META]

    rep_names = ["pre_mix_g", "pool_scale", "b_dw", "conv_ln_g", "conv_ln_b", "mix_out_g", "post_mix_g",
                 "pre_ffn_g", "post_ffn_g", "w_pool"]
    stack2 = lambda nme: jnp.stack([small_grads[l][nme] for l in range(depth)])
    small_list = [stack2(nme) for nme in rep_names] + [g_meta_part, stack2("w_dw")]
    small_list[rep_names.index("w_pool")] = small_list[rep_names.index("w_pool")].reshape(w_pool.shape)
    full_shapes = [a.shape for a in small_list]
    packed = _pack(small_list)
    summed = _sum_slots("sum_small", _gather_all("gather_small_grads", packed))
    full = _unpack(summed, full_shapes)
    rep_grads = dict(zip(rep_names, full[:len(rep_names)]))
    g_meta = lax.dynamic_slice_in_dim(full[-2], xy * meta_tokens.shape[1], meta_tokens.shape[1], axis=1)
    g_w_dw = lax.dynamic_slice_in_dim(full[-1], xy * w_dw.shape[2], w_dw.shape[2], axis=2)

    rep_w = dict(pre_mix_g=pre_mix_g, pool_scale=pool_scale, b_dw=b_dw, conv_ln_g=conv_ln_g, conv_ln_b=conv_ln_b,
                 mix_out_g=mix_out_g, post_mix_g=post_mix_g, pre_ffn_g=pre_ffn_g, post_ffn_g=post_ffn_g,
                 w_pool=w_pool)
    rep_m = dict(pre_mix_g=m_pre_mix_g, pool_scale=m_pool_scale, b_dw=m_b_dw, conv_ln_g=m_conv_ln_g,
                 conv_ln_b=m_conv_ln_b, mix_out_g=m_mix_out_g, post_mix_g=m_post_mix_g, pre_ffn_g=m_pre_ffn_g,
                 post_ffn_g=m_post_ffn_g, w_pool=m_w_pool)
    rep_v = dict(pre_mix_g=v_pre_mix_g, pool_scale=v_pool_scale, b_dw=v_b_dw, conv_ln_g=v_conv_ln_g,
                 conv_ln_b=v_conv_ln_b, mix_out_g=v_mix_out_g, post_mix_g=v_post_mix_g, pre_ffn_g=v_pre_ffn_g,
                 post_ffn_g=v_post_ffn_g, w_pool=v_w_pool)
    sm_names = rep_names + ["meta_tokens", "w_dw"]
    sm_w = [rep_w[k] for k in rep_names] + [meta_tokens, w_dw]
    sm_m = [rep_m[k] for k in rep_names] + [m_meta_tokens, m_w_dw]
    sm_v = [rep_v[k] for k in rep_names] + [v_meta_tokens, v_w_dw]
    sm_g = [rep_grads[k] for k in rep_names] + [g_meta, g_w_dw]
    sm_shapes = [a.shape for a in sm_w]
    sm_delta, sm_nm, sm_nv = _adamw_flat("adamw_small", _pack(sm_w), _pack(sm_m), _pack(sm_v), _pack(sm_g))
    small_out = {}
    for k, g, d, nm, nv in zip(sm_names, sm_g, _unpack(sm_delta, sm_shapes), _unpack(sm_nm, sm_shapes),
                               _unpack(sm_nv, sm_shapes)):
        small_out[k] = (g, d, nm, nv)

    assert depth == 2
    early = big_names[1:] + ["w_pw"]
    plane_sums = {k: _sum_slots_layers("sum_%s" % k, received[(k, 0)], received[(k, 1)]) for k in early}
    last, swapped = _scatter_and_swap("scatter_last", [big_parts[("w_in", 0)]], [plane_sums[k] for k in early])
    other_sums = dict(zip(early, swapped))
    received[("w_in", 0)] = last[0]
    plane_sums["w_in"] = _sum_slots_layers("sum_w_in", received[("w_in", 0)], received[("w_in", 1)])
    other_sums["w_in"] = _swap_core("swap_core_w_in", [plane_sums["w_in"]])[0]
    big_w = dict(w_in=(w_in, m_w_in, v_w_in), w_out=(w_out, m_w_out, v_w_out), w_gate=(w_gate, m_w_gate, v_w_gate),
                 w_up=(w_up, m_w_up, v_w_up), w_down=(w_down, m_w_down, v_w_down), w_pw=(w_pw, m_w_pw, v_w_pw))
    big_out = {}
    for k in early + ["w_in"]:
        w, m, v = big_w[k]
        big_out[k] = _adamw("adamw_%s" % k, w, m, v, plane_sums[k], other_sums[k])

    order = ["meta_tokens", "pre_mix_g", "w_in", "w_pool", "pool_scale", "w_dw", "b_dw", "conv_ln_g", "conv_ln_b",
             "w_pw", "mix_out_g", "w_out", "post_mix_g", "pre_ffn_g", "w_gate", "w_up", "w_down", "post_ffn_g"]
    res = lambda k: big_out[k] if k in big_out else small_out[k]
    outs = [loss, grad_x]
    for part in range(4):
        outs += [res(k)[part] for k in order]
    return tuple(outs)
```

```python
import functools

import jax
import jax.numpy as jnp
from jax import lax
from jax.experimental import pallas as pl
from jax.experimental.pallas import tpu as pltpu

F32 = jnp.float32
BF16 = jnp.bfloat16

D_MODEL = 2048
N_META = 16
D_ATTN = 1024
D_POOL = 512
D_CONV = 512
POOL_WINDOWS = (2, 4, 8, 16)
CONV_WIDTH = 31
D_IN_PROJ = 3 * D_ATTN + D_POOL + 2 * D_CONV
D_FF = 5632
EPS = 1e-6
BLOCK = 128
PAD = BLOCK - N_META
HALO = 32
N_SHARD = 4
N_DEV = 8
MESH = pl.DeviceIdType.MESH

ADAM_LR = 0.001
ADAM_B1 = 0.9
ADAM_B2 = 0.999
ADAM_EPS = 1e-08
ADAM_WD = 0.01
ADAM_STEP = 10

COL_Q, COL_K, COL_V = 0, D_ATTN, 2 * D_ATTN
COL_POOL = 3 * D_ATTN
COL_A = COL_POOL + D_POOL
COL_G = COL_A + D_CONV


def _call(body, *, name, out_shape, grid=None, in_specs=None, out_specs=None, scratch=(), vmem_mb=None,
          aliases=None):
    params = {}
    if grid is not None:
        params["dimension_semantics"] = ("arbitrary",) * len(grid)
    if vmem_mb is not None:
        params["vmem_limit_bytes"] = vmem_mb << 20
    kw = dict(out_shape=out_shape, name=name, compiler_params=pltpu.CompilerParams(**params))
    if grid is not None:
        kw["grid"] = grid
    if in_specs is not None:
        kw["in_specs"] = in_specs
    if out_specs is not None:
        kw["out_specs"] = out_specs
    if scratch:
        kw["scratch_shapes"] = list(scratch)
    if aliases:
        kw["input_output_aliases"] = dict(aliases)
    return pl.pallas_call(body, **kw)


def _sds(shape, dtype):
    return jax.ShapeDtypeStruct(tuple(shape), dtype)


def _pick(n, candidates):
    for c in candidates:
        if n % c == 0:
            return c
    return n


def _rowwise(name, fn, rows_in, consts, outs, accs=(), tm=BLOCK, with_row0=False, vmem_mb=None, headless=()):
    lp = rows_in[0].shape[0]
    n_in, n_c, n_o = len(rows_in), len(consts), len(outs)
    back = lambda j: (lambda i: (jnp.maximum(i - 1, 0), 0)) if j in headless else (lambda i: (i, 0))
    rows_of = lambda j: lp - tm if j in headless else lp

    def body(*refs):
        vals = [r[...] for r in refs[:n_in + n_c]]
        if with_row0:
            vals = [pl.program_id(0) * tm] + vals
        res = fn(*vals)
        if not isinstance(res, (tuple, list)):
            res = (res,)
        o_refs = refs[n_in + n_c:n_in + n_c + n_o]
        a_refs = refs[n_in + n_c + n_o:]
        for r, v in zip(o_refs, res[:n_o]):
            r[...] = v.astype(r.dtype)
        if a_refs:
            @pl.when(pl.program_id(0) == 0)
            def _():
                for r in a_refs:
                    r[...] = jnp.zeros(r.shape, r.dtype)
            for r, v in zip(a_refs, res[n_o:]):
                r[...] += v.astype(r.dtype)

    in_specs = [pl.BlockSpec((tm, a.shape[1]), back(j)) for j, a in enumerate(rows_in)]
    in_specs += [pl.BlockSpec(c.shape, lambda i: (0, 0)) for c in consts]
    out_specs = [pl.BlockSpec((tm, w), back(n_in + j)) for j, (w, _) in enumerate(outs)]
    out_specs += [pl.BlockSpec(s, lambda i: (0, 0)) for s in accs]
    out_shape = [_sds((rows_of(n_in + j), w), dt) for j, (w, dt) in enumerate(outs)] + [_sds(s, F32) for s in accs]
    res = _call(body, name=name, out_shape=out_shape, grid=(lp // tm,), in_specs=in_specs,
                out_specs=out_specs, vmem_mb=vmem_mb)(*rows_in, *consts)
    return res


def _rms(x, g):
    return x * lax.rsqrt(jnp.mean(x * x, axis=-1, keepdims=True) + EPS) * g


def _merge(oa, op, oc, ga, gp, gc):
    return jnp.concatenate([_rms(oa, ga), _rms(op, gp), _rms(oc, gc)], axis=1)


def _colsum(x):
    return jnp.sum(x, axis=0, keepdims=True)


def _sigmoid(x):
    return 1.0 / (1.0 + jnp.exp(-x))


MM_VMEM_MB = 56


def _mm_tiles(lp):
    return _pick(lp, (1408, 384, 256, 128))


def _mm_nn_col(name, a, wg, layer, out_dtype):
    lp, k = a.shape
    n = wg.shape[3]
    tm = _mm_tiles(lp)

    def body(a_ref, w_ref, o_ref):
        o_ref[...] = jnp.dot(a_ref[...], w_ref[...], preferred_element_type=F32).astype(o_ref.dtype)

    return _call(body, name=name, out_shape=_sds((lp, N_SHARD * n), out_dtype), grid=(N_SHARD, lp // tm),
                 in_specs=[pl.BlockSpec((tm, k), lambda s, i: (i, 0)),
                           pl.BlockSpec((None, None, k, n), lambda s, i: (s, layer, 0, 0))],
                 out_specs=pl.BlockSpec((tm, n), lambda s, i: (i, s)), vmem_mb=MM_VMEM_MB)(a, wg)


def _mm_nn_col_swiglu(name, a, wg, layer, gate):
    lp, k = a.shape
    n = wg.shape[3]
    tm = _pick(lp, (384, 128))

    def body(a_ref, w_ref, g_ref, u_ref, act_ref):
        up = jnp.dot(a_ref[...], w_ref[...], preferred_element_type=F32)
        g = g_ref[...]
        u_ref[...] = up
        act_ref[...] = (g * _sigmoid(g) * up).astype(act_ref.dtype)

    blk = pl.BlockSpec((tm, n), lambda s, i: (i, s))
    return _call(body, name=name, out_shape=[_sds((lp, N_SHARD * n), F32), _sds((lp, N_SHARD * n), BF16)],
                 grid=(N_SHARD, lp // tm),
                 in_specs=[pl.BlockSpec((tm, k), lambda s, i: (i, 0)),
                           pl.BlockSpec((None, None, k, n), lambda s, i: (s, layer, 0, 0)), blk],
                 out_specs=[blk, blk], vmem_mb=MM_VMEM_MB)(a, wg, gate)


def _mm_nn_row(name, a, wg, layer):
    lp = a.shape[0]
    k, n = wg.shape[2], wg.shape[3]
    tm = _mm_tiles(lp)

    def body(a_ref, w_ref, o_ref):
        part = jnp.dot(a_ref[...], w_ref[...], preferred_element_type=F32)

        @pl.when(pl.program_id(1) == 0)
        def _():
            o_ref[...] = part

        @pl.when(pl.program_id(1) != 0)
        def _():
            o_ref[...] += part

    return _call(body, name=name, out_shape=_sds((lp, n), F32), grid=(lp // tm, N_SHARD),
                 in_specs=[pl.BlockSpec((tm, k), lambda i, s: (i, s)),
                           pl.BlockSpec((None, None, k, n), lambda i, s: (s, layer, 0, 0))],
                 out_specs=pl.BlockSpec((tm, n), lambda i, s: (i, 0)), vmem_mb=MM_VMEM_MB)(a, wg)


_NT = (((1,), (1,)), ((), ()))
_TN = (((0,), (0,)), ((), ()))


def _mm_nt_col(name, dy, wg, layer):
    lp = dy.shape[0]
    k, n = wg.shape[2], wg.shape[3]
    tm = _mm_tiles(lp)

    def body(d_ref, w_ref, o_ref):
        part = lax.dot_general(d_ref[...], w_ref[...], _NT, preferred_element_type=F32)

        @pl.when(pl.program_id(1) == 0)
        def _():
            o_ref[...] = part

        @pl.when(pl.program_id(1) != 0)
        def _():
            o_ref[...] += part

    return _call(body, name=name, out_shape=_sds((lp, k), F32), grid=(lp // tm, N_SHARD),
                 in_specs=[pl.BlockSpec((tm, n), lambda i, s: (i, s)),
                           pl.BlockSpec((None, None, k, n), lambda i, s: (s, layer, 0, 0))],
                 out_specs=pl.BlockSpec((tm, k), lambda i, s: (i, 0)), vmem_mb=MM_VMEM_MB)(dy, wg)


def _mm_nt_row(name, dy, wg, layer, out_dtype):
    lp = dy.shape[0]
    k, n = wg.shape[2], wg.shape[3]
    tm = _mm_tiles(lp)

    def body(d_ref, w_ref, o_ref):
        o_ref[...] = lax.dot_general(d_ref[...], w_ref[...], _NT, preferred_element_type=F32).astype(o_ref.dtype)

    return _call(body, name=name, out_shape=_sds((lp, N_SHARD * k), out_dtype), grid=(N_SHARD, lp // tm),
                 in_specs=[pl.BlockSpec((tm, n), lambda s, i: (i, 0)),
                           pl.BlockSpec((None, None, k, n), lambda s, i: (s, layer, 0, 0))],
                 out_specs=pl.BlockSpec((tm, k), lambda s, i: (i, s)), vmem_mb=MM_VMEM_MB)(dy, wg)


def _mm_nt_row_swiglu(name, dy, wg, layer, gate, up):
    lp = dy.shape[0]
    k, n = wg.shape[2], wg.shape[3]
    tm = _pick(lp, (384, 128))

    def body(d_ref, w_ref, g_ref, u_ref, dg_ref, du_ref):
        dact = lax.dot_general(d_ref[...], w_ref[...], _NT, preferred_element_type=F32)
        g = g_ref[...]
        sg = _sigmoid(g)
        dg_ref[...] = (dact * u_ref[...] * (sg * (1.0 + g * (1.0 - sg)))).astype(dg_ref.dtype)
        du_ref[...] = (dact * (g * sg)).astype(du_ref.dtype)

    blk = pl.BlockSpec((tm, k), lambda s, i: (i, s))
    return _call(body, name=name, out_shape=[_sds((lp, N_SHARD * k), BF16)] * 2, grid=(N_SHARD, lp // tm),
                 in_specs=[pl.BlockSpec((tm, n), lambda s, i: (i, 0)),
                           pl.BlockSpec((None, None, k, n), lambda s, i: (s, layer, 0, 0)), blk, blk],
                 out_specs=[blk, blk], vmem_mb=MM_VMEM_MB)(dy, wg, gate, up)


def _mm_tn_col(name, a, dy):
    lp, k = a.shape
    n = dy.shape[1] // N_SHARD
    tm = _mm_tiles(lp)
    tk = _pick(k, (1024, 512))

    def body(a_ref, d_ref, o_ref, acc):
        @pl.when(pl.program_id(2) == 0)
        def _():
            acc[...] = jnp.zeros(acc.shape, F32)

        acc[...] += lax.dot_general(a_ref[...], d_ref[...], _TN, preferred_element_type=F32)

        @pl.when(pl.program_id(2) == pl.num_programs(2) - 1)
        def _():
            o_ref[...] = acc[...].astype(o_ref.dtype)

    return _call(body, name=name, out_shape=_sds((N_SHARD, k, n), BF16), grid=(N_SHARD, k // tk, lp // tm),
                 in_specs=[pl.BlockSpec((tm, tk), lambda s, kk, i: (i, kk)),
                           pl.BlockSpec((tm, n), lambda s, kk, i: (i, s))],
                 out_specs=pl.BlockSpec((None, tk, n), lambda s, kk, i: (s, kk, 0)),
                 scratch=[pltpu.VMEM((tk, n), F32)], vmem_mb=MM_VMEM_MB)(a, dy)


def _mm_tn_row(name, a, dy):
    lp = a.shape[0]
    k = a.shape[1] // N_SHARD
    n = dy.shape[1]
    tm = _mm_tiles(lp)
    tn = _pick(n, (1024, 512))

    def body(a_ref, d_ref, o_ref, acc):
        @pl.when(pl.program_id(2) == 0)
        def _():
            acc[...] = jnp.zeros(acc.shape, F32)

        acc[...] += lax.dot_general(a_ref[...], d_ref[...], _TN, preferred_element_type=F32)

        @pl.when(pl.program_id(2) == pl.num_programs(2) - 1)
        def _():
            o_ref[...] = acc[...].astype(o_ref.dtype)

    return _call(body, name=name, out_shape=_sds((N_SHARD, k, n), BF16), grid=(N_SHARD, n // tn, lp // tm),
                 in_specs=[pl.BlockSpec((tm, k), lambda s, j, i: (i, s)),
                           pl.BlockSpec((tm, tn), lambda s, j, i: (i, j))],
                 out_specs=pl.BlockSpec((None, k, tn), lambda s, j, i: (s, 0, j)),
                 scratch=[pltpu.VMEM((k, tn), F32)], vmem_mb=MM_VMEM_MB)(a, dy)


SUB = 16
WIDE = 2 * BLOCK
Z_CLAMP = 20.0


def _log1m_sigmoid(z):
    return -jnp.where(z > Z_CLAMP, z, jnp.log(1.0 + jnp.exp(jnp.minimum(z, Z_CLAMP))))


def _tri(tk, kind):
    r = lax.broadcasted_iota(jnp.int32, (tk, tk), 0)
    c = lax.broadcasted_iota(jnp.int32, (tk, tk), 1)
    t = {"gt": r > c, "le": r <= c}[kind]
    return jnp.where(t, 1.0, 0.0).astype(BF16)


def _strip_mask(kind, s, tk):
    if kind == "none":
        return None
    col = lax.broadcasted_iota(jnp.int32, (SUB, tk), 1)
    row = lax.broadcasted_iota(jnp.int32, (SUB, tk), 0)
    causal = (col - row) < s * SUB
    if kind == "diag":
        return causal
    if kind == "pad":
        return col >= PAD
    return causal & (col >= PAD)


def _attn_blocks(lp):
    nb = lp // BLOCK
    assert nb % 2 == 1, "sequence must be a 128-row block plus whole 256-row blocks"
    return nb, (nb + 1) // 2


class _Exchange:
    def __init__(self, kind, arrs, split=False):
        self.kind, self.arrs, self.n, self.split = kind, list(arrs), len(arrs), split

    def out_shape(self):
        if self.kind == "gather":
            return [_sds((N_SHARD,) + a.shape, a.dtype) for a in self.arrs]
        return [_sds(a.shape, a.dtype) for a in self.arrs]

    def scratch(self):
        sems = [pltpu.SemaphoreType.DMA((3 * self.n,)), pltpu.SemaphoreType.DMA((3 * self.n,)),
                pltpu.SemaphoreType.DMA((self.n,))]
        if self.split:
            sems += [pltpu.SemaphoreType.DMA((self.n,)), pltpu.SemaphoreType.DMA((self.n,))]
        return sems

    def forward(self, wait, outs, fsend, frecv):
        x, y, c = _place()
        for a in range(self.n):
            cp = pltpu.make_async_remote_copy(src_ref=outs[a], dst_ref=outs[a], send_sem=fsend.at[a],
                                              recv_sem=frecv.at[a], device_id=(x, y, 1 - c), device_id_type=MESH)
            if wait:
                pl.when(c == a % 2)(cp.wait_send)
                pl.when(c != a % 2)(cp.wait_recv)
            else:
                pl.when(c == a % 2)(cp.start)

    def copies(self, a, ins, outs, send, recv, local):
        x, y, c = _place()
        me = 2 * x + y
        own = ins[a] if self.kind == "gather" else ins[a].at[me]
        out = [pltpu.make_async_copy(own, outs[a].at[me], local.at[a])]
        for r, (px, py) in enumerate(_xy_peers(x, y)):
            src = ins[a] if self.kind == "gather" else ins[a].at[2 * px + py]
            out.append(pltpu.make_async_remote_copy(
                src_ref=src, dst_ref=outs[a].at[me], send_sem=send.at[3 * a + r], recv_sem=recv.at[3 * a + r],
                device_id=(px, py, c), device_id_type=MESH))
        return out

    def run(self, wait, ins, outs, send, recv, local):
        for a in range(self.n):
            def go(a=a):
                for cp in self.copies(a, ins, outs, send, recv, local):
                    if wait:
                        cp.wait()
                    else:
                        cp.start()
            if self.split:
                pl.when(lax.axis_index("c") == a % 2)(go)
            else:
                go()


def _attn_fwd(proj, exchange=None):
    lp = proj.shape[0]
    nb, nq = _attn_blocks(lp)
    n_pair = D_ATTN // BLOCK
    n_x = exchange.n if exchange else 0

    def body(*refs):
        q_ref, k_ref, v_ref = refs[:3]
        x_in = refs[3:3 + n_x]
        o_ref, tot_ref = refs[3 + n_x:5 + n_x]
        x_out = refs[5 + n_x:5 + 2 * n_x]
        qs, kb, vh, tri_l, tri_s, z_s, hl_s, c_s, w_s, r_s, acc_s = refs[5 + 2 * n_x:16 + 2 * n_x]
        x_sem = refs[16 + 2 * n_x:]
        p, i = pl.program_id(0), pl.program_id(1)
        m0 = lax.broadcasted_iota(jnp.int32, (1, BLOCK), 1) < (BLOCK // 2)

        if exchange:
            @pl.when((p == 0) & (i == 0))
            def _():
                exchange.run(False, x_in, x_out, *x_sem[:3])

            if exchange.split:
                @pl.when((p == n_pair - 2) & (i == 0))
                def _():
                    exchange.run(True, x_in, x_out, *x_sem[:3])
                    exchange.forward(False, x_out, *x_sem[3:])

        @pl.when(i == 0)
        def _():
            tri_l[...] = _tri(WIDE, "gt")
            tri_s[...] = _tri(BLOCK, "gt")

            def prep(b, carry):
                rows = pl.ds(pl.multiple_of(b * BLOCK, BLOCK), BLOCK)
                q = q_ref[rows, :] * 0.125
                v = v_ref[rows, :]
                qs[0, rows, :] = jnp.where(m0, q, 0.0).astype(BF16)
                qs[1, rows, :] = jnp.where(m0, 0.0, q).astype(BF16)
                kb[rows, :] = k_ref[rows, :].astype(BF16)
                vh[0, rows, :] = jnp.where(m0, v, 0.0).astype(BF16)
                vh[1, rows, :] = jnp.where(m0, 0.0, v).astype(BF16)
                return carry

            lax.fori_loop(0, nb, prep, 0)

        def tiles(q0, tq, specs):
            heads = [(t, h) for t in range(len(specs)) for h in range(2)]
            strips = [slice(s * SUB, (s + 1) * SUB) for s in range(tq // SUB)]
            for t, h in heads:
                k0, tk, _ = specs[t]
                z_s[t, h, 0:tq, 0:tk] = lax.dot_general(qs[h, pl.ds(q0, tq), :], kb[pl.ds(k0, tk), :], _NT,
                                                        preferred_element_type=F32)
            for t, h in heads:
                _, tk, kind = specs[t]
                for s, rows in enumerate(strips):
                    z = z_s[t, h, rows, 0:tk]
                    lnb = _log1m_sigmoid(z)
                    m = _strip_mask(kind, s, tk)
                    if m is not None:
                        lnb = jnp.where(m, lnb, 0.0)
                    z_s[t, h, rows, 0:tk] = z + lnb
                    hl_s[t, h, rows, 0:tk] = lnb.astype(BF16)
            for t, h in heads:
                _, tk, _ = specs[t]
                tri = tri_l if tk == WIDE else tri_s
                c_s[t, h, 0:tq, 0:tk] = jnp.dot(hl_s[t, h, 0:tq, 0:tk], tri[...], preferred_element_type=F32)
            for t, h in heads:
                _, tk, kind = specs[t]
                for s, rows in enumerate(strips):
                    r = r_s[h, rows, :]
                    c = c_s[t, h, rows, 0:tk]
                    rr = r if tk == BLOCK else jnp.concatenate([r, r], axis=1)
                    w = jnp.exp(z_s[t, h, rows, 0:tk] + c + rr)
                    m = _strip_mask(kind, s, tk)
                    if m is not None:
                        w = jnp.where(m, w, 0.0)
                    w_s[t, h, rows, 0:tk] = w.astype(BF16)
                    total = c[:, 0:1] + hl_s[t, h, rows, 0:BLOCK].astype(F32)[:, 0:1]
                    r_s[h, rows, :] = r + jnp.broadcast_to(total, (SUB, BLOCK))
            upd = None
            for t, h in heads:
                k0, tk, _ = specs[t]
                d = jnp.dot(w_s[t, h, 0:tq, 0:tk], vh[h, pl.ds(k0, tk), :], preferred_element_type=F32)
                upd = d if upd is None else upd + d
            acc_s[0:tq, :] += upd

        def finish(q0, tq):
            o_ref[pl.ds(q0, tq), :] = acc_s[0:tq, :]
            tot_ref[pl.ds(q0, tq), :] = jnp.where(m0, r_s[0, 0:tq, :], r_s[1, 0:tq, :])

        r_s[...] = jnp.zeros(r_s.shape, F32)
        acc_s[...] = jnp.zeros(acc_s.shape, F32)

        @pl.when(i == 0)
        def _():
            tiles(0, BLOCK, [(0, BLOCK, "first")])
            finish(0, BLOCK)

        @pl.when(i > 0)
        def _():
            q0 = pl.multiple_of(i * WIDE - BLOCK, BLOCK)
            full = lambda j: (pl.multiple_of(j * WIDE - BLOCK, BLOCK), WIDE, "none")
            diag, meta = (q0, WIDE, "diag"), (0, BLOCK, "pad")

            @pl.when(i == 1)
            def _():
                tiles(q0, WIDE, [diag, meta])

            @pl.when(i >= 2)
            def _():
                tiles(q0, WIDE, [diag, full(i - 1)])

                def inner(n, carry):
                    j = i - 2 - 2 * n
                    tiles(q0, WIDE, [full(j), full(j - 1)])
                    return carry

                lax.fori_loop(0, (i - 2) // 2, inner, 0)

                @pl.when(i % 2 == 1)
                def _():
                    tiles(q0, WIDE, [full(1), meta])

                @pl.when(i % 2 == 0)
                def _():
                    tiles(q0, WIDE, [meta])

            finish(q0, WIDE)

        if exchange:
            @pl.when((p == n_pair - 1) & (i == nq - 1))
            def _():
                if exchange.split:
                    exchange.forward(True, x_out, *x_sem[3:])
                else:
                    exchange.run(True, x_in, x_out, *x_sem)

    cq, ck, cv = COL_Q // BLOCK, COL_K // BLOCK, COL_V // BLOCK
    col = lambda c0: (lambda p, i: (0, c0 + p))
    scratch = [pltpu.VMEM((2, lp, BLOCK), BF16), pltpu.VMEM((lp, BLOCK), BF16), pltpu.VMEM((2, lp, BLOCK), BF16),
               pltpu.VMEM((WIDE, WIDE), BF16), pltpu.VMEM((BLOCK, BLOCK), BF16),
               pltpu.VMEM((2, 2, WIDE, WIDE), F32), pltpu.VMEM((2, 2, WIDE, WIDE), BF16),
               pltpu.VMEM((2, 2, WIDE, WIDE), F32), pltpu.VMEM((2, 2, WIDE, WIDE), BF16),
               pltpu.VMEM((2, WIDE, BLOCK), F32), pltpu.VMEM((WIDE, BLOCK), F32)]
    res = _call(body, name="attn_fwd",
                out_shape=[_sds((lp, D_ATTN), F32), _sds((lp, D_ATTN), F32)] + (exchange.out_shape() if exchange else []),
                grid=(n_pair, nq),
                in_specs=[pl.BlockSpec((lp, BLOCK), col(cq)), pl.BlockSpec((lp, BLOCK), col(ck)),
                          pl.BlockSpec((lp, BLOCK), col(cv))] + [_ANY] * n_x,
                out_specs=[pl.BlockSpec((lp, BLOCK), col(0)), pl.BlockSpec((lp, BLOCK), col(0))] + [_ANY] * n_x,
                scratch=scratch + (exchange.scratch() if exchange else []),
                vmem_mb=56)(proj, proj, proj, *(exchange.arrs if exchange else []))
    return res[0], res[1], list(res[2:])


def _attn_bwd(proj, tot, d_out, exchange=None):
    lp = proj.shape[0]
    nb, nq = _attn_blocks(lp)
    n_pair = D_ATTN // BLOCK
    n_x = exchange.n if exchange else 0
    n_s = 23

    def body(*refs):
        q_ref, k_ref, v_ref, tot_ref, do_ref = refs[:5]
        x_in = refs[5:5 + n_x]
        dq_ref, dk_ref, dv_ref = refs[5 + n_x:8 + n_x]
        x_out = refs[8 + n_x:8 + 2 * n_x]
        (qs, kb, kh, vb, doh, tge_l, tge_s, tle_l, tle_s, z_s, g_s, hl_s, c_s, gl_s, gc_s, w_s, dz_s,
         tot_s, a_s, b_s, dq_acc, dk_acc, dv_acc) = refs[8 + 2 * n_x:8 + 2 * n_x + n_s]
        x_sem = refs[8 + 2 * n_x + n_s:]
        p, i = pl.program_id(0), pl.program_id(1)
        m0 = lax.broadcasted_iota(jnp.int32, (1, BLOCK), 1) < (BLOCK // 2)

        if exchange:
            @pl.when((p == 0) & (i == 0))
            def _():
                exchange.run(False, x_in, x_out, *x_sem)

        @pl.when(i == 0)
        def _():
            tge_l[...] = _tri(WIDE, "gt")
            tge_s[...] = _tri(BLOCK, "gt")
            tle_l[...] = _tri(WIDE, "le")
            tle_s[...] = _tri(BLOCK, "le")

            def prep(b, carry):
                rows = pl.ds(pl.multiple_of(b * BLOCK, BLOCK), BLOCK)
                q = q_ref[rows, :] * 0.125
                k = k_ref[rows, :]
                do = do_ref[rows, :]
                qs[0, rows, :] = jnp.where(m0, q, 0.0).astype(BF16)
                qs[1, rows, :] = jnp.where(m0, 0.0, q).astype(BF16)
                kb[rows, :] = k.astype(BF16)
                kh[0, rows, :] = jnp.where(m0, k, 0.0).astype(BF16)
                kh[1, rows, :] = jnp.where(m0, 0.0, k).astype(BF16)
                vb[rows, :] = v_ref[rows, :].astype(BF16)
                doh[0, rows, :] = jnp.where(m0, do, 0.0).astype(BF16)
                doh[1, rows, :] = jnp.where(m0, 0.0, do).astype(BF16)
                dk_acc[rows, :] = jnp.zeros((BLOCK, BLOCK), F32)
                dv_acc[rows, :] = jnp.zeros((BLOCK, BLOCK), F32)
                return carry

            lax.fori_loop(0, nb, prep, 0)

        def wide(x, tk):
            return x if tk == BLOCK else jnp.concatenate([x, x], axis=1)

        def tiles(q0, tq, specs):
            heads = [(t, h) for t in range(len(specs)) for h in range(2)]
            strips = [slice(s * SUB, (s + 1) * SUB) for s in range(tq // SUB)]
            for t, h in heads:
                k0, tk, _ = specs[t]
                z_s[t, h, 0:tq, 0:tk] = lax.dot_general(qs[h, pl.ds(q0, tq), :], kb[pl.ds(k0, tk), :], _NT,
                                                        preferred_element_type=F32)
                g_s[t, h, 0:tq, 0:tk] = lax.dot_general(doh[h, pl.ds(q0, tq), :], vb[pl.ds(k0, tk), :], _NT,
                                                        preferred_element_type=F32)
            for t, h in heads:
                _, tk, kind = specs[t]
                for s, rows in enumerate(strips):
                    z = z_s[t, h, rows, 0:tk]
                    lnb = _log1m_sigmoid(z)
                    m = _strip_mask(kind, s, tk)
                    if m is not None:
                        lnb = jnp.where(m, lnb, 0.0)
                    z_s[t, h, rows, 0:tk] = z + lnb
                    hl_s[t, h, rows, 0:tk] = lnb.astype(BF16)
            for t, h in heads:
                _, tk, _ = specs[t]
                tri = tge_l if tk == WIDE else tge_s
                c_s[t, h, 0:tq, 0:tk] = jnp.dot(hl_s[t, h, 0:tq, 0:tk], tri[...], preferred_element_type=F32)
            for t, h in heads:
                _, tk, kind = specs[t]
                for s, rows in enumerate(strips):
                    c = c_s[t, h, rows, 0:tk]
                    total = c[:, 0:1] + hl_s[t, h, rows, 0:BLOCK].astype(F32)[:, 0:1]
                    a_next = a_s[h, rows, :] + jnp.broadcast_to(total, (SUB, BLOCK))
                    a_s[h, rows, :] = a_next
                    w = jnp.exp(z_s[t, h, rows, 0:tk] + c + wide(tot_s[h, rows, :] - a_next, tk))
                    m = _strip_mask(kind, s, tk)
                    if m is not None:
                        w = jnp.where(m, w, 0.0)
                    g = w * g_s[t, h, rows, 0:tk]
                    g_s[t, h, rows, 0:tk] = g
                    gl_s[t, h, rows, 0:tk] = g.astype(BF16)
                    w_s[t, h, rows, 0:tk] = w.astype(BF16)
            for t, h in heads:
                _, tk, _ = specs[t]
                tri = tle_l if tk == WIDE else tle_s
                gc_s[t, h, 0:tq, 0:tk] = jnp.dot(gl_s[t, h, 0:tq, 0:tk], tri[...], preferred_element_type=F32)
            for t, h in heads:
                _, tk, kind = specs[t]
                for s, rows in enumerate(strips):
                    gc = gc_s[t, h, rows, 0:tk]
                    b = b_s[h, rows, :]
                    sig = jnp.exp(z_s[t, h, rows, 0:tk])
                    dz = g_s[t, h, rows, 0:tk] - sig * (gc + wide(b, tk))
                    m = _strip_mask(kind, s, tk)
                    if m is not None:
                        dz = jnp.where(m, dz, 0.0)
                    dz_s[t, h, rows, 0:tk] = dz.astype(BF16)
                    b_s[h, rows, :] = b + jnp.broadcast_to(gc[:, tk - 1:tk], (SUB, BLOCK))
            upd = None
            for t, h in heads:
                k0, tk, _ = specs[t]
                d = jnp.dot(dz_s[t, h, 0:tq, 0:tk], kh[h, pl.ds(k0, tk), :], preferred_element_type=F32)
                upd = d if upd is None else upd + d
            dq_acc[0:tq, :] += upd
            for t, (k0, tk, _) in enumerate(specs):
                dk_acc[pl.ds(k0, tk), :] += (
                    lax.dot_general(dz_s[t, 0, 0:tq, 0:tk], qs[0, pl.ds(q0, tq), :], _TN, preferred_element_type=F32) +
                    lax.dot_general(dz_s[t, 1, 0:tq, 0:tk], qs[1, pl.ds(q0, tq), :], _TN, preferred_element_type=F32))
                dv_acc[pl.ds(k0, tk), :] += (
                    lax.dot_general(w_s[t, 0, 0:tq, 0:tk], doh[0, pl.ds(q0, tq), :], _TN, preferred_element_type=F32) +
                    lax.dot_general(w_s[t, 1, 0:tq, 0:tk], doh[1, pl.ds(q0, tq), :], _TN, preferred_element_type=F32))

        def start(q0, tq):
            tv = tot_ref[pl.ds(q0, tq), :]
            tot_s[0, 0:tq, :] = jnp.broadcast_to(tv[:, 0:1], (tq, BLOCK))
            tot_s[1, 0:tq, :] = jnp.broadcast_to(tv[:, BLOCK - 1:BLOCK], (tq, BLOCK))
            a_s[...] = jnp.zeros(a_s.shape, F32)
            b_s[...] = jnp.zeros(b_s.shape, F32)
            dq_acc[...] = jnp.zeros(dq_acc.shape, F32)

        def finish(q0, tq):
            dq_ref[pl.ds(q0, tq), :] = (dq_acc[0:tq, :] * 0.125).astype(dq_ref.dtype)

        @pl.when(i == 0)
        def _():
            start(0, BLOCK)
            tiles(0, BLOCK, [(0, BLOCK, "first")])
            finish(0, BLOCK)

        @pl.when(i > 0)
        def _():
            q0 = pl.multiple_of(i * WIDE - BLOCK, BLOCK)
            full = lambda j: (pl.multiple_of(j * WIDE - BLOCK, BLOCK), WIDE, "none")
            diag, meta = (q0, WIDE, "diag"), (0, BLOCK, "pad")
            start(q0, WIDE)

            @pl.when(i == 1)
            def _():
                tiles(q0, WIDE, [meta, diag])

            @pl.when(i >= 2)
            def _():
                odd = i % 2

                @pl.when(odd == 1)
                def _():
                    tiles(q0, WIDE, [meta, full(1)])

                @pl.when(odd == 0)
                def _():
                    tiles(q0, WIDE, [meta])

                def inner(n, carry):
                    j = 1 + odd + 2 * n
                    tiles(q0, WIDE, [full(j), full(j + 1)])
                    return carry

                lax.fori_loop(0, (i - 2) // 2, inner, 0)
                tiles(q0, WIDE, [full(i - 1), diag])

            finish(q0, WIDE)

        @pl.when(i == nq - 1)
        def _():
            dk_ref[...] = dk_acc[...].astype(dk_ref.dtype)
            dv_ref[...] = dv_acc[...].astype(dv_ref.dtype)

        if exchange:
            @pl.when((p == n_pair - 1) & (i == nq - 1))
            def _():
                exchange.run(True, x_in, x_out, *x_sem)

    cq, ck, cv = COL_Q // BLOCK, COL_K // BLOCK, COL_V // BLOCK
    col = lambda c0: (lambda p, i: (0, c0 + p))
    whole = lambda c0: pl.BlockSpec((lp, BLOCK), col(c0))
    tile4 = lambda w, dt: pltpu.VMEM((2, 2, WIDE, w), dt)
    scratch = [pltpu.VMEM((2, lp, BLOCK), BF16), pltpu.VMEM((lp, BLOCK), BF16), pltpu.VMEM((2, lp, BLOCK), BF16),
               pltpu.VMEM((lp, BLOCK), BF16), pltpu.VMEM((2, lp, BLOCK), BF16),
               pltpu.VMEM((WIDE, WIDE), BF16), pltpu.VMEM((BLOCK, BLOCK), BF16),
               pltpu.VMEM((WIDE, WIDE), BF16), pltpu.VMEM((BLOCK, BLOCK), BF16),
               tile4(WIDE, F32), tile4(WIDE, F32), tile4(WIDE, BF16), tile4(WIDE, F32),
               tile4(WIDE, BF16), tile4(WIDE, F32), tile4(WIDE, BF16), tile4(WIDE, BF16),
               pltpu.VMEM((2, WIDE, BLOCK), F32), pltpu.VMEM((2, WIDE, BLOCK), F32), pltpu.VMEM((2, WIDE, BLOCK), F32),
               pltpu.VMEM((WIDE, BLOCK), F32), pltpu.VMEM((lp, BLOCK), F32), pltpu.VMEM((lp, BLOCK), F32)]
    res = _call(body, name="attn_bwd",
                out_shape=[_sds((lp, D_ATTN), BF16)] * 3 + (exchange.out_shape() if exchange else []),
                grid=(n_pair, nq),
                in_specs=[whole(cq), whole(ck), whole(cv), whole(0), whole(0)] + [_ANY] * n_x,
                out_specs=[whole(0), whole(0), whole(0)] + [_ANY] * n_x,
                scratch=scratch + (exchange.scratch() if exchange else []),
                vmem_mb=60)(proj, proj, proj, tot, d_out, *(exchange.arrs if exchange else []))
    return res[0], res[1], res[2], list(res[3:])


def _shift_down(x, d):
    return x if d == 0 else pltpu.roll(x, d, 0)


def _shift_up(x, d):
    return x if d == 0 else pltpu.roll(x, x.shape[0] - d, 0)


def _pool_windows(ext, down):
    shift = _shift_down if down else _shift_up
    outs = []
    for g, w in enumerate(POOL_WINDOWS):
        s = ext[:, g * BLOCK:(g + 1) * BLOCK]
        d = 1
        while d < w:
            s = s + shift(s, d)
            d *= 2
        outs.append(s)
    return jnp.concatenate(outs, axis=1)


def _pool_counts(pos):
    cols = [jnp.broadcast_to(jnp.clip(pos + 1, 1, w).astype(F32), (pos.shape[0], BLOCK)) for w in POOL_WINDOWS]
    return jnp.concatenate(cols, axis=1)


def _group_dot(x, w_ref, transpose):
    outs = []
    for g in range(len(POOL_WINDOWS)):
        xg = x[:, g * BLOCK:(g + 1) * BLOCK].astype(BF16)
        wg = w_ref[g * BLOCK:(g + 1) * BLOCK, :]
        if transpose:
            outs.append(lax.dot_general(xg, wg, _NT, preferred_element_type=F32))
        else:
            outs.append(jnp.dot(xg, wg, preferred_element_type=F32))
    return jnp.concatenate(outs, axis=1)


def _pooled(prev, cur, r):
    rows = cur.shape[0]
    ext = jnp.concatenate([prev[rows - HALO:], cur], axis=0)
    pos = r * rows + lax.broadcasted_iota(jnp.int32, (rows, 1), 0) - PAD
    ws = _pool_windows(ext, down=True)[HALO:]
    return jnp.where(pos >= 0, ws / _pool_counts(pos) - cur, 0.0)


def _pool_fwd(proj, w_pool_bf, scale):
    lp = proj.shape[0]
    rows = BLOCK
    cb = COL_POOL // D_POOL

    def body(prev_ref, cur_ref, w_ref, s_ref, o_ref):
        pooled = _pooled(prev_ref[...], cur_ref[...], pl.program_id(0))
        o_ref[...] = _group_dot(pooled, w_ref, False) * s_ref[...]

    return _call(body, name="pool_fwd", out_shape=_sds((lp, D_POOL), F32), grid=(lp // rows,),
                 in_specs=[pl.BlockSpec((rows, D_POOL), lambda r: (jnp.maximum(r - 1, 0), cb)),
                           pl.BlockSpec((rows, D_POOL), lambda r: (r, cb)),
                           pl.BlockSpec((D_POOL, BLOCK), lambda r: (0, 0)),
                           pl.BlockSpec((1, D_POOL), lambda r: (0, 0))],
                 out_specs=pl.BlockSpec((rows, D_POOL), lambda r: (r, 0)))(proj, proj, w_pool_bf, scale)


def _pool_bwd(proj, d_out, w_pool_bf, scale):
    lp = proj.shape[0]
    rows = BLOCK
    n_chunk = lp // rows
    cb = COL_POOL // D_POOL

    def body(prev_ref, cur_ref, do_ref, don_ref, w_ref, s_ref, du_ref, dw_ref, ds_ref):
        r = pl.program_id(0)

        @pl.when(r == 0)
        def _():
            dw_ref[...] = jnp.zeros(dw_ref.shape, F32)
            ds_ref[...] = jnp.zeros(ds_ref.shape, F32)

        pooled = _pooled(prev_ref[...], cur_ref[...], r)
        d_ext = jnp.concatenate([do_ref[...], don_ref[0:HALO]], axis=0)
        pos = r * rows + lax.broadcasted_iota(jnp.int32, (rows + HALO, 1), 0) - PAD
        dmixed = jnp.where((pos >= 0) & (pos < lp - PAD), d_ext * s_ref[...], 0.0)
        dpooled = _group_dot(dmixed, w_ref, True)
        back = _pool_windows(dpooled / _pool_counts(pos), down=False)[0:rows]
        du = jnp.where(pos[0:rows] >= 0, back - dpooled[0:rows], 0.0)
        du_ref[...] = du.astype(du_ref.dtype)
        mixed = _group_dot(pooled, w_ref, False)
        ds_ref[...] += _colsum(do_ref[...] * mixed)
        pooled_bf = pooled.astype(BF16)
        dm_bf = dmixed[0:rows].astype(BF16)
        for g in range(len(POOL_WINDOWS)):
            sl = slice(g * BLOCK, (g + 1) * BLOCK)
            dw_ref[sl, :] += lax.dot_general(pooled_bf[:, sl], dm_bf[:, sl], _TN, preferred_element_type=F32)

    return _call(body, name="pool_bwd",
                 out_shape=[_sds((lp, D_POOL), BF16), _sds((D_POOL, BLOCK), F32), _sds((1, D_POOL), F32)],
                 grid=(n_chunk,),
                 in_specs=[pl.BlockSpec((rows, D_POOL), lambda r: (jnp.maximum(r - 1, 0), cb)),
                           pl.BlockSpec((rows, D_POOL), lambda r: (r, cb)),
                           pl.BlockSpec((rows, D_POOL), lambda r: (r, 0)),
                           pl.BlockSpec((rows, D_POOL), lambda r: (jnp.minimum(r + 1, n_chunk - 1), 0)),
                           pl.BlockSpec((D_POOL, BLOCK), lambda r: (0, 0)),
                           pl.BlockSpec((1, D_POOL), lambda r: (0, 0))],
                 out_specs=[pl.BlockSpec((rows, D_POOL), lambda r: (r, 0)),
                            pl.BlockSpec((D_POOL, BLOCK), lambda r: (0, 0)),
                            pl.BlockSpec((1, D_POOL), lambda r: (0, 0))])(proj, proj, d_out, d_out, w_pool_bf, scale)


SUBLANES = 8


def _sub_shifts(x, down):
    shift = _shift_down if down else _shift_up
    return [shift(x, b) for b in range(SUBLANES)]


def _lagged(shifts, d, lo, n, down):
    a, b = divmod(d, SUBLANES)
    start = lo - SUBLANES * a if down else lo + SUBLANES * a
    return shifts[b][start:start + n]


def _conv_taps(u_shifts, wdw_ref, lo, n):
    y = None
    for d in range(CONV_WIDTH):
        term = wdw_ref[CONV_WIDTH - 1 - d:CONV_WIDTH - d, :] * _lagged(u_shifts, d, lo, n, True)
        y = term if y is None else y + term
    return y


def _layernorm_stats(y):
    mu = jnp.mean(y, axis=-1, keepdims=True)
    yc = y - mu
    rstd = lax.rsqrt(jnp.mean(yc * yc, axis=-1, keepdims=True) + EPS)
    return yc * rstd, rstd


def _conv_fwd(proj, wdw, bdw, ln_g, ln_b, wpw_bf):
    lp = proj.shape[0]
    rows = BLOCK
    ca, cg = COL_A // D_CONV, COL_G // D_CONV

    def body(ap_ref, a_ref, gp_ref, g_ref, wdw_ref, b_ref, lg_ref, lb_ref, wpw_ref, o_ref):
        r = pl.program_id(0)
        a = jnp.concatenate([ap_ref[rows - HALO:rows], a_ref[...]], axis=0)
        g = jnp.concatenate([gp_ref[rows - HALO:rows], g_ref[...]], axis=0)
        u = a * _sigmoid(g)
        y = _conv_taps(_sub_shifts(u, True), wdw_ref, HALO, rows) + b_ref[...]
        xhat, _ = _layernorm_stats(y)
        yn = xhat * lg_ref[...] + lb_ref[...]
        pos = r * rows + lax.broadcasted_iota(jnp.int32, (rows, 1), 0) - PAD
        s = jnp.where(pos >= 0, yn * _sigmoid(yn), 0.0)
        o_ref[...] = jnp.dot(s.astype(BF16), wpw_ref[...], preferred_element_type=F32)

    prev = lambda c: (lambda r: (jnp.maximum(r - 1, 0), c))
    cur = lambda c: (lambda r: (r, c))
    const = lambda shape: pl.BlockSpec(shape, lambda r: (0, 0))
    return _call(body, name="conv_fwd", out_shape=_sds((lp, D_CONV), F32), grid=(lp // rows,),
                 in_specs=[pl.BlockSpec((rows, D_CONV), prev(ca)), pl.BlockSpec((rows, D_CONV), cur(ca)),
                           pl.BlockSpec((rows, D_CONV), prev(cg)), pl.BlockSpec((rows, D_CONV), cur(cg)),
                           const((HALO, D_CONV)), const((1, D_CONV)), const((1, D_CONV)), const((1, D_CONV)),
                           const((D_CONV, D_CONV))],
                 out_specs=pl.BlockSpec((rows, D_CONV), cur(0)))(proj, proj, proj, proj, wdw, bdw, ln_g, ln_b, wpw_bf)


def _conv_bwd(proj, d_out, wdw, bdw, ln_g, ln_b, wpw_bf):
    lp = proj.shape[0]
    rows = BLOCK
    n_chunk = lp // rows
    ca, cg = COL_A // D_CONV, COL_G // D_CONV
    ext = rows + HALO

    def body(ap_ref, a_ref, an_ref, gp_ref, g_ref, gn_ref, do_ref, don_ref, wdw_ref, b_ref, lg_ref, lb_ref,
             wpw_ref, da_ref, dg_ref, dwdw_ref, db_ref, dlg_ref, dlb_ref, dwpw_ref):
        r = pl.program_id(0)

        @pl.when(r == 0)
        def _():
            for ref in (dwdw_ref, db_ref, dlg_ref, dlb_ref, dwpw_ref):
                ref[...] = jnp.zeros(ref.shape, F32)

        a3 = jnp.concatenate([ap_ref[rows - HALO:rows], a_ref[...], an_ref[0:HALO]], axis=0)
        g3 = jnp.concatenate([gp_ref[rows - HALO:rows], g_ref[...], gn_ref[0:HALO]], axis=0)
        sig3 = _sigmoid(g3)
        u3 = a3 * sig3
        u_shifts = _sub_shifts(u3, True)
        y = _conv_taps(u_shifts, wdw_ref, HALO, ext) + b_ref[...]
        xhat, rstd = _layernorm_stats(y)
        yn = xhat * lg_ref[...] + lb_ref[...]
        sgm = _sigmoid(yn)
        pos = r * rows + lax.broadcasted_iota(jnp.int32, (ext, 1), 0) - PAD
        valid = (pos >= 0) & (pos < lp - PAD)
        d_ext = jnp.concatenate([do_ref[...], don_ref[0:HALO]], axis=0)
        ds = lax.dot_general(d_ext.astype(BF16), wpw_ref[...], _NT, preferred_element_type=F32)
        dyn = jnp.where(valid, ds * (sgm * (1.0 + yn * (1.0 - sgm))), 0.0)
        dxh = dyn * lg_ref[...]
        dy = rstd * (dxh - jnp.mean(dxh, axis=-1, keepdims=True)
                     - xhat * jnp.mean(dxh * xhat, axis=-1, keepdims=True))
        s_cur = jnp.where(valid[0:rows], (yn * sgm)[0:rows], 0.0)
        dwpw_ref[...] += lax.dot_general(s_cur.astype(BF16), do_ref[...].astype(BF16), _TN,
                                         preferred_element_type=F32)
        dlg_ref[...] += _colsum(dyn[0:rows] * xhat[0:rows])
        dlb_ref[...] += _colsum(dyn[0:rows])
        dy_cur = dy[0:rows]
        db_ref[...] += _colsum(dy_cur)
        dy_shifts = _sub_shifts(dy, False)
        du = None
        for d in range(CONV_WIDTH):
            k = CONV_WIDTH - 1 - d
            dwdw_ref[k:k + 1, :] += _colsum(dy_cur * _lagged(u_shifts, d, HALO, rows, True))
            term = wdw_ref[k:k + 1, :] * _lagged(dy_shifts, d, 0, rows, False)
            du = term if du is None else du + term
        du = jnp.where(pos[0:rows] >= 0, du, 0.0)
        sig = sig3[HALO:HALO + rows]
        da_ref[...] = (du * sig).astype(da_ref.dtype)
        dg_ref[...] = (du * a_ref[...] * sig * (1.0 - sig)).astype(dg_ref.dtype)

    prev = lambda c: (lambda r: (jnp.maximum(r - 1, 0), c))
    cur = lambda c: (lambda r: (r, c))
    nxt = lambda c: (lambda r: (jnp.minimum(r + 1, n_chunk - 1), c))
    const = lambda shape: pl.BlockSpec(shape, lambda r: (0, 0))
    blk = lambda f: pl.BlockSpec((rows, D_CONV), f)
    return _call(body, name="conv_bwd",
                 out_shape=[_sds((lp, D_CONV), BF16), _sds((lp, D_CONV), BF16), _sds((HALO, D_CONV), F32),
                            _sds((1, D_CONV), F32), _sds((1, D_CONV), F32), _sds((1, D_CONV), F32),
                            _sds((D_CONV, D_CONV), F32)],
                 grid=(n_chunk,),
                 in_specs=[blk(prev(ca)), blk(cur(ca)), blk(nxt(ca)), blk(prev(cg)), blk(cur(cg)), blk(nxt(cg)),
                           blk(cur(0)), blk(nxt(0)),
                           const((HALO, D_CONV)), const((1, D_CONV)), const((1, D_CONV)), const((1, D_CONV)),
                           const((D_CONV, D_CONV))],
                 out_specs=[blk(cur(0)), blk(cur(0)), const((HALO, D_CONV)), const((1, D_CONV)),
                            const((1, D_CONV)), const((1, D_CONV)), const((D_CONV, D_CONV))],
                 vmem_mb=48)(proj, proj, proj, proj, proj, proj, d_out, d_out, wdw, bdw, ln_g, ln_b, wpw_bf)


_ANY = pl.BlockSpec(memory_space=pl.ANY)


def _place():
    return lax.axis_index("x"), lax.axis_index("y"), lax.axis_index("c")


def _xy_peers(x, y):
    return [(1 - x, y), (x, 1 - y), (1 - x, 1 - y)]


def _exchange_now(name, ex):
    n = ex.n

    def body(*refs):
        ins, outs, sems = refs[:n], refs[n:2 * n], refs[2 * n:]
        ex.run(False, ins, outs, *sems[:3])
        ex.run(True, ins, outs, *sems[:3])
        if ex.split:
            ex.forward(False, outs, *sems[3:])
            ex.forward(True, outs, *sems[3:])

    return _call(body, name=name, out_shape=ex.out_shape(), in_specs=[_ANY] * n, out_specs=[_ANY] * n,
                 scratch=ex.scratch())(*ex.arrs)


def _tail_exchange(name, parts, arrs, small):
    ex = _Exchange("scatter", parts)
    n, m = ex.n, len(arrs)
    flips = [(fx, fy, fc) for fx in (0, 1) for fy in (0, 1) for fc in (0, 1)][1:]

    def body(*refs):
        p_in, a_in, g_in = refs[:n], refs[n:n + m], refs[n + m]
        p_out, a_out, g_out = refs[n + m + 1:2 * n + m + 1], refs[2 * n + m + 1:2 * (n + m) + 1], refs[2 * (n + m) + 1]
        send, recv, local, s_send, s_recv, g_send, g_recv, g_local = refs[2 * (n + m) + 2:]
        x, y, c = _place()
        me = 4 * x + 2 * y + c
        ex.run(False, p_in, p_out, send, recv, local)
        others = [pltpu.make_async_remote_copy(src_ref=a_in[a], dst_ref=a_out[a], send_sem=s_send.at[a],
                                               recv_sem=s_recv.at[a], device_id=(x, y, 1 - c), device_id_type=MESH)
                  for a in range(m)]
        for k, (fx, fy, fc) in enumerate(flips):
            peer = (1 - x if fx else x, 1 - y if fy else y, 1 - c if fc else c)
            others.append(pltpu.make_async_remote_copy(src_ref=g_in, dst_ref=g_out.at[me], send_sem=g_send.at[k],
                                                       recv_sem=g_recv.at[k], device_id=peer, device_id_type=MESH))
        others.append(pltpu.make_async_copy(g_in, g_out.at[me], g_local))
        for cp in others:
            cp.start()
        for cp in others:
            cp.wait()
        ex.run(True, p_in, p_out, send, recv, local)

    dma = pltpu.SemaphoreType.DMA
    res = _call(body, name=name,
                out_shape=ex.out_shape() + [_sds(a.shape, a.dtype) for a in arrs] +
                [_sds((N_DEV,) + small.shape, small.dtype)],
                in_specs=[_ANY] * (n + m + 1), out_specs=[_ANY] * (n + m + 1),
                scratch=ex.scratch() + [dma((m,)), dma((m,)), dma((N_DEV - 1,)), dma((N_DEV - 1,)), dma])(
                    *parts, *arrs, small)
    return list(res[:n]), list(res[n:n + m]), res[n + m]


def _swap_core(name, arrs):
    n = len(arrs)

    def body(*refs):
        ins, outs = refs[:n], refs[n:2 * n]
        send, recv = refs[2 * n:]
        x, y, c = _place()
        remote = []
        for a in range(n):
            cp = pltpu.make_async_remote_copy(src_ref=ins[a], dst_ref=outs[a], send_sem=send.at[a],
                                              recv_sem=recv.at[a], device_id=(x, y, 1 - c), device_id_type=MESH)
            cp.start()
            remote.append(cp)
        for cp in remote:
            cp.wait()

    return _call(body, name=name, out_shape=[_sds(a.shape, a.dtype) for a in arrs],
                 in_specs=[_ANY] * n, out_specs=[_ANY] * n,
                 scratch=[pltpu.SemaphoreType.DMA((n,)), pltpu.SemaphoreType.DMA((n,))])(*arrs)


def _sum_slots(name, stacked, out_dtype=F32):
    s, r, c = stacked.shape
    tr = _pick(r, (256, 128, 64, 8))

    def body(in_ref, o_ref):
        acc = in_ref[0].astype(F32)
        for k in range(1, s):
            acc = acc + in_ref[k].astype(F32)
        o_ref[...] = acc.astype(o_ref.dtype)

    return _call(body, name=name, out_shape=_sds((r, c), out_dtype), grid=(r // tr,),
                 in_specs=[pl.BlockSpec((s, tr, c), lambda i: (0, i, 0))],
                 out_specs=pl.BlockSpec((tr, c), lambda i: (i, 0)))(stacked)


def _sum_slots_layers(name, r0, r1):
    s, r, c = r0.shape
    tr = _pick(r, (256, 128))

    def body(a_ref, b_ref, o_ref):
        def total(ref):
            acc = ref[0].astype(F32)
            for k in range(1, s):
                acc = acc + ref[k].astype(F32)
            return acc

        @pl.when(pl.program_id(0) == 0)
        def _():
            o_ref[...] = total(a_ref)

        @pl.when(pl.program_id(0) == 1)
        def _():
            o_ref[...] = total(b_ref)

    return _call(body, name=name, out_shape=_sds((2, r, c), F32), grid=(2, r // tr),
                 in_specs=[pl.BlockSpec((s, tr, c), lambda l, i: (0, i * (1 - l), 0)),
                           pl.BlockSpec((s, tr, c), lambda l, i: (0, i * l, 0))],
                 out_specs=pl.BlockSpec((None, tr, c), lambda l, i: (l, i, 0)))(r0, r1)


def _adamw_math(w, g, m, v):
    m = ADAM_B1 * m + (1.0 - ADAM_B1) * g
    v = ADAM_B2 * v + (1.0 - ADAM_B2) * (g * g)
    m_hat = m / (1.0 - ADAM_B1 ** ADAM_STEP)
    v_hat = v / (1.0 - ADAM_B2 ** ADAM_STEP)
    delta = -ADAM_LR * (m_hat / (jnp.sqrt(v_hat) + ADAM_EPS) + ADAM_WD * w)
    return delta, m, v


def _adamw(name, w, m, v, g_mine, g_other):
    l, r, c = w.shape
    tr = _pick(r, (128, 64, 8))

    def body(w_ref, m_ref, v_ref, ga_ref, gb_ref, g_ref, d_ref, nm_ref, nv_ref):
        g = ga_ref[...] + gb_ref[...]
        delta, nm, nv = _adamw_math(w_ref[...], g, m_ref[...], v_ref[...])
        g_ref[...] = g
        d_ref[...] = delta
        nm_ref[...] = nm
        nv_ref[...] = nv

    spec = pl.BlockSpec((None, tr, c), lambda li, i: (li, i, 0))
    return _call(body, name=name, out_shape=[_sds(w.shape, F32)] * 4, grid=(l, r // tr),
                 in_specs=[spec] * 5, out_specs=[spec] * 4, vmem_mb=48)(w, m, v, g_mine, g_other)


def _adamw_flat(name, w, m, v, g):
    r, c = w.shape
    tr = _pick(r, (256, 128, 64, 8))

    def body(w_ref, m_ref, v_ref, g_ref, d_ref, nm_ref, nv_ref):
        delta, nm, nv = _adamw_math(w_ref[...], g_ref[...], m_ref[...], v_ref[...])
        d_ref[...] = delta
        nm_ref[...] = nm
        nv_ref[...] = nv

    spec = pl.BlockSpec((tr, c), lambda i: (i, 0))
    return _call(body, name=name, out_shape=[_sds(w.shape, F32)] * 3, grid=(r // tr,),
                 in_specs=[spec] * 4, out_specs=[spec] * 3)(w, m, v, g)


def _pack(arrs, row_multiple=256):
    flat = jnp.concatenate([a.reshape(-1).astype(F32) for a in arrs])
    per = BLOCK * row_multiple
    total = -(-flat.shape[0] // per) * per
    return jnp.pad(flat, (0, total - flat.shape[0])).reshape(total // BLOCK, BLOCK)


def _unpack(buf, shapes):
    flat = buf.reshape(-1)
    outs, off = [], 0
    for s in shapes:
        size = 1
        for d in s:
            size *= d
        outs.append(flat[off:off + size].reshape(s))
        off += size
    return outs


def kernel(x, meta_tokens, pre_mix_g, w_in, w_pool, pool_scale, w_dw, b_dw, conv_ln_g, conv_ln_b, w_pw, mix_out_g, w_out, post_mix_g, pre_ffn_g, w_gate, w_up, w_down, post_ffn_g, loss_target, m_meta_tokens, m_pre_mix_g, m_w_in, m_w_pool, m_pool_scale, m_w_dw, m_b_dw, m_conv_ln_g, m_conv_ln_b, m_w_pw, m_mix_out_g, m_w_out, m_post_mix_g, m_pre_ffn_g, m_w_gate, m_w_up, m_w_down, m_post_ffn_g, v_meta_tokens, v_pre_mix_g, v_w_in, v_w_pool, v_pool_scale, v_w_dw, v_b_dw, v_conv_ln_g, v_conv_ln_b, v_w_pw, v_mix_out_g, v_w_out, v_post_mix_g, v_pre_ffn_g, v_w_gate, v_w_up, v_w_down, v_post_ffn_g):
    seq = x.shape[1]
    lp = PAD + N_META + seq
    depth = w_in.shape[0]
    xy = 2 * lax.axis_index("x") + lax.axis_index("y")

    small_shapes = [meta_tokens.shape, w_dw.shape, w_pw.shape]
    small_local = _pack([meta_tokens, w_dw, w_pw], row_multiple=8)
    big_names = ["w_in", "w_out", "w_gate", "w_up", "w_down"]
    big_local = {(k, l): w[l:l + 1].astype(BF16)
                 for k, w in zip(big_names, (w_in, w_out, w_gate, w_up, w_down)) for l in range(depth)}
    wg = {}
    half = w_in.shape[2] // 2
    first = big_local[("w_in", 0)]
    lo_half, hi_half, small_all = _exchange_now(
        "gather_first", _Exchange("gather", [first[:, :, :half], first[:, :, half:], small_local], split=True))
    wg[("w_in", 0)] = jnp.concatenate([lo_half, hi_half], axis=3)
    metas, wdws, wpws = [], [], []
    for s in range(N_SHARD):
        mt, wd, wp = _unpack(small_all[s], small_shapes)
        metas.append(mt)
        wdws.append(wd)
        wpws.append(wp)
    meta_full = jnp.concatenate(metas, axis=1)
    wdw_full = jnp.concatenate(wdws, axis=2)
    wpw_full = jnp.concatenate(wpws, axis=1)
    wdw_pad = jnp.pad(wdw_full, ((0, 0), (0, HALO - CONV_WIDTH), (0, 0)))
    wpw_bf = wpw_full.astype(BF16)
    wpool_bf = w_pool.reshape(depth, D_POOL, BLOCK).astype(BF16)

    row = lambda a, i: a[i][None, :]

    h = jnp.concatenate([jnp.zeros((PAD, D_MODEL), F32), meta_full, x[0]], axis=0)
    u = _rowwise("pre_mix_norm0", lambda hh, g: _rms(hh, g), [h], [row(pre_mix_g, 0)], [(D_MODEL, BF16)])[0]
    saved = []
    for i in range(depth):
        proj = _mm_nn_col("in_proj%d" % i, u, wg[("w_in", i)], 0, F32)
        ride = [(k, i) for k in big_names[1:]] + ([("w_in", i + 1)] if i + 1 < depth else [])
        o_attn, tot, got = _attn_fwd(proj, _Exchange("gather", [big_local[k] for k in ride], split=True))
        wg.update(zip(ride, got))
        o_pool = _pool_fwd(proj, wpool_bf[i], row(pool_scale, i))
        o_conv = _conv_fwd(proj, wdw_pad[i], row(b_dw, i), row(conv_ln_g, i), row(conv_ln_b, i), wpw_bf[i])

        mix_gains = [row(mix_out_g, i)[:, :D_ATTN], row(mix_out_g, i)[:, D_ATTN:D_ATTN + D_POOL],
                     row(mix_out_g, i)[:, D_ATTN + D_POOL:]]
        merged = _rowwise("merge%d" % i, _merge, [o_attn, o_pool, o_conv], mix_gains, [(D_MODEL, BF16)])[0]
        mix = _mm_nn_row("out_proj%d" % i, merged, wg[("w_out", i)], 0)

        def post_mix(hh, mx, g1, g2):
            h1 = hh + _rms(mx, g1)
            return h1, _rms(h1, g2)

        h1, u2 = _rowwise("post_mix%d" % i, post_mix, [h, mix], [row(post_mix_g, i), row(pre_ffn_g, i)],
                          [(D_MODEL, F32), (D_MODEL, BF16)])
        gate = _mm_nn_col("ffn_gate%d" % i, u2, wg[("w_gate", i)], 0, F32)
        up, act = _mm_nn_col_swiglu("ffn_up%d" % i, u2, wg[("w_up", i)], 0, gate)
        ff = _mm_nn_row("ffn_down%d" % i, act, wg[("w_down", i)], 0)
        rec = dict(h=h, u=u, proj=proj, tot=tot, o_attn=o_attn, o_pool=o_pool, o_conv=o_conv, merged=merged,
                   mix=mix, h1=h1, u2=u2, gate=gate, up=up, act=act, ff=ff)
        saved.append(rec)
        if i + 1 < depth:
            def post_ffn(hh, f, g1, g2):
                h2 = hh + _rms(f, g1)
                return h2, _rms(h2, g2)

            h, u = _rowwise("post_ffn%d" % i, post_ffn, [h1, ff], [row(post_ffn_g, i), row(pre_mix_g, i + 1)],
                            [(D_MODEL, F32), (D_MODEL, BF16)])
        else:
            def head(row0, hh, f, tgt, g1):
                y = hh + _rms(f, g1)
                rid = row0 + lax.broadcasted_iota(jnp.int32, (y.shape[0], 1), 0)
                err = jnp.where(rid >= PAD + N_META, y - tgt, 0.0)
                part = 0.5 * jnp.sum(jnp.mean(err * err, axis=-1, keepdims=True), axis=0, keepdims=True)
                return err * (1.0 / D_MODEL), jnp.broadcast_to(part, (8, BLOCK))

            dh, loss_part = _rowwise("loss_head", head, [h1, ff, loss_target[0]], [row(post_ffn_g, i)],
                                     [(D_MODEL, F32)], accs=[(8, BLOCK)], with_row0=True, headless={2})

    loss = lax.psum(loss_part[0, 0], ("x", "y", "c"))

    small_grads = {}
    big_parts = {}
    received = {}
    for i in reversed(range(depth)):
        rec = saved[i]

        def post_ffn_b(f, d, g):
            _, vjp = jax.vjp(_rms, f, g)
            df, dg = vjp(d)
            return df, dg

        dff, g_post_ffn = _rowwise("post_ffn_b%d" % i, post_ffn_b, [rec["ff"], dh], [row(post_ffn_g, i)],
                                   [(D_MODEL, BF16)], accs=[(1, D_MODEL)])
        big_parts[("w_down", i)] = _mm_tn_row("dw_down%d" % i, rec["act"], dff)
        dgate, dup = _mm_nt_row_swiglu("d_gate_up%d" % i, dff, wg[("w_down", i)], 0, rec["gate"], rec["up"])
        big_parts[("w_gate", i)] = _mm_tn_col("dw_gate%d" % i, rec["u2"], dgate)
        big_parts[("w_up", i)] = _mm_tn_col("dw_up%d" % i, rec["u2"], dup)
        du2a = _mm_nt_col("d_u2_gate%d" % i, dgate, wg[("w_gate", i)], 0)
        du2b = _mm_nt_col("d_u2_up%d" % i, dup, wg[("w_up", i)], 0)

        def post_mix_b(h1v, mx, d, da, db, g1, g2):
            _, vjp2 = jax.vjp(_rms, h1v, g2)
            dh1, dg2 = vjp2(da + db)
            dmid = d + dh1
            _, vjp1 = jax.vjp(_rms, mx, g1)
            dmx, dg1 = vjp1(dmid)
            return dmid, dmx, dg1, dg2

        dmid, dmix, g_post_mix, g_pre_ffn = _rowwise(
            "post_mix_b%d" % i, post_mix_b, [rec["h1"], rec["mix"], dh, du2a, du2b],
            [row(post_mix_g, i), row(pre_ffn_g, i)], [(D_MODEL, F32), (D_MODEL, BF16)],
            accs=[(1, D_MODEL), (1, D_MODEL)], vmem_mb=48)
        big_parts[("w_out", i)] = _mm_tn_row("dw_out%d" % i, rec["merged"], dmix)
        dmerged = _mm_nt_row("d_merged%d" % i, dmix, wg[("w_out", i)], 0, F32)

        def merge_b(oa, op, oc, d, ga, gp, gc):
            _, vjp = jax.vjp(_merge, oa, op, oc, ga, gp, gc)
            return vjp(d)

        mix_gains = [row(mix_out_g, i)[:, :D_ATTN], row(mix_out_g, i)[:, D_ATTN:D_ATTN + D_POOL],
                     row(mix_out_g, i)[:, D_ATTN + D_POOL:]]
        do_attn, do_pool, do_conv, g_mo_a, g_mo_p, g_mo_c = _rowwise(
            "merge_b%d" % i, merge_b, [rec["o_attn"], rec["o_pool"], rec["o_conv"], dmerged], mix_gains,
            [(D_ATTN, F32), (D_POOL, F32), (D_CONV, F32)], accs=[(1, D_ATTN), (1, D_POOL), (1, D_CONV)])
        g_mix_out = jnp.concatenate([g_mo_a, g_mo_p, g_mo_c], axis=1)
        du_pool, g_w_pool, g_pool_scale = _pool_bwd(rec["proj"], do_pool, wpool_bf[i], row(pool_scale, i))
        da, dgt, g_w_dw, g_b_dw, g_ln_g, g_ln_b, g_w_pw = _conv_bwd(
            rec["proj"], do_conv, wdw_pad[i], row(b_dw, i), row(conv_ln_g, i), row(conv_ln_b, i), wpw_bf[i])
        big_parts[("w_pw", i)] = g_w_pw.reshape(N_SHARD, D_CONV // N_SHARD, D_CONV).astype(BF16)
        ride = [(k, i) for k in big_names[1:] + ["w_pw"]] + ([("w_in", i + 1)] if i + 1 < depth else [])
        dq, dk, dv, got = _attn_bwd(rec["proj"], rec["tot"], do_attn,
                                    _Exchange("scatter", [big_parts[k] for k in ride]))
        received.update(zip(ride, got))
        dproj = jnp.concatenate([dq, dk, dv, du_pool, da, dgt], axis=1)
        big_parts[("w_in", i)] = _mm_tn_col("dw_in%d" % i, rec["u"], dproj)
        du = _mm_nt_col("d_u%d" % i, dproj, wg[("w_in", i)], 0)

        def pre_mix_b(row0, hv, d, dd, g):
            _, vjp = jax.vjp(_rms, hv, g)
            dhh, dg = vjp(dd)
            out = d + dhh
            return out, dg, jnp.where(row0 == 0, out, 0.0)

        dh, g_pre_mix, dh_head = _rowwise("pre_mix_b%d" % i, pre_mix_b, [rec["h"], dmid, du], [row(pre_mix_g, i)],
                                          [(D_MODEL, F32)], accs=[(1, D_MODEL), (BLOCK, D_MODEL)], with_row0=True,
                                          headless={3} if i == 0 else ())
        small_grads[i] = dict(pre_mix_g=g_pre_mix[0], w_pool=g_w_pool, pool_scale=g_pool_scale[0],
                              w_dw=g_w_dw[:CONV_WIDTH], b_dw=g_b_dw[0], conv_ln_g=g_ln_g[0], conv_ln_b=g_ln_b[0],
                              mix_out_g=g_mix_out[0], post_mix_g=g_post_mix[0],
                              pre_ffn_g=g_pre_ffn[0], post_ffn_g=g_post_ffn[0])

    grad_x = dh[None]
    g_meta_part = dh_head[PAD:PAD + N_META]

    rep_names = ["pre_mix_g", "pool_scale", "b_dw", "conv_ln_g", "conv_ln_b", "mix_out_g", "post_mix_g",
                 "pre_ffn_g", "post_ffn_g", "w_pool"]
    stack2 = lambda nme: jnp.stack([small_grads[l][nme] for l in range(depth)])
    small_list = [stack2(nme) for nme in rep_names] + [g_meta_part, stack2("w_dw")]
    small_list[rep_names.index("w_pool")] = small_list[rep_names.index("w_pool")].reshape(w_pool.shape)
    full_shapes = [a.shape for a in small_list]
    packed = _pack(small_list)

    assert depth == 2
    early = big_names[1:] + ["w_pw"]
    plane_sums = {k: _sum_slots_layers("sum_%s" % k, received[(k, 0)], received[(k, 1)]) for k in early}
    last, swapped, small_slots = _tail_exchange("tail_exchange", [big_parts[("w_in", 0)]],
                                                [plane_sums[k] for k in early], packed)
    summed = _sum_slots("sum_small", small_slots)
    full = _unpack(summed, full_shapes)
    rep_grads = dict(zip(rep_names, full[:len(rep_names)]))
    g_meta = lax.dynamic_slice_in_dim(full[-2], xy * meta_tokens.shape[1], meta_tokens.shape[1], axis=1)
    g_w_dw = lax.dynamic_slice_in_dim(full[-1], xy * w_dw.shape[2], w_dw.shape[2], axis=2)

    rep_w = dict(pre_mix_g=pre_mix_g, pool_scale=pool_scale, b_dw=b_dw, conv_ln_g=conv_ln_g, conv_ln_b=conv_ln_b,
                 mix_out_g=mix_out_g, post_mix_g=post_mix_g, pre_ffn_g=pre_ffn_g, post_ffn_g=post_ffn_g,
                 w_pool=w_pool)
    rep_m = dict(pre_mix_g=m_pre_mix_g, pool_scale=m_pool_scale, b_dw=m_b_dw, conv_ln_g=m_conv_ln_g,
                 conv_ln_b=m_conv_ln_b, mix_out_g=m_mix_out_g, post_mix_g=m_post_mix_g, pre_ffn_g=m_pre_ffn_g,
                 post_ffn_g=m_post_ffn_g, w_pool=m_w_pool)
    rep_v = dict(pre_mix_g=v_pre_mix_g, pool_scale=v_pool_scale, b_dw=v_b_dw, conv_ln_g=v_conv_ln_g,
                 conv_ln_b=v_conv_ln_b, mix_out_g=v_mix_out_g, post_mix_g=v_post_mix_g, pre_ffn_g=v_pre_ffn_g,
                 post_ffn_g=v_post_ffn_g, w_pool=v_w_pool)
    sm_names = rep_names + ["meta_tokens", "w_dw"]
    sm_w = [rep_w[k] for k in rep_names] + [meta_tokens, w_dw]
    sm_m = [rep_m[k] for k in rep_names] + [m_meta_tokens, m_w_dw]
    sm_v = [rep_v[k] for k in rep_names] + [v_meta_tokens, v_w_dw]
    sm_g = [rep_grads[k] for k in rep_names] + [g_meta, g_w_dw]
    sm_shapes = [a.shape for a in sm_w]
    sm_delta, sm_nm, sm_nv = _adamw_flat("adamw_small", _pack(sm_w), _pack(sm_m), _pack(sm_v), _pack(sm_g))
    small_out = {}
    for k, g, d, nm, nv in zip(sm_names, sm_g, _unpack(sm_delta, sm_shapes), _unpack(sm_nm, sm_shapes),
                               _unpack(sm_nv, sm_shapes)):
        small_out[k] = (g, d, nm, nv)

    other_sums = dict(zip(early, swapped))
    received[("w_in", 0)] = last[0]
    plane_sums["w_in"] = _sum_slots_layers("sum_w_in", received[("w_in", 0)], received[("w_in", 1)])
    other_sums["w_in"] = _swap_core("swap_core_w_in", [plane_sums["w_in"]])[0]
    big_w = dict(w_in=(w_in, m_w_in, v_w_in), w_out=(w_out, m_w_out, v_w_out), w_gate=(w_gate, m_w_gate, v_w_gate),
                 w_up=(w_up, m_w_up, v_w_up), w_down=(w_down, m_w_down, v_w_down), w_pw=(w_pw, m_w_pw, v_w_pw))
    big_out = {}
    for k in early + ["w_in"]:
        w, m, v = big_w[k]
        big_out[k] = _adamw("adamw_%s" % k, w, m, v, plane_sums[k], other_sums[k])

    order = ["meta_tokens", "pre_mix_g", "w_in", "w_pool", "pool_scale", "w_dw", "b_dw", "conv_ln_g", "conv_ln_b",
             "w_pw", "mix_out_g", "w_out", "post_mix_g", "pre_ffn_g", "w_gate", "w_up", "w_down", "post_ffn_g"]
    res = lambda k: big_out[k] if k in big_out else small_out[k]
    outs = [loss, grad_x]
    for part in range(4):
        outs += [res(k)[part] for k in order]
    return tuple(outs)
```

```python
import functools

import jax
import jax.numpy as jnp
from jax import lax
from jax.experimental import pallas as pl
from jax.experimental.pallas import tpu as pltpu

F32 = jnp.float32
BF16 = jnp.bfloat16

D_MODEL = 2048
N_META = 16
D_ATTN = 1024
D_POOL = 512
D_CONV = 512
POOL_WINDOWS = (2, 4, 8, 16)
CONV_WIDTH = 31
D_IN_PROJ = 3 * D_ATTN + D_POOL + 2 * D_CONV
D_FF = 5632
EPS = 1e-6
BLOCK = 128
PAD = BLOCK - N_META
HALO = 32
N_SHARD = 4
N_DEV = 8
MESH = pl.DeviceIdType.MESH

ADAM_LR = 0.001
ADAM_B1 = 0.9
ADAM_B2 = 0.999
ADAM_EPS = 1e-08
ADAM_WD = 0.01
ADAM_STEP = 10

COL_Q, COL_K, COL_V = 0, D_ATTN, 2 * D_ATTN
COL_POOL = 3 * D_ATTN
COL_A = COL_POOL + D_POOL
COL_G = COL_A + D_CONV


def _call(body, *, name, out_shape, grid=None, in_specs=None, out_specs=None, scratch=(), vmem_mb=None,
          aliases=None):
    params = {}
    if grid is not None:
        params["dimension_semantics"] = ("arbitrary",) * len(grid)
    if vmem_mb is not None:
        params["vmem_limit_bytes"] = vmem_mb << 20
    kw = dict(out_shape=out_shape, name=name, compiler_params=pltpu.CompilerParams(**params))
    if grid is not None:
        kw["grid"] = grid
    if in_specs is not None:
        kw["in_specs"] = in_specs
    if out_specs is not None:
        kw["out_specs"] = out_specs
    if scratch:
        kw["scratch_shapes"] = list(scratch)
    if aliases:
        kw["input_output_aliases"] = dict(aliases)
    return pl.pallas_call(body, **kw)


def _sds(shape, dtype):
    return jax.ShapeDtypeStruct(tuple(shape), dtype)


def _pick(n, candidates):
    for c in candidates:
        if n % c == 0:
            return c
    return n


def _rowwise(name, fn, rows_in, consts, outs, accs=(), tm=BLOCK, with_row0=False, vmem_mb=None, headless=()):
    lp = rows_in[0].shape[0]
    n_in, n_c, n_o = len(rows_in), len(consts), len(outs)
    back = lambda j: (lambda i: (jnp.maximum(i - 1, 0), 0)) if j in headless else (lambda i: (i, 0))
    rows_of = lambda j: lp - tm if j in headless else lp

    def body(*refs):
        vals = [r[...] for r in refs[:n_in + n_c]]
        if with_row0:
            vals = [pl.program_id(0) * tm] + vals
        res = fn(*vals)
        if not isinstance(res, (tuple, list)):
            res = (res,)
        o_refs = refs[n_in + n_c:n_in + n_c + n_o]
        a_refs = refs[n_in + n_c + n_o:]
        for r, v in zip(o_refs, res[:n_o]):
            r[...] = v.astype(r.dtype)
        if a_refs:
            @pl.when(pl.program_id(0) == 0)
            def _():
                for r in a_refs:
                    r[...] = jnp.zeros(r.shape, r.dtype)
            for r, v in zip(a_refs, res[n_o:]):
                r[...] += v.astype(r.dtype)

    in_specs = [pl.BlockSpec((tm, a.shape[1]), back(j)) for j, a in enumerate(rows_in)]
    in_specs += [pl.BlockSpec(c.shape, lambda i: (0, 0)) for c in consts]
    out_specs = [pl.BlockSpec((tm, w), back(n_in + j)) for j, (w, _) in enumerate(outs)]
    out_specs += [pl.BlockSpec(s, lambda i: (0, 0)) for s in accs]
    out_shape = [_sds((rows_of(n_in + j), w), dt) for j, (w, dt) in enumerate(outs)] + [_sds(s, F32) for s in accs]
    res = _call(body, name=name, out_shape=out_shape, grid=(lp // tm,), in_specs=in_specs,
                out_specs=out_specs, vmem_mb=vmem_mb)(*rows_in, *consts)
    return res


def _rms(x, g):
    return x * lax.rsqrt(jnp.mean(x * x, axis=-1, keepdims=True) + EPS) * g


def _merge(oa, op, oc, ga, gp, gc):
    return jnp.concatenate([_rms(oa, ga), _rms(op, gp), _rms(oc, gc)], axis=1)


def _colsum(x):
    return jnp.sum(x, axis=0, keepdims=True)


def _sigmoid(x):
    return 1.0 / (1.0 + jnp.exp(-x))


MM_VMEM_MB = 56


def _mm_tiles(lp):
    return _pick(lp, (1408, 384, 256, 128))


def _mm_nn_col(name, a, wg, layer, out_dtype):
    lp, k = a.shape
    n = wg.shape[3]
    tm = _mm_tiles(lp)

    def body(a_ref, w_ref, o_ref):
        o_ref[...] = jnp.dot(a_ref[...], w_ref[...], preferred_element_type=F32).astype(o_ref.dtype)

    return _call(body, name=name, out_shape=_sds((lp, N_SHARD * n), out_dtype), grid=(N_SHARD, lp // tm),
                 in_specs=[pl.BlockSpec((tm, k), lambda s, i: (i, 0)),
                           pl.BlockSpec((None, None, k, n), lambda s, i: (s, layer, 0, 0))],
                 out_specs=pl.BlockSpec((tm, n), lambda s, i: (i, s)), vmem_mb=MM_VMEM_MB)(a, wg)


def _mm_nn_col_swiglu(name, a, wg, layer, gate):
    lp, k = a.shape
    n = wg.shape[3]
    tm = _pick(lp, (384, 128))

    def body(a_ref, w_ref, g_ref, u_ref, act_ref):
        up = jnp.dot(a_ref[...], w_ref[...], preferred_element_type=F32)
        g = g_ref[...]
        u_ref[...] = up
        act_ref[...] = (g * _sigmoid(g) * up).astype(act_ref.dtype)

    blk = pl.BlockSpec((tm, n), lambda s, i: (i, s))
    return _call(body, name=name, out_shape=[_sds((lp, N_SHARD * n), F32), _sds((lp, N_SHARD * n), BF16)],
                 grid=(N_SHARD, lp // tm),
                 in_specs=[pl.BlockSpec((tm, k), lambda s, i: (i, 0)),
                           pl.BlockSpec((None, None, k, n), lambda s, i: (s, layer, 0, 0)), blk],
                 out_specs=[blk, blk], vmem_mb=MM_VMEM_MB)(a, wg, gate)


def _mm_nn_row(name, a, wg, layer):
    lp = a.shape[0]
    k, n = wg.shape[2], wg.shape[3]
    tm = _mm_tiles(lp)

    def body(a_ref, w_ref, o_ref):
        part = jnp.dot(a_ref[...], w_ref[...], preferred_element_type=F32)

        @pl.when(pl.program_id(1) == 0)
        def _():
            o_ref[...] = part

        @pl.when(pl.program_id(1) != 0)
        def _():
            o_ref[...] += part

    return _call(body, name=name, out_shape=_sds((lp, n), F32), grid=(lp // tm, N_SHARD),
                 in_specs=[pl.BlockSpec((tm, k), lambda i, s: (i, s)),
                           pl.BlockSpec((None, None, k, n), lambda i, s: (s, layer, 0, 0))],
                 out_specs=pl.BlockSpec((tm, n), lambda i, s: (i, 0)), vmem_mb=MM_VMEM_MB)(a, wg)


_NT = (((1,), (1,)), ((), ()))
_TN = (((0,), (0,)), ((), ()))


def _mm_nt_col(name, dy, wg, layer):
    lp = dy.shape[0]
    k, n = wg.shape[2], wg.shape[3]
    tm = _mm_tiles(lp)

    def body(d_ref, w_ref, o_ref):
        part = lax.dot_general(d_ref[...], w_ref[...], _NT, preferred_element_type=F32)

        @pl.when(pl.program_id(1) == 0)
        def _():
            o_ref[...] = part

        @pl.when(pl.program_id(1) != 0)
        def _():
            o_ref[...] += part

    return _call(body, name=name, out_shape=_sds((lp, k), F32), grid=(lp // tm, N_SHARD),
                 in_specs=[pl.BlockSpec((tm, n), lambda i, s: (i, s)),
                           pl.BlockSpec((None, None, k, n), lambda i, s: (s, layer, 0, 0))],
                 out_specs=pl.BlockSpec((tm, k), lambda i, s: (i, 0)), vmem_mb=MM_VMEM_MB)(dy, wg)


def _mm_nt_row(name, dy, wg, layer, out_dtype):
    lp = dy.shape[0]
    k, n = wg.shape[2], wg.shape[3]
    tm = _mm_tiles(lp)

    def body(d_ref, w_ref, o_ref):
        o_ref[...] = lax.dot_general(d_ref[...], w_ref[...], _NT, preferred_element_type=F32).astype(o_ref.dtype)

    return _call(body, name=name, out_shape=_sds((lp, N_SHARD * k), out_dtype), grid=(N_SHARD, lp // tm),
                 in_specs=[pl.BlockSpec((tm, n), lambda s, i: (i, 0)),
                           pl.BlockSpec((None, None, k, n), lambda s, i: (s, layer, 0, 0))],
                 out_specs=pl.BlockSpec((tm, k), lambda s, i: (i, s)), vmem_mb=MM_VMEM_MB)(dy, wg)


def _mm_nt_row_swiglu(name, dy, wg, layer, gate, up):
    lp = dy.shape[0]
    k, n = wg.shape[2], wg.shape[3]
    tm = _pick(lp, (384, 128))

    def body(d_ref, w_ref, g_ref, u_ref, dg_ref, du_ref):
        dact = lax.dot_general(d_ref[...], w_ref[...], _NT, preferred_element_type=F32)
        g = g_ref[...]
        sg = _sigmoid(g)
        dg_ref[...] = (dact * u_ref[...] * (sg * (1.0 + g * (1.0 - sg)))).astype(dg_ref.dtype)
        du_ref[...] = (dact * (g * sg)).astype(du_ref.dtype)

    blk = pl.BlockSpec((tm, k), lambda s, i: (i, s))
    return _call(body, name=name, out_shape=[_sds((lp, N_SHARD * k), BF16)] * 2, grid=(N_SHARD, lp // tm),
                 in_specs=[pl.BlockSpec((tm, n), lambda s, i: (i, 0)),
                           pl.BlockSpec((None, None, k, n), lambda s, i: (s, layer, 0, 0)), blk, blk],
                 out_specs=[blk, blk], vmem_mb=MM_VMEM_MB)(dy, wg, gate, up)


def _mm_tn_col(name, a, dy):
    lp, k = a.shape
    n = dy.shape[1] // N_SHARD
    tm = _mm_tiles(lp)
    tk = _pick(k, (1024, 512))

    def body(a_ref, d_ref, o_ref, acc):
        @pl.when(pl.program_id(2) == 0)
        def _():
            acc[...] = jnp.zeros(acc.shape, F32)

        acc[...] += lax.dot_general(a_ref[...], d_ref[...], _TN, preferred_element_type=F32)

        @pl.when(pl.program_id(2) == pl.num_programs(2) - 1)
        def _():
            o_ref[...] = acc[...].astype(o_ref.dtype)

    return _call(body, name=name, out_shape=_sds((N_SHARD, k, n), BF16), grid=(N_SHARD, k // tk, lp // tm),
                 in_specs=[pl.BlockSpec((tm, tk), lambda s, kk, i: (i, kk)),
                           pl.BlockSpec((tm, n), lambda s, kk, i: (i, s))],
                 out_specs=pl.BlockSpec((None, tk, n), lambda s, kk, i: (s, kk, 0)),
                 scratch=[pltpu.VMEM((tk, n), F32)], vmem_mb=MM_VMEM_MB)(a, dy)


def _mm_tn_row(name, a, dy):
    lp = a.shape[0]
    k = a.shape[1] // N_SHARD
    n = dy.shape[1]
    tm = _mm_tiles(lp)
    tn = _pick(n, (1024, 512))

    def body(a_ref, d_ref, o_ref, acc):
        @pl.when(pl.program_id(2) == 0)
        def _():
            acc[...] = jnp.zeros(acc.shape, F32)

        acc[...] += lax.dot_general(a_ref[...], d_ref[...], _TN, preferred_element_type=F32)

        @pl.when(pl.program_id(2) == pl.num_programs(2) - 1)
        def _():
            o_ref[...] = acc[...].astype(o_ref.dtype)

    return _call(body, name=name, out_shape=_sds((N_SHARD, k, n), BF16), grid=(N_SHARD, n // tn, lp // tm),
                 in_specs=[pl.BlockSpec((tm, k), lambda s, j, i: (i, s)),
                           pl.BlockSpec((tm, tn), lambda s, j, i: (i, j))],
                 out_specs=pl.BlockSpec((None, k, tn), lambda s, j, i: (s, 0, j)),
                 scratch=[pltpu.VMEM((k, tn), F32)], vmem_mb=MM_VMEM_MB)(a, dy)


SUB = 16
WIDE = 2 * BLOCK
Z_CLAMP = 20.0


def _log1m_sigmoid(z):
    return -jnp.where(z > Z_CLAMP, z, jnp.log(1.0 + jnp.exp(jnp.minimum(z, Z_CLAMP))))


def _tri(tk, kind):
    r = lax.broadcasted_iota(jnp.int32, (tk, tk), 0)
    c = lax.broadcasted_iota(jnp.int32, (tk, tk), 1)
    t = {"gt": r > c, "le": r <= c}[kind]
    return jnp.where(t, 1.0, 0.0).astype(BF16)


def _strip_mask(kind, s, tk):
    if kind == "none":
        return None
    col = lax.broadcasted_iota(jnp.int32, (SUB, tk), 1)
    row = lax.broadcasted_iota(jnp.int32, (SUB, tk), 0)
    causal = (col - row) < s * SUB
    if kind == "diag":
        return causal
    if kind == "pad":
        return col >= PAD
    return causal & (col >= PAD)


def _attn_blocks(lp):
    nb = lp // BLOCK
    assert nb % 2 == 1, "sequence must be a 128-row block plus whole 256-row blocks"
    return nb, (nb + 1) // 2


class _Exchange:
    def __init__(self, kind, arrs, split=False):
        self.kind, self.arrs, self.n, self.split = kind, list(arrs), len(arrs), split

    def out_shape(self):
        if self.kind == "gather":
            return [_sds((N_SHARD,) + a.shape, a.dtype) for a in self.arrs]
        return [_sds(a.shape, a.dtype) for a in self.arrs]

    def scratch(self):
        sems = [pltpu.SemaphoreType.DMA((3 * self.n,)), pltpu.SemaphoreType.DMA((3 * self.n,)),
                pltpu.SemaphoreType.DMA((self.n,))]
        if self.split:
            sems += [pltpu.SemaphoreType.DMA((self.n,)), pltpu.SemaphoreType.DMA((self.n,))]
        return sems

    def forward(self, wait, outs, fsend, frecv):
        x, y, c = _place()
        for a in range(self.n):
            cp = pltpu.make_async_remote_copy(src_ref=outs[a], dst_ref=outs[a], send_sem=fsend.at[a],
                                              recv_sem=frecv.at[a], device_id=(x, y, 1 - c), device_id_type=MESH)
            if wait:
                pl.when(c == a % 2)(cp.wait_send)
                pl.when(c != a % 2)(cp.wait_recv)
            else:
                pl.when(c == a % 2)(cp.start)

    def copies(self, a, ins, outs, send, recv, local):
        x, y, c = _place()
        me = 2 * x + y
        own = ins[a] if self.kind == "gather" else ins[a].at[me]
        out = [pltpu.make_async_copy(own, outs[a].at[me], local.at[a])]
        for r, (px, py) in enumerate(_xy_peers(x, y)):
            src = ins[a] if self.kind == "gather" else ins[a].at[2 * px + py]
            out.append(pltpu.make_async_remote_copy(
                src_ref=src, dst_ref=outs[a].at[me], send_sem=send.at[3 * a + r], recv_sem=recv.at[3 * a + r],
                device_id=(px, py, c), device_id_type=MESH))
        return out

    def run(self, wait, ins, outs, send, recv, local):
        for a in range(self.n):
            def go(a=a):
                for cp in self.copies(a, ins, outs, send, recv, local):
                    if wait:
                        cp.wait()
                    else:
                        cp.start()
            if self.split:
                pl.when(lax.axis_index("c") == a % 2)(go)
            else:
                go()


def _attn_fwd(proj, exchange=None):
    lp = proj.shape[0]
    nb, nq = _attn_blocks(lp)
    n_pair = D_ATTN // BLOCK
    n_x = exchange.n if exchange else 0

    def body(*refs):
        q_ref, k_ref, v_ref = refs[:3]
        x_in = refs[3:3 + n_x]
        o_ref, tot_ref = refs[3 + n_x:5 + n_x]
        x_out = refs[5 + n_x:5 + 2 * n_x]
        qs, kb, vh, tri_l, tri_s, z_s, hl_s, c_s, w_s, r_s, acc_s = refs[5 + 2 * n_x:16 + 2 * n_x]
        x_sem = refs[16 + 2 * n_x:]
        p, i = pl.program_id(0), pl.program_id(1)
        m0 = lax.broadcasted_iota(jnp.int32, (1, BLOCK), 1) < (BLOCK // 2)

        if exchange:
            @pl.when((p == 0) & (i == 0))
            def _():
                exchange.run(False, x_in, x_out, *x_sem[:3])

            if exchange.split:
                @pl.when((p == n_pair - 2) & (i == 0))
                def _():
                    exchange.run(True, x_in, x_out, *x_sem[:3])
                    exchange.forward(False, x_out, *x_sem[3:])

        @pl.when(i == 0)
        def _():
            tri_l[...] = _tri(WIDE, "gt")
            tri_s[...] = _tri(BLOCK, "gt")

            def prep(b, carry):
                rows = pl.ds(pl.multiple_of(b * BLOCK, BLOCK), BLOCK)
                q = q_ref[rows, :] * 0.125
                v = v_ref[rows, :]
                qs[0, rows, :] = jnp.where(m0, q, 0.0).astype(BF16)
                qs[1, rows, :] = jnp.where(m0, 0.0, q).astype(BF16)
                kb[rows, :] = k_ref[rows, :].astype(BF16)
                vh[0, rows, :] = jnp.where(m0, v, 0.0).astype(BF16)
                vh[1, rows, :] = jnp.where(m0, 0.0, v).astype(BF16)
                return carry

            lax.fori_loop(0, nb, prep, 0)

        def tiles(q0, tq, specs):
            heads = [(t, h) for t in range(len(specs)) for h in range(2)]
            strips = [slice(s * SUB, (s + 1) * SUB) for s in range(tq // SUB)]
            for t, h in heads:
                k0, tk, _ = specs[t]
                z_s[t, h, 0:tq, 0:tk] = lax.dot_general(qs[h, pl.ds(q0, tq), :], kb[pl.ds(k0, tk), :], _NT,
                                                        preferred_element_type=F32)
            for t, h in heads:
                _, tk, kind = specs[t]
                for s, rows in enumerate(strips):
                    z = z_s[t, h, rows, 0:tk]
                    lnb = _log1m_sigmoid(z)
                    m = _strip_mask(kind, s, tk)
                    if m is not None:
                        lnb = jnp.where(m, lnb, 0.0)
                    z_s[t, h, rows, 0:tk] = z + lnb
                    hl_s[t, h, rows, 0:tk] = lnb.astype(BF16)
            for t, h in heads:
                _, tk, _ = specs[t]
                tri = tri_l if tk == WIDE else tri_s
                c_s[t, h, 0:tq, 0:tk] = jnp.dot(hl_s[t, h, 0:tq, 0:tk], tri[...], preferred_element_type=F32)
            for t, h in heads:
                _, tk, kind = specs[t]
                for s, rows in enumerate(strips):
                    r = r_s[h, rows, :]
                    c = c_s[t, h, rows, 0:tk]
                    rr = r if tk == BLOCK else jnp.concatenate([r, r], axis=1)
                    w = jnp.exp(z_s[t, h, rows, 0:tk] + c + rr)
                    m = _strip_mask(kind, s, tk)
                    if m is not None:
                        w = jnp.where(m, w, 0.0)
                    w_s[t, h, rows, 0:tk] = w.astype(BF16)
                    total = c[:, 0:1] + hl_s[t, h, rows, 0:BLOCK].astype(F32)[:, 0:1]
                    r_s[h, rows, :] = r + jnp.broadcast_to(total, (SUB, BLOCK))
            upd = None
            for t, h in heads:
                k0, tk, _ = specs[t]
                d = jnp.dot(w_s[t, h, 0:tq, 0:tk], vh[h, pl.ds(k0, tk), :], preferred_element_type=F32)
                upd = d if upd is None else upd + d
            acc_s[0:tq, :] += upd

        def finish(q0, tq):
            o_ref[pl.ds(q0, tq), :] = acc_s[0:tq, :]
            tot_ref[pl.ds(q0, tq), :] = jnp.where(m0, r_s[0, 0:tq, :], r_s[1, 0:tq, :])

        r_s[...] = jnp.zeros(r_s.shape, F32)
        acc_s[...] = jnp.zeros(acc_s.shape, F32)

        @pl.when(i == 0)
        def _():
            tiles(0, BLOCK, [(0, BLOCK, "first")])
            finish(0, BLOCK)

        @pl.when(i > 0)
        def _():
            q0 = pl.multiple_of(i * WIDE - BLOCK, BLOCK)
            full = lambda j: (pl.multiple_of(j * WIDE - BLOCK, BLOCK), WIDE, "none")
            diag, meta = (q0, WIDE, "diag"), (0, BLOCK, "pad")

            @pl.when(i == 1)
            def _():
                tiles(q0, WIDE, [diag, meta])

            @pl.when(i >= 2)
            def _():
                tiles(q0, WIDE, [diag, full(i - 1)])

                def inner(n, carry):
                    j = i - 2 - 2 * n
                    tiles(q0, WIDE, [full(j), full(j - 1)])
                    return carry

                lax.fori_loop(0, (i - 2) // 2, inner, 0)

                @pl.when(i % 2 == 1)
                def _():
                    tiles(q0, WIDE, [full(1), meta])

                @pl.when(i % 2 == 0)
                def _():
                    tiles(q0, WIDE, [meta])

            finish(q0, WIDE)

        if exchange:
            @pl.when((p == n_pair - 1) & (i == nq - 1))
            def _():
                if exchange.split:
                    exchange.forward(True, x_out, *x_sem[3:])
                else:
                    exchange.run(True, x_in, x_out, *x_sem)

    cq, ck, cv = COL_Q // BLOCK, COL_K // BLOCK, COL_V // BLOCK
    col = lambda c0: (lambda p, i: (0, c0 + p))
    scratch = [pltpu.VMEM((2, lp, BLOCK), BF16), pltpu.VMEM((lp, BLOCK), BF16), pltpu.VMEM((2, lp, BLOCK), BF16),
               pltpu.VMEM((WIDE, WIDE), BF16), pltpu.VMEM((BLOCK, BLOCK), BF16),
               pltpu.VMEM((2, 2, WIDE, WIDE), F32), pltpu.VMEM((2, 2, WIDE, WIDE), BF16),
               pltpu.VMEM((2, 2, WIDE, WIDE), F32), pltpu.VMEM((2, 2, WIDE, WIDE), BF16),
               pltpu.VMEM((2, WIDE, BLOCK), F32), pltpu.VMEM((WIDE, BLOCK), F32)]
    res = _call(body, name="attn_fwd",
                out_shape=[_sds((lp, D_ATTN), F32), _sds((lp, D_ATTN), F32)] + (exchange.out_shape() if exchange else []),
                grid=(n_pair, nq),
                in_specs=[pl.BlockSpec((lp, BLOCK), col(cq)), pl.BlockSpec((lp, BLOCK), col(ck)),
                          pl.BlockSpec((lp, BLOCK), col(cv))] + [_ANY] * n_x,
                out_specs=[pl.BlockSpec((lp, BLOCK), col(0)), pl.BlockSpec((lp, BLOCK), col(0))] + [_ANY] * n_x,
                scratch=scratch + (exchange.scratch() if exchange else []),
                vmem_mb=56)(proj, proj, proj, *(exchange.arrs if exchange else []))
    return res[0], res[1], list(res[2:])


def _attn_bwd(proj, tot, d_out, exchange=None):
    lp = proj.shape[0]
    nb, nq = _attn_blocks(lp)
    n_pair = D_ATTN // BLOCK
    n_x = exchange.n if exchange else 0
    n_s = 23

    def body(*refs):
        q_ref, k_ref, v_ref, tot_ref, do_ref = refs[:5]
        x_in = refs[5:5 + n_x]
        dq_ref, dk_ref, dv_ref = refs[5 + n_x:8 + n_x]
        x_out = refs[8 + n_x:8 + 2 * n_x]
        (qs, kb, kh, vb, doh, tge_l, tge_s, tle_l, tle_s, z_s, g_s, hl_s, c_s, gl_s, gc_s, w_s, dz_s,
         tot_s, a_s, b_s, dq_acc, dk_acc, dv_acc) = refs[8 + 2 * n_x:8 + 2 * n_x + n_s]
        x_sem = refs[8 + 2 * n_x + n_s:]
        p, i = pl.program_id(0), pl.program_id(1)
        m0 = lax.broadcasted_iota(jnp.int32, (1, BLOCK), 1) < (BLOCK // 2)

        if exchange:
            @pl.when((p == 0) & (i == 0))
            def _():
                exchange.run(False, x_in, x_out, *x_sem)

        @pl.when(i == 0)
        def _():
            tge_l[...] = _tri(WIDE, "gt")
            tge_s[...] = _tri(BLOCK, "gt")
            tle_l[...] = _tri(WIDE, "le")
            tle_s[...] = _tri(BLOCK, "le")

            def prep(b, carry):
                rows = pl.ds(pl.multiple_of(b * BLOCK, BLOCK), BLOCK)
                q = q_ref[rows, :] * 0.125
                k = k_ref[rows, :]
                do = do_ref[rows, :]
                qs[0, rows, :] = jnp.where(m0, q, 0.0).astype(BF16)
                qs[1, rows, :] = jnp.where(m0, 0.0, q).astype(BF16)
                kb[rows, :] = k.astype(BF16)
                kh[0, rows, :] = jnp.where(m0, k, 0.0).astype(BF16)
                kh[1, rows, :] = jnp.where(m0, 0.0, k).astype(BF16)
                vb[rows, :] = v_ref[rows, :].astype(BF16)
                doh[0, rows, :] = jnp.where(m0, do, 0.0).astype(BF16)
                doh[1, rows, :] = jnp.where(m0, 0.0, do).astype(BF16)
                dk_acc[rows, :] = jnp.zeros((BLOCK, BLOCK), F32)
                dv_acc[rows, :] = jnp.zeros((BLOCK, BLOCK), F32)
                return carry

            lax.fori_loop(0, nb, prep, 0)

        def wide(x, tk):
            return x if tk == BLOCK else jnp.concatenate([x, x], axis=1)

        def tiles(q0, tq, specs):
            heads = [(t, h) for t in range(len(specs)) for h in range(2)]
            strips = [slice(s * SUB, (s + 1) * SUB) for s in range(tq // SUB)]
            for t, h in heads:
                k0, tk, _ = specs[t]
                z_s[t, h, 0:tq, 0:tk] = lax.dot_general(qs[h, pl.ds(q0, tq), :], kb[pl.ds(k0, tk), :], _NT,
                                                        preferred_element_type=F32)
                g_s[t, h, 0:tq, 0:tk] = lax.dot_general(doh[h, pl.ds(q0, tq), :], vb[pl.ds(k0, tk), :], _NT,
                                                        preferred_element_type=F32)
            for t, h in heads:
                _, tk, kind = specs[t]
                for s, rows in enumerate(strips):
                    z = z_s[t, h, rows, 0:tk]
                    lnb = _log1m_sigmoid(z)
                    m = _strip_mask(kind, s, tk)
                    if m is not None:
                        lnb = jnp.where(m, lnb, 0.0)
                    z_s[t, h, rows, 0:tk] = z + lnb
                    hl_s[t, h, rows, 0:tk] = lnb.astype(BF16)
            for t, h in heads:
                _, tk, _ = specs[t]
                tri = tge_l if tk == WIDE else tge_s
                c_s[t, h, 0:tq, 0:tk] = jnp.dot(hl_s[t, h, 0:tq, 0:tk], tri[...], preferred_element_type=F32)
            for t, h in heads:
                _, tk, kind = specs[t]
                for s, rows in enumerate(strips):
                    c = c_s[t, h, rows, 0:tk]
                    total = c[:, 0:1] + hl_s[t, h, rows, 0:BLOCK].astype(F32)[:, 0:1]
                    a_next = a_s[h, rows, :] + jnp.broadcast_to(total, (SUB, BLOCK))
                    a_s[h, rows, :] = a_next
                    w = jnp.exp(z_s[t, h, rows, 0:tk] + c + wide(tot_s[h, rows, :] - a_next, tk))
                    m = _strip_mask(kind, s, tk)
                    if m is not None:
                        w = jnp.where(m, w, 0.0)
                    g = w * g_s[t, h, rows, 0:tk]
                    g_s[t, h, rows, 0:tk] = g
                    gl_s[t, h, rows, 0:tk] = g.astype(BF16)
                    w_s[t, h, rows, 0:tk] = w.astype(BF16)
            for t, h in heads:
                _, tk, _ = specs[t]
                tri = tle_l if tk == WIDE else tle_s
                gc_s[t, h, 0:tq, 0:tk] = jnp.dot(gl_s[t, h, 0:tq, 0:tk], tri[...], preferred_element_type=F32)
            for t, h in heads:
                _, tk, kind = specs[t]
                for s, rows in enumerate(strips):
                    gc = gc_s[t, h, rows, 0:tk]
                    b = b_s[h, rows, :]
                    sig = jnp.exp(z_s[t, h, rows, 0:tk])
                    dz = g_s[t, h, rows, 0:tk] - sig * (gc + wide(b, tk))
                    m = _strip_mask(kind, s, tk)
                    if m is not None:
                        dz = jnp.where(m, dz, 0.0)
                    dz_s[t, h, rows, 0:tk] = dz.astype(BF16)
                    b_s[h, rows, :] = b + jnp.broadcast_to(gc[:, tk - 1:tk], (SUB, BLOCK))
            upd = None
            for t, h in heads:
                k0, tk, _ = specs[t]
                d = jnp.dot(dz_s[t, h, 0:tq, 0:tk], kh[h, pl.ds(k0, tk), :], preferred_element_type=F32)
                upd = d if upd is None else upd + d
            dq_acc[0:tq, :] += upd
            for t, (k0, tk, _) in enumerate(specs):
                dk_acc[pl.ds(k0, tk), :] += (
                    lax.dot_general(dz_s[t, 0, 0:tq, 0:tk], qs[0, pl.ds(q0, tq), :], _TN, preferred_element_type=F32) +
                    lax.dot_general(dz_s[t, 1, 0:tq, 0:tk], qs[1, pl.ds(q0, tq), :], _TN, preferred_element_type=F32))
                dv_acc[pl.ds(k0, tk), :] += (
                    lax.dot_general(w_s[t, 0, 0:tq, 0:tk], doh[0, pl.ds(q0, tq), :], _TN, preferred_element_type=F32) +
                    lax.dot_general(w_s[t, 1, 0:tq, 0:tk], doh[1, pl.ds(q0, tq), :], _TN, preferred_element_type=F32))

        def start(q0, tq):
            tv = tot_ref[pl.ds(q0, tq), :]
            tot_s[0, 0:tq, :] = jnp.broadcast_to(tv[:, 0:1], (tq, BLOCK))
            tot_s[1, 0:tq, :] = jnp.broadcast_to(tv[:, BLOCK - 1:BLOCK], (tq, BLOCK))
            a_s[...] = jnp.zeros(a_s.shape, F32)
            b_s[...] = jnp.zeros(b_s.shape, F32)
            dq_acc[...] = jnp.zeros(dq_acc.shape, F32)

        def finish(q0, tq):
            dq_ref[pl.ds(q0, tq), :] = (dq_acc[0:tq, :] * 0.125).astype(dq_ref.dtype)

        @pl.when(i == 0)
        def _():
            start(0, BLOCK)
            tiles(0, BLOCK, [(0, BLOCK, "first")])
            finish(0, BLOCK)

        @pl.when(i > 0)
        def _():
            q0 = pl.multiple_of(i * WIDE - BLOCK, BLOCK)
            full = lambda j: (pl.multiple_of(j * WIDE - BLOCK, BLOCK), WIDE, "none")
            diag, meta = (q0, WIDE, "diag"), (0, BLOCK, "pad")
            start(q0, WIDE)

            @pl.when(i == 1)
            def _():
                tiles(q0, WIDE, [meta, diag])

            @pl.when(i >= 2)
            def _():
                odd = i % 2

                @pl.when(odd == 1)
                def _():
                    tiles(q0, WIDE, [meta, full(1)])

                @pl.when(odd == 0)
                def _():
                    tiles(q0, WIDE, [meta])

                def inner(n, carry):
                    j = 1 + odd + 2 * n
                    tiles(q0, WIDE, [full(j), full(j + 1)])
                    return carry

                lax.fori_loop(0, (i - 2) // 2, inner, 0)
                tiles(q0, WIDE, [full(i - 1), diag])

            finish(q0, WIDE)

        @pl.when(i == nq - 1)
        def _():
            dk_ref[...] = dk_acc[...].astype(dk_ref.dtype)
            dv_ref[...] = dv_acc[...].astype(dv_ref.dtype)

        if exchange:
            @pl.when((p == n_pair - 1) & (i == nq - 1))
            def _():
                exchange.run(True, x_in, x_out, *x_sem)

    cq, ck, cv = COL_Q // BLOCK, COL_K // BLOCK, COL_V // BLOCK
    col = lambda c0: (lambda p, i: (0, c0 + p))
    whole = lambda c0: pl.BlockSpec((lp, BLOCK), col(c0))
    tile4 = lambda w, dt: pltpu.VMEM((2, 2, WIDE, w), dt)
    scratch = [pltpu.VMEM((2, lp, BLOCK), BF16), pltpu.VMEM((lp, BLOCK), BF16), pltpu.VMEM((2, lp, BLOCK), BF16),
               pltpu.VMEM((lp, BLOCK), BF16), pltpu.VMEM((2, lp, BLOCK), BF16),
               pltpu.VMEM((WIDE, WIDE), BF16), pltpu.VMEM((BLOCK, BLOCK), BF16),
               pltpu.VMEM((WIDE, WIDE), BF16), pltpu.VMEM((BLOCK, BLOCK), BF16),
               tile4(WIDE, F32), tile4(WIDE, F32), tile4(WIDE, BF16), tile4(WIDE, F32),
               tile4(WIDE, BF16), tile4(WIDE, F32), tile4(WIDE, BF16), tile4(WIDE, BF16),
               pltpu.VMEM((2, WIDE, BLOCK), F32), pltpu.VMEM((2, WIDE, BLOCK), F32), pltpu.VMEM((2, WIDE, BLOCK), F32),
               pltpu.VMEM((WIDE, BLOCK), F32), pltpu.VMEM((lp, BLOCK), F32), pltpu.VMEM((lp, BLOCK), F32)]
    res = _call(body, name="attn_bwd",
                out_shape=[_sds((lp, D_ATTN), BF16)] * 3 + (exchange.out_shape() if exchange else []),
                grid=(n_pair, nq),
                in_specs=[whole(cq), whole(ck), whole(cv), whole(0), whole(0)] + [_ANY] * n_x,
                out_specs=[whole(0), whole(0), whole(0)] + [_ANY] * n_x,
                scratch=scratch + (exchange.scratch() if exchange else []),
                vmem_mb=60)(proj, proj, proj, tot, d_out, *(exchange.arrs if exchange else []))
    return res[0], res[1], res[2], list(res[3:])


def _shift_down(x, d):
    return x if d == 0 else pltpu.roll(x, d, 0)


def _shift_up(x, d):
    return x if d == 0 else pltpu.roll(x, x.shape[0] - d, 0)


def _pool_windows(ext, down):
    shift = _shift_down if down else _shift_up
    outs = []
    for g, w in enumerate(POOL_WINDOWS):
        s = ext[:, g * BLOCK:(g + 1) * BLOCK]
        d = 1
        while d < w:
            s = s + shift(s, d)
            d *= 2
        outs.append(s)
    return jnp.concatenate(outs, axis=1)


def _pool_counts(pos):
    cols = [jnp.broadcast_to(jnp.clip(pos + 1, 1, w).astype(F32), (pos.shape[0], BLOCK)) for w in POOL_WINDOWS]
    return jnp.concatenate(cols, axis=1)


def _group_dot(x, w_ref, transpose):
    outs = []
    for g in range(len(POOL_WINDOWS)):
        xg = x[:, g * BLOCK:(g + 1) * BLOCK].astype(BF16)
        wg = w_ref[g * BLOCK:(g + 1) * BLOCK, :]
        if transpose:
            outs.append(lax.dot_general(xg, wg, _NT, preferred_element_type=F32))
        else:
            outs.append(jnp.dot(xg, wg, preferred_element_type=F32))
    return jnp.concatenate(outs, axis=1)


def _pooled(prev, cur, r):
    rows = cur.shape[0]
    ext = jnp.concatenate([prev[rows - HALO:], cur], axis=0)
    pos = r * rows + lax.broadcasted_iota(jnp.int32, (rows, 1), 0) - PAD
    ws = _pool_windows(ext, down=True)[HALO:]
    return jnp.where(pos >= 0, ws / _pool_counts(pos) - cur, 0.0)


def _pool_fwd(proj, w_pool_bf, scale):
    lp = proj.shape[0]
    rows = BLOCK
    cb = COL_POOL // D_POOL

    def body(prev_ref, cur_ref, w_ref, s_ref, o_ref):
        pooled = _pooled(prev_ref[...], cur_ref[...], pl.program_id(0))
        o_ref[...] = _group_dot(pooled, w_ref, False) * s_ref[...]

    return _call(body, name="pool_fwd", out_shape=_sds((lp, D_POOL), F32), grid=(lp // rows,),
                 in_specs=[pl.BlockSpec((rows, D_POOL), lambda r: (jnp.maximum(r - 1, 0), cb)),
                           pl.BlockSpec((rows, D_POOL), lambda r: (r, cb)),
                           pl.BlockSpec((D_POOL, BLOCK), lambda r: (0, 0)),
                           pl.BlockSpec((1, D_POOL), lambda r: (0, 0))],
                 out_specs=pl.BlockSpec((rows, D_POOL), lambda r: (r, 0)))(proj, proj, w_pool_bf, scale)


def _pool_bwd(proj, d_out, w_pool_bf, scale):
    lp = proj.shape[0]
    rows = BLOCK
    n_chunk = lp // rows
    cb = COL_POOL // D_POOL

    def body(prev_ref, cur_ref, do_ref, don_ref, w_ref, s_ref, du_ref, dw_ref, ds_ref):
        r = pl.program_id(0)

        @pl.when(r == 0)
        def _():
            dw_ref[...] = jnp.zeros(dw_ref.shape, F32)
            ds_ref[...] = jnp.zeros(ds_ref.shape, F32)

        pooled = _pooled(prev_ref[...], cur_ref[...], r)
        d_ext = jnp.concatenate([do_ref[...], don_ref[0:HALO]], axis=0)
        pos = r * rows + lax.broadcasted_iota(jnp.int32, (rows + HALO, 1), 0) - PAD
        dmixed = jnp.where((pos >= 0) & (pos < lp - PAD), d_ext * s_ref[...], 0.0)
        dpooled = _group_dot(dmixed, w_ref, True)
        back = _pool_windows(dpooled / _pool_counts(pos), down=False)[0:rows]
        du = jnp.where(pos[0:rows] >= 0, back - dpooled[0:rows], 0.0)
        du_ref[...] = du.astype(du_ref.dtype)
        mixed = _group_dot(pooled, w_ref, False)
        ds_ref[...] += _colsum(do_ref[...] * mixed)
        pooled_bf = pooled.astype(BF16)
        dm_bf = dmixed[0:rows].astype(BF16)
        for g in range(len(POOL_WINDOWS)):
            sl = slice(g * BLOCK, (g + 1) * BLOCK)
            dw_ref[sl, :] += lax.dot_general(pooled_bf[:, sl], dm_bf[:, sl], _TN, preferred_element_type=F32)

    return _call(body, name="pool_bwd",
                 out_shape=[_sds((lp, D_POOL), BF16), _sds((D_POOL, BLOCK), F32), _sds((1, D_POOL), F32)],
                 grid=(n_chunk,),
                 in_specs=[pl.BlockSpec((rows, D_POOL), lambda r: (jnp.maximum(r - 1, 0), cb)),
                           pl.BlockSpec((rows, D_POOL), lambda r: (r, cb)),
                           pl.BlockSpec((rows, D_POOL), lambda r: (r, 0)),
                           pl.BlockSpec((rows, D_POOL), lambda r: (jnp.minimum(r + 1, n_chunk - 1), 0)),
                           pl.BlockSpec((D_POOL, BLOCK), lambda r: (0, 0)),
                           pl.BlockSpec((1, D_POOL), lambda r: (0, 0))],
                 out_specs=[pl.BlockSpec((rows, D_POOL), lambda r: (r, 0)),
                            pl.BlockSpec((D_POOL, BLOCK), lambda r: (0, 0)),
                            pl.BlockSpec((1, D_POOL), lambda r: (0, 0))])(proj, proj, d_out, d_out, w_pool_bf, scale)


SUBLANES = 8


def _sub_shifts(x, down):
    shift = _shift_down if down else _shift_up
    return [shift(x, b) for b in range(SUBLANES)]


def _lagged(shifts, d, lo, n, down):
    a, b = divmod(d, SUBLANES)
    start = lo - SUBLANES * a if down else lo + SUBLANES * a
    return shifts[b][start:start + n]


def _conv_taps(u_shifts, wdw_ref, lo, n):
    y = None
    for d in range(CONV_WIDTH):
        term = wdw_ref[CONV_WIDTH - 1 - d:CONV_WIDTH - d, :] * _lagged(u_shifts, d, lo, n, True)
        y = term if y is None else y + term
    return y


def _layernorm_stats(y):
    mu = jnp.mean(y, axis=-1, keepdims=True)
    yc = y - mu
    rstd = lax.rsqrt(jnp.mean(yc * yc, axis=-1, keepdims=True) + EPS)
    return yc * rstd, rstd


def _conv_fwd(proj, wdw, bdw, ln_g, ln_b, wpw_bf):
    lp = proj.shape[0]
    rows = BLOCK
    ca, cg = COL_A // D_CONV, COL_G // D_CONV

    def body(ap_ref, a_ref, gp_ref, g_ref, wdw_ref, b_ref, lg_ref, lb_ref, wpw_ref, o_ref, y_ref):
        r = pl.program_id(0)
        a = jnp.concatenate([ap_ref[rows - HALO:rows], a_ref[...]], axis=0)
        g = jnp.concatenate([gp_ref[rows - HALO:rows], g_ref[...]], axis=0)
        u = a * _sigmoid(g)
        y = _conv_taps(_sub_shifts(u, True), wdw_ref, HALO, rows) + b_ref[...]
        y_ref[...] = y
        xhat, _ = _layernorm_stats(y)
        yn = xhat * lg_ref[...] + lb_ref[...]
        pos = r * rows + lax.broadcasted_iota(jnp.int32, (rows, 1), 0) - PAD
        s = jnp.where(pos >= 0, yn * _sigmoid(yn), 0.0)
        o_ref[...] = jnp.dot(s.astype(BF16), wpw_ref[...], preferred_element_type=F32)

    prev = lambda c: (lambda r: (jnp.maximum(r - 1, 0), c))
    cur = lambda c: (lambda r: (r, c))
    const = lambda shape: pl.BlockSpec(shape, lambda r: (0, 0))
    return _call(body, name="conv_fwd", out_shape=[_sds((lp, D_CONV), F32)] * 2, grid=(lp // rows,),
                 in_specs=[pl.BlockSpec((rows, D_CONV), prev(ca)), pl.BlockSpec((rows, D_CONV), cur(ca)),
                           pl.BlockSpec((rows, D_CONV), prev(cg)), pl.BlockSpec((rows, D_CONV), cur(cg)),
                           const((HALO, D_CONV)), const((1, D_CONV)), const((1, D_CONV)), const((1, D_CONV)),
                           const((D_CONV, D_CONV))],
                 out_specs=[pl.BlockSpec((rows, D_CONV), cur(0))] * 2)(
                     proj, proj, proj, proj, wdw, bdw, ln_g, ln_b, wpw_bf)


def _conv_bwd(proj, d_out, y_conv, wdw, ln_g, ln_b, wpw_bf):
    lp = proj.shape[0]
    rows = BLOCK
    n_chunk = lp // rows
    ca, cg = COL_A // D_CONV, COL_G // D_CONV
    ext = rows + HALO

    def body(ap_ref, a_ref, an_ref, gp_ref, g_ref, gn_ref, do_ref, don_ref, y_ref, yn_ref, wdw_ref, lg_ref, lb_ref,
             wpw_ref, da_ref, dg_ref, dwdw_ref, db_ref, dlg_ref, dlb_ref, dwpw_ref):
        r = pl.program_id(0)

        @pl.when(r == 0)
        def _():
            for ref in (dwdw_ref, db_ref, dlg_ref, dlb_ref, dwpw_ref):
                ref[...] = jnp.zeros(ref.shape, F32)

        a3 = jnp.concatenate([ap_ref[rows - HALO:rows], a_ref[...], an_ref[0:HALO]], axis=0)
        g3 = jnp.concatenate([gp_ref[rows - HALO:rows], g_ref[...], gn_ref[0:HALO]], axis=0)
        sig3 = _sigmoid(g3)
        u3 = a3 * sig3
        u_shifts = _sub_shifts(u3, True)
        y = jnp.concatenate([y_ref[...], yn_ref[0:HALO]], axis=0)
        xhat, rstd = _layernorm_stats(y)
        yn = xhat * lg_ref[...] + lb_ref[...]
        sgm = _sigmoid(yn)
        pos = r * rows + lax.broadcasted_iota(jnp.int32, (ext, 1), 0) - PAD
        valid = (pos >= 0) & (pos < lp - PAD)
        d_ext = jnp.concatenate([do_ref[...], don_ref[0:HALO]], axis=0)
        ds = lax.dot_general(d_ext.astype(BF16), wpw_ref[...], _NT, preferred_element_type=F32)
        dyn = jnp.where(valid, ds * (sgm * (1.0 + yn * (1.0 - sgm))), 0.0)
        dxh = dyn * lg_ref[...]
        dy = rstd * (dxh - jnp.mean(dxh, axis=-1, keepdims=True)
                     - xhat * jnp.mean(dxh * xhat, axis=-1, keepdims=True))
        s_cur = jnp.where(valid[0:rows], (yn * sgm)[0:rows], 0.0)
        dwpw_ref[...] += lax.dot_general(s_cur.astype(BF16), do_ref[...].astype(BF16), _TN,
                                         preferred_element_type=F32)
        dlg_ref[...] += _colsum(dyn[0:rows] * xhat[0:rows])
        dlb_ref[...] += _colsum(dyn[0:rows])
        dy_cur = dy[0:rows]
        db_ref[...] += _colsum(dy_cur)
        dy_shifts = _sub_shifts(dy, False)
        du = None
        for d in range(CONV_WIDTH):
            k = CONV_WIDTH - 1 - d
            dwdw_ref[k:k + 1, :] += _colsum(dy_cur * _lagged(u_shifts, d, HALO, rows, True))
            term = wdw_ref[k:k + 1, :] * _lagged(dy_shifts, d, 0, rows, False)
            du = term if du is None else du + term
        du = jnp.where(pos[0:rows] >= 0, du, 0.0)
        sig = sig3[HALO:HALO + rows]
        da_ref[...] = (du * sig).astype(da_ref.dtype)
        dg_ref[...] = (du * a_ref[...] * sig * (1.0 - sig)).astype(dg_ref.dtype)

    prev = lambda c: (lambda r: (jnp.maximum(r - 1, 0), c))
    cur = lambda c: (lambda r: (r, c))
    nxt = lambda c: (lambda r: (jnp.minimum(r + 1, n_chunk - 1), c))
    const = lambda shape: pl.BlockSpec(shape, lambda r: (0, 0))
    blk = lambda f: pl.BlockSpec((rows, D_CONV), f)
    return _call(body, name="conv_bwd",
                 out_shape=[_sds((lp, D_CONV), BF16), _sds((lp, D_CONV), BF16), _sds((HALO, D_CONV), F32),
                            _sds((1, D_CONV), F32), _sds((1, D_CONV), F32), _sds((1, D_CONV), F32),
                            _sds((D_CONV, D_CONV), F32)],
                 grid=(n_chunk,),
                 in_specs=[blk(prev(ca)), blk(cur(ca)), blk(nxt(ca)), blk(prev(cg)), blk(cur(cg)), blk(nxt(cg)),
                           blk(cur(0)), blk(nxt(0)), blk(cur(0)), blk(nxt(0)),
                           const((HALO, D_CONV)), const((1, D_CONV)), const((1, D_CONV)),
                           const((D_CONV, D_CONV))],
                 out_specs=[blk(cur(0)), blk(cur(0)), const((HALO, D_CONV)), const((1, D_CONV)),
                            const((1, D_CONV)), const((1, D_CONV)), const((D_CONV, D_CONV))],
                 vmem_mb=48)(proj, proj, proj, proj, proj, proj, d_out, d_out, y_conv, y_conv, wdw, ln_g, ln_b, wpw_bf)


_ANY = pl.BlockSpec(memory_space=pl.ANY)


def _place():
    return lax.axis_index("x"), lax.axis_index("y"), lax.axis_index("c")


def _xy_peers(x, y):
    return [(1 - x, y), (x, 1 - y), (1 - x, 1 - y)]


def _exchange_now(name, ex):
    n = ex.n

    def body(*refs):
        ins, outs, sems = refs[:n], refs[n:2 * n], refs[2 * n:]
        ex.run(False, ins, outs, *sems[:3])
        ex.run(True, ins, outs, *sems[:3])
        if ex.split:
            ex.forward(False, outs, *sems[3:])
            ex.forward(True, outs, *sems[3:])

    return _call(body, name=name, out_shape=ex.out_shape(), in_specs=[_ANY] * n, out_specs=[_ANY] * n,
                 scratch=ex.scratch())(*ex.arrs)


def _tail_exchange(name, parts, arrs, small):
    ex = _Exchange("scatter", parts)
    n, m = ex.n, len(arrs)
    flips = [(fx, fy, fc) for fx in (0, 1) for fy in (0, 1) for fc in (0, 1)][1:]

    def body(*refs):
        p_in, a_in, g_in = refs[:n], refs[n:n + m], refs[n + m]
        p_out, a_out, g_out = refs[n + m + 1:2 * n + m + 1], refs[2 * n + m + 1:2 * (n + m) + 1], refs[2 * (n + m) + 1]
        send, recv, local, s_send, s_recv, g_send, g_recv, g_local = refs[2 * (n + m) + 2:]
        x, y, c = _place()
        me = 4 * x + 2 * y + c
        ex.run(False, p_in, p_out, send, recv, local)
        others = [pltpu.make_async_remote_copy(src_ref=a_in[a], dst_ref=a_out[a], send_sem=s_send.at[a],
                                               recv_sem=s_recv.at[a], device_id=(x, y, 1 - c), device_id_type=MESH)
                  for a in range(m)]
        for k, (fx, fy, fc) in enumerate(flips):
            peer = (1 - x if fx else x, 1 - y if fy else y, 1 - c if fc else c)
            others.append(pltpu.make_async_remote_copy(src_ref=g_in, dst_ref=g_out.at[me], send_sem=g_send.at[k],
                                                       recv_sem=g_recv.at[k], device_id=peer, device_id_type=MESH))
        others.append(pltpu.make_async_copy(g_in, g_out.at[me], g_local))
        for cp in others:
            cp.start()
        for cp in others:
            cp.wait()
        ex.run(True, p_in, p_out, send, recv, local)

    dma = pltpu.SemaphoreType.DMA
    res = _call(body, name=name,
                out_shape=ex.out_shape() + [_sds(a.shape, a.dtype) for a in arrs] +
                [_sds((N_DEV,) + small.shape, small.dtype)],
                in_specs=[_ANY] * (n + m + 1), out_specs=[_ANY] * (n + m + 1),
                scratch=ex.scratch() + [dma((m,)), dma((m,)), dma((N_DEV - 1,)), dma((N_DEV - 1,)), dma])(
                    *parts, *arrs, small)
    return list(res[:n]), list(res[n:n + m]), res[n + m]


def _swap_core(name, arrs):
    n = len(arrs)

    def body(*refs):
        ins, outs = refs[:n], refs[n:2 * n]
        send, recv = refs[2 * n:]
        x, y, c = _place()
        remote = []
        for a in range(n):
            cp = pltpu.make_async_remote_copy(src_ref=ins[a], dst_ref=outs[a], send_sem=send.at[a],
                                              recv_sem=recv.at[a], device_id=(x, y, 1 - c), device_id_type=MESH)
            cp.start()
            remote.append(cp)
        for cp in remote:
            cp.wait()

    return _call(body, name=name, out_shape=[_sds(a.shape, a.dtype) for a in arrs],
                 in_specs=[_ANY] * n, out_specs=[_ANY] * n,
                 scratch=[pltpu.SemaphoreType.DMA((n,)), pltpu.SemaphoreType.DMA((n,))])(*arrs)


def _sum_slots(name, stacked, out_dtype=F32):
    s, r, c = stacked.shape
    tr = _pick(r, (256, 128, 64, 8))

    def body(in_ref, o_ref):
        acc = in_ref[0].astype(F32)
        for k in range(1, s):
            acc = acc + in_ref[k].astype(F32)
        o_ref[...] = acc.astype(o_ref.dtype)

    return _call(body, name=name, out_shape=_sds((r, c), out_dtype), grid=(r // tr,),
                 in_specs=[pl.BlockSpec((s, tr, c), lambda i: (0, i, 0))],
                 out_specs=pl.BlockSpec((tr, c), lambda i: (i, 0)))(stacked)


def _sum_slots_layers(name, r0, r1):
    s, r, c = r0.shape
    tr = _pick(r, (256, 128))

    def body(a_ref, b_ref, o_ref):
        def total(ref):
            acc = ref[0].astype(F32)
            for k in range(1, s):
                acc = acc + ref[k].astype(F32)
            return acc

        @pl.when(pl.program_id(0) == 0)
        def _():
            o_ref[...] = total(a_ref)

        @pl.when(pl.program_id(0) == 1)
        def _():
            o_ref[...] = total(b_ref)

    return _call(body, name=name, out_shape=_sds((2, r, c), F32), grid=(2, r // tr),
                 in_specs=[pl.BlockSpec((s, tr, c), lambda l, i: (0, i * (1 - l), 0)),
                           pl.BlockSpec((s, tr, c), lambda l, i: (0, i * l, 0))],
                 out_specs=pl.BlockSpec((None, tr, c), lambda l, i: (l, i, 0)))(r0, r1)


def _adamw_math(w, g, m, v):
    m = ADAM_B1 * m + (1.0 - ADAM_B1) * g
    v = ADAM_B2 * v + (1.0 - ADAM_B2) * (g * g)
    m_hat = m / (1.0 - ADAM_B1 ** ADAM_STEP)
    v_hat = v / (1.0 - ADAM_B2 ** ADAM_STEP)
    delta = -ADAM_LR * (m_hat / (jnp.sqrt(v_hat) + ADAM_EPS) + ADAM_WD * w)
    return delta, m, v


def _adamw(name, w, m, v, g_mine, g_other):
    l, r, c = w.shape
    tr = _pick(r, (128, 64, 8))

    def body(w_ref, m_ref, v_ref, ga_ref, gb_ref, g_ref, d_ref, nm_ref, nv_ref):
        g = ga_ref[...] + gb_ref[...]
        delta, nm, nv = _adamw_math(w_ref[...], g, m_ref[...], v_ref[...])
        g_ref[...] = g
        d_ref[...] = delta
        nm_ref[...] = nm
        nv_ref[...] = nv

    spec = pl.BlockSpec((None, tr, c), lambda li, i: (li, i, 0))
    return _call(body, name=name, out_shape=[_sds(w.shape, F32)] * 4, grid=(l, r // tr),
                 in_specs=[spec] * 5, out_specs=[spec] * 4, vmem_mb=48)(w, m, v, g_mine, g_other)


def _adamw_flat(name, w, m, v, g):
    r, c = w.shape
    tr = _pick(r, (256, 128, 64, 8))

    def body(w_ref, m_ref, v_ref, g_ref, d_ref, nm_ref, nv_ref):
        delta, nm, nv = _adamw_math(w_ref[...], g_ref[...], m_ref[...], v_ref[...])
        d_ref[...] = delta
        nm_ref[...] = nm
        nv_ref[...] = nv

    spec = pl.BlockSpec((tr, c), lambda i: (i, 0))
    return _call(body, name=name, out_shape=[_sds(w.shape, F32)] * 3, grid=(r // tr,),
                 in_specs=[spec] * 4, out_specs=[spec] * 3)(w, m, v, g)


def _pack(arrs, row_multiple=256):
    flat = jnp.concatenate([a.reshape(-1).astype(F32) for a in arrs])
    per = BLOCK * row_multiple
    total = -(-flat.shape[0] // per) * per
    return jnp.pad(flat, (0, total - flat.shape[0])).reshape(total // BLOCK, BLOCK)


def _unpack(buf, shapes):
    flat = buf.reshape(-1)
    outs, off = [], 0
    for s in shapes:
        size = 1
        for d in s:
            size *= d
        outs.append(flat[off:off + size].reshape(s))
        off += size
    return outs


def kernel(x, meta_tokens, pre_mix_g, w_in, w_pool, pool_scale, w_dw, b_dw, conv_ln_g, conv_ln_b, w_pw, mix_out_g, w_out, post_mix_g, pre_ffn_g, w_gate, w_up, w_down, post_ffn_g, loss_target, m_meta_tokens, m_pre_mix_g, m_w_in, m_w_pool, m_pool_scale, m_w_dw, m_b_dw, m_conv_ln_g, m_conv_ln_b, m_w_pw, m_mix_out_g, m_w_out, m_post_mix_g, m_pre_ffn_g, m_w_gate, m_w_up, m_w_down, m_post_ffn_g, v_meta_tokens, v_pre_mix_g, v_w_in, v_w_pool, v_pool_scale, v_w_dw, v_b_dw, v_conv_ln_g, v_conv_ln_b, v_w_pw, v_mix_out_g, v_w_out, v_post_mix_g, v_pre_ffn_g, v_w_gate, v_w_up, v_w_down, v_post_ffn_g):
    seq = x.shape[1]
    lp = PAD + N_META + seq
    depth = w_in.shape[0]
    xy = 2 * lax.axis_index("x") + lax.axis_index("y")

    small_shapes = [meta_tokens.shape, w_dw.shape, w_pw.shape]
    small_local = _pack([meta_tokens, w_dw, w_pw], row_multiple=8)
    big_names = ["w_in", "w_out", "w_gate", "w_up", "w_down"]
    big_local = {(k, l): w[l:l + 1].astype(BF16)
                 for k, w in zip(big_names, (w_in, w_out, w_gate, w_up, w_down)) for l in range(depth)}
    wg = {}
    half = w_in.shape[2] // 2
    first = big_local[("w_in", 0)]
    lo_half, hi_half, small_all = _exchange_now(
        "gather_first", _Exchange("gather", [first[:, :, :half], first[:, :, half:], small_local], split=True))
    wg[("w_in", 0)] = jnp.concatenate([lo_half, hi_half], axis=3)
    metas, wdws, wpws = [], [], []
    for s in range(N_SHARD):
        mt, wd, wp = _unpack(small_all[s], small_shapes)
        metas.append(mt)
        wdws.append(wd)
        wpws.append(wp)
    meta_full = jnp.concatenate(metas, axis=1)
    wdw_full = jnp.concatenate(wdws, axis=2)
    wpw_full = jnp.concatenate(wpws, axis=1)
    wdw_pad = jnp.pad(wdw_full, ((0, 0), (0, HALO - CONV_WIDTH), (0, 0)))
    wpw_bf = wpw_full.astype(BF16)
    wpool_bf = w_pool.reshape(depth, D_POOL, BLOCK).astype(BF16)

    row = lambda a, i: a[i][None, :]

    h = jnp.concatenate([jnp.zeros((PAD, D_MODEL), F32), meta_full, x[0]], axis=0)
    u = _rowwise("pre_mix_norm0", lambda hh, g: _rms(hh, g), [h], [row(pre_mix_g, 0)], [(D_MODEL, BF16)])[0]
    saved = []
    for i in range(depth):
        proj = _mm_nn_col("in_proj%d" % i, u, wg[("w_in", i)], 0, F32)
        ride = [(k, i) for k in big_names[1:]] + ([("w_in", i + 1)] if i + 1 < depth else [])
        o_attn, tot, got = _attn_fwd(proj, _Exchange("gather", [big_local[k] for k in ride], split=True))
        wg.update(zip(ride, got))
        o_pool = _pool_fwd(proj, wpool_bf[i], row(pool_scale, i))
        o_conv, y_conv = _conv_fwd(proj, wdw_pad[i], row(b_dw, i), row(conv_ln_g, i), row(conv_ln_b, i), wpw_bf[i])

        mix_gains = [row(mix_out_g, i)[:, :D_ATTN], row(mix_out_g, i)[:, D_ATTN:D_ATTN + D_POOL],
                     row(mix_out_g, i)[:, D_ATTN + D_POOL:]]
        merged = _rowwise("merge%d" % i, _merge, [o_attn, o_pool, o_conv], mix_gains, [(D_MODEL, BF16)])[0]
        mix = _mm_nn_row("out_proj%d" % i, merged, wg[("w_out", i)], 0)

        def post_mix(hh, mx, g1, g2):
            h1 = hh + _rms(mx, g1)
            return h1, _rms(h1, g2)

        h1, u2 = _rowwise("post_mix%d" % i, post_mix, [h, mix], [row(post_mix_g, i), row(pre_ffn_g, i)],
                          [(D_MODEL, F32), (D_MODEL, BF16)])
        gate = _mm_nn_col("ffn_gate%d" % i, u2, wg[("w_gate", i)], 0, F32)
        up, act = _mm_nn_col_swiglu("ffn_up%d" % i, u2, wg[("w_up", i)], 0, gate)
        ff = _mm_nn_row("ffn_down%d" % i, act, wg[("w_down", i)], 0)
        rec = dict(h=h, u=u, proj=proj, tot=tot, o_attn=o_attn, o_pool=o_pool, o_conv=o_conv, y_conv=y_conv,
                   merged=merged,
                   mix=mix, h1=h1, u2=u2, gate=gate, up=up, act=act, ff=ff)
        saved.append(rec)
        if i + 1 < depth:
            def post_ffn(hh, f, g1, g2):
                h2 = hh + _rms(f, g1)
                return h2, _rms(h2, g2)

            h, u = _rowwise("post_ffn%d" % i, post_ffn, [h1, ff], [row(post_ffn_g, i), row(pre_mix_g, i + 1)],
                            [(D_MODEL, F32), (D_MODEL, BF16)])
        else:
            def head(row0, hh, f, tgt, g1):
                y = hh + _rms(f, g1)
                rid = row0 + lax.broadcasted_iota(jnp.int32, (y.shape[0], 1), 0)
                err = jnp.where(rid >= PAD + N_META, y - tgt, 0.0)
                part = 0.5 * jnp.sum(jnp.mean(err * err, axis=-1, keepdims=True), axis=0, keepdims=True)
                return err * (1.0 / D_MODEL), jnp.broadcast_to(part, (8, BLOCK))

            dh, loss_part = _rowwise("loss_head", head, [h1, ff, loss_target[0]], [row(post_ffn_g, i)],
                                     [(D_MODEL, F32)], accs=[(8, BLOCK)], with_row0=True, headless={2})

    loss = lax.psum(loss_part[0, 0], ("x", "y", "c"))

    small_grads = {}
    big_parts = {}
    received = {}
    for i in reversed(range(depth)):
        rec = saved[i]

        def post_ffn_b(f, d, g):
            _, vjp = jax.vjp(_rms, f, g)
            df, dg = vjp(d)
            return df, dg

        dff, g_post_ffn = _rowwise("post_ffn_b%d" % i, post_ffn_b, [rec["ff"], dh], [row(post_ffn_g, i)],
                                   [(D_MODEL, BF16)], accs=[(1, D_MODEL)])
        big_parts[("w_down", i)] = _mm_tn_row("dw_down%d" % i, rec["act"], dff)
        dgate, dup = _mm_nt_row_swiglu("d_gate_up%d" % i, dff, wg[("w_down", i)], 0, rec["gate"], rec["up"])
        big_parts[("w_gate", i)] = _mm_tn_col("dw_gate%d" % i, rec["u2"], dgate)
        big_parts[("w_up", i)] = _mm_tn_col("dw_up%d" % i, rec["u2"], dup)
        du2a = _mm_nt_col("d_u2_gate%d" % i, dgate, wg[("w_gate", i)], 0)
        du2b = _mm_nt_col("d_u2_up%d" % i, dup, wg[("w_up", i)], 0)

        def post_mix_b(h1v, mx, d, da, db, g1, g2):
            _, vjp2 = jax.vjp(_rms, h1v, g2)
            dh1, dg2 = vjp2(da + db)
            dmid = d + dh1
            _, vjp1 = jax.vjp(_rms, mx, g1)
            dmx, dg1 = vjp1(dmid)
            return dmid, dmx, dg1, dg2

        dmid, dmix, g_post_mix, g_pre_ffn = _rowwise(
            "post_mix_b%d" % i, post_mix_b, [rec["h1"], rec["mix"], dh, du2a, du2b],
            [row(post_mix_g, i), row(pre_ffn_g, i)], [(D_MODEL, F32), (D_MODEL, BF16)],
            accs=[(1, D_MODEL), (1, D_MODEL)], vmem_mb=48)
        big_parts[("w_out", i)] = _mm_tn_row("dw_out%d" % i, rec["merged"], dmix)
        dmerged = _mm_nt_row("d_merged%d" % i, dmix, wg[("w_out", i)], 0, F32)

        def merge_b(oa, op, oc, d, ga, gp, gc):
            _, vjp = jax.vjp(_merge, oa, op, oc, ga, gp, gc)
            return vjp(d)

        mix_gains = [row(mix_out_g, i)[:, :D_ATTN], row(mix_out_g, i)[:, D_ATTN:D_ATTN + D_POOL],
                     row(mix_out_g, i)[:, D_ATTN + D_POOL:]]
        do_attn, do_pool, do_conv, g_mo_a, g_mo_p, g_mo_c = _rowwise(
            "merge_b%d" % i, merge_b, [rec["o_attn"], rec["o_pool"], rec["o_conv"], dmerged], mix_gains,
            [(D_ATTN, F32), (D_POOL, F32), (D_CONV, F32)], accs=[(1, D_ATTN), (1, D_POOL), (1, D_CONV)])
        g_mix_out = jnp.concatenate([g_mo_a, g_mo_p, g_mo_c], axis=1)
        du_pool, g_w_pool, g_pool_scale = _pool_bwd(rec["proj"], do_pool, wpool_bf[i], row(pool_scale, i))
        da, dgt, g_w_dw, g_b_dw, g_ln_g, g_ln_b, g_w_pw = _conv_bwd(
            rec["proj"], do_conv, rec["y_conv"], wdw_pad[i], row(conv_ln_g, i), row(conv_ln_b, i), wpw_bf[i])
        big_parts[("w_pw", i)] = g_w_pw.reshape(N_SHARD, D_CONV // N_SHARD, D_CONV).astype(BF16)
        ride = [(k, i) for k in big_names[1:] + ["w_pw"]] + ([("w_in", i + 1)] if i + 1 < depth else [])
        dq, dk, dv, got = _attn_bwd(rec["proj"], rec["tot"], do_attn,
                                    _Exchange("scatter", [big_parts[k] for k in ride]))
        received.update(zip(ride, got))
        dproj = jnp.concatenate([dq, dk, dv, du_pool, da, dgt], axis=1)
        big_parts[("w_in", i)] = _mm_tn_col("dw_in%d" % i, rec["u"], dproj)
        du = _mm_nt_col("d_u%d" % i, dproj, wg[("w_in", i)], 0)

        def pre_mix_b(row0, hv, d, dd, g):
            _, vjp = jax.vjp(_rms, hv, g)
            dhh, dg = vjp(dd)
            out = d + dhh
            return out, dg, jnp.where(row0 == 0, out, 0.0)

        dh, g_pre_mix, dh_head = _rowwise("pre_mix_b%d" % i, pre_mix_b, [rec["h"], dmid, du], [row(pre_mix_g, i)],
                                          [(D_MODEL, F32)], accs=[(1, D_MODEL), (BLOCK, D_MODEL)], with_row0=True,
                                          headless={3} if i == 0 else ())
        small_grads[i] = dict(pre_mix_g=g_pre_mix[0], w_pool=g_w_pool, pool_scale=g_pool_scale[0],
                              w_dw=g_w_dw[:CONV_WIDTH], b_dw=g_b_dw[0], conv_ln_g=g_ln_g[0], conv_ln_b=g_ln_b[0],
                              mix_out_g=g_mix_out[0], post_mix_g=g_post_mix[0],
                              pre_ffn_g=g_pre_ffn[0], post_ffn_g=g_post_ffn[0])

    grad_x = dh[None]
    g_meta_part = dh_head[PAD:PAD + N_META]

    rep_names = ["pre_mix_g", "pool_scale", "b_dw", "conv_ln_g", "conv_ln_b", "mix_out_g", "post_mix_g",
                 "pre_ffn_g", "post_ffn_g", "w_pool"]
    stack2 = lambda nme: jnp.stack([small_grads[l][nme] for l in range(depth)])
    small_list = [stack2(nme) for nme in rep_names] + [g_meta_part, stack2("w_dw")]
    small_list[rep_names.index("w_pool")] = small_list[rep_names.index("w_pool")].reshape(w_pool.shape)
    full_shapes = [a.shape for a in small_list]
    packed = _pack(small_list)

    assert depth == 2
    early = big_names[1:] + ["w_pw"]
    plane_sums = {k: _sum_slots_layers("sum_%s" % k, received[(k, 0)], received[(k, 1)]) for k in early}
    last, swapped, small_slots = _tail_exchange("tail_exchange", [big_parts[("w_in", 0)]],
                                                [plane_sums[k] for k in early], packed)
    summed = _sum_slots("sum_small", small_slots)
    full = _unpack(summed, full_shapes)
    rep_grads = dict(zip(rep_names, full[:len(rep_names)]))
    g_meta = lax.dynamic_slice_in_dim(full[-2], xy * meta_tokens.shape[1], meta_tokens.shape[1], axis=1)
    g_w_dw = lax.dynamic_slice_in_dim(full[-1], xy * w_dw.shape[2], w_dw.shape[2], axis=2)

    rep_w = dict(pre_mix_g=pre_mix_g, pool_scale=pool_scale, b_dw=b_dw, conv_ln_g=conv_ln_g, conv_ln_b=conv_ln_b,
                 mix_out_g=mix_out_g, post_mix_g=post_mix_g, pre_ffn_g=pre_ffn_g, post_ffn_g=post_ffn_g,
                 w_pool=w_pool)
    rep_m = dict(pre_mix_g=m_pre_mix_g, pool_scale=m_pool_scale, b_dw=m_b_dw, conv_ln_g=m_conv_ln_g,
                 conv_ln_b=m_conv_ln_b, mix_out_g=m_mix_out_g, post_mix_g=m_post_mix_g, pre_ffn_g=m_pre_ffn_g,
                 post_ffn_g=m_post_ffn_g, w_pool=m_w_pool)
    rep_v = dict(pre_mix_g=v_pre_mix_g, pool_scale=v_pool_scale, b_dw=v_b_dw, conv_ln_g=v_conv_ln_g,
                 conv_ln_b=v_conv_ln_b, mix_out_g=v_mix_out_g, post_mix_g=v_post_mix_g, pre_ffn_g=v_pre_ffn_g,
                 post_ffn_g=v_post_ffn_g, w_pool=v_w_pool)
    sm_names = rep_names + ["meta_tokens", "w_dw"]
    sm_w = [rep_w[k] for k in rep_names] + [meta_tokens, w_dw]
    sm_m = [rep_m[k] for k in rep_names] + [m_meta_tokens, m_w_dw]
    sm_v = [rep_v[k] for k in rep_names] + [v_meta_tokens, v_w_dw]
    sm_g = [rep_grads[k] for k in rep_names] + [g_meta, g_w_dw]
    sm_shapes = [a.shape for a in sm_w]
    sm_delta, sm_nm, sm_nv = _adamw_flat("adamw_small", _pack(sm_w), _pack(sm_m), _pack(sm_v), _pack(sm_g))
    small_out = {}
    for k, g, d, nm, nv in zip(sm_names, sm_g, _unpack(sm_delta, sm_shapes), _unpack(sm_nm, sm_shapes),
                               _unpack(sm_nv, sm_shapes)):
        small_out[k] = (g, d, nm, nv)

    other_sums = dict(zip(early, swapped))
    received[("w_in", 0)] = last[0]
    plane_sums["w_in"] = _sum_slots_layers("sum_w_in", received[("w_in", 0)], received[("w_in", 1)])
    other_sums["w_in"] = _swap_core("swap_core_w_in", [plane_sums["w_in"]])[0]
    big_w = dict(w_in=(w_in, m_w_in, v_w_in), w_out=(w_out, m_w_out, v_w_out), w_gate=(w_gate, m_w_gate, v_w_gate),
                 w_up=(w_up, m_w_up, v_w_up), w_down=(w_down, m_w_down, v_w_down), w_pw=(w_pw, m_w_pw, v_w_pw))
    big_out = {}
    for k in early + ["w_in"]:
        w, m, v = big_w[k]
        big_out[k] = _adamw("adamw_%s" % k, w, m, v, plane_sums[k], other_sums[k])

    order = ["meta_tokens", "pre_mix_g", "w_in", "w_pool", "pool_scale", "w_dw", "b_dw", "conv_ln_g", "conv_ln_b",
             "w_pw", "mix_out_g", "w_out", "post_mix_g", "pre_ffn_g", "w_gate", "w_up", "w_down", "post_ffn_g"]
    res = lambda k: big_out[k] if k in big_out else small_out[k]
    outs = [loss, grad_x]
    for part in range(4):
        outs += [res(k)[part] for k in order]
    return tuple(outs)
```

```python
import functools

import jax
import jax.numpy as jnp
from jax import lax
from jax.experimental import pallas as pl
from jax.experimental.pallas import tpu as pltpu

F32 = jnp.float32
BF16 = jnp.bfloat16

D_MODEL = 2048
N_META = 16
D_ATTN = 1024
D_POOL = 512
D_CONV = 512
POOL_WINDOWS = (2, 4, 8, 16)
CONV_WIDTH = 31
D_IN_PROJ = 3 * D_ATTN + D_POOL + 2 * D_CONV
D_FF = 5632
EPS = 1e-6
BLOCK = 128
PAD = BLOCK - N_META
HALO = 32
N_SHARD = 4
N_DEV = 8
MESH = pl.DeviceIdType.MESH

ADAM_LR = 0.001
ADAM_B1 = 0.9
ADAM_B2 = 0.999
ADAM_EPS = 1e-08
ADAM_WD = 0.01
ADAM_STEP = 10

COL_Q, COL_K, COL_V = 0, D_ATTN, 2 * D_ATTN
COL_POOL = 3 * D_ATTN
COL_A = COL_POOL + D_POOL
COL_G = COL_A + D_CONV


def _call(body, *, name, out_shape, grid=None, in_specs=None, out_specs=None, scratch=(), vmem_mb=None,
          aliases=None):
    params = {}
    if grid is not None:
        params["dimension_semantics"] = ("arbitrary",) * len(grid)
    if vmem_mb is not None:
        params["vmem_limit_bytes"] = vmem_mb << 20
    kw = dict(out_shape=out_shape, name=name, compiler_params=pltpu.CompilerParams(**params))
    if grid is not None:
        kw["grid"] = grid
    if in_specs is not None:
        kw["in_specs"] = in_specs
    if out_specs is not None:
        kw["out_specs"] = out_specs
    if scratch:
        kw["scratch_shapes"] = list(scratch)
    if aliases:
        kw["input_output_aliases"] = dict(aliases)
    return pl.pallas_call(body, **kw)


def _sds(shape, dtype):
    return jax.ShapeDtypeStruct(tuple(shape), dtype)


def _pick(n, candidates):
    for c in candidates:
        if n % c == 0:
            return c
    return n


def _rowwise(name, fn, rows_in, consts, outs, accs=(), tm=BLOCK, with_row0=False, vmem_mb=None, headless=()):
    lp = rows_in[0].shape[0]
    n_in, n_c, n_o = len(rows_in), len(consts), len(outs)
    back = lambda j: (lambda i: (jnp.maximum(i - 1, 0), 0)) if j in headless else (lambda i: (i, 0))
    rows_of = lambda j: lp - tm if j in headless else lp

    def body(*refs):
        vals = [r[...] for r in refs[:n_in + n_c]]
        if with_row0:
            vals = [pl.program_id(0) * tm] + vals
        res = fn(*vals)
        if not isinstance(res, (tuple, list)):
            res = (res,)
        o_refs = refs[n_in + n_c:n_in + n_c + n_o]
        a_refs = refs[n_in + n_c + n_o:]
        for r, v in zip(o_refs, res[:n_o]):
            r[...] = v.astype(r.dtype)
        if a_refs:
            @pl.when(pl.program_id(0) == 0)
            def _():
                for r in a_refs:
                    r[...] = jnp.zeros(r.shape, r.dtype)
            for r, v in zip(a_refs, res[n_o:]):
                r[...] += v.astype(r.dtype)

    in_specs = [pl.BlockSpec((tm, a.shape[1]), back(j)) for j, a in enumerate(rows_in)]
    in_specs += [pl.BlockSpec(c.shape, lambda i: (0, 0)) for c in consts]
    out_specs = [pl.BlockSpec((tm, w), back(n_in + j)) for j, (w, _) in enumerate(outs)]
    out_specs += [pl.BlockSpec(s, lambda i: (0, 0)) for s in accs]
    out_shape = [_sds((rows_of(n_in + j), w), dt) for j, (w, dt) in enumerate(outs)] + [_sds(s, F32) for s in accs]
    res = _call(body, name=name, out_shape=out_shape, grid=(lp // tm,), in_specs=in_specs,
                out_specs=out_specs, vmem_mb=vmem_mb)(*rows_in, *consts)
    return res


def _rms(x, g):
    return x * lax.rsqrt(jnp.mean(x * x, axis=-1, keepdims=True) + EPS) * g


def _merge(oa, op, oc, ga, gp, gc):
    return jnp.concatenate([_rms(oa, ga), _rms(op, gp), _rms(oc, gc)], axis=1)


def _colsum(x):
    return jnp.sum(x, axis=0, keepdims=True)


def _sigmoid(x):
    return 1.0 / (1.0 + jnp.exp(-x))


MM_VMEM_MB = 56


def _mm_tiles(lp):
    return _pick(lp, (1408, 384, 256, 128))


def _mm_nn_col(name, a, wg, layer, out_dtype):
    lp, k = a.shape
    n = wg.shape[3]
    tm = _mm_tiles(lp)

    def body(a_ref, w_ref, o_ref):
        o_ref[...] = jnp.dot(a_ref[...], w_ref[...], preferred_element_type=F32).astype(o_ref.dtype)

    return _call(body, name=name, out_shape=_sds((lp, N_SHARD * n), out_dtype), grid=(N_SHARD, lp // tm),
                 in_specs=[pl.BlockSpec((tm, k), lambda s, i: (i, 0)),
                           pl.BlockSpec((None, None, k, n), lambda s, i: (s, layer, 0, 0))],
                 out_specs=pl.BlockSpec((tm, n), lambda s, i: (i, s)), vmem_mb=MM_VMEM_MB)(a, wg)


def _mm_nn_col_swiglu(name, a, wg, layer, gate):
    lp, k = a.shape
    n = wg.shape[3]
    tm = _pick(lp, (384, 128))

    def body(a_ref, w_ref, g_ref, u_ref, act_ref):
        up = jnp.dot(a_ref[...], w_ref[...], preferred_element_type=F32)
        g = g_ref[...]
        u_ref[...] = up
        act_ref[...] = (g * _sigmoid(g) * up).astype(act_ref.dtype)

    blk = pl.BlockSpec((tm, n), lambda s, i: (i, s))
    return _call(body, name=name, out_shape=[_sds((lp, N_SHARD * n), F32), _sds((lp, N_SHARD * n), BF16)],
                 grid=(N_SHARD, lp // tm),
                 in_specs=[pl.BlockSpec((tm, k), lambda s, i: (i, 0)),
                           pl.BlockSpec((None, None, k, n), lambda s, i: (s, layer, 0, 0)), blk],
                 out_specs=[blk, blk], vmem_mb=MM_VMEM_MB)(a, wg, gate)


def _mm_nn_row(name, a, wg, layer):
    lp = a.shape[0]
    k, n = wg.shape[2], wg.shape[3]
    tm = _mm_tiles(lp)

    def body(a_ref, w_ref, o_ref):
        part = jnp.dot(a_ref[...], w_ref[...], preferred_element_type=F32)

        @pl.when(pl.program_id(1) == 0)
        def _():
            o_ref[...] = part

        @pl.when(pl.program_id(1) != 0)
        def _():
            o_ref[...] += part

    return _call(body, name=name, out_shape=_sds((lp, n), F32), grid=(lp // tm, N_SHARD),
                 in_specs=[pl.BlockSpec((tm, k), lambda i, s: (i, s)),
                           pl.BlockSpec((None, None, k, n), lambda i, s: (s, layer, 0, 0))],
                 out_specs=pl.BlockSpec((tm, n), lambda i, s: (i, 0)), vmem_mb=MM_VMEM_MB)(a, wg)


_NT = (((1,), (1,)), ((), ()))
_TN = (((0,), (0,)), ((), ()))


def _mm_nt_col(name, dy, wg, layer):
    lp = dy.shape[0]
    k, n = wg.shape[2], wg.shape[3]
    tm = _mm_tiles(lp)

    def body(d_ref, w_ref, o_ref):
        part = lax.dot_general(d_ref[...], w_ref[...], _NT, preferred_element_type=F32)

        @pl.when(pl.program_id(1) == 0)
        def _():
            o_ref[...] = part

        @pl.when(pl.program_id(1) != 0)
        def _():
            o_ref[...] += part

    return _call(body, name=name, out_shape=_sds((lp, k), F32), grid=(lp // tm, N_SHARD),
                 in_specs=[pl.BlockSpec((tm, n), lambda i, s: (i, s)),
                           pl.BlockSpec((None, None, k, n), lambda i, s: (s, layer, 0, 0))],
                 out_specs=pl.BlockSpec((tm, k), lambda i, s: (i, 0)), vmem_mb=MM_VMEM_MB)(dy, wg)


def _mm_nt_row(name, dy, wg, layer, out_dtype):
    lp = dy.shape[0]
    k, n = wg.shape[2], wg.shape[3]
    tm = _mm_tiles(lp)

    def body(d_ref, w_ref, o_ref):
        o_ref[...] = lax.dot_general(d_ref[...], w_ref[...], _NT, preferred_element_type=F32).astype(o_ref.dtype)

    return _call(body, name=name, out_shape=_sds((lp, N_SHARD * k), out_dtype), grid=(N_SHARD, lp // tm),
                 in_specs=[pl.BlockSpec((tm, n), lambda s, i: (i, 0)),
                           pl.BlockSpec((None, None, k, n), lambda s, i: (s, layer, 0, 0))],
                 out_specs=pl.BlockSpec((tm, k), lambda s, i: (i, s)), vmem_mb=MM_VMEM_MB)(dy, wg)


def _mm_nt_row_swiglu(name, dy, wg, layer, gate, up):
    lp = dy.shape[0]
    k, n = wg.shape[2], wg.shape[3]
    tm = _pick(lp, (384, 128))

    def body(d_ref, w_ref, g_ref, u_ref, dg_ref, du_ref):
        dact = lax.dot_general(d_ref[...], w_ref[...], _NT, preferred_element_type=F32)
        g = g_ref[...]
        sg = _sigmoid(g)
        dg_ref[...] = (dact * u_ref[...] * (sg * (1.0 + g * (1.0 - sg)))).astype(dg_ref.dtype)
        du_ref[...] = (dact * (g * sg)).astype(du_ref.dtype)

    blk = pl.BlockSpec((tm, k), lambda s, i: (i, s))
    return _call(body, name=name, out_shape=[_sds((lp, N_SHARD * k), BF16)] * 2, grid=(N_SHARD, lp // tm),
                 in_specs=[pl.BlockSpec((tm, n), lambda s, i: (i, 0)),
                           pl.BlockSpec((None, None, k, n), lambda s, i: (s, layer, 0, 0)), blk, blk],
                 out_specs=[blk, blk], vmem_mb=MM_VMEM_MB)(dy, wg, gate, up)


def _mm_tn_col(name, a, dy):
    lp, k = a.shape
    n = dy.shape[1] // N_SHARD
    tm = _mm_tiles(lp)
    tk = _pick(k, (1024, 512))

    def body(a_ref, d_ref, o_ref, acc):
        @pl.when(pl.program_id(2) == 0)
        def _():
            acc[...] = jnp.zeros(acc.shape, F32)

        acc[...] += lax.dot_general(a_ref[...], d_ref[...], _TN, preferred_element_type=F32)

        @pl.when(pl.program_id(2) == pl.num_programs(2) - 1)
        def _():
            o_ref[...] = acc[...].astype(o_ref.dtype)

    return _call(body, name=name, out_shape=_sds((N_SHARD, k, n), BF16), grid=(N_SHARD, k // tk, lp // tm),
                 in_specs=[pl.BlockSpec((tm, tk), lambda s, kk, i: (i, kk)),
                           pl.BlockSpec((tm, n), lambda s, kk, i: (i, s))],
                 out_specs=pl.BlockSpec((None, tk, n), lambda s, kk, i: (s, kk, 0)),
                 scratch=[pltpu.VMEM((tk, n), F32)], vmem_mb=MM_VMEM_MB)(a, dy)


def _mm_tn_row(name, a, dy):
    lp = a.shape[0]
    k = a.shape[1] // N_SHARD
    n = dy.shape[1]
    tm = _mm_tiles(lp)
    tn = _pick(n, (1024, 512))

    def body(a_ref, d_ref, o_ref, acc):
        @pl.when(pl.program_id(2) == 0)
        def _():
            acc[...] = jnp.zeros(acc.shape, F32)

        acc[...] += lax.dot_general(a_ref[...], d_ref[...], _TN, preferred_element_type=F32)

        @pl.when(pl.program_id(2) == pl.num_programs(2) - 1)
        def _():
            o_ref[...] = acc[...].astype(o_ref.dtype)

    return _call(body, name=name, out_shape=_sds((N_SHARD, k, n), BF16), grid=(N_SHARD, n // tn, lp // tm),
                 in_specs=[pl.BlockSpec((tm, k), lambda s, j, i: (i, s)),
                           pl.BlockSpec((tm, tn), lambda s, j, i: (i, j))],
                 out_specs=pl.BlockSpec((None, k, tn), lambda s, j, i: (s, 0, j)),
                 scratch=[pltpu.VMEM((k, tn), F32)], vmem_mb=MM_VMEM_MB)(a, dy)


SUB = 16
WIDE = 2 * BLOCK
Z_CLAMP = 20.0


def _log1m_sigmoid(z):
    return -jnp.where(z > Z_CLAMP, z, jnp.log(1.0 + jnp.exp(jnp.minimum(z, Z_CLAMP))))


def _tri(tk, kind):
    r = lax.broadcasted_iota(jnp.int32, (tk, tk), 0)
    c = lax.broadcasted_iota(jnp.int32, (tk, tk), 1)
    t = {"gt": r > c, "le": r <= c}[kind]
    return jnp.where(t, 1.0, 0.0).astype(BF16)


def _strip_mask(kind, s, tk):
    if kind == "none":
        return None
    col = lax.broadcasted_iota(jnp.int32, (SUB, tk), 1)
    row = lax.broadcasted_iota(jnp.int32, (SUB, tk), 0)
    causal = (col - row) < s * SUB
    if kind == "diag":
        return causal
    if kind == "pad":
        return col >= PAD
    return causal & (col >= PAD)


def _attn_blocks(lp):
    nb = lp // BLOCK
    assert nb % 2 == 1, "sequence must be a 128-row block plus whole 256-row blocks"
    return nb, (nb + 1) // 2


class _Exchange:
    def __init__(self, kind, arrs, split=False):
        self.kind, self.arrs, self.n, self.split = kind, list(arrs), len(arrs), split

    def out_shape(self):
        if self.kind == "gather":
            return [_sds((N_SHARD,) + a.shape, a.dtype) for a in self.arrs]
        return [_sds(a.shape, a.dtype) for a in self.arrs]

    def scratch(self):
        sems = [pltpu.SemaphoreType.DMA((3 * self.n,)), pltpu.SemaphoreType.DMA((3 * self.n,)),
                pltpu.SemaphoreType.DMA((self.n,))]
        if self.split:
            sems += [pltpu.SemaphoreType.DMA((self.n,)), pltpu.SemaphoreType.DMA((self.n,))]
        return sems

    def forward(self, wait, outs, fsend, frecv):
        x, y, c = _place()
        for a in range(self.n):
            cp = pltpu.make_async_remote_copy(src_ref=outs[a], dst_ref=outs[a], send_sem=fsend.at[a],
                                              recv_sem=frecv.at[a], device_id=(x, y, 1 - c), device_id_type=MESH)
            if wait:
                pl.when(c == a % 2)(cp.wait_send)
                pl.when(c != a % 2)(cp.wait_recv)
            else:
                pl.when(c == a % 2)(cp.start)

    def copies(self, a, ins, outs, send, recv, local):
        x, y, c = _place()
        me = 2 * x + y
        own = ins[a] if self.kind == "gather" else ins[a].at[me]
        out = [pltpu.make_async_copy(own, outs[a].at[me], local.at[a])]
        for r, (px, py) in enumerate(_xy_peers(x, y)):
            src = ins[a] if self.kind == "gather" else ins[a].at[2 * px + py]
            out.append(pltpu.make_async_remote_copy(
                src_ref=src, dst_ref=outs[a].at[me], send_sem=send.at[3 * a + r], recv_sem=recv.at[3 * a + r],
                device_id=(px, py, c), device_id_type=MESH))
        return out

    def run(self, wait, ins, outs, send, recv, local):
        for a in range(self.n):
            def go(a=a):
                for cp in self.copies(a, ins, outs, send, recv, local):
                    if wait:
                        cp.wait()
                    else:
                        cp.start()
            if self.split:
                pl.when(lax.axis_index("c") == a % 2)(go)
            else:
                go()


def _attn_fwd(proj, exchange=None):
    lp = proj.shape[0]
    nb, nq = _attn_blocks(lp)
    n_pair = D_ATTN // BLOCK
    n_x = exchange.n if exchange else 0

    def body(*refs):
        q_ref, k_ref, v_ref = refs[:3]
        x_in = refs[3:3 + n_x]
        o_ref, tot_ref = refs[3 + n_x:5 + n_x]
        x_out = refs[5 + n_x:5 + 2 * n_x]
        qs, kb, vh, tri_l, tri_s, z_s, hl_s, c_s, w_s, r_s, acc_s = refs[5 + 2 * n_x:16 + 2 * n_x]
        x_sem = refs[16 + 2 * n_x:]
        p, i = pl.program_id(0), pl.program_id(1)
        m0 = lax.broadcasted_iota(jnp.int32, (1, BLOCK), 1) < (BLOCK // 2)

        if exchange:
            @pl.when((p == 0) & (i == 0))
            def _():
                exchange.run(False, x_in, x_out, *x_sem[:3])

            if exchange.split:
                @pl.when((p == n_pair - 2) & (i == 0))
                def _():
                    exchange.run(True, x_in, x_out, *x_sem[:3])
                    exchange.forward(False, x_out, *x_sem[3:])

        @pl.when(i == 0)
        def _():
            tri_l[...] = _tri(WIDE, "gt")
            tri_s[...] = _tri(BLOCK, "gt")

            def prep(b, carry):
                rows = pl.ds(pl.multiple_of(b * BLOCK, BLOCK), BLOCK)
                q = q_ref[rows, :] * 0.125
                v = v_ref[rows, :]
                qs[0, rows, :] = jnp.where(m0, q, 0.0).astype(BF16)
                qs[1, rows, :] = jnp.where(m0, 0.0, q).astype(BF16)
                kb[rows, :] = k_ref[rows, :].astype(BF16)
                vh[0, rows, :] = jnp.where(m0, v, 0.0).astype(BF16)
                vh[1, rows, :] = jnp.where(m0, 0.0, v).astype(BF16)
                return carry

            lax.fori_loop(0, nb, prep, 0)

        def tiles(q0, tq, specs):
            heads = [(t, h) for t in range(len(specs)) for h in range(2)]
            strips = [slice(s * SUB, (s + 1) * SUB) for s in range(tq // SUB)]
            for t, h in heads:
                k0, tk, _ = specs[t]
                z_s[t, h, 0:tq, 0:tk] = lax.dot_general(qs[h, pl.ds(q0, tq), :], kb[pl.ds(k0, tk), :], _NT,
                                                        preferred_element_type=F32)
            for t, h in heads:
                _, tk, kind = specs[t]
                for s, rows in enumerate(strips):
                    z = z_s[t, h, rows, 0:tk]
                    lnb = _log1m_sigmoid(z)
                    m = _strip_mask(kind, s, tk)
                    if m is not None:
                        lnb = jnp.where(m, lnb, 0.0)
                    z_s[t, h, rows, 0:tk] = z + lnb
                    hl_s[t, h, rows, 0:tk] = lnb.astype(BF16)
            for t, h in heads:
                _, tk, _ = specs[t]
                tri = tri_l if tk == WIDE else tri_s
                c_s[t, h, 0:tq, 0:tk] = jnp.dot(hl_s[t, h, 0:tq, 0:tk], tri[...], preferred_element_type=F32)
            for t, h in heads:
                _, tk, kind = specs[t]
                for s, rows in enumerate(strips):
                    r = r_s[h, rows, :]
                    c = c_s[t, h, rows, 0:tk]
                    rr = r if tk == BLOCK else jnp.concatenate([r, r], axis=1)
                    w = jnp.exp(z_s[t, h, rows, 0:tk] + c + rr)
                    m = _strip_mask(kind, s, tk)
                    if m is not None:
                        w = jnp.where(m, w, 0.0)
                    w_s[t, h, rows, 0:tk] = w.astype(BF16)
                    total = c[:, 0:1] + hl_s[t, h, rows, 0:BLOCK].astype(F32)[:, 0:1]
                    r_s[h, rows, :] = r + jnp.broadcast_to(total, (SUB, BLOCK))
            upd = None
            for t, h in heads:
                k0, tk, _ = specs[t]
                d = jnp.dot(w_s[t, h, 0:tq, 0:tk], vh[h, pl.ds(k0, tk), :], preferred_element_type=F32)
                upd = d if upd is None else upd + d
            acc_s[0:tq, :] += upd

        def finish(q0, tq):
            o_ref[pl.ds(q0, tq), :] = acc_s[0:tq, :]
            tot_ref[pl.ds(q0, tq), :] = jnp.where(m0, r_s[0, 0:tq, :], r_s[1, 0:tq, :])

        r_s[...] = jnp.zeros(r_s.shape, F32)
        acc_s[...] = jnp.zeros(acc_s.shape, F32)

        @pl.when(i == 0)
        def _():
            tiles(0, BLOCK, [(0, BLOCK, "first")])
            finish(0, BLOCK)

        @pl.when(i > 0)
        def _():
            q0 = pl.multiple_of(i * WIDE - BLOCK, BLOCK)
            full = lambda j: (pl.multiple_of(j * WIDE - BLOCK, BLOCK), WIDE, "none")
            diag, meta = (q0, WIDE, "diag"), (0, BLOCK, "pad")

            @pl.when(i == 1)
            def _():
                tiles(q0, WIDE, [diag, meta])

            @pl.when(i >= 2)
            def _():
                tiles(q0, WIDE, [diag, full(i - 1)])

                def inner(n, carry):
                    j = i - 2 - 2 * n
                    tiles(q0, WIDE, [full(j), full(j - 1)])
                    return carry

                lax.fori_loop(0, (i - 2) // 2, inner, 0)

                @pl.when(i % 2 == 1)
                def _():
                    tiles(q0, WIDE, [full(1), meta])

                @pl.when(i % 2 == 0)
                def _():
                    tiles(q0, WIDE, [meta])

            finish(q0, WIDE)

        if exchange:
            @pl.when((p == n_pair - 1) & (i == nq - 1))
            def _():
                if exchange.split:
                    exchange.forward(True, x_out, *x_sem[3:])
                else:
                    exchange.run(True, x_in, x_out, *x_sem)

    cq, ck, cv = COL_Q // BLOCK, COL_K // BLOCK, COL_V // BLOCK
    col = lambda c0: (lambda p, i: (0, c0 + p))
    scratch = [pltpu.VMEM((2, lp, BLOCK), BF16), pltpu.VMEM((lp, BLOCK), BF16), pltpu.VMEM((2, lp, BLOCK), BF16),
               pltpu.VMEM((WIDE, WIDE), BF16), pltpu.VMEM((BLOCK, BLOCK), BF16),
               pltpu.VMEM((2, 2, WIDE, WIDE), F32), pltpu.VMEM((2, 2, WIDE, WIDE), BF16),
               pltpu.VMEM((2, 2, WIDE, WIDE), F32), pltpu.VMEM((2, 2, WIDE, WIDE), BF16),
               pltpu.VMEM((2, WIDE, BLOCK), F32), pltpu.VMEM((WIDE, BLOCK), F32)]
    res = _call(body, name="attn_fwd",
                out_shape=[_sds((lp, D_ATTN), F32), _sds((lp, D_ATTN), F32)] + (exchange.out_shape() if exchange else []),
                grid=(n_pair, nq),
                in_specs=[pl.BlockSpec((lp, BLOCK), col(cq)), pl.BlockSpec((lp, BLOCK), col(ck)),
                          pl.BlockSpec((lp, BLOCK), col(cv))] + [_ANY] * n_x,
                out_specs=[pl.BlockSpec((lp, BLOCK), col(0)), pl.BlockSpec((lp, BLOCK), col(0))] + [_ANY] * n_x,
                scratch=scratch + (exchange.scratch() if exchange else []),
                vmem_mb=56)(proj, proj, proj, *(exchange.arrs if exchange else []))
    return res[0], res[1], list(res[2:])


def _attn_bwd(proj, tot, d_out, exchange=None):
    lp = proj.shape[0]
    nb, nq = _attn_blocks(lp)
    n_pair = D_ATTN // BLOCK
    n_x = exchange.n if exchange else 0
    n_s = 23

    def body(*refs):
        q_ref, k_ref, v_ref, tot_ref, do_ref = refs[:5]
        x_in = refs[5:5 + n_x]
        dq_ref, dk_ref, dv_ref = refs[5 + n_x:8 + n_x]
        x_out = refs[8 + n_x:8 + 2 * n_x]
        (qs, kb, kh, vb, doh, tge_l, tge_s, tle_l, tle_s, z_s, g_s, hl_s, c_s, gl_s, gc_s, w_s, dz_s,
         tot_s, a_s, b_s, dq_acc, dk_acc, dv_acc) = refs[8 + 2 * n_x:8 + 2 * n_x + n_s]
        x_sem = refs[8 + 2 * n_x + n_s:]
        p, i = pl.program_id(0), pl.program_id(1)
        m0 = lax.broadcasted_iota(jnp.int32, (1, BLOCK), 1) < (BLOCK // 2)

        if exchange:
            @pl.when((p == 0) & (i == 0))
            def _():
                exchange.run(False, x_in, x_out, *x_sem)

        @pl.when(i == 0)
        def _():
            tge_l[...] = _tri(WIDE, "gt")
            tge_s[...] = _tri(BLOCK, "gt")
            tle_l[...] = _tri(WIDE, "le")
            tle_s[...] = _tri(BLOCK, "le")

            def prep(b, carry):
                rows = pl.ds(pl.multiple_of(b * BLOCK, BLOCK), BLOCK)
                q = q_ref[rows, :] * 0.125
                k = k_ref[rows, :]
                do = do_ref[rows, :]
                qs[0, rows, :] = jnp.where(m0, q, 0.0).astype(BF16)
                qs[1, rows, :] = jnp.where(m0, 0.0, q).astype(BF16)
                kb[rows, :] = k.astype(BF16)
                kh[0, rows, :] = jnp.where(m0, k, 0.0).astype(BF16)
                kh[1, rows, :] = jnp.where(m0, 0.0, k).astype(BF16)
                vb[rows, :] = v_ref[rows, :].astype(BF16)
                doh[0, rows, :] = jnp.where(m0, do, 0.0).astype(BF16)
                doh[1, rows, :] = jnp.where(m0, 0.0, do).astype(BF16)
                dk_acc[rows, :] = jnp.zeros((BLOCK, BLOCK), F32)
                dv_acc[rows, :] = jnp.zeros((BLOCK, BLOCK), F32)
                return carry

            lax.fori_loop(0, nb, prep, 0)

        def wide(x, tk):
            return x if tk == BLOCK else jnp.concatenate([x, x], axis=1)

        def tiles(q0, tq, specs):
            heads = [(t, h) for t in range(len(specs)) for h in range(2)]
            strips = [slice(s * SUB, (s + 1) * SUB) for s in range(tq // SUB)]
            for t, h in heads:
                k0, tk, _ = specs[t]
                z_s[t, h, 0:tq, 0:tk] = lax.dot_general(qs[h, pl.ds(q0, tq), :], kb[pl.ds(k0, tk), :], _NT,
                                                        preferred_element_type=F32)
                g_s[t, h, 0:tq, 0:tk] = lax.dot_general(doh[h, pl.ds(q0, tq), :], vb[pl.ds(k0, tk), :], _NT,
                                                        preferred_element_type=F32)
            for t, h in heads:
                _, tk, kind = specs[t]
                for s, rows in enumerate(strips):
                    z = z_s[t, h, rows, 0:tk]
                    lnb = _log1m_sigmoid(z)
                    m = _strip_mask(kind, s, tk)
                    if m is not None:
                        lnb = jnp.where(m, lnb, 0.0)
                    z_s[t, h, rows, 0:tk] = z + lnb
                    hl_s[t, h, rows, 0:tk] = lnb.astype(BF16)
            for t, h in heads:
                _, tk, _ = specs[t]
                tri = tge_l if tk == WIDE else tge_s
                c_s[t, h, 0:tq, 0:tk] = jnp.dot(hl_s[t, h, 0:tq, 0:tk], tri[...], preferred_element_type=F32)
            for t, h in heads:
                _, tk, kind = specs[t]
                for s, rows in enumerate(strips):
                    c = c_s[t, h, rows, 0:tk]
                    total = c[:, 0:1] + hl_s[t, h, rows, 0:BLOCK].astype(F32)[:, 0:1]
                    a_next = a_s[h, rows, :] + jnp.broadcast_to(total, (SUB, BLOCK))
                    a_s[h, rows, :] = a_next
                    w = jnp.exp(z_s[t, h, rows, 0:tk] + c + wide(tot_s[h, rows, :] - a_next, tk))
                    m = _strip_mask(kind, s, tk)
                    if m is not None:
                        w = jnp.where(m, w, 0.0)
                    g = w * g_s[t, h, rows, 0:tk]
                    g_s[t, h, rows, 0:tk] = g
                    gl_s[t, h, rows, 0:tk] = g.astype(BF16)
                    w_s[t, h, rows, 0:tk] = w.astype(BF16)
            for t, h in heads:
                _, tk, _ = specs[t]
                tri = tle_l if tk == WIDE else tle_s
                gc_s[t, h, 0:tq, 0:tk] = jnp.dot(gl_s[t, h, 0:tq, 0:tk], tri[...], preferred_element_type=F32)
            for t, h in heads:
                _, tk, kind = specs[t]
                for s, rows in enumerate(strips):
                    gc = gc_s[t, h, rows, 0:tk]
                    b = b_s[h, rows, :]
                    sig = jnp.exp(z_s[t, h, rows, 0:tk])
                    dz = g_s[t, h, rows, 0:tk] - sig * (gc + wide(b, tk))
                    m = _strip_mask(kind, s, tk)
                    if m is not None:
                        dz = jnp.where(m, dz, 0.0)
                    dz_s[t, h, rows, 0:tk] = dz.astype(BF16)
                    b_s[h, rows, :] = b + jnp.broadcast_to(gc[:, tk - 1:tk], (SUB, BLOCK))
            upd = None
            for t, h in heads:
                k0, tk, _ = specs[t]
                d = jnp.dot(dz_s[t, h, 0:tq, 0:tk], kh[h, pl.ds(k0, tk), :], preferred_element_type=F32)
                upd = d if upd is None else upd + d
            dq_acc[0:tq, :] += upd
            for t, (k0, tk, _) in enumerate(specs):
                dk_acc[pl.ds(k0, tk), :] += (
                    lax.dot_general(dz_s[t, 0, 0:tq, 0:tk], qs[0, pl.ds(q0, tq), :], _TN, preferred_element_type=F32) +
                    lax.dot_general(dz_s[t, 1, 0:tq, 0:tk], qs[1, pl.ds(q0, tq), :], _TN, preferred_element_type=F32))
                dv_acc[pl.ds(k0, tk), :] += (
                    lax.dot_general(w_s[t, 0, 0:tq, 0:tk], doh[0, pl.ds(q0, tq), :], _TN, preferred_element_type=F32) +
                    lax.dot_general(w_s[t, 1, 0:tq, 0:tk], doh[1, pl.ds(q0, tq), :], _TN, preferred_element_type=F32))

        def start(q0, tq):
            tv = tot_ref[pl.ds(q0, tq), :]
            tot_s[0, 0:tq, :] = jnp.broadcast_to(tv[:, 0:1], (tq, BLOCK))
            tot_s[1, 0:tq, :] = jnp.broadcast_to(tv[:, BLOCK - 1:BLOCK], (tq, BLOCK))
            a_s[...] = jnp.zeros(a_s.shape, F32)
            b_s[...] = jnp.zeros(b_s.shape, F32)
            dq_acc[...] = jnp.zeros(dq_acc.shape, F32)

        def finish(q0, tq):
            dq_ref[pl.ds(q0, tq), :] = (dq_acc[0:tq, :] * 0.125).astype(dq_ref.dtype)

        @pl.when(i == 0)
        def _():
            start(0, BLOCK)
            tiles(0, BLOCK, [(0, BLOCK, "first")])
            finish(0, BLOCK)

        @pl.when(i > 0)
        def _():
            q0 = pl.multiple_of(i * WIDE - BLOCK, BLOCK)
            full = lambda j: (pl.multiple_of(j * WIDE - BLOCK, BLOCK), WIDE, "none")
            diag, meta = (q0, WIDE, "diag"), (0, BLOCK, "pad")
            start(q0, WIDE)

            @pl.when(i == 1)
            def _():
                tiles(q0, WIDE, [meta, diag])

            @pl.when(i >= 2)
            def _():
                odd = i % 2

                @pl.when(odd == 1)
                def _():
                    tiles(q0, WIDE, [meta, full(1)])

                @pl.when(odd == 0)
                def _():
                    tiles(q0, WIDE, [meta])

                def inner(n, carry):
                    j = 1 + odd + 2 * n
                    tiles(q0, WIDE, [full(j), full(j + 1)])
                    return carry

                lax.fori_loop(0, (i - 2) // 2, inner, 0)
                tiles(q0, WIDE, [full(i - 1), diag])

            finish(q0, WIDE)

        @pl.when(i == nq - 1)
        def _():
            dk_ref[...] = dk_acc[...].astype(dk_ref.dtype)
            dv_ref[...] = dv_acc[...].astype(dv_ref.dtype)

        if exchange:
            @pl.when((p == n_pair - 1) & (i == nq - 1))
            def _():
                exchange.run(True, x_in, x_out, *x_sem)

    cq, ck, cv = COL_Q // BLOCK, COL_K // BLOCK, COL_V // BLOCK
    col = lambda c0: (lambda p, i: (0, c0 + p))
    whole = lambda c0: pl.BlockSpec((lp, BLOCK), col(c0))
    tile4 = lambda w, dt: pltpu.VMEM((2, 2, WIDE, w), dt)
    scratch = [pltpu.VMEM((2, lp, BLOCK), BF16), pltpu.VMEM((lp, BLOCK), BF16), pltpu.VMEM((2, lp, BLOCK), BF16),
               pltpu.VMEM((lp, BLOCK), BF16), pltpu.VMEM((2, lp, BLOCK), BF16),
               pltpu.VMEM((WIDE, WIDE), BF16), pltpu.VMEM((BLOCK, BLOCK), BF16),
               pltpu.VMEM((WIDE, WIDE), BF16), pltpu.VMEM((BLOCK, BLOCK), BF16),
               tile4(WIDE, F32), tile4(WIDE, F32), tile4(WIDE, BF16), tile4(WIDE, F32),
               tile4(WIDE, BF16), tile4(WIDE, F32), tile4(WIDE, BF16), tile4(WIDE, BF16),
               pltpu.VMEM((2, WIDE, BLOCK), F32), pltpu.VMEM((2, WIDE, BLOCK), F32), pltpu.VMEM((2, WIDE, BLOCK), F32),
               pltpu.VMEM((WIDE, BLOCK), F32), pltpu.VMEM((lp, BLOCK), F32), pltpu.VMEM((lp, BLOCK), F32)]
    res = _call(body, name="attn_bwd",
                out_shape=[_sds((lp, D_ATTN), BF16)] * 3 + (exchange.out_shape() if exchange else []),
                grid=(n_pair, nq),
                in_specs=[whole(cq), whole(ck), whole(cv), whole(0), whole(0)] + [_ANY] * n_x,
                out_specs=[whole(0), whole(0), whole(0)] + [_ANY] * n_x,
                scratch=scratch + (exchange.scratch() if exchange else []),
                vmem_mb=60)(proj, proj, proj, tot, d_out, *(exchange.arrs if exchange else []))
    return res[0], res[1], res[2], list(res[3:])


def _shift_down(x, d):
    return x if d == 0 else pltpu.roll(x, d, 0)


def _shift_up(x, d):
    return x if d == 0 else pltpu.roll(x, x.shape[0] - d, 0)


def _pool_windows(ext, down):
    shift = _shift_down if down else _shift_up
    outs = []
    for g, w in enumerate(POOL_WINDOWS):
        s = ext[:, g * BLOCK:(g + 1) * BLOCK]
        d = 1
        while d < w:
            s = s + shift(s, d)
            d *= 2
        outs.append(s)
    return jnp.concatenate(outs, axis=1)


def _pool_counts(pos):
    cols = [jnp.broadcast_to(jnp.clip(pos + 1, 1, w).astype(F32), (pos.shape[0], BLOCK)) for w in POOL_WINDOWS]
    return jnp.concatenate(cols, axis=1)


def _group_dot(x, w_ref, transpose):
    outs = []
    for g in range(len(POOL_WINDOWS)):
        xg = x[:, g * BLOCK:(g + 1) * BLOCK].astype(BF16)
        wg = w_ref[g * BLOCK:(g + 1) * BLOCK, :]
        if transpose:
            outs.append(lax.dot_general(xg, wg, _NT, preferred_element_type=F32))
        else:
            outs.append(jnp.dot(xg, wg, preferred_element_type=F32))
    return jnp.concatenate(outs, axis=1)


def _pooled(prev, cur, r):
    rows = cur.shape[0]
    ext = jnp.concatenate([prev[rows - HALO:], cur], axis=0)
    pos = r * rows + lax.broadcasted_iota(jnp.int32, (rows, 1), 0) - PAD
    ws = _pool_windows(ext, down=True)[HALO:]
    return jnp.where(pos >= 0, ws / _pool_counts(pos) - cur, 0.0)


def _pool_fwd(proj, w_pool_bf, scale):
    lp = proj.shape[0]
    rows = BLOCK
    cb = COL_POOL // D_POOL

    def body(prev_ref, cur_ref, w_ref, s_ref, o_ref):
        pooled = _pooled(prev_ref[...], cur_ref[...], pl.program_id(0))
        o_ref[...] = _group_dot(pooled, w_ref, False) * s_ref[...]

    return _call(body, name="pool_fwd", out_shape=_sds((lp, D_POOL), F32), grid=(lp // rows,),
                 in_specs=[pl.BlockSpec((rows, D_POOL), lambda r: (jnp.maximum(r - 1, 0), cb)),
                           pl.BlockSpec((rows, D_POOL), lambda r: (r, cb)),
                           pl.BlockSpec((D_POOL, BLOCK), lambda r: (0, 0)),
                           pl.BlockSpec((1, D_POOL), lambda r: (0, 0))],
                 out_specs=pl.BlockSpec((rows, D_POOL), lambda r: (r, 0)))(proj, proj, w_pool_bf, scale)


def _pool_bwd(proj, d_out, w_pool_bf, scale):
    lp = proj.shape[0]
    rows = BLOCK
    n_chunk = lp // rows
    cb = COL_POOL // D_POOL

    def body(prev_ref, cur_ref, do_ref, don_ref, w_ref, s_ref, du_ref, dw_ref, ds_ref):
        r = pl.program_id(0)

        @pl.when(r == 0)
        def _():
            dw_ref[...] = jnp.zeros(dw_ref.shape, F32)
            ds_ref[...] = jnp.zeros(ds_ref.shape, F32)

        pooled = _pooled(prev_ref[...], cur_ref[...], r)
        d_ext = jnp.concatenate([do_ref[...], don_ref[0:HALO]], axis=0)
        pos = r * rows + lax.broadcasted_iota(jnp.int32, (rows + HALO, 1), 0) - PAD
        dmixed = jnp.where((pos >= 0) & (pos < lp - PAD), d_ext * s_ref[...], 0.0)
        dpooled = _group_dot(dmixed, w_ref, True)
        back = _pool_windows(dpooled / _pool_counts(pos), down=False)[0:rows]
        du = jnp.where(pos[0:rows] >= 0, back - dpooled[0:rows], 0.0)
        du_ref[...] = du.astype(du_ref.dtype)
        mixed = _group_dot(pooled, w_ref, False)
        ds_ref[...] += _colsum(do_ref[...] * mixed)
        pooled_bf = pooled.astype(BF16)
        dm_bf = dmixed[0:rows].astype(BF16)
        for g in range(len(POOL_WINDOWS)):
            sl = slice(g * BLOCK, (g + 1) * BLOCK)
            dw_ref[sl, :] += lax.dot_general(pooled_bf[:, sl], dm_bf[:, sl], _TN, preferred_element_type=F32)

    return _call(body, name="pool_bwd",
                 out_shape=[_sds((lp, D_POOL), BF16), _sds((D_POOL, BLOCK), F32), _sds((1, D_POOL), F32)],
                 grid=(n_chunk,),
                 in_specs=[pl.BlockSpec((rows, D_POOL), lambda r: (jnp.maximum(r - 1, 0), cb)),
                           pl.BlockSpec((rows, D_POOL), lambda r: (r, cb)),
                           pl.BlockSpec((rows, D_POOL), lambda r: (r, 0)),
                           pl.BlockSpec((rows, D_POOL), lambda r: (jnp.minimum(r + 1, n_chunk - 1), 0)),
                           pl.BlockSpec((D_POOL, BLOCK), lambda r: (0, 0)),
                           pl.BlockSpec((1, D_POOL), lambda r: (0, 0))],
                 out_specs=[pl.BlockSpec((rows, D_POOL), lambda r: (r, 0)),
                            pl.BlockSpec((D_POOL, BLOCK), lambda r: (0, 0)),
                            pl.BlockSpec((1, D_POOL), lambda r: (0, 0))])(proj, proj, d_out, d_out, w_pool_bf, scale)


SUBLANES = 8


def _sub_shifts(x, down):
    shift = _shift_down if down else _shift_up
    return [shift(x, b) for b in range(SUBLANES)]


def _lagged(shifts, d, lo, n, down):
    a, b = divmod(d, SUBLANES)
    start = lo - SUBLANES * a if down else lo + SUBLANES * a
    return shifts[b][start:start + n]


def _conv_taps(u_shifts, wdw_ref, lo, n):
    y = None
    for d in range(CONV_WIDTH):
        term = wdw_ref[CONV_WIDTH - 1 - d:CONV_WIDTH - d, :] * _lagged(u_shifts, d, lo, n, True)
        y = term if y is None else y + term
    return y


def _layernorm_stats(y):
    mu = jnp.mean(y, axis=-1, keepdims=True)
    yc = y - mu
    rstd = lax.rsqrt(jnp.mean(yc * yc, axis=-1, keepdims=True) + EPS)
    return yc * rstd, rstd


def _conv_fwd(proj, wdw, bdw, ln_g, ln_b, wpw_bf):
    lp = proj.shape[0]
    rows = BLOCK
    ca, cg = COL_A // D_CONV, COL_G // D_CONV

    def body(ap_ref, a_ref, gp_ref, g_ref, wdw_ref, b_ref, lg_ref, lb_ref, wpw_ref, o_ref, y_ref):
        r = pl.program_id(0)
        a = jnp.concatenate([ap_ref[rows - HALO:rows], a_ref[...]], axis=0)
        g = jnp.concatenate([gp_ref[rows - HALO:rows], g_ref[...]], axis=0)
        u = a * _sigmoid(g)
        y = _conv_taps(_sub_shifts(u, True), wdw_ref, HALO, rows) + b_ref[...]
        y_ref[...] = y
        xhat, _ = _layernorm_stats(y)
        yn = xhat * lg_ref[...] + lb_ref[...]
        pos = r * rows + lax.broadcasted_iota(jnp.int32, (rows, 1), 0) - PAD
        s = jnp.where(pos >= 0, yn * _sigmoid(yn), 0.0)
        o_ref[...] = jnp.dot(s.astype(BF16), wpw_ref[...], preferred_element_type=F32)

    prev = lambda c: (lambda r: (jnp.maximum(r - 1, 0), c))
    cur = lambda c: (lambda r: (r, c))
    const = lambda shape: pl.BlockSpec(shape, lambda r: (0, 0))
    return _call(body, name="conv_fwd", out_shape=[_sds((lp, D_CONV), F32)] * 2, grid=(lp // rows,),
                 in_specs=[pl.BlockSpec((rows, D_CONV), prev(ca)), pl.BlockSpec((rows, D_CONV), cur(ca)),
                           pl.BlockSpec((rows, D_CONV), prev(cg)), pl.BlockSpec((rows, D_CONV), cur(cg)),
                           const((HALO, D_CONV)), const((1, D_CONV)), const((1, D_CONV)), const((1, D_CONV)),
                           const((D_CONV, D_CONV))],
                 out_specs=[pl.BlockSpec((rows, D_CONV), cur(0))] * 2)(
                     proj, proj, proj, proj, wdw, bdw, ln_g, ln_b, wpw_bf)


def _conv_bwd(proj, d_out, y_conv, wdw, ln_g, ln_b, wpw_bf):
    lp = proj.shape[0]
    rows = BLOCK
    n_chunk = lp // rows
    ca, cg = COL_A // D_CONV, COL_G // D_CONV
    ext = rows + HALO

    def body(ap_ref, a_ref, an_ref, gp_ref, g_ref, gn_ref, do_ref, don_ref, y_ref, yn_ref, wdw_ref, lg_ref, lb_ref,
             wpw_ref, da_ref, dg_ref, dwdw_ref, db_ref, dlg_ref, dlb_ref, dwpw_ref):
        r = pl.program_id(0)

        @pl.when(r == 0)
        def _():
            for ref in (dwdw_ref, db_ref, dlg_ref, dlb_ref, dwpw_ref):
                ref[...] = jnp.zeros(ref.shape, F32)

        a3 = jnp.concatenate([ap_ref[rows - HALO:rows], a_ref[...], an_ref[0:HALO]], axis=0)
        g3 = jnp.concatenate([gp_ref[rows - HALO:rows], g_ref[...], gn_ref[0:HALO]], axis=0)
        sig3 = _sigmoid(g3)
        u3 = a3 * sig3
        u_shifts = _sub_shifts(u3, True)
        y = jnp.concatenate([y_ref[...], yn_ref[0:HALO]], axis=0)
        xhat, rstd = _layernorm_stats(y)
        yn = xhat * lg_ref[...] + lb_ref[...]
        sgm = _sigmoid(yn)
        pos = r * rows + lax.broadcasted_iota(jnp.int32, (ext, 1), 0) - PAD
        valid = (pos >= 0) & (pos < lp - PAD)
        d_ext = jnp.concatenate([do_ref[...], don_ref[0:HALO]], axis=0)
        ds = lax.dot_general(d_ext.astype(BF16), wpw_ref[...], _NT, preferred_element_type=F32)
        dyn = jnp.where(valid, ds * (sgm * (1.0 + yn * (1.0 - sgm))), 0.0)
        dxh = dyn * lg_ref[...]
        dy = rstd * (dxh - jnp.mean(dxh, axis=-1, keepdims=True)
                     - xhat * jnp.mean(dxh * xhat, axis=-1, keepdims=True))
        s_cur = jnp.where(valid[0:rows], (yn * sgm)[0:rows], 0.0)
        dwpw_ref[...] += lax.dot_general(s_cur.astype(BF16), do_ref[...].astype(BF16), _TN,
                                         preferred_element_type=F32)
        dlg_ref[...] += _colsum(dyn[0:rows] * xhat[0:rows])
        dlb_ref[...] += _colsum(dyn[0:rows])
        dy_cur = dy[0:rows]
        db_ref[...] += _colsum(dy_cur)
        dy_shifts = _sub_shifts(dy, False)
        du = None
        for d in range(CONV_WIDTH):
            k = CONV_WIDTH - 1 - d
            dwdw_ref[k:k + 1, :] += _colsum(dy_cur * _lagged(u_shifts, d, HALO, rows, True))
            term = wdw_ref[k:k + 1, :] * _lagged(dy_shifts, d, 0, rows, False)
            du = term if du is None else du + term
        du = jnp.where(pos[0:rows] >= 0, du, 0.0)
        sig = sig3[HALO:HALO + rows]
        da_ref[...] = (du * sig).astype(da_ref.dtype)
        dg_ref[...] = (du * a_ref[...] * sig * (1.0 - sig)).astype(dg_ref.dtype)

    prev = lambda c: (lambda r: (jnp.maximum(r - 1, 0), c))
    cur = lambda c: (lambda r: (r, c))
    nxt = lambda c: (lambda r: (jnp.minimum(r + 1, n_chunk - 1), c))
    const = lambda shape: pl.BlockSpec(shape, lambda r: (0, 0))
    blk = lambda f: pl.BlockSpec((rows, D_CONV), f)
    return _call(body, name="conv_bwd",
                 out_shape=[_sds((lp, D_CONV), BF16), _sds((lp, D_CONV), BF16), _sds((HALO, D_CONV), F32),
                            _sds((1, D_CONV), F32), _sds((1, D_CONV), F32), _sds((1, D_CONV), F32),
                            _sds((D_CONV, D_CONV), F32)],
                 grid=(n_chunk,),
                 in_specs=[blk(prev(ca)), blk(cur(ca)), blk(nxt(ca)), blk(prev(cg)), blk(cur(cg)), blk(nxt(cg)),
                           blk(cur(0)), blk(nxt(0)), blk(cur(0)), blk(nxt(0)),
                           const((HALO, D_CONV)), const((1, D_CONV)), const((1, D_CONV)),
                           const((D_CONV, D_CONV))],
                 out_specs=[blk(cur(0)), blk(cur(0)), const((HALO, D_CONV)), const((1, D_CONV)),
                            const((1, D_CONV)), const((1, D_CONV)), const((D_CONV, D_CONV))],
                 vmem_mb=48)(proj, proj, proj, proj, proj, proj, d_out, d_out, y_conv, y_conv, wdw, ln_g, ln_b, wpw_bf)


_ANY = pl.BlockSpec(memory_space=pl.ANY)


def _place():
    return lax.axis_index("x"), lax.axis_index("y"), lax.axis_index("c")


def _xy_peers(x, y):
    return [(1 - x, y), (x, 1 - y), (1 - x, 1 - y)]


def _exchange_now(name, ex):
    n = ex.n

    def body(*refs):
        ins, outs, sems = refs[:n], refs[n:2 * n], refs[2 * n:]
        ex.run(False, ins, outs, *sems[:3])
        ex.run(True, ins, outs, *sems[:3])
        if ex.split:
            ex.forward(False, outs, *sems[3:])
            ex.forward(True, outs, *sems[3:])

    return _call(body, name=name, out_shape=ex.out_shape(), in_specs=[_ANY] * n, out_specs=[_ANY] * n,
                 scratch=ex.scratch())(*ex.arrs)


def _tail_exchange(name, parts, arrs, small):
    ex = _Exchange("scatter", parts)
    n, m = ex.n, len(arrs)
    flips = [(fx, fy, fc) for fx in (0, 1) for fy in (0, 1) for fc in (0, 1)][1:]

    def body(*refs):
        p_in, a_in, g_in = refs[:n], refs[n:n + m], refs[n + m]
        p_out, a_out, g_out = refs[n + m + 1:2 * n + m + 1], refs[2 * n + m + 1:2 * (n + m) + 1], refs[2 * (n + m) + 1]
        send, recv, local, s_send, s_recv, g_send, g_recv, g_local = refs[2 * (n + m) + 2:]
        x, y, c = _place()
        me = 4 * x + 2 * y + c
        ex.run(False, p_in, p_out, send, recv, local)
        others = [pltpu.make_async_remote_copy(src_ref=a_in[a], dst_ref=a_out[a], send_sem=s_send.at[a],
                                               recv_sem=s_recv.at[a], device_id=(x, y, 1 - c), device_id_type=MESH)
                  for a in range(m)]
        for k, (fx, fy, fc) in enumerate(flips):
            peer = (1 - x if fx else x, 1 - y if fy else y, 1 - c if fc else c)
            others.append(pltpu.make_async_remote_copy(src_ref=g_in, dst_ref=g_out.at[me], send_sem=g_send.at[k],
                                                       recv_sem=g_recv.at[k], device_id=peer, device_id_type=MESH))
        others.append(pltpu.make_async_copy(g_in, g_out.at[me], g_local))
        for cp in others:
            cp.start()
        for cp in others:
            cp.wait()
        ex.run(True, p_in, p_out, send, recv, local)

    dma = pltpu.SemaphoreType.DMA
    res = _call(body, name=name,
                out_shape=ex.out_shape() + [_sds(a.shape, a.dtype) for a in arrs] +
                [_sds((N_DEV,) + small.shape, small.dtype)],
                in_specs=[_ANY] * (n + m + 1), out_specs=[_ANY] * (n + m + 1),
                scratch=ex.scratch() + [dma((m,)), dma((m,)), dma((N_DEV - 1,)), dma((N_DEV - 1,)), dma])(
                    *parts, *arrs, small)
    return list(res[:n]), list(res[n:n + m]), res[n + m]


def _swap_core(name, arrs):
    n = len(arrs)

    def body(*refs):
        ins, outs = refs[:n], refs[n:2 * n]
        send, recv = refs[2 * n:]
        x, y, c = _place()
        remote = []
        for a in range(n):
            cp = pltpu.make_async_remote_copy(src_ref=ins[a], dst_ref=outs[a], send_sem=send.at[a],
                                              recv_sem=recv.at[a], device_id=(x, y, 1 - c), device_id_type=MESH)
            cp.start()
            remote.append(cp)
        for cp in remote:
            cp.wait()

    return _call(body, name=name, out_shape=[_sds(a.shape, a.dtype) for a in arrs],
                 in_specs=[_ANY] * n, out_specs=[_ANY] * n,
                 scratch=[pltpu.SemaphoreType.DMA((n,)), pltpu.SemaphoreType.DMA((n,))])(*arrs)


def _sum_slots(name, stacked, out_dtype=F32):
    s, r, c = stacked.shape
    tr = _pick(r, (256, 128, 64, 8))

    def body(in_ref, o_ref):
        acc = in_ref[0].astype(F32)
        for k in range(1, s):
            acc = acc + in_ref[k].astype(F32)
        o_ref[...] = acc.astype(o_ref.dtype)

    return _call(body, name=name, out_shape=_sds((r, c), out_dtype), grid=(r // tr,),
                 in_specs=[pl.BlockSpec((s, tr, c), lambda i: (0, i, 0))],
                 out_specs=pl.BlockSpec((tr, c), lambda i: (i, 0)))(stacked)


def _sum_slots_layers(name, r0, r1):
    s, r, c = r0.shape
    tr = _pick(r, (256, 128))

    def body(a_ref, b_ref, o_ref):
        def total(ref):
            acc = ref[0].astype(F32)
            for k in range(1, s):
                acc = acc + ref[k].astype(F32)
            return acc

        @pl.when(pl.program_id(0) == 0)
        def _():
            o_ref[...] = total(a_ref)

        @pl.when(pl.program_id(0) == 1)
        def _():
            o_ref[...] = total(b_ref)

    return _call(body, name=name, out_shape=_sds((2, r, c), F32), grid=(2, r // tr),
                 in_specs=[pl.BlockSpec((s, tr, c), lambda l, i: (0, i * (1 - l), 0)),
                           pl.BlockSpec((s, tr, c), lambda l, i: (0, i * l, 0))],
                 out_specs=pl.BlockSpec((None, tr, c), lambda l, i: (l, i, 0)))(r0, r1)


def _adamw_math(w, g, m, v):
    m = ADAM_B1 * m + (1.0 - ADAM_B1) * g
    v = ADAM_B2 * v + (1.0 - ADAM_B2) * (g * g)
    m_hat = m / (1.0 - ADAM_B1 ** ADAM_STEP)
    v_hat = v / (1.0 - ADAM_B2 ** ADAM_STEP)
    delta = -ADAM_LR * (m_hat / (jnp.sqrt(v_hat) + ADAM_EPS) + ADAM_WD * w)
    return delta, m, v


def _adamw(name, w, m, v, g_mine, g_other):
    l, r, c = w.shape
    tr = _pick(r, (128, 64, 8))

    def body(w_ref, m_ref, v_ref, ga_ref, gb_ref, g_ref, d_ref, nm_ref, nv_ref):
        g = ga_ref[...] + gb_ref[...]
        delta, nm, nv = _adamw_math(w_ref[...], g, m_ref[...], v_ref[...])
        g_ref[...] = g
        d_ref[...] = delta
        nm_ref[...] = nm
        nv_ref[...] = nv

    spec = pl.BlockSpec((None, tr, c), lambda li, i: (li, i, 0))
    return _call(body, name=name, out_shape=[_sds(w.shape, F32)] * 4, grid=(l, r // tr),
                 in_specs=[spec] * 5, out_specs=[spec] * 4, vmem_mb=48)(w, m, v, g_mine, g_other)


def _adamw_flat(name, w, m, v, g):
    r, c = w.shape
    tr = _pick(r, (256, 128, 64, 8))

    def body(w_ref, m_ref, v_ref, g_ref, d_ref, nm_ref, nv_ref):
        delta, nm, nv = _adamw_math(w_ref[...], g_ref[...], m_ref[...], v_ref[...])
        d_ref[...] = delta
        nm_ref[...] = nm
        nv_ref[...] = nv

    spec = pl.BlockSpec((tr, c), lambda i: (i, 0))
    return _call(body, name=name, out_shape=[_sds(w.shape, F32)] * 3, grid=(r // tr,),
                 in_specs=[spec] * 4, out_specs=[spec] * 3)(w, m, v, g)


def _pack(arrs, row_multiple=256):
    flat = jnp.concatenate([a.reshape(-1).astype(F32) for a in arrs])
    per = BLOCK * row_multiple
    total = -(-flat.shape[0] // per) * per
    return jnp.pad(flat, (0, total - flat.shape[0])).reshape(total // BLOCK, BLOCK)


def _unpack(buf, shapes):
    flat = buf.reshape(-1)
    outs, off = [], 0
    for s in shapes:
        size = 1
        for d in s:
            size *= d
        outs.append(flat[off:off + size].reshape(s))
        off += size
    return outs


def kernel(x, meta_tokens, pre_mix_g, w_in, w_pool, pool_scale, w_dw, b_dw, conv_ln_g, conv_ln_b, w_pw, mix_out_g, w_out, post_mix_g, pre_ffn_g, w_gate, w_up, w_down, post_ffn_g, loss_target, m_meta_tokens, m_pre_mix_g, m_w_in, m_w_pool, m_pool_scale, m_w_dw, m_b_dw, m_conv_ln_g, m_conv_ln_b, m_w_pw, m_mix_out_g, m_w_out, m_post_mix_g, m_pre_ffn_g, m_w_gate, m_w_up, m_w_down, m_post_ffn_g, v_meta_tokens, v_pre_mix_g, v_w_in, v_w_pool, v_pool_scale, v_w_dw, v_b_dw, v_conv_ln_g, v_conv_ln_b, v_w_pw, v_mix_out_g, v_w_out, v_post_mix_g, v_pre_ffn_g, v_w_gate, v_w_up, v_w_down, v_post_ffn_g):
    seq = x.shape[1]
    lp = PAD + N_META + seq
    depth = w_in.shape[0]
    xy = 2 * lax.axis_index("x") + lax.axis_index("y")

    small_shapes = [meta_tokens.shape, w_dw.shape, w_pw.shape]
    small_local = _pack([meta_tokens, w_dw, w_pw], row_multiple=8)
    big_names = ["w_in", "w_out", "w_gate", "w_up", "w_down"]
    big_local = {(k, l): w[l:l + 1].astype(BF16)
                 for k, w in zip(big_names, (w_in, w_out, w_gate, w_up, w_down)) for l in range(depth)}
    wg = {}
    half = w_in.shape[2] // 2
    first = big_local[("w_in", 0)]
    lo_half, hi_half, small_all = _exchange_now(
        "gather_first", _Exchange("gather", [first[:, :, :half], first[:, :, half:], small_local], split=True))
    wg[("w_in", 0)] = jnp.concatenate([lo_half, hi_half], axis=3)
    metas, wdws, wpws = [], [], []
    for s in range(N_SHARD):
        mt, wd, wp = _unpack(small_all[s], small_shapes)
        metas.append(mt)
        wdws.append(wd)
        wpws.append(wp)
    meta_full = jnp.concatenate(metas, axis=1)
    wdw_full = jnp.concatenate(wdws, axis=2)
    wpw_full = jnp.concatenate(wpws, axis=1)
    wdw_pad = jnp.pad(wdw_full, ((0, 0), (0, HALO - CONV_WIDTH), (0, 0)))
    wpw_bf = wpw_full.astype(BF16)
    wpool_bf = w_pool.reshape(depth, D_POOL, BLOCK).astype(BF16)

    row = lambda a, i: a[i][None, :]

    h = jnp.concatenate([jnp.zeros((PAD, D_MODEL), F32), meta_full, x[0]], axis=0)
    light = dict(tm=_pick(lp, (3 * BLOCK, BLOCK)), vmem_mb=48)
    u = _rowwise("pre_mix_norm0", lambda hh, g: _rms(hh, g), [h], [row(pre_mix_g, 0)], [(D_MODEL, BF16)],
                 **light)[0]
    saved = []
    for i in range(depth):
        proj = _mm_nn_col("in_proj%d" % i, u, wg[("w_in", i)], 0, F32)
        ride = [(k, i) for k in big_names[1:]] + ([("w_in", i + 1)] if i + 1 < depth else [])
        o_attn, tot, got = _attn_fwd(proj, _Exchange("gather", [big_local[k] for k in ride], split=True))
        wg.update(zip(ride, got))
        o_pool = _pool_fwd(proj, wpool_bf[i], row(pool_scale, i))
        o_conv, y_conv = _conv_fwd(proj, wdw_pad[i], row(b_dw, i), row(conv_ln_g, i), row(conv_ln_b, i), wpw_bf[i])

        mix_gains = [row(mix_out_g, i)[:, :D_ATTN], row(mix_out_g, i)[:, D_ATTN:D_ATTN + D_POOL],
                     row(mix_out_g, i)[:, D_ATTN + D_POOL:]]
        merged = _rowwise("merge%d" % i, _merge, [o_attn, o_pool, o_conv], mix_gains, [(D_MODEL, BF16)], **light)[0]
        mix = _mm_nn_row("out_proj%d" % i, merged, wg[("w_out", i)], 0)

        def post_mix(hh, mx, g1, g2):
            h1 = hh + _rms(mx, g1)
            return h1, _rms(h1, g2)

        h1, u2 = _rowwise("post_mix%d" % i, post_mix, [h, mix], [row(post_mix_g, i), row(pre_ffn_g, i)],
                          [(D_MODEL, F32), (D_MODEL, BF16)], **light)
        gate = _mm_nn_col("ffn_gate%d" % i, u2, wg[("w_gate", i)], 0, F32)
        up, act = _mm_nn_col_swiglu("ffn_up%d" % i, u2, wg[("w_up", i)], 0, gate)
        ff = _mm_nn_row("ffn_down%d" % i, act, wg[("w_down", i)], 0)
        rec = dict(h=h, u=u, proj=proj, tot=tot, o_attn=o_attn, o_pool=o_pool, o_conv=o_conv, y_conv=y_conv,
                   merged=merged,
                   mix=mix, h1=h1, u2=u2, gate=gate, up=up, act=act, ff=ff)
        saved.append(rec)
        if i + 1 < depth:
            def post_ffn(hh, f, g1, g2):
                h2 = hh + _rms(f, g1)
                return h2, _rms(h2, g2)

            h, u = _rowwise("post_ffn%d" % i, post_ffn, [h1, ff], [row(post_ffn_g, i), row(pre_mix_g, i + 1)],
                            [(D_MODEL, F32), (D_MODEL, BF16)], **light)
        else:
            def head(row0, hh, f, tgt, g1):
                y = hh + _rms(f, g1)
                rid = row0 + lax.broadcasted_iota(jnp.int32, (y.shape[0], 1), 0)
                err = jnp.where(rid >= PAD + N_META, y - tgt, 0.0)
                part = 0.5 * jnp.sum(jnp.mean(err * err, axis=-1, keepdims=True), axis=0, keepdims=True)
                return err * (1.0 / D_MODEL), jnp.broadcast_to(part, (8, BLOCK))

            dh, loss_part = _rowwise("loss_head", head, [h1, ff, loss_target[0]], [row(post_ffn_g, i)],
                                     [(D_MODEL, F32)], accs=[(8, BLOCK)], with_row0=True, headless={2})

    loss = lax.psum(loss_part[0, 0], ("x", "y", "c"))

    small_grads = {}
    big_parts = {}
    received = {}
    for i in reversed(range(depth)):
        rec = saved[i]

        def post_ffn_b(f, d, g):
            _, vjp = jax.vjp(_rms, f, g)
            df, dg = vjp(d)
            return df, dg

        dff, g_post_ffn = _rowwise("post_ffn_b%d" % i, post_ffn_b, [rec["ff"], dh], [row(post_ffn_g, i)],
                                   [(D_MODEL, BF16)], accs=[(1, D_MODEL)], **light)
        big_parts[("w_down", i)] = _mm_tn_row("dw_down%d" % i, rec["act"], dff)
        dgate, dup = _mm_nt_row_swiglu("d_gate_up%d" % i, dff, wg[("w_down", i)], 0, rec["gate"], rec["up"])
        big_parts[("w_gate", i)] = _mm_tn_col("dw_gate%d" % i, rec["u2"], dgate)
        big_parts[("w_up", i)] = _mm_tn_col("dw_up%d" % i, rec["u2"], dup)
        du2a = _mm_nt_col("d_u2_gate%d" % i, dgate, wg[("w_gate", i)], 0)
        du2b = _mm_nt_col("d_u2_up%d" % i, dup, wg[("w_up", i)], 0)

        def post_mix_b(h1v, mx, d, da, db, g1, g2):
            _, vjp2 = jax.vjp(_rms, h1v, g2)
            dh1, dg2 = vjp2(da + db)
            dmid = d + dh1
            _, vjp1 = jax.vjp(_rms, mx, g1)
            dmx, dg1 = vjp1(dmid)
            return dmid, dmx, dg1, dg2

        dmid, dmix, g_post_mix, g_pre_ffn = _rowwise(
            "post_mix_b%d" % i, post_mix_b, [rec["h1"], rec["mix"], dh, du2a, du2b],
            [row(post_mix_g, i), row(pre_ffn_g, i)], [(D_MODEL, F32), (D_MODEL, BF16)],
            accs=[(1, D_MODEL), (1, D_MODEL)], vmem_mb=48)
        big_parts[("w_out", i)] = _mm_tn_row("dw_out%d" % i, rec["merged"], dmix)
        dmerged = _mm_nt_row("d_merged%d" % i, dmix, wg[("w_out", i)], 0, F32)

        def merge_b(oa, op, oc, d, ga, gp, gc):
            _, vjp = jax.vjp(_merge, oa, op, oc, ga, gp, gc)
            return vjp(d)

        mix_gains = [row(mix_out_g, i)[:, :D_ATTN], row(mix_out_g, i)[:, D_ATTN:D_ATTN + D_POOL],
                     row(mix_out_g, i)[:, D_ATTN + D_POOL:]]
        do_attn, do_pool, do_conv, g_mo_a, g_mo_p, g_mo_c = _rowwise(
            "merge_b%d" % i, merge_b, [rec["o_attn"], rec["o_pool"], rec["o_conv"], dmerged], mix_gains,
            [(D_ATTN, F32), (D_POOL, F32), (D_CONV, F32)], accs=[(1, D_ATTN), (1, D_POOL), (1, D_CONV)])
        g_mix_out = jnp.concatenate([g_mo_a, g_mo_p, g_mo_c], axis=1)
        du_pool, g_w_pool, g_pool_scale = _pool_bwd(rec["proj"], do_pool, wpool_bf[i], row(pool_scale, i))
        da, dgt, g_w_dw, g_b_dw, g_ln_g, g_ln_b, g_w_pw = _conv_bwd(
            rec["proj"], do_conv, rec["y_conv"], wdw_pad[i], row(conv_ln_g, i), row(conv_ln_b, i), wpw_bf[i])
        big_parts[("w_pw", i)] = g_w_pw.reshape(N_SHARD, D_CONV // N_SHARD, D_CONV).astype(BF16)
        ride = [(k, i) for k in big_names[1:] + ["w_pw"]] + ([("w_in", i + 1)] if i + 1 < depth else [])
        dq, dk, dv, got = _attn_bwd(rec["proj"], rec["tot"], do_attn,
                                    _Exchange("scatter", [big_parts[k] for k in ride]))
        received.update(zip(ride, got))
        dproj = jnp.concatenate([dq, dk, dv, du_pool, da, dgt], axis=1)
        big_parts[("w_in", i)] = _mm_tn_col("dw_in%d" % i, rec["u"], dproj)
        du = _mm_nt_col("d_u%d" % i, dproj, wg[("w_in", i)], 0)

        def pre_mix_b(row0, hv, d, dd, g):
            _, vjp = jax.vjp(_rms, hv, g)
            dhh, dg = vjp(dd)
            out = d + dhh
            return out, dg, jnp.where(row0 == 0, out, 0.0)

        dh, g_pre_mix, dh_head = _rowwise("pre_mix_b%d" % i, pre_mix_b, [rec["h"], dmid, du], [row(pre_mix_g, i)],
                                          [(D_MODEL, F32)], accs=[(1, D_MODEL), (BLOCK, D_MODEL)], with_row0=True,
                                          headless={3} if i == 0 else ())
        small_grads[i] = dict(pre_mix_g=g_pre_mix[0], w_pool=g_w_pool, pool_scale=g_pool_scale[0],
                              w_dw=g_w_dw[:CONV_WIDTH], b_dw=g_b_dw[0], conv_ln_g=g_ln_g[0], conv_ln_b=g_ln_b[0],
                              mix_out_g=g_mix_out[0], post_mix_g=g_post_mix[0],
                              pre_ffn_g=g_pre_ffn[0], post_ffn_g=g_post_ffn[0])

    grad_x = dh[None]
    g_meta_part = dh_head[PAD:PAD + N_META]

    rep_names = ["pre_mix_g", "pool_scale", "b_dw", "conv_ln_g", "conv_ln_b", "mix_out_g", "post_mix_g",
                 "pre_ffn_g", "post_ffn_g", "w_pool"]
    stack2 = lambda nme: jnp.stack([small_grads[l][nme] for l in range(depth)])
    small_list = [stack2(nme) for nme in rep_names] + [g_meta_part, stack2("w_dw")]
    small_list[rep_names.index("w_pool")] = small_list[rep_names.index("w_pool")].reshape(w_pool.shape)
    full_shapes = [a.shape for a in small_list]
    packed = _pack(small_list)

    assert depth == 2
    early = big_names[1:] + ["w_pw"]
    plane_sums = {k: _sum_slots_layers("sum_%s" % k, received[(k, 0)], received[(k, 1)]) for k in early}
    last, swapped, small_slots = _tail_exchange("tail_exchange", [big_parts[("w_in", 0)]],
                                                [plane_sums[k] for k in early], packed)
    summed = _sum_slots("sum_small", small_slots)
    full = _unpack(summed, full_shapes)
    rep_grads = dict(zip(rep_names, full[:len(rep_names)]))
    g_meta = lax.dynamic_slice_in_dim(full[-2], xy * meta_tokens.shape[1], meta_tokens.shape[1], axis=1)
    g_w_dw = lax.dynamic_slice_in_dim(full[-1], xy * w_dw.shape[2], w_dw.shape[2], axis=2)

    rep_w = dict(pre_mix_g=pre_mix_g, pool_scale=pool_scale, b_dw=b_dw, conv_ln_g=conv_ln_g, conv_ln_b=conv_ln_b,
                 mix_out_g=mix_out_g, post_mix_g=post_mix_g, pre_ffn_g=pre_ffn_g, post_ffn_g=post_ffn_g,
                 w_pool=w_pool)
    rep_m = dict(pre_mix_g=m_pre_mix_g, pool_scale=m_pool_scale, b_dw=m_b_dw, conv_ln_g=m_conv_ln_g,
                 conv_ln_b=m_conv_ln_b, mix_out_g=m_mix_out_g, post_mix_g=m_post_mix_g, pre_ffn_g=m_pre_ffn_g,
                 post_ffn_g=m_post_ffn_g, w_pool=m_w_pool)
    rep_v = dict(pre_mix_g=v_pre_mix_g, pool_scale=v_pool_scale, b_dw=v_b_dw, conv_ln_g=v_conv_ln_g,
                 conv_ln_b=v_conv_ln_b, mix_out_g=v_mix_out_g, post_mix_g=v_post_mix_g, pre_ffn_g=v_pre_ffn_g,
                 post_ffn_g=v_post_ffn_g, w_pool=v_w_pool)
    sm_names = rep_names + ["meta_tokens", "w_dw"]
    sm_w = [rep_w[k] for k in rep_names] + [meta_tokens, w_dw]
    sm_m = [rep_m[k] for k in rep_names] + [m_meta_tokens, m_w_dw]
    sm_v = [rep_v[k] for k in rep_names] + [v_meta_tokens, v_w_dw]
    sm_g = [rep_grads[k] for k in rep_names] + [g_meta, g_w_dw]
    sm_shapes = [a.shape for a in sm_w]
    sm_delta, sm_nm, sm_nv = _adamw_flat("adamw_small", _pack(sm_w), _pack(sm_m), _pack(sm_v), _pack(sm_g))
    small_out = {}
    for k, g, d, nm, nv in zip(sm_names, sm_g, _unpack(sm_delta, sm_shapes), _unpack(sm_nm, sm_shapes),
                               _unpack(sm_nv, sm_shapes)):
        small_out[k] = (g, d, nm, nv)

    other_sums = dict(zip(early, swapped))
    received[("w_in", 0)] = last[0]
    plane_sums["w_in"] = _sum_slots_layers("sum_w_in", received[("w_in", 0)], received[("w_in", 1)])
    other_sums["w_in"] = _swap_core("swap_core_w_in", [plane_sums["w_in"]])[0]
    big_w = dict(w_in=(w_in, m_w_in, v_w_in), w_out=(w_out, m_w_out, v_w_out), w_gate=(w_gate, m_w_gate, v_w_gate),
                 w_up=(w_up, m_w_up, v_w_up), w_down=(w_down, m_w_down, v_w_down), w_pw=(w_pw, m_w_pw, v_w_pw))
    big_out = {}
    for k in early + ["w_in"]:
        w, m, v = big_w[k]
        big_out[k] = _adamw("adamw_%s" % k, w, m, v, plane_sums[k], other_sums[k])

    order = ["meta_tokens", "pre_mix_g", "w_in", "w_pool", "pool_scale", "w_dw", "b_dw", "conv_ln_g", "conv_ln_b",
             "w_pw", "mix_out_g", "w_out", "post_mix_g", "pre_ffn_g", "w_gate", "w_up", "w_down", "post_ffn_g"]
    res = lambda k: big_out[k] if k in big_out else small_out[k]
    outs = [loss, grad_x]
    for part in range(4):
        outs += [res(k)[part] for k in order]
    return tuple(outs)
```

```python
import functools

import jax
import jax.numpy as jnp
from jax import lax
from jax.experimental import pallas as pl
from jax.experimental.pallas import tpu as pltpu

F32 = jnp.float32
BF16 = jnp.bfloat16

D_MODEL = 2048
N_META = 16
D_ATTN = 1024
D_POOL = 512
D_CONV = 512
POOL_WINDOWS = (2, 4, 8, 16)
CONV_WIDTH = 31
D_IN_PROJ = 3 * D_ATTN + D_POOL + 2 * D_CONV
D_FF = 5632
EPS = 1e-6
BLOCK = 128
PAD = BLOCK - N_META
HALO = 32
N_SHARD = 4
N_DEV = 8
MESH = pl.DeviceIdType.MESH

ADAM_LR = 0.001
ADAM_B1 = 0.9
ADAM_B2 = 0.999
ADAM_EPS = 1e-08
ADAM_WD = 0.01
ADAM_STEP = 10

COL_Q, COL_K, COL_V = 0, D_ATTN, 2 * D_ATTN
COL_POOL = 3 * D_ATTN
COL_A = COL_POOL + D_POOL
COL_G = COL_A + D_CONV


def _call(body, *, name, out_shape, grid=None, in_specs=None, out_specs=None, scratch=(), vmem_mb=None,
          aliases=None):
    params = {}
    if grid is not None:
        params["dimension_semantics"] = ("arbitrary",) * len(grid)
    if vmem_mb is not None:
        params["vmem_limit_bytes"] = vmem_mb << 20
    kw = dict(out_shape=out_shape, name=name, compiler_params=pltpu.CompilerParams(**params))
    if grid is not None:
        kw["grid"] = grid
    if in_specs is not None:
        kw["in_specs"] = in_specs
    if out_specs is not None:
        kw["out_specs"] = out_specs
    if scratch:
        kw["scratch_shapes"] = list(scratch)
    if aliases:
        kw["input_output_aliases"] = dict(aliases)
    return pl.pallas_call(body, **kw)


def _sds(shape, dtype):
    return jax.ShapeDtypeStruct(tuple(shape), dtype)


def _pick(n, candidates):
    for c in candidates:
        if n % c == 0:
            return c
    return n


def _rowwise(name, fn, rows_in, consts, outs, accs=(), tm=BLOCK, with_row0=False, vmem_mb=None, headless=()):
    lp = rows_in[0].shape[0]
    n_in, n_c, n_o = len(rows_in), len(consts), len(outs)
    back = lambda j: (lambda i: (jnp.maximum(i - 1, 0), 0)) if j in headless else (lambda i: (i, 0))
    rows_of = lambda j: lp - tm if j in headless else lp

    def body(*refs):
        vals = [r[...] for r in refs[:n_in + n_c]]
        if with_row0:
            vals = [pl.program_id(0) * tm] + vals
        res = fn(*vals)
        if not isinstance(res, (tuple, list)):
            res = (res,)
        o_refs = refs[n_in + n_c:n_in + n_c + n_o]
        a_refs = refs[n_in + n_c + n_o:]
        for r, v in zip(o_refs, res[:n_o]):
            r[...] = v.astype(r.dtype)
        if a_refs:
            @pl.when(pl.program_id(0) == 0)
            def _():
                for r in a_refs:
                    r[...] = jnp.zeros(r.shape, r.dtype)
            for r, v in zip(a_refs, res[n_o:]):
                r[...] += v.astype(r.dtype)

    in_specs = [pl.BlockSpec((tm, a.shape[1]), back(j)) for j, a in enumerate(rows_in)]
    in_specs += [pl.BlockSpec(c.shape, lambda i: (0, 0)) for c in consts]
    out_specs = [pl.BlockSpec((tm, w), back(n_in + j)) for j, (w, _) in enumerate(outs)]
    out_specs += [pl.BlockSpec(s, lambda i: (0, 0)) for s in accs]
    out_shape = [_sds((rows_of(n_in + j), w), dt) for j, (w, dt) in enumerate(outs)] + [_sds(s, F32) for s in accs]
    res = _call(body, name=name, out_shape=out_shape, grid=(lp // tm,), in_specs=in_specs,
                out_specs=out_specs, vmem_mb=vmem_mb)(*rows_in, *consts)
    return res


def _rms(x, g):
    return x * lax.rsqrt(jnp.mean(x * x, axis=-1, keepdims=True) + EPS) * g


def _merge(oa, op, oc, ga, gp, gc):
    return jnp.concatenate([_rms(oa, ga), _rms(op, gp), _rms(oc, gc)], axis=1)


def _colsum(x):
    return jnp.sum(x, axis=0, keepdims=True)


def _sigmoid(x):
    return 1.0 / (1.0 + jnp.exp(-x))


MM_VMEM_MB = 56


def _mm_tiles(lp):
    return _pick(lp, (1408, 384, 256, 128))


def _mm_nn_col(name, a, wg, layer, out_dtype):
    lp, k = a.shape
    n = wg.shape[3]
    tm = _mm_tiles(lp)

    def body(a_ref, w_ref, o_ref):
        o_ref[...] = jnp.dot(a_ref[...], w_ref[...], preferred_element_type=F32).astype(o_ref.dtype)

    return _call(body, name=name, out_shape=_sds((lp, N_SHARD * n), out_dtype), grid=(N_SHARD, lp // tm),
                 in_specs=[pl.BlockSpec((tm, k), lambda s, i: (i, 0)),
                           pl.BlockSpec((None, None, k, n), lambda s, i: (s, layer, 0, 0))],
                 out_specs=pl.BlockSpec((tm, n), lambda s, i: (i, s)), vmem_mb=MM_VMEM_MB)(a, wg)


def _mm_nn_col_swiglu(name, a, wg, layer, gate):
    lp, k = a.shape
    n = wg.shape[3]
    tm = _pick(lp, (704, 384, 128))

    def body(a_ref, w_ref, g_ref, u_ref, act_ref):
        up = jnp.dot(a_ref[...], w_ref[...], preferred_element_type=F32)
        g = g_ref[...]
        u_ref[...] = up
        act_ref[...] = (g * _sigmoid(g) * up).astype(act_ref.dtype)

    blk = pl.BlockSpec((tm, n), lambda s, i: (i, s))
    return _call(body, name=name, out_shape=[_sds((lp, N_SHARD * n), F32), _sds((lp, N_SHARD * n), BF16)],
                 grid=(N_SHARD, lp // tm),
                 in_specs=[pl.BlockSpec((tm, k), lambda s, i: (i, 0)),
                           pl.BlockSpec((None, None, k, n), lambda s, i: (s, layer, 0, 0)), blk],
                 out_specs=[blk, blk], vmem_mb=MM_VMEM_MB)(a, wg, gate)


def _mm_nn_row(name, a, wg, layer):
    lp = a.shape[0]
    k, n = wg.shape[2], wg.shape[3]
    tm = _mm_tiles(lp)

    def body(a_ref, w_ref, o_ref):
        part = jnp.dot(a_ref[...], w_ref[...], preferred_element_type=F32)

        @pl.when(pl.program_id(1) == 0)
        def _():
            o_ref[...] = part

        @pl.when(pl.program_id(1) != 0)
        def _():
            o_ref[...] += part

    return _call(body, name=name, out_shape=_sds((lp, n), F32), grid=(lp // tm, N_SHARD),
                 in_specs=[pl.BlockSpec((tm, k), lambda i, s: (i, s)),
                           pl.BlockSpec((None, None, k, n), lambda i, s: (s, layer, 0, 0))],
                 out_specs=pl.BlockSpec((tm, n), lambda i, s: (i, 0)), vmem_mb=MM_VMEM_MB)(a, wg)


_NT = (((1,), (1,)), ((), ()))
_TN = (((0,), (0,)), ((), ()))


def _mm_nt_col(name, dy, wg, layer):
    lp = dy.shape[0]
    k, n = wg.shape[2], wg.shape[3]
    tm = _mm_tiles(lp)

    def body(d_ref, w_ref, o_ref):
        part = lax.dot_general(d_ref[...], w_ref[...], _NT, preferred_element_type=F32)

        @pl.when(pl.program_id(1) == 0)
        def _():
            o_ref[...] = part

        @pl.when(pl.program_id(1) != 0)
        def _():
            o_ref[...] += part

    return _call(body, name=name, out_shape=_sds((lp, k), F32), grid=(lp // tm, N_SHARD),
                 in_specs=[pl.BlockSpec((tm, n), lambda i, s: (i, s)),
                           pl.BlockSpec((None, None, k, n), lambda i, s: (s, layer, 0, 0))],
                 out_specs=pl.BlockSpec((tm, k), lambda i, s: (i, 0)), vmem_mb=MM_VMEM_MB)(dy, wg)


def _mm_nt_row(name, dy, wg, layer, out_dtype):
    lp = dy.shape[0]
    k, n = wg.shape[2], wg.shape[3]
    tm = _mm_tiles(lp)

    def body(d_ref, w_ref, o_ref):
        o_ref[...] = lax.dot_general(d_ref[...], w_ref[...], _NT, preferred_element_type=F32).astype(o_ref.dtype)

    return _call(body, name=name, out_shape=_sds((lp, N_SHARD * k), out_dtype), grid=(N_SHARD, lp // tm),
                 in_specs=[pl.BlockSpec((tm, n), lambda s, i: (i, 0)),
                           pl.BlockSpec((None, None, k, n), lambda s, i: (s, layer, 0, 0))],
                 out_specs=pl.BlockSpec((tm, k), lambda s, i: (i, s)), vmem_mb=MM_VMEM_MB)(dy, wg)


def _mm_nt_row_swiglu(name, dy, wg, layer, gate, up):
    lp = dy.shape[0]
    k, n = wg.shape[2], wg.shape[3]
    tm = _pick(lp, (704, 384, 128))

    def body(d_ref, w_ref, g_ref, u_ref, dg_ref, du_ref):
        dact = lax.dot_general(d_ref[...], w_ref[...], _NT, preferred_element_type=F32)
        g = g_ref[...]
        sg = _sigmoid(g)
        dg_ref[...] = (dact * u_ref[...] * (sg * (1.0 + g * (1.0 - sg)))).astype(dg_ref.dtype)
        du_ref[...] = (dact * (g * sg)).astype(du_ref.dtype)

    blk = pl.BlockSpec((tm, k), lambda s, i: (i, s))
    return _call(body, name=name, out_shape=[_sds((lp, N_SHARD * k), BF16)] * 2, grid=(N_SHARD, lp // tm),
                 in_specs=[pl.BlockSpec((tm, n), lambda s, i: (i, 0)),
                           pl.BlockSpec((None, None, k, n), lambda s, i: (s, layer, 0, 0)), blk, blk],
                 out_specs=[blk, blk], vmem_mb=MM_VMEM_MB)(dy, wg, gate, up)


def _mm_tn_col(name, a, dy):
    lp, k = a.shape
    n = dy.shape[1] // N_SHARD
    tm = _mm_tiles(lp)
    tk = _pick(k, (1024, 512))

    def body(a_ref, d_ref, o_ref, acc):
        @pl.when(pl.program_id(2) == 0)
        def _():
            acc[...] = jnp.zeros(acc.shape, F32)

        acc[...] += lax.dot_general(a_ref[...], d_ref[...], _TN, preferred_element_type=F32)

        @pl.when(pl.program_id(2) == pl.num_programs(2) - 1)
        def _():
            o_ref[...] = acc[...].astype(o_ref.dtype)

    return _call(body, name=name, out_shape=_sds((N_SHARD, k, n), BF16), grid=(N_SHARD, k // tk, lp // tm),
                 in_specs=[pl.BlockSpec((tm, tk), lambda s, kk, i: (i, kk)),
                           pl.BlockSpec((tm, n), lambda s, kk, i: (i, s))],
                 out_specs=pl.BlockSpec((None, tk, n), lambda s, kk, i: (s, kk, 0)),
                 scratch=[pltpu.VMEM((tk, n), F32)], vmem_mb=MM_VMEM_MB)(a, dy)


def _mm_tn_row(name, a, dy):
    lp = a.shape[0]
    k = a.shape[1] // N_SHARD
    n = dy.shape[1]
    tm = _mm_tiles(lp)
    tn = _pick(n, (1024, 512))

    def body(a_ref, d_ref, o_ref, acc):
        @pl.when(pl.program_id(2) == 0)
        def _():
            acc[...] = jnp.zeros(acc.shape, F32)

        acc[...] += lax.dot_general(a_ref[...], d_ref[...], _TN, preferred_element_type=F32)

        @pl.when(pl.program_id(2) == pl.num_programs(2) - 1)
        def _():
            o_ref[...] = acc[...].astype(o_ref.dtype)

    return _call(body, name=name, out_shape=_sds((N_SHARD, k, n), BF16), grid=(N_SHARD, n // tn, lp // tm),
                 in_specs=[pl.BlockSpec((tm, k), lambda s, j, i: (i, s)),
                           pl.BlockSpec((tm, tn), lambda s, j, i: (i, j))],
                 out_specs=pl.BlockSpec((None, k, tn), lambda s, j, i: (s, 0, j)),
                 scratch=[pltpu.VMEM((k, tn), F32)], vmem_mb=MM_VMEM_MB)(a, dy)


SUB = 16
WIDE = 2 * BLOCK
Z_CLAMP = 20.0


def _log1m_sigmoid(z):
    return -jnp.where(z > Z_CLAMP, z, jnp.log(1.0 + jnp.exp(jnp.minimum(z, Z_CLAMP))))


def _tri(tk, kind):
    r = lax.broadcasted_iota(jnp.int32, (tk, tk), 0)
    c = lax.broadcasted_iota(jnp.int32, (tk, tk), 1)
    t = {"gt": r > c, "le": r <= c}[kind]
    return jnp.where(t, 1.0, 0.0).astype(BF16)


def _strip_mask(kind, s, tk):
    if kind == "none":
        return None
    col = lax.broadcasted_iota(jnp.int32, (SUB, tk), 1)
    row = lax.broadcasted_iota(jnp.int32, (SUB, tk), 0)
    causal = (col - row) < s * SUB
    if kind == "diag":
        return causal
    if kind == "pad":
        return col >= PAD
    return causal & (col >= PAD)


def _attn_blocks(lp):
    nb = lp // BLOCK
    assert nb % 2 == 1, "sequence must be a 128-row block plus whole 256-row blocks"
    return nb, (nb + 1) // 2


class _Exchange:
    def __init__(self, kind, arrs, split=False):
        self.kind, self.arrs, self.n, self.split = kind, list(arrs), len(arrs), split

    def out_shape(self):
        if self.kind == "gather":
            return [_sds((N_SHARD,) + a.shape, a.dtype) for a in self.arrs]
        return [_sds(a.shape, a.dtype) for a in self.arrs]

    def scratch(self):
        sems = [pltpu.SemaphoreType.DMA((3 * self.n,)), pltpu.SemaphoreType.DMA((3 * self.n,)),
                pltpu.SemaphoreType.DMA((self.n,))]
        if self.split:
            sems += [pltpu.SemaphoreType.DMA((self.n,)), pltpu.SemaphoreType.DMA((self.n,))]
        return sems

    def forward(self, wait, outs, fsend, frecv):
        x, y, c = _place()
        for a in range(self.n):
            cp = pltpu.make_async_remote_copy(src_ref=outs[a], dst_ref=outs[a], send_sem=fsend.at[a],
                                              recv_sem=frecv.at[a], device_id=(x, y, 1 - c), device_id_type=MESH)
            if wait:
                pl.when(c == a % 2)(cp.wait_send)
                pl.when(c != a % 2)(cp.wait_recv)
            else:
                pl.when(c == a % 2)(cp.start)

    def copies(self, a, ins, outs, send, recv, local):
        x, y, c = _place()
        me = 2 * x + y
        own = ins[a] if self.kind == "gather" else ins[a].at[me]
        out = [pltpu.make_async_copy(own, outs[a].at[me], local.at[a])]
        for r, (px, py) in enumerate(_xy_peers(x, y)):
            src = ins[a] if self.kind == "gather" else ins[a].at[2 * px + py]
            out.append(pltpu.make_async_remote_copy(
                src_ref=src, dst_ref=outs[a].at[me], send_sem=send.at[3 * a + r], recv_sem=recv.at[3 * a + r],
                device_id=(px, py, c), device_id_type=MESH))
        return out

    def run(self, wait, ins, outs, send, recv, local):
        for a in range(self.n):
            def go(a=a):
                for cp in self.copies(a, ins, outs, send, recv, local):
                    if wait:
                        cp.wait()
                    else:
                        cp.start()
            if self.split:
                pl.when(lax.axis_index("c") == a % 2)(go)
            else:
                go()


def _attn_fwd(proj, exchange=None):
    lp = proj.shape[0]
    nb, nq = _attn_blocks(lp)
    n_pair = D_ATTN // BLOCK
    n_x = exchange.n if exchange else 0

    def body(*refs):
        q_ref, k_ref, v_ref = refs[:3]
        x_in = refs[3:3 + n_x]
        o_ref, tot_ref = refs[3 + n_x:5 + n_x]
        x_out = refs[5 + n_x:5 + 2 * n_x]
        qs, kb, vh, tri_l, tri_s, z_s, hl_s, c_s, w_s, r_s, acc_s = refs[5 + 2 * n_x:16 + 2 * n_x]
        x_sem = refs[16 + 2 * n_x:]
        p, i = pl.program_id(0), pl.program_id(1)
        m0 = lax.broadcasted_iota(jnp.int32, (1, BLOCK), 1) < (BLOCK // 2)

        if exchange:
            @pl.when((p == 0) & (i == 0))
            def _():
                exchange.run(False, x_in, x_out, *x_sem[:3])

            if exchange.split:
                @pl.when((p == n_pair - 2) & (i == 0))
                def _():
                    exchange.run(True, x_in, x_out, *x_sem[:3])
                    exchange.forward(False, x_out, *x_sem[3:])

        @pl.when(i == 0)
        def _():
            tri_l[...] = _tri(WIDE, "gt")
            tri_s[...] = _tri(BLOCK, "gt")

            def prep(b, carry):
                rows = pl.ds(pl.multiple_of(b * BLOCK, BLOCK), BLOCK)
                q = q_ref[rows, :] * 0.125
                v = v_ref[rows, :]
                qs[0, rows, :] = jnp.where(m0, q, 0.0).astype(BF16)
                qs[1, rows, :] = jnp.where(m0, 0.0, q).astype(BF16)
                kb[rows, :] = k_ref[rows, :].astype(BF16)
                vh[0, rows, :] = jnp.where(m0, v, 0.0).astype(BF16)
                vh[1, rows, :] = jnp.where(m0, 0.0, v).astype(BF16)
                return carry

            lax.fori_loop(0, nb, prep, 0)

        def tiles(q0, tq, specs):
            heads = [(t, h) for t in range(len(specs)) for h in range(2)]
            strips = [slice(s * SUB, (s + 1) * SUB) for s in range(tq // SUB)]
            for t, h in heads:
                k0, tk, _ = specs[t]
                z_s[t, h, 0:tq, 0:tk] = lax.dot_general(qs[h, pl.ds(q0, tq), :], kb[pl.ds(k0, tk), :], _NT,
                                                        preferred_element_type=F32)
            for t, h in heads:
                _, tk, kind = specs[t]
                for s, rows in enumerate(strips):
                    z = z_s[t, h, rows, 0:tk]
                    lnb = _log1m_sigmoid(z)
                    m = _strip_mask(kind, s, tk)
                    if m is not None:
                        lnb = jnp.where(m, lnb, 0.0)
                    z_s[t, h, rows, 0:tk] = z + lnb
                    hl_s[t, h, rows, 0:tk] = lnb.astype(BF16)
            for t, h in heads:
                _, tk, _ = specs[t]
                tri = tri_l if tk == WIDE else tri_s
                c_s[t, h, 0:tq, 0:tk] = jnp.dot(hl_s[t, h, 0:tq, 0:tk], tri[...], preferred_element_type=F32)
            for t, h in heads:
                _, tk, kind = specs[t]
                for s, rows in enumerate(strips):
                    r = r_s[h, rows, :]
                    c = c_s[t, h, rows, 0:tk]
                    rr = r if tk == BLOCK else jnp.concatenate([r, r], axis=1)
                    w = jnp.exp(z_s[t, h, rows, 0:tk] + c + rr)
                    m = _strip_mask(kind, s, tk)
                    if m is not None:
                        w = jnp.where(m, w, 0.0)
                    w_s[t, h, rows, 0:tk] = w.astype(BF16)
                    total = c[:, 0:1] + hl_s[t, h, rows, 0:BLOCK].astype(F32)[:, 0:1]
                    r_s[h, rows, :] = r + jnp.broadcast_to(total, (SUB, BLOCK))
            upd = None
            for t, h in heads:
                k0, tk, _ = specs[t]
                d = jnp.dot(w_s[t, h, 0:tq, 0:tk], vh[h, pl.ds(k0, tk), :], preferred_element_type=F32)
                upd = d if upd is None else upd + d
            acc_s[0:tq, :] += upd

        def finish(q0, tq):
            o_ref[pl.ds(q0, tq), :] = acc_s[0:tq, :]
            tot_ref[pl.ds(q0, tq), :] = jnp.where(m0, r_s[0, 0:tq, :], r_s[1, 0:tq, :])

        r_s[...] = jnp.zeros(r_s.shape, F32)
        acc_s[...] = jnp.zeros(acc_s.shape, F32)

        @pl.when(i == 0)
        def _():
            tiles(0, BLOCK, [(0, BLOCK, "first")])
            finish(0, BLOCK)

        @pl.when(i > 0)
        def _():
            q0 = pl.multiple_of(i * WIDE - BLOCK, BLOCK)
            full = lambda j: (pl.multiple_of(j * WIDE - BLOCK, BLOCK), WIDE, "none")
            diag, meta = (q0, WIDE, "diag"), (0, BLOCK, "pad")

            @pl.when(i == 1)
            def _():
                tiles(q0, WIDE, [diag, meta])

            @pl.when(i >= 2)
            def _():
                tiles(q0, WIDE, [diag, full(i - 1)])

                def inner(n, carry):
                    j = i - 2 - 2 * n
                    tiles(q0, WIDE, [full(j), full(j - 1)])
                    return carry

                lax.fori_loop(0, (i - 2) // 2, inner, 0)

                @pl.when(i % 2 == 1)
                def _():
                    tiles(q0, WIDE, [full(1), meta])

                @pl.when(i % 2 == 0)
                def _():
                    tiles(q0, WIDE, [meta])

            finish(q0, WIDE)

        if exchange:
            @pl.when((p == n_pair - 1) & (i == nq - 1))
            def _():
                if exchange.split:
                    exchange.forward(True, x_out, *x_sem[3:])
                else:
                    exchange.run(True, x_in, x_out, *x_sem)

    cq, ck, cv = COL_Q // BLOCK, COL_K // BLOCK, COL_V // BLOCK
    col = lambda c0: (lambda p, i: (0, c0 + p))
    scratch = [pltpu.VMEM((2, lp, BLOCK), BF16), pltpu.VMEM((lp, BLOCK), BF16), pltpu.VMEM((2, lp, BLOCK), BF16),
               pltpu.VMEM((WIDE, WIDE), BF16), pltpu.VMEM((BLOCK, BLOCK), BF16),
               pltpu.VMEM((2, 2, WIDE, WIDE), F32), pltpu.VMEM((2, 2, WIDE, WIDE), BF16),
               pltpu.VMEM((2, 2, WIDE, WIDE), F32), pltpu.VMEM((2, 2, WIDE, WIDE), BF16),
               pltpu.VMEM((2, WIDE, BLOCK), F32), pltpu.VMEM((WIDE, BLOCK), F32)]
    res = _call(body, name="attn_fwd",
                out_shape=[_sds((lp, D_ATTN), F32), _sds((lp, D_ATTN), F32)] + (exchange.out_shape() if exchange else []),
                grid=(n_pair, nq),
                in_specs=[pl.BlockSpec((lp, BLOCK), col(cq)), pl.BlockSpec((lp, BLOCK), col(ck)),
                          pl.BlockSpec((lp, BLOCK), col(cv))] + [_ANY] * n_x,
                out_specs=[pl.BlockSpec((lp, BLOCK), col(0)), pl.BlockSpec((lp, BLOCK), col(0))] + [_ANY] * n_x,
                scratch=scratch + (exchange.scratch() if exchange else []),
                vmem_mb=56)(proj, proj, proj, *(exchange.arrs if exchange else []))
    return res[0], res[1], list(res[2:])


def _attn_bwd(proj, tot, d_out, exchange=None):
    lp = proj.shape[0]
    nb, nq = _attn_blocks(lp)
    n_pair = D_ATTN // BLOCK
    n_x = exchange.n if exchange else 0
    n_s = 23

    def body(*refs):
        q_ref, k_ref, v_ref, tot_ref, do_ref = refs[:5]
        x_in = refs[5:5 + n_x]
        dq_ref, dk_ref, dv_ref = refs[5 + n_x:8 + n_x]
        x_out = refs[8 + n_x:8 + 2 * n_x]
        (qs, kb, kh, vb, doh, tge_l, tge_s, tle_l, tle_s, z_s, g_s, hl_s, c_s, gl_s, gc_s, w_s, dz_s,
         tot_s, a_s, b_s, dq_acc, dk_acc, dv_acc) = refs[8 + 2 * n_x:8 + 2 * n_x + n_s]
        x_sem = refs[8 + 2 * n_x + n_s:]
        p, i = pl.program_id(0), pl.program_id(1)
        m0 = lax.broadcasted_iota(jnp.int32, (1, BLOCK), 1) < (BLOCK // 2)

        if exchange:
            @pl.when((p == 0) & (i == 0))
            def _():
                exchange.run(False, x_in, x_out, *x_sem)

        @pl.when(i == 0)
        def _():
            tge_l[...] = _tri(WIDE, "gt")
            tge_s[...] = _tri(BLOCK, "gt")
            tle_l[...] = _tri(WIDE, "le")
            tle_s[...] = _tri(BLOCK, "le")

            def prep(b, carry):
                rows = pl.ds(pl.multiple_of(b * BLOCK, BLOCK), BLOCK)
                q = q_ref[rows, :] * 0.125
                k = k_ref[rows, :]
                do = do_ref[rows, :]
                qs[0, rows, :] = jnp.where(m0, q, 0.0).astype(BF16)
                qs[1, rows, :] = jnp.where(m0, 0.0, q).astype(BF16)
                kb[rows, :] = k.astype(BF16)
                kh[0, rows, :] = jnp.where(m0, k, 0.0).astype(BF16)
                kh[1, rows, :] = jnp.where(m0, 0.0, k).astype(BF16)
                vb[rows, :] = v_ref[rows, :].astype(BF16)
                doh[0, rows, :] = jnp.where(m0, do, 0.0).astype(BF16)
                doh[1, rows, :] = jnp.where(m0, 0.0, do).astype(BF16)
                dk_acc[rows, :] = jnp.zeros((BLOCK, BLOCK), F32)
                dv_acc[rows, :] = jnp.zeros((BLOCK, BLOCK), F32)
                return carry

            lax.fori_loop(0, nb, prep, 0)

        def wide(x, tk):
            return x if tk == BLOCK else jnp.concatenate([x, x], axis=1)

        def tiles(q0, tq, specs):
            heads = [(t, h) for t in range(len(specs)) for h in range(2)]
            strips = [slice(s * SUB, (s + 1) * SUB) for s in range(tq // SUB)]
            for t, h in heads:
                k0, tk, _ = specs[t]
                z_s[t, h, 0:tq, 0:tk] = lax.dot_general(qs[h, pl.ds(q0, tq), :], kb[pl.ds(k0, tk), :], _NT,
                                                        preferred_element_type=F32)
                g_s[t, h, 0:tq, 0:tk] = lax.dot_general(doh[h, pl.ds(q0, tq), :], vb[pl.ds(k0, tk), :], _NT,
                                                        preferred_element_type=F32)
            for t, h in heads:
                _, tk, kind = specs[t]
                for s, rows in enumerate(strips):
                    z = z_s[t, h, rows, 0:tk]
                    lnb = _log1m_sigmoid(z)
                    m = _strip_mask(kind, s, tk)
                    if m is not None:
                        lnb = jnp.where(m, lnb, 0.0)
                    z_s[t, h, rows, 0:tk] = z + lnb
                    hl_s[t, h, rows, 0:tk] = lnb.astype(BF16)
            for t, h in heads:
                _, tk, _ = specs[t]
                tri = tge_l if tk == WIDE else tge_s
                c_s[t, h, 0:tq, 0:tk] = jnp.dot(hl_s[t, h, 0:tq, 0:tk], tri[...], preferred_element_type=F32)
            for t, h in heads:
                _, tk, kind = specs[t]
                for s, rows in enumerate(strips):
                    c = c_s[t, h, rows, 0:tk]
                    total = c[:, 0:1] + hl_s[t, h, rows, 0:BLOCK].astype(F32)[:, 0:1]
                    a_next = a_s[h, rows, :] + jnp.broadcast_to(total, (SUB, BLOCK))
                    a_s[h, rows, :] = a_next
                    w = jnp.exp(z_s[t, h, rows, 0:tk] + c + wide(tot_s[h, rows, :] - a_next, tk))
                    m = _strip_mask(kind, s, tk)
                    if m is not None:
                        w = jnp.where(m, w, 0.0)
                    g = w * g_s[t, h, rows, 0:tk]
                    g_s[t, h, rows, 0:tk] = g
                    gl_s[t, h, rows, 0:tk] = g.astype(BF16)
                    w_s[t, h, rows, 0:tk] = w.astype(BF16)
            for t, h in heads:
                _, tk, _ = specs[t]
                tri = tle_l if tk == WIDE else tle_s
                gc_s[t, h, 0:tq, 0:tk] = jnp.dot(gl_s[t, h, 0:tq, 0:tk], tri[...], preferred_element_type=F32)
            for t, h in heads:
                _, tk, kind = specs[t]
                for s, rows in enumerate(strips):
                    gc = gc_s[t, h, rows, 0:tk]
                    b = b_s[h, rows, :]
                    sig = jnp.exp(z_s[t, h, rows, 0:tk])
                    dz = g_s[t, h, rows, 0:tk] - sig * (gc + wide(b, tk))
                    m = _strip_mask(kind, s, tk)
                    if m is not None:
                        dz = jnp.where(m, dz, 0.0)
                    dz_s[t, h, rows, 0:tk] = dz.astype(BF16)
                    b_s[h, rows, :] = b + jnp.broadcast_to(gc[:, tk - 1:tk], (SUB, BLOCK))
            upd = None
            for t, h in heads:
                k0, tk, _ = specs[t]
                d = jnp.dot(dz_s[t, h, 0:tq, 0:tk], kh[h, pl.ds(k0, tk), :], preferred_element_type=F32)
                upd = d if upd is None else upd + d
            dq_acc[0:tq, :] += upd
            for t, (k0, tk, _) in enumerate(specs):
                dk_acc[pl.ds(k0, tk), :] += (
                    lax.dot_general(dz_s[t, 0, 0:tq, 0:tk], qs[0, pl.ds(q0, tq), :], _TN, preferred_element_type=F32) +
                    lax.dot_general(dz_s[t, 1, 0:tq, 0:tk], qs[1, pl.ds(q0, tq), :], _TN, preferred_element_type=F32))
                dv_acc[pl.ds(k0, tk), :] += (
                    lax.dot_general(w_s[t, 0, 0:tq, 0:tk], doh[0, pl.ds(q0, tq), :], _TN, preferred_element_type=F32) +
                    lax.dot_general(w_s[t, 1, 0:tq, 0:tk], doh[1, pl.ds(q0, tq), :], _TN, preferred_element_type=F32))

        def start(q0, tq):
            tv = tot_ref[pl.ds(q0, tq), :]
            tot_s[0, 0:tq, :] = jnp.broadcast_to(tv[:, 0:1], (tq, BLOCK))
            tot_s[1, 0:tq, :] = jnp.broadcast_to(tv[:, BLOCK - 1:BLOCK], (tq, BLOCK))
            a_s[...] = jnp.zeros(a_s.shape, F32)
            b_s[...] = jnp.zeros(b_s.shape, F32)
            dq_acc[...] = jnp.zeros(dq_acc.shape, F32)

        def finish(q0, tq):
            dq_ref[pl.ds(q0, tq), :] = (dq_acc[0:tq, :] * 0.125).astype(dq_ref.dtype)

        @pl.when(i == 0)
        def _():
            start(0, BLOCK)
            tiles(0, BLOCK, [(0, BLOCK, "first")])
            finish(0, BLOCK)

        @pl.when(i > 0)
        def _():
            q0 = pl.multiple_of(i * WIDE - BLOCK, BLOCK)
            full = lambda j: (pl.multiple_of(j * WIDE - BLOCK, BLOCK), WIDE, "none")
            diag, meta = (q0, WIDE, "diag"), (0, BLOCK, "pad")
            start(q0, WIDE)

            @pl.when(i == 1)
            def _():
                tiles(q0, WIDE, [meta, diag])

            @pl.when(i >= 2)
            def _():
                odd = i % 2

                @pl.when(odd == 1)
                def _():
                    tiles(q0, WIDE, [meta, full(1)])

                @pl.when(odd == 0)
                def _():
                    tiles(q0, WIDE, [meta])

                def inner(n, carry):
                    j = 1 + odd + 2 * n
                    tiles(q0, WIDE, [full(j), full(j + 1)])
                    return carry

                lax.fori_loop(0, (i - 2) // 2, inner, 0)
                tiles(q0, WIDE, [full(i - 1), diag])

            finish(q0, WIDE)

        @pl.when(i == nq - 1)
        def _():
            dk_ref[...] = dk_acc[...].astype(dk_ref.dtype)
            dv_ref[...] = dv_acc[...].astype(dv_ref.dtype)

        if exchange:
            @pl.when((p == n_pair - 1) & (i == nq - 1))
            def _():
                exchange.run(True, x_in, x_out, *x_sem)

    cq, ck, cv = COL_Q // BLOCK, COL_K // BLOCK, COL_V // BLOCK
    col = lambda c0: (lambda p, i: (0, c0 + p))
    whole = lambda c0: pl.BlockSpec((lp, BLOCK), col(c0))
    tile4 = lambda w, dt: pltpu.VMEM((2, 2, WIDE, w), dt)
    scratch = [pltpu.VMEM((2, lp, BLOCK), BF16), pltpu.VMEM((lp, BLOCK), BF16), pltpu.VMEM((2, lp, BLOCK), BF16),
               pltpu.VMEM((lp, BLOCK), BF16), pltpu.VMEM((2, lp, BLOCK), BF16),
               pltpu.VMEM((WIDE, WIDE), BF16), pltpu.VMEM((BLOCK, BLOCK), BF16),
               pltpu.VMEM((WIDE, WIDE), BF16), pltpu.VMEM((BLOCK, BLOCK), BF16),
               tile4(WIDE, F32), tile4(WIDE, F32), tile4(WIDE, BF16), tile4(WIDE, F32),
               tile4(WIDE, BF16), tile4(WIDE, F32), tile4(WIDE, BF16), tile4(WIDE, BF16),
               pltpu.VMEM((2, WIDE, BLOCK), F32), pltpu.VMEM((2, WIDE, BLOCK), F32), pltpu.VMEM((2, WIDE, BLOCK), F32),
               pltpu.VMEM((WIDE, BLOCK), F32), pltpu.VMEM((lp, BLOCK), F32), pltpu.VMEM((lp, BLOCK), F32)]
    res = _call(body, name="attn_bwd",
                out_shape=[_sds((lp, D_ATTN), BF16)] * 3 + (exchange.out_shape() if exchange else []),
                grid=(n_pair, nq),
                in_specs=[whole(cq), whole(ck), whole(cv), whole(0), whole(0)] + [_ANY] * n_x,
                out_specs=[whole(0), whole(0), whole(0)] + [_ANY] * n_x,
                scratch=scratch + (exchange.scratch() if exchange else []),
                vmem_mb=60)(proj, proj, proj, tot, d_out, *(exchange.arrs if exchange else []))
    return res[0], res[1], res[2], list(res[3:])


def _shift_down(x, d):
    return x if d == 0 else pltpu.roll(x, d, 0)


def _shift_up(x, d):
    return x if d == 0 else pltpu.roll(x, x.shape[0] - d, 0)


def _pool_windows(ext, down):
    shift = _shift_down if down else _shift_up
    outs = []
    for g, w in enumerate(POOL_WINDOWS):
        s = ext[:, g * BLOCK:(g + 1) * BLOCK]
        d = 1
        while d < w:
            s = s + shift(s, d)
            d *= 2
        outs.append(s)
    return jnp.concatenate(outs, axis=1)


def _pool_counts(pos):
    cols = [jnp.broadcast_to(jnp.clip(pos + 1, 1, w).astype(F32), (pos.shape[0], BLOCK)) for w in POOL_WINDOWS]
    return jnp.concatenate(cols, axis=1)


def _group_dot(x, w_ref, transpose):
    outs = []
    for g in range(len(POOL_WINDOWS)):
        xg = x[:, g * BLOCK:(g + 1) * BLOCK].astype(BF16)
        wg = w_ref[g * BLOCK:(g + 1) * BLOCK, :]
        if transpose:
            outs.append(lax.dot_general(xg, wg, _NT, preferred_element_type=F32))
        else:
            outs.append(jnp.dot(xg, wg, preferred_element_type=F32))
    return jnp.concatenate(outs, axis=1)


def _pooled(prev, cur, r):
    rows = cur.shape[0]
    ext = jnp.concatenate([prev[rows - HALO:], cur], axis=0)
    pos = r * rows + lax.broadcasted_iota(jnp.int32, (rows, 1), 0) - PAD
    ws = _pool_windows(ext, down=True)[HALO:]
    return jnp.where(pos >= 0, ws / _pool_counts(pos) - cur, 0.0)


def _pool_fwd(proj, w_pool_bf, scale):
    lp = proj.shape[0]
    rows = BLOCK
    cb = COL_POOL // D_POOL

    def body(prev_ref, cur_ref, w_ref, s_ref, o_ref):
        pooled = _pooled(prev_ref[...], cur_ref[...], pl.program_id(0))
        o_ref[...] = _group_dot(pooled, w_ref, False) * s_ref[...]

    return _call(body, name="pool_fwd", out_shape=_sds((lp, D_POOL), F32), grid=(lp // rows,),
                 in_specs=[pl.BlockSpec((rows, D_POOL), lambda r: (jnp.maximum(r - 1, 0), cb)),
                           pl.BlockSpec((rows, D_POOL), lambda r: (r, cb)),
                           pl.BlockSpec((D_POOL, BLOCK), lambda r: (0, 0)),
                           pl.BlockSpec((1, D_POOL), lambda r: (0, 0))],
                 out_specs=pl.BlockSpec((rows, D_POOL), lambda r: (r, 0)))(proj, proj, w_pool_bf, scale)


def _pool_bwd(proj, d_out, w_pool_bf, scale):
    lp = proj.shape[0]
    rows = BLOCK
    n_chunk = lp // rows
    cb = COL_POOL // D_POOL

    def body(prev_ref, cur_ref, do_ref, don_ref, w_ref, s_ref, du_ref, dw_ref, ds_ref):
        r = pl.program_id(0)

        @pl.when(r == 0)
        def _():
            dw_ref[...] = jnp.zeros(dw_ref.shape, F32)
            ds_ref[...] = jnp.zeros(ds_ref.shape, F32)

        pooled = _pooled(prev_ref[...], cur_ref[...], r)
        d_ext = jnp.concatenate([do_ref[...], don_ref[0:HALO]], axis=0)
        pos = r * rows + lax.broadcasted_iota(jnp.int32, (rows + HALO, 1), 0) - PAD
        dmixed = jnp.where((pos >= 0) & (pos < lp - PAD), d_ext * s_ref[...], 0.0)
        dpooled = _group_dot(dmixed, w_ref, True)
        back = _pool_windows(dpooled / _pool_counts(pos), down=False)[0:rows]
        du = jnp.where(pos[0:rows] >= 0, back - dpooled[0:rows], 0.0)
        du_ref[...] = du.astype(du_ref.dtype)
        mixed = _group_dot(pooled, w_ref, False)
        ds_ref[...] += _colsum(do_ref[...] * mixed)
        pooled_bf = pooled.astype(BF16)
        dm_bf = dmixed[0:rows].astype(BF16)
        for g in range(len(POOL_WINDOWS)):
            sl = slice(g * BLOCK, (g + 1) * BLOCK)
            dw_ref[sl, :] += lax.dot_general(pooled_bf[:, sl], dm_bf[:, sl], _TN, preferred_element_type=F32)

    return _call(body, name="pool_bwd",
                 out_shape=[_sds((lp, D_POOL), BF16), _sds((D_POOL, BLOCK), F32), _sds((1, D_POOL), F32)],
                 grid=(n_chunk,),
                 in_specs=[pl.BlockSpec((rows, D_POOL), lambda r: (jnp.maximum(r - 1, 0), cb)),
                           pl.BlockSpec((rows, D_POOL), lambda r: (r, cb)),
                           pl.BlockSpec((rows, D_POOL), lambda r: (r, 0)),
                           pl.BlockSpec((rows, D_POOL), lambda r: (jnp.minimum(r + 1, n_chunk - 1), 0)),
                           pl.BlockSpec((D_POOL, BLOCK), lambda r: (0, 0)),
                           pl.BlockSpec((1, D_POOL), lambda r: (0, 0))],
                 out_specs=[pl.BlockSpec((rows, D_POOL), lambda r: (r, 0)),
                            pl.BlockSpec((D_POOL, BLOCK), lambda r: (0, 0)),
                            pl.BlockSpec((1, D_POOL), lambda r: (0, 0))])(proj, proj, d_out, d_out, w_pool_bf, scale)


SUBLANES = 8


def _sub_shifts(x, down):
    shift = _shift_down if down else _shift_up
    return [shift(x, b) for b in range(SUBLANES)]


def _lagged(shifts, d, lo, n, down):
    a, b = divmod(d, SUBLANES)
    start = lo - SUBLANES * a if down else lo + SUBLANES * a
    return shifts[b][start:start + n]


def _conv_taps(u_shifts, wdw_ref, lo, n):
    y = None
    for d in range(CONV_WIDTH):
        term = wdw_ref[CONV_WIDTH - 1 - d:CONV_WIDTH - d, :] * _lagged(u_shifts, d, lo, n, True)
        y = term if y is None else y + term
    return y


def _layernorm_stats(y):
    mu = jnp.mean(y, axis=-1, keepdims=True)
    yc = y - mu
    rstd = lax.rsqrt(jnp.mean(yc * yc, axis=-1, keepdims=True) + EPS)
    return yc * rstd, rstd


def _conv_fwd(proj, wdw, bdw, ln_g, ln_b, wpw_bf):
    lp = proj.shape[0]
    rows = BLOCK
    ca, cg = COL_A // D_CONV, COL_G // D_CONV

    def body(ap_ref, a_ref, gp_ref, g_ref, wdw_ref, b_ref, lg_ref, lb_ref, wpw_ref, o_ref, y_ref):
        r = pl.program_id(0)
        a = jnp.concatenate([ap_ref[rows - HALO:rows], a_ref[...]], axis=0)
        g = jnp.concatenate([gp_ref[rows - HALO:rows], g_ref[...]], axis=0)
        u = a * _sigmoid(g)
        y = _conv_taps(_sub_shifts(u, True), wdw_ref, HALO, rows) + b_ref[...]
        y_ref[...] = y
        xhat, _ = _layernorm_stats(y)
        yn = xhat * lg_ref[...] + lb_ref[...]
        pos = r * rows + lax.broadcasted_iota(jnp.int32, (rows, 1), 0) - PAD
        s = jnp.where(pos >= 0, yn * _sigmoid(yn), 0.0)
        o_ref[...] = jnp.dot(s.astype(BF16), wpw_ref[...], preferred_element_type=F32)

    prev = lambda c: (lambda r: (jnp.maximum(r - 1, 0), c))
    cur = lambda c: (lambda r: (r, c))
    const = lambda shape: pl.BlockSpec(shape, lambda r: (0, 0))
    return _call(body, name="conv_fwd", out_shape=[_sds((lp, D_CONV), F32)] * 2, grid=(lp // rows,),
                 in_specs=[pl.BlockSpec((rows, D_CONV), prev(ca)), pl.BlockSpec((rows, D_CONV), cur(ca)),
                           pl.BlockSpec((rows, D_CONV), prev(cg)), pl.BlockSpec((rows, D_CONV), cur(cg)),
                           const((HALO, D_CONV)), const((1, D_CONV)), const((1, D_CONV)), const((1, D_CONV)),
                           const((D_CONV, D_CONV))],
                 out_specs=[pl.BlockSpec((rows, D_CONV), cur(0))] * 2)(
                     proj, proj, proj, proj, wdw, bdw, ln_g, ln_b, wpw_bf)


def _conv_bwd(proj, d_out, y_conv, wdw, ln_g, ln_b, wpw_bf):
    lp = proj.shape[0]
    rows = BLOCK
    n_chunk = lp // rows
    ca, cg = COL_A // D_CONV, COL_G // D_CONV
    ext = rows + HALO

    def body(ap_ref, a_ref, an_ref, gp_ref, g_ref, gn_ref, do_ref, don_ref, y_ref, yn_ref, wdw_ref, lg_ref, lb_ref,
             wpw_ref, da_ref, dg_ref, dwdw_ref, db_ref, dlg_ref, dlb_ref, dwpw_ref):
        r = pl.program_id(0)

        @pl.when(r == 0)
        def _():
            for ref in (dwdw_ref, db_ref, dlg_ref, dlb_ref, dwpw_ref):
                ref[...] = jnp.zeros(ref.shape, F32)

        a3 = jnp.concatenate([ap_ref[rows - HALO:rows], a_ref[...], an_ref[0:HALO]], axis=0)
        g3 = jnp.concatenate([gp_ref[rows - HALO:rows], g_ref[...], gn_ref[0:HALO]], axis=0)
        sig3 = _sigmoid(g3)
        u3 = a3 * sig3
        u_shifts = _sub_shifts(u3, True)
        y = jnp.concatenate([y_ref[...], yn_ref[0:HALO]], axis=0)
        xhat, rstd = _layernorm_stats(y)
        yn = xhat * lg_ref[...] + lb_ref[...]
        sgm = _sigmoid(yn)
        pos = r * rows + lax.broadcasted_iota(jnp.int32, (ext, 1), 0) - PAD
        valid = (pos >= 0) & (pos < lp - PAD)
        d_ext = jnp.concatenate([do_ref[...], don_ref[0:HALO]], axis=0)
        ds = lax.dot_general(d_ext.astype(BF16), wpw_ref[...], _NT, preferred_element_type=F32)
        dyn = jnp.where(valid, ds * (sgm * (1.0 + yn * (1.0 - sgm))), 0.0)
        dxh = dyn * lg_ref[...]
        dy = rstd * (dxh - jnp.mean(dxh, axis=-1, keepdims=True)
                     - xhat * jnp.mean(dxh * xhat, axis=-1, keepdims=True))
        s_cur = jnp.where(valid[0:rows], (yn * sgm)[0:rows], 0.0)
        dwpw_ref[...] += lax.dot_general(s_cur.astype(BF16), do_ref[...].astype(BF16), _TN,
                                         preferred_element_type=F32)
        dlg_ref[...] += _colsum(dyn[0:rows] * xhat[0:rows])
        dlb_ref[...] += _colsum(dyn[0:rows])
        dy_cur = dy[0:rows]
        db_ref[...] += _colsum(dy_cur)
        dy_shifts = _sub_shifts(dy, False)
        du = None
        for d in range(CONV_WIDTH):
            k = CONV_WIDTH - 1 - d
            dwdw_ref[k:k + 1, :] += _colsum(dy_cur * _lagged(u_shifts, d, HALO, rows, True))
            term = wdw_ref[k:k + 1, :] * _lagged(dy_shifts, d, 0, rows, False)
            du = term if du is None else du + term
        du = jnp.where(pos[0:rows] >= 0, du, 0.0)
        sig = sig3[HALO:HALO + rows]
        da_ref[...] = (du * sig).astype(da_ref.dtype)
        dg_ref[...] = (du * a_ref[...] * sig * (1.0 - sig)).astype(dg_ref.dtype)

    prev = lambda c: (lambda r: (jnp.maximum(r - 1, 0), c))
    cur = lambda c: (lambda r: (r, c))
    nxt = lambda c: (lambda r: (jnp.minimum(r + 1, n_chunk - 1), c))
    const = lambda shape: pl.BlockSpec(shape, lambda r: (0, 0))
    blk = lambda f: pl.BlockSpec((rows, D_CONV), f)
    return _call(body, name="conv_bwd",
                 out_shape=[_sds((lp, D_CONV), BF16), _sds((lp, D_CONV), BF16), _sds((HALO, D_CONV), F32),
                            _sds((1, D_CONV), F32), _sds((1, D_CONV), F32), _sds((1, D_CONV), F32),
                            _sds((D_CONV, D_CONV), F32)],
                 grid=(n_chunk,),
                 in_specs=[blk(prev(ca)), blk(cur(ca)), blk(nxt(ca)), blk(prev(cg)), blk(cur(cg)), blk(nxt(cg)),
                           blk(cur(0)), blk(nxt(0)), blk(cur(0)), blk(nxt(0)),
                           const((HALO, D_CONV)), const((1, D_CONV)), const((1, D_CONV)),
                           const((D_CONV, D_CONV))],
                 out_specs=[blk(cur(0)), blk(cur(0)), const((HALO, D_CONV)), const((1, D_CONV)),
                            const((1, D_CONV)), const((1, D_CONV)), const((D_CONV, D_CONV))],
                 vmem_mb=48)(proj, proj, proj, proj, proj, proj, d_out, d_out, y_conv, y_conv, wdw, ln_g, ln_b, wpw_bf)


_ANY = pl.BlockSpec(memory_space=pl.ANY)


def _place():
    return lax.axis_index("x"), lax.axis_index("y"), lax.axis_index("c")


def _xy_peers(x, y):
    return [(1 - x, y), (x, 1 - y), (1 - x, 1 - y)]


def _exchange_now(name, ex):
    n = ex.n

    def body(*refs):
        ins, outs, sems = refs[:n], refs[n:2 * n], refs[2 * n:]
        ex.run(False, ins, outs, *sems[:3])
        ex.run(True, ins, outs, *sems[:3])
        if ex.split:
            ex.forward(False, outs, *sems[3:])
            ex.forward(True, outs, *sems[3:])

    return _call(body, name=name, out_shape=ex.out_shape(), in_specs=[_ANY] * n, out_specs=[_ANY] * n,
                 scratch=ex.scratch())(*ex.arrs)


def _tail_exchange(name, parts, arrs, small):
    ex = _Exchange("scatter", parts)
    n, m = ex.n, len(arrs)
    flips = [(fx, fy, fc) for fx in (0, 1) for fy in (0, 1) for fc in (0, 1)][1:]

    def body(*refs):
        p_in, a_in, g_in = refs[:n], refs[n:n + m], refs[n + m]
        p_out, a_out, g_out = refs[n + m + 1:2 * n + m + 1], refs[2 * n + m + 1:2 * (n + m) + 1], refs[2 * (n + m) + 1]
        send, recv, local, s_send, s_recv, g_send, g_recv, g_local = refs[2 * (n + m) + 2:]
        x, y, c = _place()
        me = 4 * x + 2 * y + c
        ex.run(False, p_in, p_out, send, recv, local)
        others = [pltpu.make_async_remote_copy(src_ref=a_in[a], dst_ref=a_out[a], send_sem=s_send.at[a],
                                               recv_sem=s_recv.at[a], device_id=(x, y, 1 - c), device_id_type=MESH)
                  for a in range(m)]
        for k, (fx, fy, fc) in enumerate(flips):
            peer = (1 - x if fx else x, 1 - y if fy else y, 1 - c if fc else c)
            others.append(pltpu.make_async_remote_copy(src_ref=g_in, dst_ref=g_out.at[me], send_sem=g_send.at[k],
                                                       recv_sem=g_recv.at[k], device_id=peer, device_id_type=MESH))
        others.append(pltpu.make_async_copy(g_in, g_out.at[me], g_local))
        for cp in others:
            cp.start()
        for cp in others:
            cp.wait()
        ex.run(True, p_in, p_out, send, recv, local)

    dma = pltpu.SemaphoreType.DMA
    res = _call(body, name=name,
                out_shape=ex.out_shape() + [_sds(a.shape, a.dtype) for a in arrs] +
                [_sds((N_DEV,) + small.shape, small.dtype)],
                in_specs=[_ANY] * (n + m + 1), out_specs=[_ANY] * (n + m + 1),
                scratch=ex.scratch() + [dma((m,)), dma((m,)), dma((N_DEV - 1,)), dma((N_DEV - 1,)), dma])(
                    *parts, *arrs, small)
    return list(res[:n]), list(res[n:n + m]), res[n + m]


def _swap_core(name, arrs):
    n = len(arrs)

    def body(*refs):
        ins, outs = refs[:n], refs[n:2 * n]
        send, recv = refs[2 * n:]
        x, y, c = _place()
        remote = []
        for a in range(n):
            cp = pltpu.make_async_remote_copy(src_ref=ins[a], dst_ref=outs[a], send_sem=send.at[a],
                                              recv_sem=recv.at[a], device_id=(x, y, 1 - c), device_id_type=MESH)
            cp.start()
            remote.append(cp)
        for cp in remote:
            cp.wait()

    return _call(body, name=name, out_shape=[_sds(a.shape, a.dtype) for a in arrs],
                 in_specs=[_ANY] * n, out_specs=[_ANY] * n,
                 scratch=[pltpu.SemaphoreType.DMA((n,)), pltpu.SemaphoreType.DMA((n,))])(*arrs)


def _sum_slots(name, stacked, out_dtype=F32):
    s, r, c = stacked.shape
    tr = _pick(r, (256, 128, 64, 8))

    def body(in_ref, o_ref):
        acc = in_ref[0].astype(F32)
        for k in range(1, s):
            acc = acc + in_ref[k].astype(F32)
        o_ref[...] = acc.astype(o_ref.dtype)

    return _call(body, name=name, out_shape=_sds((r, c), out_dtype), grid=(r // tr,),
                 in_specs=[pl.BlockSpec((s, tr, c), lambda i: (0, i, 0))],
                 out_specs=pl.BlockSpec((tr, c), lambda i: (i, 0)))(stacked)


def _sum_slots_layers(name, r0, r1):
    s, r, c = r0.shape
    tr = _pick(r, (256, 128))

    def body(a_ref, b_ref, o_ref):
        def total(ref):
            acc = ref[0].astype(F32)
            for k in range(1, s):
                acc = acc + ref[k].astype(F32)
            return acc

        @pl.when(pl.program_id(0) == 0)
        def _():
            o_ref[...] = total(a_ref)

        @pl.when(pl.program_id(0) == 1)
        def _():
            o_ref[...] = total(b_ref)

    return _call(body, name=name, out_shape=_sds((2, r, c), F32), grid=(2, r // tr),
                 in_specs=[pl.BlockSpec((s, tr, c), lambda l, i: (0, i * (1 - l), 0)),
                           pl.BlockSpec((s, tr, c), lambda l, i: (0, i * l, 0))],
                 out_specs=pl.BlockSpec((None, tr, c), lambda l, i: (l, i, 0)))(r0, r1)


def _adamw_math(w, g, m, v):
    m = ADAM_B1 * m + (1.0 - ADAM_B1) * g
    v = ADAM_B2 * v + (1.0 - ADAM_B2) * (g * g)
    m_hat = m / (1.0 - ADAM_B1 ** ADAM_STEP)
    v_hat = v / (1.0 - ADAM_B2 ** ADAM_STEP)
    delta = -ADAM_LR * (m_hat / (jnp.sqrt(v_hat) + ADAM_EPS) + ADAM_WD * w)
    return delta, m, v


def _adamw(name, w, m, v, g_mine, g_other):
    l, r, c = w.shape
    tr = _pick(r, (128, 64, 8))

    def body(w_ref, m_ref, v_ref, ga_ref, gb_ref, g_ref, d_ref, nm_ref, nv_ref):
        g = ga_ref[...] + gb_ref[...]
        delta, nm, nv = _adamw_math(w_ref[...], g, m_ref[...], v_ref[...])
        g_ref[...] = g
        d_ref[...] = delta
        nm_ref[...] = nm
        nv_ref[...] = nv

    spec = pl.BlockSpec((None, tr, c), lambda li, i: (li, i, 0))
    return _call(body, name=name, out_shape=[_sds(w.shape, F32)] * 4, grid=(l, r // tr),
                 in_specs=[spec] * 5, out_specs=[spec] * 4, vmem_mb=48)(w, m, v, g_mine, g_other)


def _adamw_flat(name, w, m, v, g):
    r, c = w.shape
    tr = _pick(r, (256, 128, 64, 8))

    def body(w_ref, m_ref, v_ref, g_ref, d_ref, nm_ref, nv_ref):
        delta, nm, nv = _adamw_math(w_ref[...], g_ref[...], m_ref[...], v_ref[...])
        d_ref[...] = delta
        nm_ref[...] = nm
        nv_ref[...] = nv

    spec = pl.BlockSpec((tr, c), lambda i: (i, 0))
    return _call(body, name=name, out_shape=[_sds(w.shape, F32)] * 3, grid=(r // tr,),
                 in_specs=[spec] * 4, out_specs=[spec] * 3)(w, m, v, g)


def _pack(arrs, row_multiple=256):
    flat = jnp.concatenate([a.reshape(-1).astype(F32) for a in arrs])
    per = BLOCK * row_multiple
    total = -(-flat.shape[0] // per) * per
    return jnp.pad(flat, (0, total - flat.shape[0])).reshape(total // BLOCK, BLOCK)


def _unpack(buf, shapes):
    flat = buf.reshape(-1)
    outs, off = [], 0
    for s in shapes:
        size = 1
        for d in s:
            size *= d
        outs.append(flat[off:off + size].reshape(s))
        off += size
    return outs


def kernel(x, meta_tokens, pre_mix_g, w_in, w_pool, pool_scale, w_dw, b_dw, conv_ln_g, conv_ln_b, w_pw, mix_out_g, w_out, post_mix_g, pre_ffn_g, w_gate, w_up, w_down, post_ffn_g, loss_target, m_meta_tokens, m_pre_mix_g, m_w_in, m_w_pool, m_pool_scale, m_w_dw, m_b_dw, m_conv_ln_g, m_conv_ln_b, m_w_pw, m_mix_out_g, m_w_out, m_post_mix_g, m_pre_ffn_g, m_w_gate, m_w_up, m_w_down, m_post_ffn_g, v_meta_tokens, v_pre_mix_g, v_w_in, v_w_pool, v_pool_scale, v_w_dw, v_b_dw, v_conv_ln_g, v_conv_ln_b, v_w_pw, v_mix_out_g, v_w_out, v_post_mix_g, v_pre_ffn_g, v_w_gate, v_w_up, v_w_down, v_post_ffn_g):
    seq = x.shape[1]
    lp = PAD + N_META + seq
    depth = w_in.shape[0]
    xy = 2 * lax.axis_index("x") + lax.axis_index("y")

    small_shapes = [meta_tokens.shape, w_dw.shape, w_pw.shape]
    small_local = _pack([meta_tokens, w_dw, w_pw], row_multiple=8)
    big_names = ["w_in", "w_out", "w_gate", "w_up", "w_down"]
    big_local = {(k, l): w[l:l + 1].astype(BF16)
                 for k, w in zip(big_names, (w_in, w_out, w_gate, w_up, w_down)) for l in range(depth)}
    wg = {}
    half = w_in.shape[2] // 2
    first = big_local[("w_in", 0)]
    lo_half, hi_half, small_all = _exchange_now(
        "gather_first", _Exchange("gather", [first[:, :, :half], first[:, :, half:], small_local], split=True))
    wg[("w_in", 0)] = jnp.concatenate([lo_half, hi_half], axis=3)
    metas, wdws, wpws = [], [], []
    for s in range(N_SHARD):
        mt, wd, wp = _unpack(small_all[s], small_shapes)
        metas.append(mt)
        wdws.append(wd)
        wpws.append(wp)
    meta_full = jnp.concatenate(metas, axis=1)
    wdw_full = jnp.concatenate(wdws, axis=2)
    wpw_full = jnp.concatenate(wpws, axis=1)
    wdw_pad = jnp.pad(wdw_full, ((0, 0), (0, HALO - CONV_WIDTH), (0, 0)))
    wpw_bf = wpw_full.astype(BF16)
    wpool_bf = w_pool.reshape(depth, D_POOL, BLOCK).astype(BF16)

    row = lambda a, i: a[i][None, :]

    h = jnp.concatenate([jnp.zeros((PAD, D_MODEL), F32), meta_full, x[0]], axis=0)
    light = dict(tm=_pick(lp, (3 * BLOCK, BLOCK)), vmem_mb=48)
    u = _rowwise("pre_mix_norm0", lambda hh, g: _rms(hh, g), [h], [row(pre_mix_g, 0)], [(D_MODEL, BF16)],
                 **light)[0]
    saved = []
    for i in range(depth):
        proj = _mm_nn_col("in_proj%d" % i, u, wg[("w_in", i)], 0, F32)
        ride = [(k, i) for k in big_names[1:]] + ([("w_in", i + 1)] if i + 1 < depth else [])
        o_attn, tot, got = _attn_fwd(proj, _Exchange("gather", [big_local[k] for k in ride], split=True))
        wg.update(zip(ride, got))
        o_pool = _pool_fwd(proj, wpool_bf[i], row(pool_scale, i))
        o_conv, y_conv = _conv_fwd(proj, wdw_pad[i], row(b_dw, i), row(conv_ln_g, i), row(conv_ln_b, i), wpw_bf[i])

        mix_gains = [row(mix_out_g, i)[:, :D_ATTN], row(mix_out_g, i)[:, D_ATTN:D_ATTN + D_POOL],
                     row(mix_out_g, i)[:, D_ATTN + D_POOL:]]
        merged = _rowwise("merge%d" % i, _merge, [o_attn, o_pool, o_conv], mix_gains, [(D_MODEL, BF16)], **light)[0]
        mix = _mm_nn_row("out_proj%d" % i, merged, wg[("w_out", i)], 0)

        def post_mix(hh, mx, g1, g2):
            h1 = hh + _rms(mx, g1)
            return h1, _rms(h1, g2)

        h1, u2 = _rowwise("post_mix%d" % i, post_mix, [h, mix], [row(post_mix_g, i), row(pre_ffn_g, i)],
                          [(D_MODEL, F32), (D_MODEL, BF16)], **light)
        gate = _mm_nn_col("ffn_gate%d" % i, u2, wg[("w_gate", i)], 0, F32)
        up, act = _mm_nn_col_swiglu("ffn_up%d" % i, u2, wg[("w_up", i)], 0, gate)
        ff = _mm_nn_row("ffn_down%d" % i, act, wg[("w_down", i)], 0)
        rec = dict(h=h, u=u, proj=proj, tot=tot, o_attn=o_attn, o_pool=o_pool, o_conv=o_conv, y_conv=y_conv,
                   merged=merged,
                   mix=mix, h1=h1, u2=u2, gate=gate, up=up, act=act, ff=ff)
        saved.append(rec)
        if i + 1 < depth:
            def post_ffn(hh, f, g1, g2):
                h2 = hh + _rms(f, g1)
                return h2, _rms(h2, g2)

            h, u = _rowwise("post_ffn%d" % i, post_ffn, [h1, ff], [row(post_ffn_g, i), row(pre_mix_g, i + 1)],
                            [(D_MODEL, F32), (D_MODEL, BF16)], **light)
        else:
            def head(row0, hh, f, tgt, g1):
                y = hh + _rms(f, g1)
                rid = row0 + lax.broadcasted_iota(jnp.int32, (y.shape[0], 1), 0)
                err = jnp.where(rid >= PAD + N_META, y - tgt, 0.0)
                part = 0.5 * jnp.sum(jnp.mean(err * err, axis=-1, keepdims=True), axis=0, keepdims=True)
                return err * (1.0 / D_MODEL), jnp.broadcast_to(part, (8, BLOCK))

            dh, loss_part = _rowwise("loss_head", head, [h1, ff, loss_target[0]], [row(post_ffn_g, i)],
                                     [(D_MODEL, F32)], accs=[(8, BLOCK)], with_row0=True, headless={2})

    loss = lax.psum(loss_part[0, 0], ("x", "y", "c"))

    small_grads = {}
    big_parts = {}
    received = {}
    for i in reversed(range(depth)):
        rec = saved[i]

        def post_ffn_b(f, d, g):
            _, vjp = jax.vjp(_rms, f, g)
            df, dg = vjp(d)
            return df, dg

        dff, g_post_ffn = _rowwise("post_ffn_b%d" % i, post_ffn_b, [rec["ff"], dh], [row(post_ffn_g, i)],
                                   [(D_MODEL, BF16)], accs=[(1, D_MODEL)], **light)
        big_parts[("w_down", i)] = _mm_tn_row("dw_down%d" % i, rec["act"], dff)
        dgate, dup = _mm_nt_row_swiglu("d_gate_up%d" % i, dff, wg[("w_down", i)], 0, rec["gate"], rec["up"])
        big_parts[("w_gate", i)] = _mm_tn_col("dw_gate%d" % i, rec["u2"], dgate)
        big_parts[("w_up", i)] = _mm_tn_col("dw_up%d" % i, rec["u2"], dup)
        du2a = _mm_nt_col("d_u2_gate%d" % i, dgate, wg[("w_gate", i)], 0)
        du2b = _mm_nt_col("d_u2_up%d" % i, dup, wg[("w_up", i)], 0)

        def post_mix_b(h1v, mx, d, da, db, g1, g2):
            _, vjp2 = jax.vjp(_rms, h1v, g2)
            dh1, dg2 = vjp2(da + db)
            dmid = d + dh1
            _, vjp1 = jax.vjp(_rms, mx, g1)
            dmx, dg1 = vjp1(dmid)
            return dmid, dmx, dg1, dg2

        dmid, dmix, g_post_mix, g_pre_ffn = _rowwise(
            "post_mix_b%d" % i, post_mix_b, [rec["h1"], rec["mix"], dh, du2a, du2b],
            [row(post_mix_g, i), row(pre_ffn_g, i)], [(D_MODEL, F32), (D_MODEL, BF16)],
            accs=[(1, D_MODEL), (1, D_MODEL)], vmem_mb=48)
        big_parts[("w_out", i)] = _mm_tn_row("dw_out%d" % i, rec["merged"], dmix)
        dmerged = _mm_nt_row("d_merged%d" % i, dmix, wg[("w_out", i)], 0, F32)

        def merge_b(oa, op, oc, d, ga, gp, gc):
            _, vjp = jax.vjp(_merge, oa, op, oc, ga, gp, gc)
            return vjp(d)

        mix_gains = [row(mix_out_g, i)[:, :D_ATTN], row(mix_out_g, i)[:, D_ATTN:D_ATTN + D_POOL],
                     row(mix_out_g, i)[:, D_ATTN + D_POOL:]]
        do_attn, do_pool, do_conv, g_mo_a, g_mo_p, g_mo_c = _rowwise(
            "merge_b%d" % i, merge_b, [rec["o_attn"], rec["o_pool"], rec["o_conv"], dmerged], mix_gains,
            [(D_ATTN, F32), (D_POOL, F32), (D_CONV, F32)], accs=[(1, D_ATTN), (1, D_POOL), (1, D_CONV)], **light)
        g_mix_out = jnp.concatenate([g_mo_a, g_mo_p, g_mo_c], axis=1)
        du_pool, g_w_pool, g_pool_scale = _pool_bwd(rec["proj"], do_pool, wpool_bf[i], row(pool_scale, i))
        da, dgt, g_w_dw, g_b_dw, g_ln_g, g_ln_b, g_w_pw = _conv_bwd(
            rec["proj"], do_conv, rec["y_conv"], wdw_pad[i], row(conv_ln_g, i), row(conv_ln_b, i), wpw_bf[i])
        big_parts[("w_pw", i)] = g_w_pw.reshape(N_SHARD, D_CONV // N_SHARD, D_CONV).astype(BF16)
        ride = [(k, i) for k in big_names[1:] + ["w_pw"]] + ([("w_in", i + 1)] if i + 1 < depth else [])
        dq, dk, dv, got = _attn_bwd(rec["proj"], rec["tot"], do_attn,
                                    _Exchange("scatter", [big_parts[k] for k in ride]))
        received.update(zip(ride, got))
        dproj = jnp.concatenate([dq, dk, dv, du_pool, da, dgt], axis=1)
        big_parts[("w_in", i)] = _mm_tn_col("dw_in%d" % i, rec["u"], dproj)
        du = _mm_nt_col("d_u%d" % i, dproj, wg[("w_in", i)], 0)

        def pre_mix_b(row0, hv, d, dd, g):
            _, vjp = jax.vjp(_rms, hv, g)
            dhh, dg = vjp(dd)
            out = d + dhh
            return out, dg, jnp.where(row0 == 0, out, 0.0)

        dh, g_pre_mix, dh_head = _rowwise("pre_mix_b%d" % i, pre_mix_b, [rec["h"], dmid, du], [row(pre_mix_g, i)],
                                          [(D_MODEL, F32)], accs=[(1, D_MODEL), (BLOCK, D_MODEL)], with_row0=True,
                                          headless={3} if i == 0 else ())
        small_grads[i] = dict(pre_mix_g=g_pre_mix[0], w_pool=g_w_pool, pool_scale=g_pool_scale[0],
                              w_dw=g_w_dw[:CONV_WIDTH], b_dw=g_b_dw[0], conv_ln_g=g_ln_g[0], conv_ln_b=g_ln_b[0],
                              mix_out_g=g_mix_out[0], post_mix_g=g_post_mix[0],
                              pre_ffn_g=g_pre_ffn[0], post_ffn_g=g_post_ffn[0])

    grad_x = dh[None]
    g_meta_part = dh_head[PAD:PAD + N_META]

    rep_names = ["pre_mix_g", "pool_scale", "b_dw", "conv_ln_g", "conv_ln_b", "mix_out_g", "post_mix_g",
                 "pre_ffn_g", "post_ffn_g", "w_pool"]
    stack2 = lambda nme: jnp.stack([small_grads[l][nme] for l in range(depth)])
    small_list = [stack2(nme) for nme in rep_names] + [g_meta_part, stack2("w_dw")]
    small_list[rep_names.index("w_pool")] = small_list[rep_names.index("w_pool")].reshape(w_pool.shape)
    full_shapes = [a.shape for a in small_list]
    packed = _pack(small_list)

    assert depth == 2
    early = big_names[1:] + ["w_pw"]
    plane_sums = {k: _sum_slots_layers("sum_%s" % k, received[(k, 0)], received[(k, 1)]) for k in early}
    last, swapped, small_slots = _tail_exchange("tail_exchange", [big_parts[("w_in", 0)]],
                                                [plane_sums[k] for k in early], packed)
    summed = _sum_slots("sum_small", small_slots)
    full = _unpack(summed, full_shapes)
    rep_grads = dict(zip(rep_names, full[:len(rep_names)]))
    g_meta = lax.dynamic_slice_in_dim(full[-2], xy * meta_tokens.shape[1], meta_tokens.shape[1], axis=1)
    g_w_dw = lax.dynamic_slice_in_dim(full[-1], xy * w_dw.shape[2], w_dw.shape[2], axis=2)

    rep_w = dict(pre_mix_g=pre_mix_g, pool_scale=pool_scale, b_dw=b_dw, conv_ln_g=conv_ln_g, conv_ln_b=conv_ln_b,
                 mix_out_g=mix_out_g, post_mix_g=post_mix_g, pre_ffn_g=pre_ffn_g, post_ffn_g=post_ffn_g,
                 w_pool=w_pool)
    rep_m = dict(pre_mix_g=m_pre_mix_g, pool_scale=m_pool_scale, b_dw=m_b_dw, conv_ln_g=m_conv_ln_g,
                 conv_ln_b=m_conv_ln_b, mix_out_g=m_mix_out_g, post_mix_g=m_post_mix_g, pre_ffn_g=m_pre_ffn_g,
                 post_ffn_g=m_post_ffn_g, w_pool=m_w_pool)
    rep_v = dict(pre_mix_g=v_pre_mix_g, pool_scale=v_pool_scale, b_dw=v_b_dw, conv_ln_g=v_conv_ln_g,
                 conv_ln_b=v_conv_ln_b, mix_out_g=v_mix_out_g, post_mix_g=v_post_mix_g, pre_ffn_g=v_pre_ffn_g,
                 post_ffn_g=v_post_ffn_g, w_pool=v_w_pool)
    sm_names = rep_names + ["meta_tokens", "w_dw"]
    sm_w = [rep_w[k] for k in rep_names] + [meta_tokens, w_dw]
    sm_m = [rep_m[k] for k in rep_names] + [m_meta_tokens, m_w_dw]
    sm_v = [rep_v[k] for k in rep_names] + [v_meta_tokens, v_w_dw]
    sm_g = [rep_grads[k] for k in rep_names] + [g_meta, g_w_dw]
    sm_shapes = [a.shape for a in sm_w]
    sm_delta, sm_nm, sm_nv = _adamw_flat("adamw_small", _pack(sm_w), _pack(sm_m), _pack(sm_v), _pack(sm_g))
    small_out = {}
    for k, g, d, nm, nv in zip(sm_names, sm_g, _unpack(sm_delta, sm_shapes), _unpack(sm_nm, sm_shapes),
                               _unpack(sm_nv, sm_shapes)):
        small_out[k] = (g, d, nm, nv)

    other_sums = dict(zip(early, swapped))
    received[("w_in", 0)] = last[0]
    plane_sums["w_in"] = _sum_slots_layers("sum_w_in", received[("w_in", 0)], received[("w_in", 1)])
    other_sums["w_in"] = _swap_core("swap_core_w_in", [plane_sums["w_in"]])[0]
    big_w = dict(w_in=(w_in, m_w_in, v_w_in), w_out=(w_out, m_w_out, v_w_out), w_gate=(w_gate, m_w_gate, v_w_gate),
                 w_up=(w_up, m_w_up, v_w_up), w_down=(w_down, m_w_down, v_w_down), w_pw=(w_pw, m_w_pw, v_w_pw))
    big_out = {}
    for k in early + ["w_in"]:
        w, m, v = big_w[k]
        big_out[k] = _adamw("adamw_%s" % k, w, m, v, plane_sums[k], other_sums[k])

    order = ["meta_tokens", "pre_mix_g", "w_in", "w_pool", "pool_scale", "w_dw", "b_dw", "conv_ln_g", "conv_ln_b",
             "w_pw", "mix_out_g", "w_out", "post_mix_g", "pre_ffn_g", "w_gate", "w_up", "w_down", "post_ffn_g"]
    res = lambda k: big_out[k] if k in big_out else small_out[k]
    outs = [loss, grad_x]
    for part in range(4):
        outs += [res(k)[part] for k in order]
    return tuple(outs)
```

```python
import functools

import jax
import jax.numpy as jnp
from jax import lax
from jax.experimental import pallas as pl
from jax.experimental.pallas import tpu as pltpu

F32 = jnp.float32
BF16 = jnp.bfloat16

D_MODEL = 2048
N_META = 16
D_ATTN = 1024
D_POOL = 512
D_CONV = 512
POOL_WINDOWS = (2, 4, 8, 16)
CONV_WIDTH = 31
D_IN_PROJ = 3 * D_ATTN + D_POOL + 2 * D_CONV
D_FF = 5632
EPS = 1e-6
BLOCK = 128
PAD = BLOCK - N_META
HALO = 32
N_SHARD = 4
N_DEV = 8
MESH = pl.DeviceIdType.MESH

ADAM_LR = 0.001
ADAM_B1 = 0.9
ADAM_B2 = 0.999
ADAM_EPS = 1e-08
ADAM_WD = 0.01
ADAM_STEP = 10

COL_Q, COL_K, COL_V = 0, D_ATTN, 2 * D_ATTN
COL_POOL = 3 * D_ATTN
COL_A = COL_POOL + D_POOL
COL_G = COL_A + D_CONV


def _call(body, *, name, out_shape, grid=None, in_specs=None, out_specs=None, scratch=(), vmem_mb=None,
          aliases=None):
    params = {}
    if grid is not None:
        params["dimension_semantics"] = ("arbitrary",) * len(grid)
    if vmem_mb is not None:
        params["vmem_limit_bytes"] = vmem_mb << 20
    kw = dict(out_shape=out_shape, name=name, compiler_params=pltpu.CompilerParams(**params))
    if grid is not None:
        kw["grid"] = grid
    if in_specs is not None:
        kw["in_specs"] = in_specs
    if out_specs is not None:
        kw["out_specs"] = out_specs
    if scratch:
        kw["scratch_shapes"] = list(scratch)
    if aliases:
        kw["input_output_aliases"] = dict(aliases)
    return pl.pallas_call(body, **kw)


def _sds(shape, dtype):
    return jax.ShapeDtypeStruct(tuple(shape), dtype)


def _pick(n, candidates):
    for c in candidates:
        if n % c == 0:
            return c
    return n


def _rowwise(name, fn, rows_in, consts, outs, accs=(), tm=BLOCK, with_row0=False, vmem_mb=None, headless=()):
    lp = rows_in[0].shape[0]
    n_in, n_c, n_o = len(rows_in), len(consts), len(outs)
    back = lambda j: (lambda i: (jnp.maximum(i - 1, 0), 0)) if j in headless else (lambda i: (i, 0))
    rows_of = lambda j: lp - tm if j in headless else lp

    def body(*refs):
        vals = [r[...] for r in refs[:n_in + n_c]]
        if with_row0:
            vals = [pl.program_id(0) * tm] + vals
        res = fn(*vals)
        if not isinstance(res, (tuple, list)):
            res = (res,)
        o_refs = refs[n_in + n_c:n_in + n_c + n_o]
        a_refs = refs[n_in + n_c + n_o:]
        for r, v in zip(o_refs, res[:n_o]):
            r[...] = v.astype(r.dtype)
        if a_refs:
            @pl.when(pl.program_id(0) == 0)
            def _():
                for r in a_refs:
                    r[...] = jnp.zeros(r.shape, r.dtype)
            for r, v in zip(a_refs, res[n_o:]):
                r[...] += v.astype(r.dtype)

    in_specs = [pl.BlockSpec((tm, a.shape[1]), back(j)) for j, a in enumerate(rows_in)]
    in_specs += [pl.BlockSpec(c.shape, lambda i: (0, 0)) for c in consts]
    out_specs = [pl.BlockSpec((tm, w), back(n_in + j)) for j, (w, _) in enumerate(outs)]
    out_specs += [pl.BlockSpec(s, lambda i: (0, 0)) for s in accs]
    out_shape = [_sds((rows_of(n_in + j), w), dt) for j, (w, dt) in enumerate(outs)] + [_sds(s, F32) for s in accs]
    res = _call(body, name=name, out_shape=out_shape, grid=(lp // tm,), in_specs=in_specs,
                out_specs=out_specs, vmem_mb=vmem_mb)(*rows_in, *consts)
    return res


def _rms(x, g):
    return x * lax.rsqrt(jnp.mean(x * x, axis=-1, keepdims=True) + EPS) * g


def _merge(oa, op, oc, ga, gp, gc):
    return jnp.concatenate([_rms(oa, ga), _rms(op, gp), _rms(oc, gc)], axis=1)


def _colsum(x):
    return jnp.sum(x, axis=0, keepdims=True)


def _sigmoid(x):
    return 1.0 / (1.0 + jnp.exp(-x))


MM_VMEM_MB = 56


def _mm_tiles(lp):
    return _pick(lp, (1408, 384, 256, 128))


def _mm_nn_col(name, a, wg, layer, out_dtype):
    lp, k = a.shape
    n = wg.shape[3]
    tm = _mm_tiles(lp)

    def body(a_ref, w_ref, o_ref):
        o_ref[...] = jnp.dot(a_ref[...], w_ref[...], preferred_element_type=F32).astype(o_ref.dtype)

    return _call(body, name=name, out_shape=_sds((lp, N_SHARD * n), out_dtype), grid=(N_SHARD, lp // tm),
                 in_specs=[pl.BlockSpec((tm, k), lambda s, i: (i, 0)),
                           pl.BlockSpec((None, None, k, n), lambda s, i: (s, layer, 0, 0))],
                 out_specs=pl.BlockSpec((tm, n), lambda s, i: (i, s)), vmem_mb=MM_VMEM_MB)(a, wg)


def _mm_nn_col_swiglu(name, a, wg, layer, gate):
    lp, k = a.shape
    n = wg.shape[3]
    tm = _pick(lp, (704, 384, 128))

    def body(a_ref, w_ref, g_ref, u_ref, act_ref):
        up = jnp.dot(a_ref[...], w_ref[...], preferred_element_type=F32)
        g = g_ref[...]
        u_ref[...] = up
        act_ref[...] = (g * _sigmoid(g) * up).astype(act_ref.dtype)

    blk = pl.BlockSpec((tm, n), lambda s, i: (i, s))
    return _call(body, name=name, out_shape=[_sds((lp, N_SHARD * n), F32), _sds((lp, N_SHARD * n), BF16)],
                 grid=(N_SHARD, lp // tm),
                 in_specs=[pl.BlockSpec((tm, k), lambda s, i: (i, 0)),
                           pl.BlockSpec((None, None, k, n), lambda s, i: (s, layer, 0, 0)), blk],
                 out_specs=[blk, blk], vmem_mb=MM_VMEM_MB)(a, wg, gate)


def _mm_nn_row(name, a, wg, layer):
    lp = a.shape[0]
    k, n = wg.shape[2], wg.shape[3]
    tm = _mm_tiles(lp)

    def body(a_ref, w_ref, o_ref):
        part = jnp.dot(a_ref[...], w_ref[...], preferred_element_type=F32)

        @pl.when(pl.program_id(1) == 0)
        def _():
            o_ref[...] = part

        @pl.when(pl.program_id(1) != 0)
        def _():
            o_ref[...] += part

    return _call(body, name=name, out_shape=_sds((lp, n), F32), grid=(lp // tm, N_SHARD),
                 in_specs=[pl.BlockSpec((tm, k), lambda i, s: (i, s)),
                           pl.BlockSpec((None, None, k, n), lambda i, s: (s, layer, 0, 0))],
                 out_specs=pl.BlockSpec((tm, n), lambda i, s: (i, 0)), vmem_mb=MM_VMEM_MB)(a, wg)


_NT = (((1,), (1,)), ((), ()))
_TN = (((0,), (0,)), ((), ()))


def _mm_nt_col(name, dy, wg, layer):
    lp = dy.shape[0]
    k, n = wg.shape[2], wg.shape[3]
    tm = _mm_tiles(lp)

    def body(d_ref, w_ref, o_ref):
        part = lax.dot_general(d_ref[...], w_ref[...], _NT, preferred_element_type=F32)

        @pl.when(pl.program_id(1) == 0)
        def _():
            o_ref[...] = part

        @pl.when(pl.program_id(1) != 0)
        def _():
            o_ref[...] += part

    return _call(body, name=name, out_shape=_sds((lp, k), F32), grid=(lp // tm, N_SHARD),
                 in_specs=[pl.BlockSpec((tm, n), lambda i, s: (i, s)),
                           pl.BlockSpec((None, None, k, n), lambda i, s: (s, layer, 0, 0))],
                 out_specs=pl.BlockSpec((tm, k), lambda i, s: (i, 0)), vmem_mb=MM_VMEM_MB)(dy, wg)


def _mm_nt_row(name, dy, wg, layer, out_dtype):
    lp = dy.shape[0]
    k, n = wg.shape[2], wg.shape[3]
    tm = _mm_tiles(lp)

    def body(d_ref, w_ref, o_ref):
        o_ref[...] = lax.dot_general(d_ref[...], w_ref[...], _NT, preferred_element_type=F32).astype(o_ref.dtype)

    return _call(body, name=name, out_shape=_sds((lp, N_SHARD * k), out_dtype), grid=(N_SHARD, lp // tm),
                 in_specs=[pl.BlockSpec((tm, n), lambda s, i: (i, 0)),
                           pl.BlockSpec((None, None, k, n), lambda s, i: (s, layer, 0, 0))],
                 out_specs=pl.BlockSpec((tm, k), lambda s, i: (i, s)), vmem_mb=MM_VMEM_MB)(dy, wg)


def _mm_nt_row_swiglu(name, dy, wg, layer, gate, up):
    lp = dy.shape[0]
    k, n = wg.shape[2], wg.shape[3]
    tm = _pick(lp, (704, 384, 128))

    def body(d_ref, w_ref, g_ref, u_ref, dg_ref, du_ref):
        dact = lax.dot_general(d_ref[...], w_ref[...], _NT, preferred_element_type=F32)
        g = g_ref[...]
        sg = _sigmoid(g)
        dg_ref[...] = (dact * u_ref[...] * (sg * (1.0 + g * (1.0 - sg)))).astype(dg_ref.dtype)
        du_ref[...] = (dact * (g * sg)).astype(du_ref.dtype)

    blk = pl.BlockSpec((tm, k), lambda s, i: (i, s))
    return _call(body, name=name, out_shape=[_sds((lp, N_SHARD * k), BF16)] * 2, grid=(N_SHARD, lp // tm),
                 in_specs=[pl.BlockSpec((tm, n), lambda s, i: (i, 0)),
                           pl.BlockSpec((None, None, k, n), lambda s, i: (s, layer, 0, 0)), blk, blk],
                 out_specs=[blk, blk], vmem_mb=MM_VMEM_MB)(dy, wg, gate, up)


def _mm_tn_col(name, a, dy):
    lp, k = a.shape
    n = dy.shape[1] // N_SHARD
    tm = _mm_tiles(lp)
    tk = _pick(k, (1024, 512))

    def body(a_ref, d_ref, o_ref, acc):
        @pl.when(pl.program_id(2) == 0)
        def _():
            acc[...] = jnp.zeros(acc.shape, F32)

        acc[...] += lax.dot_general(a_ref[...], d_ref[...], _TN, preferred_element_type=F32)

        @pl.when(pl.program_id(2) == pl.num_programs(2) - 1)
        def _():
            o_ref[...] = acc[...].astype(o_ref.dtype)

    return _call(body, name=name, out_shape=_sds((N_SHARD, k, n), BF16), grid=(N_SHARD, k // tk, lp // tm),
                 in_specs=[pl.BlockSpec((tm, tk), lambda s, kk, i: (i, kk)),
                           pl.BlockSpec((tm, n), lambda s, kk, i: (i, s))],
                 out_specs=pl.BlockSpec((None, tk, n), lambda s, kk, i: (s, kk, 0)),
                 scratch=[pltpu.VMEM((tk, n), F32)], vmem_mb=MM_VMEM_MB)(a, dy)


def _mm_tn_row(name, a, dy):
    lp = a.shape[0]
    k = a.shape[1] // N_SHARD
    n = dy.shape[1]
    tm = _mm_tiles(lp)
    tn = _pick(n, (1024, 512))

    def body(a_ref, d_ref, o_ref, acc):
        @pl.when(pl.program_id(2) == 0)
        def _():
            acc[...] = jnp.zeros(acc.shape, F32)

        acc[...] += lax.dot_general(a_ref[...], d_ref[...], _TN, preferred_element_type=F32)

        @pl.when(pl.program_id(2) == pl.num_programs(2) - 1)
        def _():
            o_ref[...] = acc[...].astype(o_ref.dtype)

    return _call(body, name=name, out_shape=_sds((N_SHARD, k, n), BF16), grid=(N_SHARD, n // tn, lp // tm),
                 in_specs=[pl.BlockSpec((tm, k), lambda s, j, i: (i, s)),
                           pl.BlockSpec((tm, tn), lambda s, j, i: (i, j))],
                 out_specs=pl.BlockSpec((None, k, tn), lambda s, j, i: (s, 0, j)),
                 scratch=[pltpu.VMEM((k, tn), F32)], vmem_mb=MM_VMEM_MB)(a, dy)


SUB = 16
WIDE = 2 * BLOCK
Z_CLAMP = 20.0


def _log1m_sigmoid(z):
    return -jnp.where(z > Z_CLAMP, z, jnp.log(1.0 + jnp.exp(jnp.minimum(z, Z_CLAMP))))


def _tri(tk, kind):
    r = lax.broadcasted_iota(jnp.int32, (tk, tk), 0)
    c = lax.broadcasted_iota(jnp.int32, (tk, tk), 1)
    t = {"gt": r > c, "le": r <= c}[kind]
    return jnp.where(t, 1.0, 0.0).astype(BF16)


def _strip_mask(kind, s, tk):
    if kind == "none":
        return None
    col = lax.broadcasted_iota(jnp.int32, (SUB, tk), 1)
    row = lax.broadcasted_iota(jnp.int32, (SUB, tk), 0)
    causal = (col - row) < s * SUB
    if kind == "diag":
        return causal
    if kind == "pad":
        return col >= PAD
    return causal & (col >= PAD)


def _attn_blocks(lp):
    nb = lp // BLOCK
    assert nb % 2 == 1, "sequence must be a 128-row block plus whole 256-row blocks"
    return nb, (nb + 1) // 2


class _Exchange:
    def __init__(self, kind, arrs, split=False):
        self.kind, self.arrs, self.n, self.split = kind, list(arrs), len(arrs), split

    def out_shape(self):
        if self.kind == "gather":
            return [_sds((N_SHARD,) + a.shape, a.dtype) for a in self.arrs]
        return [_sds(a.shape, a.dtype) for a in self.arrs]

    def scratch(self):
        sems = [pltpu.SemaphoreType.DMA((3 * self.n,)), pltpu.SemaphoreType.DMA((3 * self.n,)),
                pltpu.SemaphoreType.DMA((self.n,))]
        if self.split:
            sems += [pltpu.SemaphoreType.DMA((self.n,)), pltpu.SemaphoreType.DMA((self.n,))]
        return sems

    def forward(self, wait, outs, fsend, frecv):
        x, y, c = _place()
        for a in range(self.n):
            cp = pltpu.make_async_remote_copy(src_ref=outs[a], dst_ref=outs[a], send_sem=fsend.at[a],
                                              recv_sem=frecv.at[a], device_id=(x, y, 1 - c), device_id_type=MESH)
            if wait:
                pl.when(c == a % 2)(cp.wait_send)
                pl.when(c != a % 2)(cp.wait_recv)
            else:
                pl.when(c == a % 2)(cp.start)

    def copies(self, a, ins, outs, send, recv, local):
        x, y, c = _place()
        me = 2 * x + y
        own = ins[a] if self.kind == "gather" else ins[a].at[me]
        out = [pltpu.make_async_copy(own, outs[a].at[me], local.at[a])]
        for r, (px, py) in enumerate(_xy_peers(x, y)):
            src = ins[a] if self.kind == "gather" else ins[a].at[2 * px + py]
            out.append(pltpu.make_async_remote_copy(
                src_ref=src, dst_ref=outs[a].at[me], send_sem=send.at[3 * a + r], recv_sem=recv.at[3 * a + r],
                device_id=(px, py, c), device_id_type=MESH))
        return out

    def run(self, wait, ins, outs, send, recv, local):
        for a in range(self.n):
            def go(a=a):
                for cp in self.copies(a, ins, outs, send, recv, local):
                    if wait:
                        cp.wait()
                    else:
                        cp.start()
            if self.split:
                pl.when(lax.axis_index("c") == a % 2)(go)
            else:
                go()


def _attn_fwd(proj, exchange=None):
    lp = proj.shape[0]
    nb, nq = _attn_blocks(lp)
    n_pair = D_ATTN // BLOCK
    n_x = exchange.n if exchange else 0

    def body(*refs):
        q_ref, k_ref, v_ref = refs[:3]
        x_in = refs[3:3 + n_x]
        o_ref, tot_ref = refs[3 + n_x:5 + n_x]
        x_out = refs[5 + n_x:5 + 2 * n_x]
        qs, kb, vh, tri_l, tri_s, z_s, hl_s, c_s, w_s, r_s, acc_s = refs[5 + 2 * n_x:16 + 2 * n_x]
        x_sem = refs[16 + 2 * n_x:]
        p, i = pl.program_id(0), pl.program_id(1)
        m0 = lax.broadcasted_iota(jnp.int32, (1, BLOCK), 1) < (BLOCK // 2)

        if exchange:
            @pl.when((p == 0) & (i == 0))
            def _():
                exchange.run(False, x_in, x_out, *x_sem[:3])

            if exchange.split:
                @pl.when((p == n_pair - 2) & (i == 0))
                def _():
                    exchange.run(True, x_in, x_out, *x_sem[:3])
                    exchange.forward(False, x_out, *x_sem[3:])

        @pl.when(i == 0)
        def _():
            tri_l[...] = _tri(WIDE, "gt")
            tri_s[...] = _tri(BLOCK, "gt")

            def prep(b, carry):
                rows = pl.ds(pl.multiple_of(b * BLOCK, BLOCK), BLOCK)
                q = q_ref[rows, :] * 0.125
                v = v_ref[rows, :]
                qs[0, rows, :] = jnp.where(m0, q, 0.0).astype(BF16)
                qs[1, rows, :] = jnp.where(m0, 0.0, q).astype(BF16)
                kb[rows, :] = k_ref[rows, :].astype(BF16)
                vh[0, rows, :] = jnp.where(m0, v, 0.0).astype(BF16)
                vh[1, rows, :] = jnp.where(m0, 0.0, v).astype(BF16)
                return carry

            lax.fori_loop(0, nb, prep, 0)

        def tiles(q0, tq, specs):
            heads = [(t, h) for t in range(len(specs)) for h in range(2)]
            strips = [slice(s * SUB, (s + 1) * SUB) for s in range(tq // SUB)]
            for t, h in heads:
                k0, tk, _ = specs[t]
                z_s[t, h, 0:tq, 0:tk] = lax.dot_general(qs[h, pl.ds(q0, tq), :], kb[pl.ds(k0, tk), :], _NT,
                                                        preferred_element_type=F32)
            for t, h in heads:
                _, tk, kind = specs[t]
                for s, rows in enumerate(strips):
                    z = z_s[t, h, rows, 0:tk]
                    lnb = _log1m_sigmoid(z)
                    m = _strip_mask(kind, s, tk)
                    if m is not None:
                        lnb = jnp.where(m, lnb, 0.0)
                    z_s[t, h, rows, 0:tk] = z + lnb
                    hl_s[t, h, rows, 0:tk] = lnb.astype(BF16)
            for t, h in heads:
                _, tk, _ = specs[t]
                tri = tri_l if tk == WIDE else tri_s
                c_s[t, h, 0:tq, 0:tk] = jnp.dot(hl_s[t, h, 0:tq, 0:tk], tri[...], preferred_element_type=F32)
            for t, h in heads:
                _, tk, kind = specs[t]
                for s, rows in enumerate(strips):
                    r = r_s[h, rows, :]
                    c = c_s[t, h, rows, 0:tk]
                    rr = r if tk == BLOCK else jnp.concatenate([r, r], axis=1)
                    w = jnp.exp(z_s[t, h, rows, 0:tk] + c + rr)
                    m = _strip_mask(kind, s, tk)
                    if m is not None:
                        w = jnp.where(m, w, 0.0)
                    w_s[t, h, rows, 0:tk] = w.astype(BF16)
                    total = c[:, 0:1] + hl_s[t, h, rows, 0:BLOCK].astype(F32)[:, 0:1]
                    r_s[h, rows, :] = r + jnp.broadcast_to(total, (SUB, BLOCK))
            upd = None
            for t, h in heads:
                k0, tk, _ = specs[t]
                d = jnp.dot(w_s[t, h, 0:tq, 0:tk], vh[h, pl.ds(k0, tk), :], preferred_element_type=F32)
                upd = d if upd is None else upd + d
            acc_s[0:tq, :] += upd

        def finish(q0, tq):
            o_ref[pl.ds(q0, tq), :] = acc_s[0:tq, :]
            tot_ref[pl.ds(q0, tq), :] = jnp.where(m0, r_s[0, 0:tq, :], r_s[1, 0:tq, :])

        r_s[...] = jnp.zeros(r_s.shape, F32)
        acc_s[...] = jnp.zeros(acc_s.shape, F32)

        @pl.when(i == 0)
        def _():
            tiles(0, BLOCK, [(0, BLOCK, "first")])
            finish(0, BLOCK)

        @pl.when(i > 0)
        def _():
            q0 = pl.multiple_of(i * WIDE - BLOCK, BLOCK)
            full = lambda j: (pl.multiple_of(j * WIDE - BLOCK, BLOCK), WIDE, "none")
            diag, meta = (q0, WIDE, "diag"), (0, BLOCK, "pad")

            @pl.when(i == 1)
            def _():
                tiles(q0, WIDE, [diag, meta])

            @pl.when(i >= 2)
            def _():
                tiles(q0, WIDE, [diag, full(i - 1)])

                def inner(n, carry):
                    j = i - 2 - 2 * n
                    tiles(q0, WIDE, [full(j), full(j - 1)])
                    return carry

                lax.fori_loop(0, (i - 2) // 2, inner, 0)

                @pl.when(i % 2 == 1)
                def _():
                    tiles(q0, WIDE, [full(1), meta])

                @pl.when(i % 2 == 0)
                def _():
                    tiles(q0, WIDE, [meta])

            finish(q0, WIDE)

        if exchange:
            @pl.when((p == n_pair - 1) & (i == nq - 1))
            def _():
                if exchange.split:
                    exchange.forward(True, x_out, *x_sem[3:])
                else:
                    exchange.run(True, x_in, x_out, *x_sem)

    cq, ck, cv = COL_Q // BLOCK, COL_K // BLOCK, COL_V // BLOCK
    col = lambda c0: (lambda p, i: (0, c0 + p))
    scratch = [pltpu.VMEM((2, lp, BLOCK), BF16), pltpu.VMEM((lp, BLOCK), BF16), pltpu.VMEM((2, lp, BLOCK), BF16),
               pltpu.VMEM((WIDE, WIDE), BF16), pltpu.VMEM((BLOCK, BLOCK), BF16),
               pltpu.VMEM((2, 2, WIDE, WIDE), F32), pltpu.VMEM((2, 2, WIDE, WIDE), BF16),
               pltpu.VMEM((2, 2, WIDE, WIDE), F32), pltpu.VMEM((2, 2, WIDE, WIDE), BF16),
               pltpu.VMEM((2, WIDE, BLOCK), F32), pltpu.VMEM((WIDE, BLOCK), F32)]
    res = _call(body, name="attn_fwd",
                out_shape=[_sds((lp, D_ATTN), F32), _sds((lp, D_ATTN), F32)] + (exchange.out_shape() if exchange else []),
                grid=(n_pair, nq),
                in_specs=[pl.BlockSpec((lp, BLOCK), col(cq)), pl.BlockSpec((lp, BLOCK), col(ck)),
                          pl.BlockSpec((lp, BLOCK), col(cv))] + [_ANY] * n_x,
                out_specs=[pl.BlockSpec((lp, BLOCK), col(0)), pl.BlockSpec((lp, BLOCK), col(0))] + [_ANY] * n_x,
                scratch=scratch + (exchange.scratch() if exchange else []),
                vmem_mb=56)(proj, proj, proj, *(exchange.arrs if exchange else []))
    return res[0], res[1], list(res[2:])


def _attn_bwd(proj, tot, d_out, exchange=None):
    lp = proj.shape[0]
    nb, nq = _attn_blocks(lp)
    n_pair = D_ATTN // BLOCK
    n_x = exchange.n if exchange else 0
    n_s = 23

    def body(*refs):
        q_ref, k_ref, v_ref, tot_ref, do_ref = refs[:5]
        x_in = refs[5:5 + n_x]
        dq_ref, dk_ref, dv_ref = refs[5 + n_x:8 + n_x]
        x_out = refs[8 + n_x:8 + 2 * n_x]
        (qs, kb, kh, vb, doh, tge_l, tge_s, tle_l, tle_s, z_s, g_s, hl_s, c_s, gl_s, gc_s, w_s, dz_s,
         tot_s, a_s, b_s, dq_acc, dk_acc, dv_acc) = refs[8 + 2 * n_x:8 + 2 * n_x + n_s]
        x_sem = refs[8 + 2 * n_x + n_s:]
        p, i = pl.program_id(0), pl.program_id(1)
        m0 = lax.broadcasted_iota(jnp.int32, (1, BLOCK), 1) < (BLOCK // 2)

        if exchange:
            @pl.when((p == 0) & (i == 0))
            def _():
                exchange.run(False, x_in, x_out, *x_sem)

        @pl.when(i == 0)
        def _():
            tge_l[...] = _tri(WIDE, "gt")
            tge_s[...] = _tri(BLOCK, "gt")
            tle_l[...] = _tri(WIDE, "le")
            tle_s[...] = _tri(BLOCK, "le")

            def prep(b, carry):
                rows = pl.ds(pl.multiple_of(b * BLOCK, BLOCK), BLOCK)
                q = q_ref[rows, :] * 0.125
                k = k_ref[rows, :]
                do = do_ref[rows, :]
                qs[0, rows, :] = jnp.where(m0, q, 0.0).astype(BF16)
                qs[1, rows, :] = jnp.where(m0, 0.0, q).astype(BF16)
                kb[rows, :] = k.astype(BF16)
                kh[0, rows, :] = jnp.where(m0, k, 0.0).astype(BF16)
                kh[1, rows, :] = jnp.where(m0, 0.0, k).astype(BF16)
                vb[rows, :] = v_ref[rows, :].astype(BF16)
                doh[0, rows, :] = jnp.where(m0, do, 0.0).astype(BF16)
                doh[1, rows, :] = jnp.where(m0, 0.0, do).astype(BF16)
                dk_acc[rows, :] = jnp.zeros((BLOCK, BLOCK), F32)
                dv_acc[rows, :] = jnp.zeros((BLOCK, BLOCK), F32)
                return carry

            lax.fori_loop(0, nb, prep, 0)

        def wide(x, tk):
            return x if tk == BLOCK else jnp.concatenate([x, x], axis=1)

        def tiles(q0, tq, specs):
            heads = [(t, h) for t in range(len(specs)) for h in range(2)]
            strips = [slice(s * SUB, (s + 1) * SUB) for s in range(tq // SUB)]
            for t, h in heads:
                k0, tk, _ = specs[t]
                z_s[t, h, 0:tq, 0:tk] = lax.dot_general(qs[h, pl.ds(q0, tq), :], kb[pl.ds(k0, tk), :], _NT,
                                                        preferred_element_type=F32)
                g_s[t, h, 0:tq, 0:tk] = lax.dot_general(doh[h, pl.ds(q0, tq), :], vb[pl.ds(k0, tk), :], _NT,
                                                        preferred_element_type=F32)
            for t, h in heads:
                _, tk, kind = specs[t]
                for s, rows in enumerate(strips):
                    z = z_s[t, h, rows, 0:tk]
                    lnb = _log1m_sigmoid(z)
                    m = _strip_mask(kind, s, tk)
                    if m is not None:
                        lnb = jnp.where(m, lnb, 0.0)
                    z_s[t, h, rows, 0:tk] = z + lnb
                    hl_s[t, h, rows, 0:tk] = lnb.astype(BF16)
            for t, h in heads:
                _, tk, _ = specs[t]
                tri = tge_l if tk == WIDE else tge_s
                c_s[t, h, 0:tq, 0:tk] = jnp.dot(hl_s[t, h, 0:tq, 0:tk], tri[...], preferred_element_type=F32)
            for t, h in heads:
                _, tk, kind = specs[t]
                for s, rows in enumerate(strips):
                    c = c_s[t, h, rows, 0:tk]
                    total = c[:, 0:1] + hl_s[t, h, rows, 0:BLOCK].astype(F32)[:, 0:1]
                    a_next = a_s[h, rows, :] + jnp.broadcast_to(total, (SUB, BLOCK))
                    a_s[h, rows, :] = a_next
                    w = jnp.exp(z_s[t, h, rows, 0:tk] + c + wide(tot_s[h, rows, :] - a_next, tk))
                    m = _strip_mask(kind, s, tk)
                    if m is not None:
                        w = jnp.where(m, w, 0.0)
                    g = w * g_s[t, h, rows, 0:tk]
                    g_s[t, h, rows, 0:tk] = g
                    gl_s[t, h, rows, 0:tk] = g.astype(BF16)
                    w_s[t, h, rows, 0:tk] = w.astype(BF16)
            for t, h in heads:
                _, tk, _ = specs[t]
                tri = tle_l if tk == WIDE else tle_s
                gc_s[t, h, 0:tq, 0:tk] = jnp.dot(gl_s[t, h, 0:tq, 0:tk], tri[...], preferred_element_type=F32)
            for t, h in heads:
                _, tk, kind = specs[t]
                for s, rows in enumerate(strips):
                    gc = gc_s[t, h, rows, 0:tk]
                    b = b_s[h, rows, :]
                    sig = jnp.exp(z_s[t, h, rows, 0:tk])
                    dz = g_s[t, h, rows, 0:tk] - sig * (gc + wide(b, tk))
                    m = _strip_mask(kind, s, tk)
                    if m is not None:
                        dz = jnp.where(m, dz, 0.0)
                    dz_s[t, h, rows, 0:tk] = dz.astype(BF16)
                    b_s[h, rows, :] = b + jnp.broadcast_to(gc[:, tk - 1:tk], (SUB, BLOCK))
            upd = None
            for t, h in heads:
                k0, tk, _ = specs[t]
                d = jnp.dot(dz_s[t, h, 0:tq, 0:tk], kh[h, pl.ds(k0, tk), :], preferred_element_type=F32)
                upd = d if upd is None else upd + d
            dq_acc[0:tq, :] += upd
            for t, (k0, tk, _) in enumerate(specs):
                dk_acc[pl.ds(k0, tk), :] += (
                    lax.dot_general(dz_s[t, 0, 0:tq, 0:tk], qs[0, pl.ds(q0, tq), :], _TN, preferred_element_type=F32) +
                    lax.dot_general(dz_s[t, 1, 0:tq, 0:tk], qs[1, pl.ds(q0, tq), :], _TN, preferred_element_type=F32))
                dv_acc[pl.ds(k0, tk), :] += (
                    lax.dot_general(w_s[t, 0, 0:tq, 0:tk], doh[0, pl.ds(q0, tq), :], _TN, preferred_element_type=F32) +
                    lax.dot_general(w_s[t, 1, 0:tq, 0:tk], doh[1, pl.ds(q0, tq), :], _TN, preferred_element_type=F32))

        def start(q0, tq):
            tv = tot_ref[pl.ds(q0, tq), :]
            tot_s[0, 0:tq, :] = jnp.broadcast_to(tv[:, 0:1], (tq, BLOCK))
            tot_s[1, 0:tq, :] = jnp.broadcast_to(tv[:, BLOCK - 1:BLOCK], (tq, BLOCK))
            a_s[...] = jnp.zeros(a_s.shape, F32)
            b_s[...] = jnp.zeros(b_s.shape, F32)
            dq_acc[...] = jnp.zeros(dq_acc.shape, F32)

        def finish(q0, tq):
            dq_ref[pl.ds(q0, tq), :] = (dq_acc[0:tq, :] * 0.125).astype(dq_ref.dtype)

        @pl.when(i == 0)
        def _():
            start(0, BLOCK)
            tiles(0, BLOCK, [(0, BLOCK, "first")])
            finish(0, BLOCK)

        @pl.when(i > 0)
        def _():
            q0 = pl.multiple_of(i * WIDE - BLOCK, BLOCK)
            full = lambda j: (pl.multiple_of(j * WIDE - BLOCK, BLOCK), WIDE, "none")
            diag, meta = (q0, WIDE, "diag"), (0, BLOCK, "pad")
            start(q0, WIDE)

            @pl.when(i == 1)
            def _():
                tiles(q0, WIDE, [meta, diag])

            @pl.when(i >= 2)
            def _():
                odd = i % 2

                @pl.when(odd == 1)
                def _():
                    tiles(q0, WIDE, [meta, full(1)])

                @pl.when(odd == 0)
                def _():
                    tiles(q0, WIDE, [meta])

                def inner(n, carry):
                    j = 1 + odd + 2 * n
                    tiles(q0, WIDE, [full(j), full(j + 1)])
                    return carry

                lax.fori_loop(0, (i - 2) // 2, inner, 0)
                tiles(q0, WIDE, [full(i - 1), diag])

            finish(q0, WIDE)

        @pl.when(i == nq - 1)
        def _():
            dk_ref[...] = dk_acc[...].astype(dk_ref.dtype)
            dv_ref[...] = dv_acc[...].astype(dv_ref.dtype)

        if exchange:
            @pl.when((p == n_pair - 1) & (i == nq - 1))
            def _():
                exchange.run(True, x_in, x_out, *x_sem)

    cq, ck, cv = COL_Q // BLOCK, COL_K // BLOCK, COL_V // BLOCK
    col = lambda c0: (lambda p, i: (0, c0 + p))
    whole = lambda c0: pl.BlockSpec((lp, BLOCK), col(c0))
    tile4 = lambda w, dt: pltpu.VMEM((2, 2, WIDE, w), dt)
    scratch = [pltpu.VMEM((2, lp, BLOCK), BF16), pltpu.VMEM((lp, BLOCK), BF16), pltpu.VMEM((2, lp, BLOCK), BF16),
               pltpu.VMEM((lp, BLOCK), BF16), pltpu.VMEM((2, lp, BLOCK), BF16),
               pltpu.VMEM((WIDE, WIDE), BF16), pltpu.VMEM((BLOCK, BLOCK), BF16),
               pltpu.VMEM((WIDE, WIDE), BF16), pltpu.VMEM((BLOCK, BLOCK), BF16),
               tile4(WIDE, F32), tile4(WIDE, F32), tile4(WIDE, BF16), tile4(WIDE, F32),
               tile4(WIDE, BF16), tile4(WIDE, F32), tile4(WIDE, BF16), tile4(WIDE, BF16),
               pltpu.VMEM((2, WIDE, BLOCK), F32), pltpu.VMEM((2, WIDE, BLOCK), F32), pltpu.VMEM((2, WIDE, BLOCK), F32),
               pltpu.VMEM((WIDE, BLOCK), F32), pltpu.VMEM((lp, BLOCK), F32), pltpu.VMEM((lp, BLOCK), F32)]
    res = _call(body, name="attn_bwd",
                out_shape=[_sds((lp, D_ATTN), BF16)] * 3 + (exchange.out_shape() if exchange else []),
                grid=(n_pair, nq),
                in_specs=[whole(cq), whole(ck), whole(cv), whole(0), whole(0)] + [_ANY] * n_x,
                out_specs=[whole(0), whole(0), whole(0)] + [_ANY] * n_x,
                scratch=scratch + (exchange.scratch() if exchange else []),
                vmem_mb=60)(proj, proj, proj, tot, d_out, *(exchange.arrs if exchange else []))
    return res[0], res[1], res[2], list(res[3:])


def _shift_down(x, d):
    return x if d == 0 else pltpu.roll(x, d, 0)


def _shift_up(x, d):
    return x if d == 0 else pltpu.roll(x, x.shape[0] - d, 0)


def _pool_windows(ext, down):
    shift = _shift_down if down else _shift_up
    outs = []
    for g, w in enumerate(POOL_WINDOWS):
        s = ext[:, g * BLOCK:(g + 1) * BLOCK]
        d = 1
        while d < w:
            s = s + shift(s, d)
            d *= 2
        outs.append(s)
    return jnp.concatenate(outs, axis=1)


def _pool_counts(pos):
    cols = [jnp.broadcast_to(jnp.clip(pos + 1, 1, w).astype(F32), (pos.shape[0], BLOCK)) for w in POOL_WINDOWS]
    return jnp.concatenate(cols, axis=1)


def _group_dot(x, w_ref, transpose):
    outs = []
    for g in range(len(POOL_WINDOWS)):
        xg = x[:, g * BLOCK:(g + 1) * BLOCK].astype(BF16)
        wg = w_ref[g * BLOCK:(g + 1) * BLOCK, :]
        if transpose:
            outs.append(lax.dot_general(xg, wg, _NT, preferred_element_type=F32))
        else:
            outs.append(jnp.dot(xg, wg, preferred_element_type=F32))
    return jnp.concatenate(outs, axis=1)


def _pooled(prev, cur, r):
    rows = cur.shape[0]
    ext = jnp.concatenate([prev[rows - HALO:], cur], axis=0)
    pos = r * rows + lax.broadcasted_iota(jnp.int32, (rows, 1), 0) - PAD
    ws = _pool_windows(ext, down=True)[HALO:]
    return jnp.where(pos >= 0, ws / _pool_counts(pos) - cur, 0.0)


def _pool_fwd(proj, w_pool_bf, scale):
    lp = proj.shape[0]
    rows = BLOCK
    cb = COL_POOL // D_POOL

    def body(prev_ref, cur_ref, w_ref, s_ref, o_ref):
        pooled = _pooled(prev_ref[...], cur_ref[...], pl.program_id(0))
        o_ref[...] = _group_dot(pooled, w_ref, False) * s_ref[...]

    return _call(body, name="pool_fwd", out_shape=_sds((lp, D_POOL), F32), grid=(lp // rows,),
                 in_specs=[pl.BlockSpec((rows, D_POOL), lambda r: (jnp.maximum(r - 1, 0), cb)),
                           pl.BlockSpec((rows, D_POOL), lambda r: (r, cb)),
                           pl.BlockSpec((D_POOL, BLOCK), lambda r: (0, 0)),
                           pl.BlockSpec((1, D_POOL), lambda r: (0, 0))],
                 out_specs=pl.BlockSpec((rows, D_POOL), lambda r: (r, 0)))(proj, proj, w_pool_bf, scale)


def _pool_bwd(proj, d_out, w_pool_bf, scale):
    lp = proj.shape[0]
    rows = BLOCK
    n_chunk = lp // rows
    cb = COL_POOL // D_POOL

    def body(prev_ref, cur_ref, do_ref, don_ref, w_ref, s_ref, du_ref, dw_ref, ds_ref):
        r = pl.program_id(0)

        @pl.when(r == 0)
        def _():
            dw_ref[...] = jnp.zeros(dw_ref.shape, F32)
            ds_ref[...] = jnp.zeros(ds_ref.shape, F32)

        pooled = _pooled(prev_ref[...], cur_ref[...], r)
        d_ext = jnp.concatenate([do_ref[...], don_ref[0:HALO]], axis=0)
        pos = r * rows + lax.broadcasted_iota(jnp.int32, (rows + HALO, 1), 0) - PAD
        dmixed = jnp.where((pos >= 0) & (pos < lp - PAD), d_ext * s_ref[...], 0.0)
        dpooled = _group_dot(dmixed, w_ref, True)
        back = _pool_windows(dpooled / _pool_counts(pos), down=False)[0:rows]
        du = jnp.where(pos[0:rows] >= 0, back - dpooled[0:rows], 0.0)
        du_ref[...] = du.astype(du_ref.dtype)
        mixed = _group_dot(pooled, w_ref, False)
        ds_ref[...] += _colsum(do_ref[...] * mixed)
        pooled_bf = pooled.astype(BF16)
        dm_bf = dmixed[0:rows].astype(BF16)
        for g in range(len(POOL_WINDOWS)):
            sl = slice(g * BLOCK, (g + 1) * BLOCK)
            dw_ref[sl, :] += lax.dot_general(pooled_bf[:, sl], dm_bf[:, sl], _TN, preferred_element_type=F32)

    return _call(body, name="pool_bwd",
                 out_shape=[_sds((lp, D_POOL), BF16), _sds((D_POOL, BLOCK), F32), _sds((1, D_POOL), F32)],
                 grid=(n_chunk,),
                 in_specs=[pl.BlockSpec((rows, D_POOL), lambda r: (jnp.maximum(r - 1, 0), cb)),
                           pl.BlockSpec((rows, D_POOL), lambda r: (r, cb)),
                           pl.BlockSpec((rows, D_POOL), lambda r: (r, 0)),
                           pl.BlockSpec((rows, D_POOL), lambda r: (jnp.minimum(r + 1, n_chunk - 1), 0)),
                           pl.BlockSpec((D_POOL, BLOCK), lambda r: (0, 0)),
                           pl.BlockSpec((1, D_POOL), lambda r: (0, 0))],
                 out_specs=[pl.BlockSpec((rows, D_POOL), lambda r: (r, 0)),
                            pl.BlockSpec((D_POOL, BLOCK), lambda r: (0, 0)),
                            pl.BlockSpec((1, D_POOL), lambda r: (0, 0))])(proj, proj, d_out, d_out, w_pool_bf, scale)


SUBLANES = 8


def _sub_shifts(x, down):
    shift = _shift_down if down else _shift_up
    return [shift(x, b) for b in range(SUBLANES)]


def _lagged(shifts, d, lo, n, down):
    a, b = divmod(d, SUBLANES)
    start = lo - SUBLANES * a if down else lo + SUBLANES * a
    return shifts[b][start:start + n]


def _conv_taps(u_shifts, wdw_ref, lo, n):
    y = None
    for d in range(CONV_WIDTH):
        term = wdw_ref[CONV_WIDTH - 1 - d:CONV_WIDTH - d, :] * _lagged(u_shifts, d, lo, n, True)
        y = term if y is None else y + term
    return y


def _layernorm_stats(y):
    mu = jnp.mean(y, axis=-1, keepdims=True)
    yc = y - mu
    rstd = lax.rsqrt(jnp.mean(yc * yc, axis=-1, keepdims=True) + EPS)
    return yc * rstd, rstd


def _conv_fwd(proj, wdw, bdw, ln_g, ln_b, wpw_bf):
    lp = proj.shape[0]
    rows = BLOCK
    ca, cg = COL_A // D_CONV, COL_G // D_CONV

    def body(ap_ref, a_ref, gp_ref, g_ref, wdw_ref, b_ref, lg_ref, lb_ref, wpw_ref, o_ref, y_ref):
        r = pl.program_id(0)
        a = jnp.concatenate([ap_ref[rows - HALO:rows], a_ref[...]], axis=0)
        g = jnp.concatenate([gp_ref[rows - HALO:rows], g_ref[...]], axis=0)
        u = a * _sigmoid(g)
        y = _conv_taps(_sub_shifts(u, True), wdw_ref, HALO, rows) + b_ref[...]
        y_ref[...] = y
        xhat, _ = _layernorm_stats(y)
        yn = xhat * lg_ref[...] + lb_ref[...]
        pos = r * rows + lax.broadcasted_iota(jnp.int32, (rows, 1), 0) - PAD
        s = jnp.where(pos >= 0, yn * _sigmoid(yn), 0.0)
        o_ref[...] = jnp.dot(s.astype(BF16), wpw_ref[...], preferred_element_type=F32)

    prev = lambda c: (lambda r: (jnp.maximum(r - 1, 0), c))
    cur = lambda c: (lambda r: (r, c))
    const = lambda shape: pl.BlockSpec(shape, lambda r: (0, 0))
    return _call(body, name="conv_fwd", out_shape=[_sds((lp, D_CONV), F32)] * 2, grid=(lp // rows,),
                 in_specs=[pl.BlockSpec((rows, D_CONV), prev(ca)), pl.BlockSpec((rows, D_CONV), cur(ca)),
                           pl.BlockSpec((rows, D_CONV), prev(cg)), pl.BlockSpec((rows, D_CONV), cur(cg)),
                           const((HALO, D_CONV)), const((1, D_CONV)), const((1, D_CONV)), const((1, D_CONV)),
                           const((D_CONV, D_CONV))],
                 out_specs=[pl.BlockSpec((rows, D_CONV), cur(0))] * 2)(
                     proj, proj, proj, proj, wdw, bdw, ln_g, ln_b, wpw_bf)


def _conv_bwd(proj, d_out, y_conv, wdw, ln_g, ln_b, wpw_bf):
    lp = proj.shape[0]
    rows = BLOCK
    n_chunk = lp // rows
    ca, cg = COL_A // D_CONV, COL_G // D_CONV
    ext = rows + HALO

    def body(ap_ref, a_ref, an_ref, gp_ref, g_ref, gn_ref, do_ref, don_ref, y_ref, yn_ref, wdw_ref, lg_ref, lb_ref,
             wpw_ref, da_ref, dg_ref, dwdw_ref, db_ref, dlg_ref, dlb_ref, dwpw_ref):
        r = pl.program_id(0)

        @pl.when(r == 0)
        def _():
            for ref in (dwdw_ref, db_ref, dlg_ref, dlb_ref, dwpw_ref):
                ref[...] = jnp.zeros(ref.shape, F32)

        a3 = jnp.concatenate([ap_ref[rows - HALO:rows], a_ref[...], an_ref[0:HALO]], axis=0)
        g3 = jnp.concatenate([gp_ref[rows - HALO:rows], g_ref[...], gn_ref[0:HALO]], axis=0)
        sig3 = _sigmoid(g3)
        u3 = a3 * sig3
        u_shifts = _sub_shifts(u3, True)
        y = jnp.concatenate([y_ref[...], yn_ref[0:HALO]], axis=0)
        xhat, rstd = _layernorm_stats(y)
        yn = xhat * lg_ref[...] + lb_ref[...]
        sgm = _sigmoid(yn)
        pos = r * rows + lax.broadcasted_iota(jnp.int32, (ext, 1), 0) - PAD
        valid = (pos >= 0) & (pos < lp - PAD)
        d_ext = jnp.concatenate([do_ref[...], don_ref[0:HALO]], axis=0)
        ds = lax.dot_general(d_ext.astype(BF16), wpw_ref[...], _NT, preferred_element_type=F32)
        dyn = jnp.where(valid, ds * (sgm * (1.0 + yn * (1.0 - sgm))), 0.0)
        dxh = dyn * lg_ref[...]
        dy = rstd * (dxh - jnp.mean(dxh, axis=-1, keepdims=True)
                     - xhat * jnp.mean(dxh * xhat, axis=-1, keepdims=True))
        s_cur = jnp.where(valid[0:rows], (yn * sgm)[0:rows], 0.0)
        dwpw_ref[...] += lax.dot_general(s_cur.astype(BF16), do_ref[...].astype(BF16), _TN,
                                         preferred_element_type=F32)
        dlg_ref[...] += _colsum(dyn[0:rows] * xhat[0:rows])
        dlb_ref[...] += _colsum(dyn[0:rows])
        dy_cur = dy[0:rows]
        db_ref[...] += _colsum(dy_cur)
        dy_shifts = _sub_shifts(dy, False)
        du = None
        for d in range(CONV_WIDTH):
            k = CONV_WIDTH - 1 - d
            dwdw_ref[k:k + 1, :] += _colsum(dy_cur * _lagged(u_shifts, d, HALO, rows, True))
            term = wdw_ref[k:k + 1, :] * _lagged(dy_shifts, d, 0, rows, False)
            du = term if du is None else du + term
        du = jnp.where(pos[0:rows] >= 0, du, 0.0)
        sig = sig3[HALO:HALO + rows]
        da_ref[...] = (du * sig).astype(da_ref.dtype)
        dg_ref[...] = (du * a_ref[...] * sig * (1.0 - sig)).astype(dg_ref.dtype)

    prev = lambda c: (lambda r: (jnp.maximum(r - 1, 0), c))
    cur = lambda c: (lambda r: (r, c))
    nxt = lambda c: (lambda r: (jnp.minimum(r + 1, n_chunk - 1), c))
    const = lambda shape: pl.BlockSpec(shape, lambda r: (0, 0))
    blk = lambda f: pl.BlockSpec((rows, D_CONV), f)
    return _call(body, name="conv_bwd",
                 out_shape=[_sds((lp, D_CONV), BF16), _sds((lp, D_CONV), BF16), _sds((HALO, D_CONV), F32),
                            _sds((1, D_CONV), F32), _sds((1, D_CONV), F32), _sds((1, D_CONV), F32),
                            _sds((D_CONV, D_CONV), F32)],
                 grid=(n_chunk,),
                 in_specs=[blk(prev(ca)), blk(cur(ca)), blk(nxt(ca)), blk(prev(cg)), blk(cur(cg)), blk(nxt(cg)),
                           blk(cur(0)), blk(nxt(0)), blk(cur(0)), blk(nxt(0)),
                           const((HALO, D_CONV)), const((1, D_CONV)), const((1, D_CONV)),
                           const((D_CONV, D_CONV))],
                 out_specs=[blk(cur(0)), blk(cur(0)), const((HALO, D_CONV)), const((1, D_CONV)),
                            const((1, D_CONV)), const((1, D_CONV)), const((D_CONV, D_CONV))],
                 vmem_mb=48)(proj, proj, proj, proj, proj, proj, d_out, d_out, y_conv, y_conv, wdw, ln_g, ln_b, wpw_bf)


_ANY = pl.BlockSpec(memory_space=pl.ANY)


def _place():
    return lax.axis_index("x"), lax.axis_index("y"), lax.axis_index("c")


def _xy_peers(x, y):
    return [(1 - x, y), (x, 1 - y), (1 - x, 1 - y)]


def _exchange_now(name, ex):
    n = ex.n

    def body(*refs):
        ins, outs, sems = refs[:n], refs[n:2 * n], refs[2 * n:]
        ex.run(False, ins, outs, *sems[:3])
        ex.run(True, ins, outs, *sems[:3])
        if ex.split:
            ex.forward(False, outs, *sems[3:])
            ex.forward(True, outs, *sems[3:])

    return _call(body, name=name, out_shape=ex.out_shape(), in_specs=[_ANY] * n, out_specs=[_ANY] * n,
                 scratch=ex.scratch())(*ex.arrs)


def _tail_exchange(name, parts, arrs, small):
    ex = _Exchange("scatter", parts)
    n, m = ex.n, len(arrs)
    flips = [(fx, fy, fc) for fx in (0, 1) for fy in (0, 1) for fc in (0, 1)][1:]

    def body(*refs):
        p_in, a_in, g_in = refs[:n], refs[n:n + m], refs[n + m]
        p_out, a_out, g_out = refs[n + m + 1:2 * n + m + 1], refs[2 * n + m + 1:2 * (n + m) + 1], refs[2 * (n + m) + 1]
        send, recv, local, s_send, s_recv, g_send, g_recv, g_local = refs[2 * (n + m) + 2:]
        x, y, c = _place()
        me = 4 * x + 2 * y + c
        ex.run(False, p_in, p_out, send, recv, local)
        others = [pltpu.make_async_remote_copy(src_ref=a_in[a], dst_ref=a_out[a], send_sem=s_send.at[a],
                                               recv_sem=s_recv.at[a], device_id=(x, y, 1 - c), device_id_type=MESH)
                  for a in range(m)]
        for k, (fx, fy, fc) in enumerate(flips):
            peer = (1 - x if fx else x, 1 - y if fy else y, 1 - c if fc else c)
            others.append(pltpu.make_async_remote_copy(src_ref=g_in, dst_ref=g_out.at[me], send_sem=g_send.at[k],
                                                       recv_sem=g_recv.at[k], device_id=peer, device_id_type=MESH))
        others.append(pltpu.make_async_copy(g_in, g_out.at[me], g_local))
        for cp in others:
            cp.start()
        for cp in others:
            cp.wait()
        ex.run(True, p_in, p_out, send, recv, local)

    dma = pltpu.SemaphoreType.DMA
    res = _call(body, name=name,
                out_shape=ex.out_shape() + [_sds(a.shape, a.dtype) for a in arrs] +
                [_sds((N_DEV,) + small.shape, small.dtype)],
                in_specs=[_ANY] * (n + m + 1), out_specs=[_ANY] * (n + m + 1),
                scratch=ex.scratch() + [dma((m,)), dma((m,)), dma((N_DEV - 1,)), dma((N_DEV - 1,)), dma])(
                    *parts, *arrs, small)
    return list(res[:n]), list(res[n:n + m]), res[n + m]


def _swap_core(name, arrs):
    n = len(arrs)

    def body(*refs):
        ins, outs = refs[:n], refs[n:2 * n]
        send, recv = refs[2 * n:]
        x, y, c = _place()
        remote = []
        for a in range(n):
            cp = pltpu.make_async_remote_copy(src_ref=ins[a], dst_ref=outs[a], send_sem=send.at[a],
                                              recv_sem=recv.at[a], device_id=(x, y, 1 - c), device_id_type=MESH)
            cp.start()
            remote.append(cp)
        for cp in remote:
            cp.wait()

    return _call(body, name=name, out_shape=[_sds(a.shape, a.dtype) for a in arrs],
                 in_specs=[_ANY] * n, out_specs=[_ANY] * n,
                 scratch=[pltpu.SemaphoreType.DMA((n,)), pltpu.SemaphoreType.DMA((n,))])(*arrs)


def _sum_slots(name, stacked, out_dtype=F32):
    s, r, c = stacked.shape
    tr = _pick(r, (256, 128, 64, 8))

    def body(in_ref, o_ref):
        acc = in_ref[0].astype(F32)
        for k in range(1, s):
            acc = acc + in_ref[k].astype(F32)
        o_ref[...] = acc.astype(o_ref.dtype)

    return _call(body, name=name, out_shape=_sds((r, c), out_dtype), grid=(r // tr,),
                 in_specs=[pl.BlockSpec((s, tr, c), lambda i: (0, i, 0))],
                 out_specs=pl.BlockSpec((tr, c), lambda i: (i, 0)))(stacked)


def _sum_slots_layers(name, r0, r1):
    s, r, c = r0.shape
    tr = _pick(r, (256, 128))

    def body(a_ref, b_ref, o_ref):
        def total(ref):
            acc = ref[0].astype(F32)
            for k in range(1, s):
                acc = acc + ref[k].astype(F32)
            return acc

        @pl.when(pl.program_id(0) == 0)
        def _():
            o_ref[...] = total(a_ref)

        @pl.when(pl.program_id(0) == 1)
        def _():
            o_ref[...] = total(b_ref)

    return _call(body, name=name, out_shape=_sds((2, r, c), F32), grid=(2, r // tr),
                 in_specs=[pl.BlockSpec((s, tr, c), lambda l, i: (0, i * (1 - l), 0)),
                           pl.BlockSpec((s, tr, c), lambda l, i: (0, i * l, 0))],
                 out_specs=pl.BlockSpec((None, tr, c), lambda l, i: (l, i, 0)))(r0, r1)


def _adamw_math(w, g, m, v):
    m = ADAM_B1 * m + (1.0 - ADAM_B1) * g
    v = ADAM_B2 * v + (1.0 - ADAM_B2) * (g * g)
    m_hat = m / (1.0 - ADAM_B1 ** ADAM_STEP)
    v_hat = v / (1.0 - ADAM_B2 ** ADAM_STEP)
    delta = -ADAM_LR * (m_hat / (jnp.sqrt(v_hat) + ADAM_EPS) + ADAM_WD * w)
    return delta, m, v


def _adamw(name, w, m, v, g_mine, g_other):
    l, r, c = w.shape
    tr = _pick(r, (256, 128, 64, 8))

    def body(w_ref, m_ref, v_ref, ga_ref, gb_ref, g_ref, d_ref, nm_ref, nv_ref):
        g = ga_ref[...] + gb_ref[...]
        delta, nm, nv = _adamw_math(w_ref[...], g, m_ref[...], v_ref[...])
        g_ref[...] = g
        d_ref[...] = delta
        nm_ref[...] = nm
        nv_ref[...] = nv

    spec = pl.BlockSpec((None, tr, c), lambda li, i: (li, i, 0))
    return _call(body, name=name, out_shape=[_sds(w.shape, F32)] * 4, grid=(l, r // tr),
                 in_specs=[spec] * 5, out_specs=[spec] * 4, vmem_mb=48)(w, m, v, g_mine, g_other)


def _adamw_flat(name, w, m, v, g):
    r, c = w.shape
    tr = _pick(r, (256, 128, 64, 8))

    def body(w_ref, m_ref, v_ref, g_ref, d_ref, nm_ref, nv_ref):
        delta, nm, nv = _adamw_math(w_ref[...], g_ref[...], m_ref[...], v_ref[...])
        d_ref[...] = delta
        nm_ref[...] = nm
        nv_ref[...] = nv

    spec = pl.BlockSpec((tr, c), lambda i: (i, 0))
    return _call(body, name=name, out_shape=[_sds(w.shape, F32)] * 3, grid=(r // tr,),
                 in_specs=[spec] * 4, out_specs=[spec] * 3)(w, m, v, g)


def _pack(arrs, row_multiple=256):
    flat = jnp.concatenate([a.reshape(-1).astype(F32) for a in arrs])
    per = BLOCK * row_multiple
    total = -(-flat.shape[0] // per) * per
    return jnp.pad(flat, (0, total - flat.shape[0])).reshape(total // BLOCK, BLOCK)


def _unpack(buf, shapes):
    flat = buf.reshape(-1)
    outs, off = [], 0
    for s in shapes:
        size = 1
        for d in s:
            size *= d
        outs.append(flat[off:off + size].reshape(s))
        off += size
    return outs


def kernel(x, meta_tokens, pre_mix_g, w_in, w_pool, pool_scale, w_dw, b_dw, conv_ln_g, conv_ln_b, w_pw, mix_out_g, w_out, post_mix_g, pre_ffn_g, w_gate, w_up, w_down, post_ffn_g, loss_target, m_meta_tokens, m_pre_mix_g, m_w_in, m_w_pool, m_pool_scale, m_w_dw, m_b_dw, m_conv_ln_g, m_conv_ln_b, m_w_pw, m_mix_out_g, m_w_out, m_post_mix_g, m_pre_ffn_g, m_w_gate, m_w_up, m_w_down, m_post_ffn_g, v_meta_tokens, v_pre_mix_g, v_w_in, v_w_pool, v_pool_scale, v_w_dw, v_b_dw, v_conv_ln_g, v_conv_ln_b, v_w_pw, v_mix_out_g, v_w_out, v_post_mix_g, v_pre_ffn_g, v_w_gate, v_w_up, v_w_down, v_post_ffn_g):
    seq = x.shape[1]
    lp = PAD + N_META + seq
    depth = w_in.shape[0]
    xy = 2 * lax.axis_index("x") + lax.axis_index("y")

    small_shapes = [meta_tokens.shape, w_dw.shape, w_pw.shape]
    small_local = _pack([meta_tokens, w_dw, w_pw], row_multiple=8)
    big_names = ["w_in", "w_out", "w_gate", "w_up", "w_down"]
    big_local = {(k, l): w[l:l + 1].astype(BF16)
                 for k, w in zip(big_names, (w_in, w_out, w_gate, w_up, w_down)) for l in range(depth)}
    wg = {}
    half = w_in.shape[2] // 2
    first = big_local[("w_in", 0)]
    lo_half, hi_half, small_all = _exchange_now(
        "gather_first", _Exchange("gather", [first[:, :, :half], first[:, :, half:], small_local], split=True))
    wg[("w_in", 0)] = jnp.concatenate([lo_half, hi_half], axis=3)
    metas, wdws, wpws = [], [], []
    for s in range(N_SHARD):
        mt, wd, wp = _unpack(small_all[s], small_shapes)
        metas.append(mt)
        wdws.append(wd)
        wpws.append(wp)
    meta_full = jnp.concatenate(metas, axis=1)
    wdw_full = jnp.concatenate(wdws, axis=2)
    wpw_full = jnp.concatenate(wpws, axis=1)
    wdw_pad = jnp.pad(wdw_full, ((0, 0), (0, HALO - CONV_WIDTH), (0, 0)))
    wpw_bf = wpw_full.astype(BF16)
    wpool_bf = w_pool.reshape(depth, D_POOL, BLOCK).astype(BF16)

    row = lambda a, i: a[i][None, :]

    h = jnp.concatenate([jnp.zeros((PAD, D_MODEL), F32), meta_full, x[0]], axis=0)
    light = dict(tm=_pick(lp, (3 * BLOCK, BLOCK)), vmem_mb=48)
    u = _rowwise("pre_mix_norm0", lambda hh, g: _rms(hh, g), [h], [row(pre_mix_g, 0)], [(D_MODEL, BF16)],
                 **light)[0]
    saved = []
    for i in range(depth):
        proj = _mm_nn_col("in_proj%d" % i, u, wg[("w_in", i)], 0, F32)
        ride = [(k, i) for k in big_names[1:]] + ([("w_in", i + 1)] if i + 1 < depth else [])
        o_attn, tot, got = _attn_fwd(proj, _Exchange("gather", [big_local[k] for k in ride], split=True))
        wg.update(zip(ride, got))
        o_pool = _pool_fwd(proj, wpool_bf[i], row(pool_scale, i))
        o_conv, y_conv = _conv_fwd(proj, wdw_pad[i], row(b_dw, i), row(conv_ln_g, i), row(conv_ln_b, i), wpw_bf[i])

        mix_gains = [row(mix_out_g, i)[:, :D_ATTN], row(mix_out_g, i)[:, D_ATTN:D_ATTN + D_POOL],
                     row(mix_out_g, i)[:, D_ATTN + D_POOL:]]
        merged = _rowwise("merge%d" % i, _merge, [o_attn, o_pool, o_conv], mix_gains, [(D_MODEL, BF16)], **light)[0]
        mix = _mm_nn_row("out_proj%d" % i, merged, wg[("w_out", i)], 0)

        def post_mix(hh, mx, g1, g2):
            h1 = hh + _rms(mx, g1)
            return h1, _rms(h1, g2)

        h1, u2 = _rowwise("post_mix%d" % i, post_mix, [h, mix], [row(post_mix_g, i), row(pre_ffn_g, i)],
                          [(D_MODEL, F32), (D_MODEL, BF16)], **light)
        gate = _mm_nn_col("ffn_gate%d" % i, u2, wg[("w_gate", i)], 0, F32)
        up, act = _mm_nn_col_swiglu("ffn_up%d" % i, u2, wg[("w_up", i)], 0, gate)
        ff = _mm_nn_row("ffn_down%d" % i, act, wg[("w_down", i)], 0)
        rec = dict(h=h, u=u, proj=proj, tot=tot, o_attn=o_attn, o_pool=o_pool, o_conv=o_conv, y_conv=y_conv,
                   merged=merged,
                   mix=mix, h1=h1, u2=u2, gate=gate, up=up, act=act, ff=ff)
        saved.append(rec)
        if i + 1 < depth:
            def post_ffn(hh, f, g1, g2):
                h2 = hh + _rms(f, g1)
                return h2, _rms(h2, g2)

            h, u = _rowwise("post_ffn%d" % i, post_ffn, [h1, ff], [row(post_ffn_g, i), row(pre_mix_g, i + 1)],
                            [(D_MODEL, F32), (D_MODEL, BF16)], **light)
        else:
            def head(row0, hh, f, tgt, g1):
                y = hh + _rms(f, g1)
                rid = row0 + lax.broadcasted_iota(jnp.int32, (y.shape[0], 1), 0)
                err = jnp.where(rid >= PAD + N_META, y - tgt, 0.0)
                part = 0.5 * jnp.sum(jnp.mean(err * err, axis=-1, keepdims=True), axis=0, keepdims=True)
                return err * (1.0 / D_MODEL), jnp.broadcast_to(part, (8, BLOCK))

            dh, loss_part = _rowwise("loss_head", head, [h1, ff, loss_target[0]], [row(post_ffn_g, i)],
                                     [(D_MODEL, F32)], accs=[(8, BLOCK)], with_row0=True, headless={2})

    loss = lax.psum(loss_part[0, 0], ("x", "y", "c"))

    small_grads = {}
    big_parts = {}
    received = {}
    for i in reversed(range(depth)):
        rec = saved[i]

        def post_ffn_b(f, d, g):
            _, vjp = jax.vjp(_rms, f, g)
            df, dg = vjp(d)
            return df, dg

        dff, g_post_ffn = _rowwise("post_ffn_b%d" % i, post_ffn_b, [rec["ff"], dh], [row(post_ffn_g, i)],
                                   [(D_MODEL, BF16)], accs=[(1, D_MODEL)], **light)
        big_parts[("w_down", i)] = _mm_tn_row("dw_down%d" % i, rec["act"], dff)
        dgate, dup = _mm_nt_row_swiglu("d_gate_up%d" % i, dff, wg[("w_down", i)], 0, rec["gate"], rec["up"])
        big_parts[("w_gate", i)] = _mm_tn_col("dw_gate%d" % i, rec["u2"], dgate)
        big_parts[("w_up", i)] = _mm_tn_col("dw_up%d" % i, rec["u2"], dup)
        du2a = _mm_nt_col("d_u2_gate%d" % i, dgate, wg[("w_gate", i)], 0)
        du2b = _mm_nt_col("d_u2_up%d" % i, dup, wg[("w_up", i)], 0)

        def post_mix_b(h1v, mx, d, da, db, g1, g2):
            _, vjp2 = jax.vjp(_rms, h1v, g2)
            dh1, dg2 = vjp2(da + db)
            dmid = d + dh1
            _, vjp1 = jax.vjp(_rms, mx, g1)
            dmx, dg1 = vjp1(dmid)
            return dmid, dmx, dg1, dg2

        dmid, dmix, g_post_mix, g_pre_ffn = _rowwise(
            "post_mix_b%d" % i, post_mix_b, [rec["h1"], rec["mix"], dh, du2a, du2b],
            [row(post_mix_g, i), row(pre_ffn_g, i)], [(D_MODEL, F32), (D_MODEL, BF16)],
            accs=[(1, D_MODEL), (1, D_MODEL)], tm=_pick(lp, (3 * BLOCK, BLOCK)), vmem_mb=60)
        big_parts[("w_out", i)] = _mm_tn_row("dw_out%d" % i, rec["merged"], dmix)
        dmerged = _mm_nt_row("d_merged%d" % i, dmix, wg[("w_out", i)], 0, F32)

        def merge_b(oa, op, oc, d, ga, gp, gc):
            _, vjp = jax.vjp(_merge, oa, op, oc, ga, gp, gc)
            return vjp(d)

        mix_gains = [row(mix_out_g, i)[:, :D_ATTN], row(mix_out_g, i)[:, D_ATTN:D_ATTN + D_POOL],
                     row(mix_out_g, i)[:, D_ATTN + D_POOL:]]
        do_attn, do_pool, do_conv, g_mo_a, g_mo_p, g_mo_c = _rowwise(
            "merge_b%d" % i, merge_b, [rec["o_attn"], rec["o_pool"], rec["o_conv"], dmerged], mix_gains,
            [(D_ATTN, F32), (D_POOL, F32), (D_CONV, F32)], accs=[(1, D_ATTN), (1, D_POOL), (1, D_CONV)], **light)
        g_mix_out = jnp.concatenate([g_mo_a, g_mo_p, g_mo_c], axis=1)
        du_pool, g_w_pool, g_pool_scale = _pool_bwd(rec["proj"], do_pool, wpool_bf[i], row(pool_scale, i))
        da, dgt, g_w_dw, g_b_dw, g_ln_g, g_ln_b, g_w_pw = _conv_bwd(
            rec["proj"], do_conv, rec["y_conv"], wdw_pad[i], row(conv_ln_g, i), row(conv_ln_b, i), wpw_bf[i])
        big_parts[("w_pw", i)] = g_w_pw.reshape(N_SHARD, D_CONV // N_SHARD, D_CONV).astype(BF16)
        ride = [(k, i) for k in big_names[1:] + ["w_pw"]] + ([("w_in", i + 1)] if i + 1 < depth else [])
        dq, dk, dv, got = _attn_bwd(rec["proj"], rec["tot"], do_attn,
                                    _Exchange("scatter", [big_parts[k] for k in ride]))
        received.update(zip(ride, got))
        dproj = jnp.concatenate([dq, dk, dv, du_pool, da, dgt], axis=1)
        big_parts[("w_in", i)] = _mm_tn_col("dw_in%d" % i, rec["u"], dproj)
        du = _mm_nt_col("d_u%d" % i, dproj, wg[("w_in", i)], 0)

        def pre_mix_b(row0, hv, d, dd, g):
            _, vjp = jax.vjp(_rms, hv, g)
            dhh, dg = vjp(dd)
            out = d + dhh
            return out, dg, jnp.where(row0 == 0, out, 0.0)

        dh, g_pre_mix, dh_head = _rowwise("pre_mix_b%d" % i, pre_mix_b, [rec["h"], dmid, du], [row(pre_mix_g, i)],
                                          [(D_MODEL, F32)], accs=[(1, D_MODEL), (BLOCK, D_MODEL)], with_row0=True,
                                          headless={3} if i == 0 else ())
        small_grads[i] = dict(pre_mix_g=g_pre_mix[0], w_pool=g_w_pool, pool_scale=g_pool_scale[0],
                              w_dw=g_w_dw[:CONV_WIDTH], b_dw=g_b_dw[0], conv_ln_g=g_ln_g[0], conv_ln_b=g_ln_b[0],
                              mix_out_g=g_mix_out[0], post_mix_g=g_post_mix[0],
                              pre_ffn_g=g_pre_ffn[0], post_ffn_g=g_post_ffn[0])

    grad_x = dh[None]
    g_meta_part = dh_head[PAD:PAD + N_META]

    rep_names = ["pre_mix_g", "pool_scale", "b_dw", "conv_ln_g", "conv_ln_b", "mix_out_g", "post_mix_g",
                 "pre_ffn_g", "post_ffn_g", "w_pool"]
    stack2 = lambda nme: jnp.stack([small_grads[l][nme] for l in range(depth)])
    small_list = [stack2(nme) for nme in rep_names] + [g_meta_part, stack2("w_dw")]
    small_list[rep_names.index("w_pool")] = small_list[rep_names.index("w_pool")].reshape(w_pool.shape)
    full_shapes = [a.shape for a in small_list]
    packed = _pack(small_list)

    assert depth == 2
    early = big_names[1:] + ["w_pw"]
    plane_sums = {k: _sum_slots_layers("sum_%s" % k, received[(k, 0)], received[(k, 1)]) for k in early}
    last, swapped, small_slots = _tail_exchange("tail_exchange", [big_parts[("w_in", 0)]],
                                                [plane_sums[k] for k in early], packed)
    summed = _sum_slots("sum_small", small_slots)
    full = _unpack(summed, full_shapes)
    rep_grads = dict(zip(rep_names, full[:len(rep_names)]))
    g_meta = lax.dynamic_slice_in_dim(full[-2], xy * meta_tokens.shape[1], meta_tokens.shape[1], axis=1)
    g_w_dw = lax.dynamic_slice_in_dim(full[-1], xy * w_dw.shape[2], w_dw.shape[2], axis=2)

    rep_w = dict(pre_mix_g=pre_mix_g, pool_scale=pool_scale, b_dw=b_dw, conv_ln_g=conv_ln_g, conv_ln_b=conv_ln_b,
                 mix_out_g=mix_out_g, post_mix_g=post_mix_g, pre_ffn_g=pre_ffn_g, post_ffn_g=post_ffn_g,
                 w_pool=w_pool)
    rep_m = dict(pre_mix_g=m_pre_mix_g, pool_scale=m_pool_scale, b_dw=m_b_dw, conv_ln_g=m_conv_ln_g,
                 conv_ln_b=m_conv_ln_b, mix_out_g=m_mix_out_g, post_mix_g=m_post_mix_g, pre_ffn_g=m_pre_ffn_g,
                 post_ffn_g=m_post_ffn_g, w_pool=m_w_pool)
    rep_v = dict(pre_mix_g=v_pre_mix_g, pool_scale=v_pool_scale, b_dw=v_b_dw, conv_ln_g=v_conv_ln_g,
                 conv_ln_b=v_conv_ln_b, mix_out_g=v_mix_out_g, post_mix_g=v_post_mix_g, pre_ffn_g=v_pre_ffn_g,
                 post_ffn_g=v_post_ffn_g, w_pool=v_w_pool)
    sm_names = rep_names + ["meta_tokens", "w_dw"]
    sm_w = [rep_w[k] for k in rep_names] + [meta_tokens, w_dw]
    sm_m = [rep_m[k] for k in rep_names] + [m_meta_tokens, m_w_dw]
    sm_v = [rep_v[k] for k in rep_names] + [v_meta_tokens, v_w_dw]
    sm_g = [rep_grads[k] for k in rep_names] + [g_meta, g_w_dw]
    sm_shapes = [a.shape for a in sm_w]
    sm_delta, sm_nm, sm_nv = _adamw_flat("adamw_small", _pack(sm_w), _pack(sm_m), _pack(sm_v), _pack(sm_g))
    small_out = {}
    for k, g, d, nm, nv in zip(sm_names, sm_g, _unpack(sm_delta, sm_shapes), _unpack(sm_nm, sm_shapes),
                               _unpack(sm_nv, sm_shapes)):
        small_out[k] = (g, d, nm, nv)

    other_sums = dict(zip(early, swapped))
    received[("w_in", 0)] = last[0]
    plane_sums["w_in"] = _sum_slots_layers("sum_w_in", received[("w_in", 0)], received[("w_in", 1)])
    other_sums["w_in"] = _swap_core("swap_core_w_in", [plane_sums["w_in"]])[0]
    big_w = dict(w_in=(w_in, m_w_in, v_w_in), w_out=(w_out, m_w_out, v_w_out), w_gate=(w_gate, m_w_gate, v_w_gate),
                 w_up=(w_up, m_w_up, v_w_up), w_down=(w_down, m_w_down, v_w_down), w_pw=(w_pw, m_w_pw, v_w_pw))
    big_out = {}
    for k in early + ["w_in"]:
        w, m, v = big_w[k]
        big_out[k] = _adamw("adamw_%s" % k, w, m, v, plane_sums[k], other_sums[k])

    order = ["meta_tokens", "pre_mix_g", "w_in", "w_pool", "pool_scale", "w_dw", "b_dw", "conv_ln_g", "conv_ln_b",
             "w_pw", "mix_out_g", "w_out", "post_mix_g", "pre_ffn_g", "w_gate", "w_up", "w_down", "post_ffn_g"]
    res = lambda k: big_out[k] if k in big_out else small_out[k]
    outs = [loss, grad_x]
    for part in range(4):
        outs += [res(k)[part] for k in order]
    return tuple(outs)
```

```python
import functools

import jax
import jax.numpy as jnp
from jax import lax
from jax.experimental import pallas as pl
from jax.experimental.pallas import tpu as pltpu

F32 = jnp.float32
BF16 = jnp.bfloat16

D_MODEL = 2048
N_META = 16
D_ATTN = 1024
D_POOL = 512
D_CONV = 512
POOL_WINDOWS = (2, 4, 8, 16)
CONV_WIDTH = 31
D_IN_PROJ = 3 * D_ATTN + D_POOL + 2 * D_CONV
D_FF = 5632
EPS = 1e-6
BLOCK = 128
PAD = BLOCK - N_META
HALO = 32
N_SHARD = 4
N_DEV = 8
MESH = pl.DeviceIdType.MESH

ADAM_LR = 0.001
ADAM_B1 = 0.9
ADAM_B2 = 0.999
ADAM_EPS = 1e-08
ADAM_WD = 0.01
ADAM_STEP = 10

COL_Q, COL_K, COL_V = 0, D_ATTN, 2 * D_ATTN
COL_POOL = 3 * D_ATTN
COL_A = COL_POOL + D_POOL
COL_G = COL_A + D_CONV


def _call(body, *, name, out_shape, grid=None, in_specs=None, out_specs=None, scratch=(), vmem_mb=None,
          aliases=None):
    params = {}
    if grid is not None:
        params["dimension_semantics"] = ("arbitrary",) * len(grid)
    if vmem_mb is not None:
        params["vmem_limit_bytes"] = vmem_mb << 20
    kw = dict(out_shape=out_shape, name=name, compiler_params=pltpu.CompilerParams(**params))
    if grid is not None:
        kw["grid"] = grid
    if in_specs is not None:
        kw["in_specs"] = in_specs
    if out_specs is not None:
        kw["out_specs"] = out_specs
    if scratch:
        kw["scratch_shapes"] = list(scratch)
    if aliases:
        kw["input_output_aliases"] = dict(aliases)
    return pl.pallas_call(body, **kw)


def _sds(shape, dtype):
    return jax.ShapeDtypeStruct(tuple(shape), dtype)


def _pick(n, candidates):
    for c in candidates:
        if n % c == 0:
            return c
    return n


def _rowwise(name, fn, rows_in, consts, outs, accs=(), tm=BLOCK, with_row0=False, vmem_mb=None, headless=()):
    lp = rows_in[0].shape[0]
    n_in, n_c, n_o = len(rows_in), len(consts), len(outs)
    back = lambda j: (lambda i: (jnp.maximum(i - 1, 0), 0)) if j in headless else (lambda i: (i, 0))
    rows_of = lambda j: lp - tm if j in headless else lp

    def body(*refs):
        vals = [r[...] for r in refs[:n_in + n_c]]
        if with_row0:
            vals = [pl.program_id(0) * tm] + vals
        res = fn(*vals)
        if not isinstance(res, (tuple, list)):
            res = (res,)
        o_refs = refs[n_in + n_c:n_in + n_c + n_o]
        a_refs = refs[n_in + n_c + n_o:]
        for r, v in zip(o_refs, res[:n_o]):
            r[...] = v.astype(r.dtype)
        if a_refs:
            @pl.when(pl.program_id(0) == 0)
            def _():
                for r in a_refs:
                    r[...] = jnp.zeros(r.shape, r.dtype)
            for r, v in zip(a_refs, res[n_o:]):
                r[...] += v.astype(r.dtype)

    in_specs = [pl.BlockSpec((tm, a.shape[1]), back(j)) for j, a in enumerate(rows_in)]
    in_specs += [pl.BlockSpec(c.shape, lambda i: (0, 0)) for c in consts]
    out_specs = [pl.BlockSpec((tm, w), back(n_in + j)) for j, (w, _) in enumerate(outs)]
    out_specs += [pl.BlockSpec(s, lambda i: (0, 0)) for s in accs]
    out_shape = [_sds((rows_of(n_in + j), w), dt) for j, (w, dt) in enumerate(outs)] + [_sds(s, F32) for s in accs]
    res = _call(body, name=name, out_shape=out_shape, grid=(lp // tm,), in_specs=in_specs,
                out_specs=out_specs, vmem_mb=vmem_mb)(*rows_in, *consts)
    return res


def _rms(x, g):
    return x * lax.rsqrt(jnp.mean(x * x, axis=-1, keepdims=True) + EPS) * g


def _merge(oa, op, oc, ga, gp, gc):
    return jnp.concatenate([_rms(oa, ga), _rms(op, gp), _rms(oc, gc)], axis=1)


def _colsum(x):
    return jnp.sum(x, axis=0, keepdims=True)


def _sigmoid(x):
    return 1.0 / (1.0 + jnp.exp(-x))


MM_VMEM_MB = 56


def _mm_tiles(lp):
    return _pick(lp, (1408, 384, 256, 128))


def _mm_nn_col(name, a, wg, layer, out_dtype):
    lp, k = a.shape
    n = wg.shape[3]
    tm = _mm_tiles(lp)

    def body(a_ref, w_ref, o_ref):
        o_ref[...] = jnp.dot(a_ref[...], w_ref[...], preferred_element_type=F32).astype(o_ref.dtype)

    return _call(body, name=name, out_shape=_sds((lp, N_SHARD * n), out_dtype), grid=(N_SHARD, lp // tm),
                 in_specs=[pl.BlockSpec((tm, k), lambda s, i: (i, 0)),
                           pl.BlockSpec((None, None, k, n), lambda s, i: (s, layer, 0, 0))],
                 out_specs=pl.BlockSpec((tm, n), lambda s, i: (i, s)), vmem_mb=MM_VMEM_MB)(a, wg)


def _mm_nn_col_swiglu(name, a, wg, layer, gate):
    lp, k = a.shape
    n = wg.shape[3]
    tm = _pick(lp, (704, 384, 128))

    def body(a_ref, w_ref, g_ref, u_ref, act_ref):
        up = jnp.dot(a_ref[...], w_ref[...], preferred_element_type=F32)
        g = g_ref[...]
        u_ref[...] = up
        act_ref[...] = (g * _sigmoid(g) * up).astype(act_ref.dtype)

    blk = pl.BlockSpec((tm, n), lambda s, i: (i, s))
    return _call(body, name=name, out_shape=[_sds((lp, N_SHARD * n), F32), _sds((lp, N_SHARD * n), BF16)],
                 grid=(N_SHARD, lp // tm),
                 in_specs=[pl.BlockSpec((tm, k), lambda s, i: (i, 0)),
                           pl.BlockSpec((None, None, k, n), lambda s, i: (s, layer, 0, 0)), blk],
                 out_specs=[blk, blk], vmem_mb=MM_VMEM_MB)(a, wg, gate)


def _mm_nn_row(name, a, wg, layer):
    lp = a.shape[0]
    k, n = wg.shape[2], wg.shape[3]
    tm = _mm_tiles(lp)

    def body(a_ref, w_ref, o_ref):
        part = jnp.dot(a_ref[...], w_ref[...], preferred_element_type=F32)

        @pl.when(pl.program_id(1) == 0)
        def _():
            o_ref[...] = part

        @pl.when(pl.program_id(1) != 0)
        def _():
            o_ref[...] += part

    return _call(body, name=name, out_shape=_sds((lp, n), F32), grid=(lp // tm, N_SHARD),
                 in_specs=[pl.BlockSpec((tm, k), lambda i, s: (i, s)),
                           pl.BlockSpec((None, None, k, n), lambda i, s: (s, layer, 0, 0))],
                 out_specs=pl.BlockSpec((tm, n), lambda i, s: (i, 0)), vmem_mb=MM_VMEM_MB)(a, wg)


_NT = (((1,), (1,)), ((), ()))
_TN = (((0,), (0,)), ((), ()))


def _mm_nt_col(name, dy, wg, layer):
    lp = dy.shape[0]
    k, n = wg.shape[2], wg.shape[3]
    tm = _mm_tiles(lp)

    def body(d_ref, w_ref, o_ref):
        part = lax.dot_general(d_ref[...], w_ref[...], _NT, preferred_element_type=F32)

        @pl.when(pl.program_id(1) == 0)
        def _():
            o_ref[...] = part

        @pl.when(pl.program_id(1) != 0)
        def _():
            o_ref[...] += part

    return _call(body, name=name, out_shape=_sds((lp, k), F32), grid=(lp // tm, N_SHARD),
                 in_specs=[pl.BlockSpec((tm, n), lambda i, s: (i, s)),
                           pl.BlockSpec((None, None, k, n), lambda i, s: (s, layer, 0, 0))],
                 out_specs=pl.BlockSpec((tm, k), lambda i, s: (i, 0)), vmem_mb=MM_VMEM_MB)(dy, wg)


def _mm_nt_col_pair(name, dy1, wg1, dy2, wg2, layer):
    lp = dy1.shape[0]
    k, n = wg1.shape[2], wg1.shape[3]
    tm = _pick(lp, (704, 384, 128))
    first = lambda s: jnp.minimum(s, N_SHARD - 1)
    second = lambda s: jnp.maximum(s - N_SHARD, 0)

    def body(d1_ref, w1_ref, d2_ref, w2_ref, o_ref):
        s = pl.program_id(1)

        @pl.when(s == 0)
        def _():
            o_ref[...] = lax.dot_general(d1_ref[...], w1_ref[...], _NT, preferred_element_type=F32)

        @pl.when((s > 0) & (s < N_SHARD))
        def _():
            o_ref[...] += lax.dot_general(d1_ref[...], w1_ref[...], _NT, preferred_element_type=F32)

        @pl.when(s >= N_SHARD)
        def _():
            o_ref[...] += lax.dot_general(d2_ref[...], w2_ref[...], _NT, preferred_element_type=F32)

    return _call(body, name=name, out_shape=_sds((lp, k), F32), grid=(lp // tm, 2 * N_SHARD),
                 in_specs=[pl.BlockSpec((tm, n), lambda i, s: (i, first(s))),
                           pl.BlockSpec((None, None, k, n), lambda i, s: (first(s), layer, 0, 0)),
                           pl.BlockSpec((tm, n), lambda i, s: (i, second(s))),
                           pl.BlockSpec((None, None, k, n), lambda i, s: (second(s), layer, 0, 0))],
                 out_specs=pl.BlockSpec((tm, k), lambda i, s: (i, 0)), vmem_mb=MM_VMEM_MB)(dy1, wg1, dy2, wg2)


def _mm_nt_row(name, dy, wg, layer, out_dtype):
    lp = dy.shape[0]
    k, n = wg.shape[2], wg.shape[3]
    tm = _mm_tiles(lp)

    def body(d_ref, w_ref, o_ref):
        o_ref[...] = lax.dot_general(d_ref[...], w_ref[...], _NT, preferred_element_type=F32).astype(o_ref.dtype)

    return _call(body, name=name, out_shape=_sds((lp, N_SHARD * k), out_dtype), grid=(N_SHARD, lp // tm),
                 in_specs=[pl.BlockSpec((tm, n), lambda s, i: (i, 0)),
                           pl.BlockSpec((None, None, k, n), lambda s, i: (s, layer, 0, 0))],
                 out_specs=pl.BlockSpec((tm, k), lambda s, i: (i, s)), vmem_mb=MM_VMEM_MB)(dy, wg)


def _mm_nt_row_swiglu(name, dy, wg, layer, gate, up):
    lp = dy.shape[0]
    k, n = wg.shape[2], wg.shape[3]
    tm = _pick(lp, (704, 384, 128))

    def body(d_ref, w_ref, g_ref, u_ref, dg_ref, du_ref):
        dact = lax.dot_general(d_ref[...], w_ref[...], _NT, preferred_element_type=F32)
        g = g_ref[...]
        sg = _sigmoid(g)
        dg_ref[...] = (dact * u_ref[...] * (sg * (1.0 + g * (1.0 - sg)))).astype(dg_ref.dtype)
        du_ref[...] = (dact * (g * sg)).astype(du_ref.dtype)

    blk = pl.BlockSpec((tm, k), lambda s, i: (i, s))
    return _call(body, name=name, out_shape=[_sds((lp, N_SHARD * k), BF16)] * 2, grid=(N_SHARD, lp // tm),
                 in_specs=[pl.BlockSpec((tm, n), lambda s, i: (i, 0)),
                           pl.BlockSpec((None, None, k, n), lambda s, i: (s, layer, 0, 0)), blk, blk],
                 out_specs=[blk, blk], vmem_mb=MM_VMEM_MB)(dy, wg, gate, up)


def _mm_tn_col(name, a, dy):
    lp, k = a.shape
    n = dy.shape[1] // N_SHARD
    tm = _mm_tiles(lp)
    tk = _pick(k, (1024, 512))

    def body(a_ref, d_ref, o_ref, acc):
        @pl.when(pl.program_id(2) == 0)
        def _():
            acc[...] = jnp.zeros(acc.shape, F32)

        acc[...] += lax.dot_general(a_ref[...], d_ref[...], _TN, preferred_element_type=F32)

        @pl.when(pl.program_id(2) == pl.num_programs(2) - 1)
        def _():
            o_ref[...] = acc[...].astype(o_ref.dtype)

    return _call(body, name=name, out_shape=_sds((N_SHARD, k, n), BF16), grid=(N_SHARD, k // tk, lp // tm),
                 in_specs=[pl.BlockSpec((tm, tk), lambda s, kk, i: (i, kk)),
                           pl.BlockSpec((tm, n), lambda s, kk, i: (i, s))],
                 out_specs=pl.BlockSpec((None, tk, n), lambda s, kk, i: (s, kk, 0)),
                 scratch=[pltpu.VMEM((tk, n), F32)], vmem_mb=MM_VMEM_MB)(a, dy)


def _mm_tn_row(name, a, dy):
    lp = a.shape[0]
    k = a.shape[1] // N_SHARD
    n = dy.shape[1]
    tm = _mm_tiles(lp)
    tn = _pick(n, (1024, 512))

    def body(a_ref, d_ref, o_ref, acc):
        @pl.when(pl.program_id(2) == 0)
        def _():
            acc[...] = jnp.zeros(acc.shape, F32)

        acc[...] += lax.dot_general(a_ref[...], d_ref[...], _TN, preferred_element_type=F32)

        @pl.when(pl.program_id(2) == pl.num_programs(2) - 1)
        def _():
            o_ref[...] = acc[...].astype(o_ref.dtype)

    return _call(body, name=name, out_shape=_sds((N_SHARD, k, n), BF16), grid=(N_SHARD, n // tn, lp // tm),
                 in_specs=[pl.BlockSpec((tm, k), lambda s, j, i: (i, s)),
                           pl.BlockSpec((tm, tn), lambda s, j, i: (i, j))],
                 out_specs=pl.BlockSpec((None, k, tn), lambda s, j, i: (s, 0, j)),
                 scratch=[pltpu.VMEM((k, tn), F32)], vmem_mb=MM_VMEM_MB)(a, dy)


SUB = 16
WIDE = 2 * BLOCK
Z_CLAMP = 20.0


def _log1m_sigmoid(z):
    return -jnp.where(z > Z_CLAMP, z, jnp.log(1.0 + jnp.exp(jnp.minimum(z, Z_CLAMP))))


def _tri(tk, kind):
    r = lax.broadcasted_iota(jnp.int32, (tk, tk), 0)
    c = lax.broadcasted_iota(jnp.int32, (tk, tk), 1)
    t = {"gt": r > c, "le": r <= c}[kind]
    return jnp.where(t, 1.0, 0.0).astype(BF16)


def _strip_mask(kind, s, tk):
    if kind == "none":
        return None
    col = lax.broadcasted_iota(jnp.int32, (SUB, tk), 1)
    row = lax.broadcasted_iota(jnp.int32, (SUB, tk), 0)
    causal = (col - row) < s * SUB
    if kind == "diag":
        return causal
    if kind == "pad":
        return col >= PAD
    return causal & (col >= PAD)


def _attn_blocks(lp):
    nb = lp // BLOCK
    assert nb % 2 == 1, "sequence must be a 128-row block plus whole 256-row blocks"
    return nb, (nb + 1) // 2


class _Exchange:
    def __init__(self, kind, arrs, split=False):
        self.kind, self.arrs, self.n, self.split = kind, list(arrs), len(arrs), split

    def out_shape(self):
        if self.kind == "gather":
            return [_sds((N_SHARD,) + a.shape, a.dtype) for a in self.arrs]
        return [_sds(a.shape, a.dtype) for a in self.arrs]

    def scratch(self):
        sems = [pltpu.SemaphoreType.DMA((3 * self.n,)), pltpu.SemaphoreType.DMA((3 * self.n,)),
                pltpu.SemaphoreType.DMA((self.n,))]
        if self.split:
            sems += [pltpu.SemaphoreType.DMA((self.n,)), pltpu.SemaphoreType.DMA((self.n,))]
        return sems

    def forward(self, wait, outs, fsend, frecv):
        x, y, c = _place()
        for a in range(self.n):
            cp = pltpu.make_async_remote_copy(src_ref=outs[a], dst_ref=outs[a], send_sem=fsend.at[a],
                                              recv_sem=frecv.at[a], device_id=(x, y, 1 - c), device_id_type=MESH)
            if wait:
                pl.when(c == a % 2)(cp.wait_send)
                pl.when(c != a % 2)(cp.wait_recv)
            else:
                pl.when(c == a % 2)(cp.start)

    def copies(self, a, ins, outs, send, recv, local):
        x, y, c = _place()
        me = 2 * x + y
        own = ins[a] if self.kind == "gather" else ins[a].at[me]
        out = [pltpu.make_async_copy(own, outs[a].at[me], local.at[a])]
        for r, (px, py) in enumerate(_xy_peers(x, y)):
            src = ins[a] if self.kind == "gather" else ins[a].at[2 * px + py]
            out.append(pltpu.make_async_remote_copy(
                src_ref=src, dst_ref=outs[a].at[me], send_sem=send.at[3 * a + r], recv_sem=recv.at[3 * a + r],
                device_id=(px, py, c), device_id_type=MESH))
        return out

    def run(self, wait, ins, outs, send, recv, local):
        for a in range(self.n):
            def go(a=a):
                for cp in self.copies(a, ins, outs, send, recv, local):
                    if wait:
                        cp.wait()
                    else:
                        cp.start()
            if self.split:
                pl.when(lax.axis_index("c") == a % 2)(go)
            else:
                go()


def _attn_fwd(proj, exchange=None):
    lp = proj.shape[0]
    nb, nq = _attn_blocks(lp)
    n_pair = D_ATTN // BLOCK
    n_x = exchange.n if exchange else 0

    def body(*refs):
        q_ref, k_ref, v_ref = refs[:3]
        x_in = refs[3:3 + n_x]
        o_ref, tot_ref = refs[3 + n_x:5 + n_x]
        x_out = refs[5 + n_x:5 + 2 * n_x]
        qs, kb, vh, tri_l, tri_s, z_s, hl_s, c_s, w_s, r_s, acc_s = refs[5 + 2 * n_x:16 + 2 * n_x]
        x_sem = refs[16 + 2 * n_x:]
        p, i = pl.program_id(0), pl.program_id(1)
        m0 = lax.broadcasted_iota(jnp.int32, (1, BLOCK), 1) < (BLOCK // 2)

        if exchange:
            @pl.when((p == 0) & (i == 0))
            def _():
                exchange.run(False, x_in, x_out, *x_sem[:3])

            if exchange.split:
                @pl.when((p == n_pair - 2) & (i == 0))
                def _():
                    exchange.run(True, x_in, x_out, *x_sem[:3])
                    exchange.forward(False, x_out, *x_sem[3:])

        @pl.when(i == 0)
        def _():
            tri_l[...] = _tri(WIDE, "gt")
            tri_s[...] = _tri(BLOCK, "gt")

            def prep(b, carry):
                rows = pl.ds(pl.multiple_of(b * BLOCK, BLOCK), BLOCK)
                q = q_ref[rows, :] * 0.125
                v = v_ref[rows, :]
                qs[0, rows, :] = jnp.where(m0, q, 0.0).astype(BF16)
                qs[1, rows, :] = jnp.where(m0, 0.0, q).astype(BF16)
                kb[rows, :] = k_ref[rows, :].astype(BF16)
                vh[0, rows, :] = jnp.where(m0, v, 0.0).astype(BF16)
                vh[1, rows, :] = jnp.where(m0, 0.0, v).astype(BF16)
                return carry

            lax.fori_loop(0, nb, prep, 0)

        def tiles(q0, tq, specs):
            heads = [(t, h) for t in range(len(specs)) for h in range(2)]
            strips = [slice(s * SUB, (s + 1) * SUB) for s in range(tq // SUB)]
            for t, h in heads:
                k0, tk, _ = specs[t]
                z_s[t, h, 0:tq, 0:tk] = lax.dot_general(qs[h, pl.ds(q0, tq), :], kb[pl.ds(k0, tk), :], _NT,
                                                        preferred_element_type=F32)
            for t, h in heads:
                _, tk, kind = specs[t]
                for s, rows in enumerate(strips):
                    z = z_s[t, h, rows, 0:tk]
                    lnb = _log1m_sigmoid(z)
                    m = _strip_mask(kind, s, tk)
                    if m is not None:
                        lnb = jnp.where(m, lnb, 0.0)
                    z_s[t, h, rows, 0:tk] = z + lnb
                    hl_s[t, h, rows, 0:tk] = lnb.astype(BF16)
            for t, h in heads:
                _, tk, _ = specs[t]
                tri = tri_l if tk == WIDE else tri_s
                c_s[t, h, 0:tq, 0:tk] = jnp.dot(hl_s[t, h, 0:tq, 0:tk], tri[...], preferred_element_type=F32)
            for t, h in heads:
                _, tk, kind = specs[t]
                for s, rows in enumerate(strips):
                    r = r_s[h, rows, :]
                    c = c_s[t, h, rows, 0:tk]
                    rr = r if tk == BLOCK else jnp.concatenate([r, r], axis=1)
                    w = jnp.exp(z_s[t, h, rows, 0:tk] + c + rr)
                    m = _strip_mask(kind, s, tk)
                    if m is not None:
                        w = jnp.where(m, w, 0.0)
                    w_s[t, h, rows, 0:tk] = w.astype(BF16)
                    total = c[:, 0:1] + hl_s[t, h, rows, 0:BLOCK].astype(F32)[:, 0:1]
                    r_s[h, rows, :] = r + jnp.broadcast_to(total, (SUB, BLOCK))
            upd = None
            for t, h in heads:
                k0, tk, _ = specs[t]
                d = jnp.dot(w_s[t, h, 0:tq, 0:tk], vh[h, pl.ds(k0, tk), :], preferred_element_type=F32)
                upd = d if upd is None else upd + d
            acc_s[0:tq, :] += upd

        def finish(q0, tq):
            o_ref[pl.ds(q0, tq), :] = acc_s[0:tq, :]
            tot_ref[pl.ds(q0, tq), :] = jnp.where(m0, r_s[0, 0:tq, :], r_s[1, 0:tq, :])

        r_s[...] = jnp.zeros(r_s.shape, F32)
        acc_s[...] = jnp.zeros(acc_s.shape, F32)

        @pl.when(i == 0)
        def _():
            tiles(0, BLOCK, [(0, BLOCK, "first")])
            finish(0, BLOCK)

        @pl.when(i > 0)
        def _():
            q0 = pl.multiple_of(i * WIDE - BLOCK, BLOCK)
            full = lambda j: (pl.multiple_of(j * WIDE - BLOCK, BLOCK), WIDE, "none")
            diag, meta = (q0, WIDE, "diag"), (0, BLOCK, "pad")

            @pl.when(i == 1)
            def _():
                tiles(q0, WIDE, [diag, meta])

            @pl.when(i >= 2)
            def _():
                tiles(q0, WIDE, [diag, full(i - 1)])

                def inner(n, carry):
                    j = i - 2 - 2 * n
                    tiles(q0, WIDE, [full(j), full(j - 1)])
                    return carry

                lax.fori_loop(0, (i - 2) // 2, inner, 0)

                @pl.when(i % 2 == 1)
                def _():
                    tiles(q0, WIDE, [full(1), meta])

                @pl.when(i % 2 == 0)
                def _():
                    tiles(q0, WIDE, [meta])

            finish(q0, WIDE)

        if exchange:
            @pl.when((p == n_pair - 1) & (i == nq - 1))
            def _():
                if exchange.split:
                    exchange.forward(True, x_out, *x_sem[3:])
                else:
                    exchange.run(True, x_in, x_out, *x_sem)

    cq, ck, cv = COL_Q // BLOCK, COL_K // BLOCK, COL_V // BLOCK
    col = lambda c0: (lambda p, i: (0, c0 + p))
    scratch = [pltpu.VMEM((2, lp, BLOCK), BF16), pltpu.VMEM((lp, BLOCK), BF16), pltpu.VMEM((2, lp, BLOCK), BF16),
               pltpu.VMEM((WIDE, WIDE), BF16), pltpu.VMEM((BLOCK, BLOCK), BF16),
               pltpu.VMEM((2, 2, WIDE, WIDE), F32), pltpu.VMEM((2, 2, WIDE, WIDE), BF16),
               pltpu.VMEM((2, 2, WIDE, WIDE), F32), pltpu.VMEM((2, 2, WIDE, WIDE), BF16),
               pltpu.VMEM((2, WIDE, BLOCK), F32), pltpu.VMEM((WIDE, BLOCK), F32)]
    res = _call(body, name="attn_fwd",
                out_shape=[_sds((lp, D_ATTN), F32), _sds((lp, D_ATTN), F32)] + (exchange.out_shape() if exchange else []),
                grid=(n_pair, nq),
                in_specs=[pl.BlockSpec((lp, BLOCK), col(cq)), pl.BlockSpec((lp, BLOCK), col(ck)),
                          pl.BlockSpec((lp, BLOCK), col(cv))] + [_ANY] * n_x,
                out_specs=[pl.BlockSpec((lp, BLOCK), col(0)), pl.BlockSpec((lp, BLOCK), col(0))] + [_ANY] * n_x,
                scratch=scratch + (exchange.scratch() if exchange else []),
                vmem_mb=56)(proj, proj, proj, *(exchange.arrs if exchange else []))
    return res[0], res[1], list(res[2:])


def _attn_bwd(proj, tot, d_out, exchange=None):
    lp = proj.shape[0]
    nb, nq = _attn_blocks(lp)
    n_pair = D_ATTN // BLOCK
    n_x = exchange.n if exchange else 0
    n_s = 23

    def body(*refs):
        q_ref, k_ref, v_ref, tot_ref, do_ref = refs[:5]
        x_in = refs[5:5 + n_x]
        dq_ref, dk_ref, dv_ref = refs[5 + n_x:8 + n_x]
        x_out = refs[8 + n_x:8 + 2 * n_x]
        (qs, kb, kh, vb, doh, tge_l, tge_s, tle_l, tle_s, z_s, g_s, hl_s, c_s, gl_s, gc_s, w_s, dz_s,
         tot_s, a_s, b_s, dq_acc, dk_acc, dv_acc) = refs[8 + 2 * n_x:8 + 2 * n_x + n_s]
        x_sem = refs[8 + 2 * n_x + n_s:]
        p, i = pl.program_id(0), pl.program_id(1)
        m0 = lax.broadcasted_iota(jnp.int32, (1, BLOCK), 1) < (BLOCK // 2)

        if exchange:
            @pl.when((p == 0) & (i == 0))
            def _():
                exchange.run(False, x_in, x_out, *x_sem)

        @pl.when(i == 0)
        def _():
            tge_l[...] = _tri(WIDE, "gt")
            tge_s[...] = _tri(BLOCK, "gt")
            tle_l[...] = _tri(WIDE, "le")
            tle_s[...] = _tri(BLOCK, "le")

            def prep(b, carry):
                rows = pl.ds(pl.multiple_of(b * BLOCK, BLOCK), BLOCK)
                q = q_ref[rows, :] * 0.125
                k = k_ref[rows, :]
                do = do_ref[rows, :]
                qs[0, rows, :] = jnp.where(m0, q, 0.0).astype(BF16)
                qs[1, rows, :] = jnp.where(m0, 0.0, q).astype(BF16)
                kb[rows, :] = k.astype(BF16)
                kh[0, rows, :] = jnp.where(m0, k, 0.0).astype(BF16)
                kh[1, rows, :] = jnp.where(m0, 0.0, k).astype(BF16)
                vb[rows, :] = v_ref[rows, :].astype(BF16)
                doh[0, rows, :] = jnp.where(m0, do, 0.0).astype(BF16)
                doh[1, rows, :] = jnp.where(m0, 0.0, do).astype(BF16)
                dk_acc[rows, :] = jnp.zeros((BLOCK, BLOCK), F32)
                dv_acc[rows, :] = jnp.zeros((BLOCK, BLOCK), F32)
                return carry

            lax.fori_loop(0, nb, prep, 0)

        def wide(x, tk):
            return x if tk == BLOCK else jnp.concatenate([x, x], axis=1)

        def tiles(q0, tq, specs):
            heads = [(t, h) for t in range(len(specs)) for h in range(2)]
            strips = [slice(s * SUB, (s + 1) * SUB) for s in range(tq // SUB)]
            for t, h in heads:
                k0, tk, _ = specs[t]
                z_s[t, h, 0:tq, 0:tk] = lax.dot_general(qs[h, pl.ds(q0, tq), :], kb[pl.ds(k0, tk), :], _NT,
                                                        preferred_element_type=F32)
                g_s[t, h, 0:tq, 0:tk] = lax.dot_general(doh[h, pl.ds(q0, tq), :], vb[pl.ds(k0, tk), :], _NT,
                                                        preferred_element_type=F32)
            for t, h in heads:
                _, tk, kind = specs[t]
                for s, rows in enumerate(strips):
                    z = z_s[t, h, rows, 0:tk]
                    lnb = _log1m_sigmoid(z)
                    m = _strip_mask(kind, s, tk)
                    if m is not None:
                        lnb = jnp.where(m, lnb, 0.0)
                    z_s[t, h, rows, 0:tk] = z + lnb
                    hl_s[t, h, rows, 0:tk] = lnb.astype(BF16)
            for t, h in heads:
                _, tk, _ = specs[t]
                tri = tge_l if tk == WIDE else tge_s
                c_s[t, h, 0:tq, 0:tk] = jnp.dot(hl_s[t, h, 0:tq, 0:tk], tri[...], preferred_element_type=F32)
            for t, h in heads:
                _, tk, kind = specs[t]
                for s, rows in enumerate(strips):
                    c = c_s[t, h, rows, 0:tk]
                    total = c[:, 0:1] + hl_s[t, h, rows, 0:BLOCK].astype(F32)[:, 0:1]
                    a_next = a_s[h, rows, :] + jnp.broadcast_to(total, (SUB, BLOCK))
                    a_s[h, rows, :] = a_next
                    w = jnp.exp(z_s[t, h, rows, 0:tk] + c + wide(tot_s[h, rows, :] - a_next, tk))
                    m = _strip_mask(kind, s, tk)
                    if m is not None:
                        w = jnp.where(m, w, 0.0)
                    g = w * g_s[t, h, rows, 0:tk]
                    g_s[t, h, rows, 0:tk] = g
                    gl_s[t, h, rows, 0:tk] = g.astype(BF16)
                    w_s[t, h, rows, 0:tk] = w.astype(BF16)
            for t, h in heads:
                _, tk, _ = specs[t]
                tri = tle_l if tk == WIDE else tle_s
                gc_s[t, h, 0:tq, 0:tk] = jnp.dot(gl_s[t, h, 0:tq, 0:tk], tri[...], preferred_element_type=F32)
            for t, h in heads:
                _, tk, kind = specs[t]
                for s, rows in enumerate(strips):
                    gc = gc_s[t, h, rows, 0:tk]
                    b = b_s[h, rows, :]
                    sig = jnp.exp(z_s[t, h, rows, 0:tk])
                    dz = g_s[t, h, rows, 0:tk] - sig * (gc + wide(b, tk))
                    m = _strip_mask(kind, s, tk)
                    if m is not None:
                        dz = jnp.where(m, dz, 0.0)
                    dz_s[t, h, rows, 0:tk] = dz.astype(BF16)
                    b_s[h, rows, :] = b + jnp.broadcast_to(gc[:, tk - 1:tk], (SUB, BLOCK))
            upd = None
            for t, h in heads:
                k0, tk, _ = specs[t]
                d = jnp.dot(dz_s[t, h, 0:tq, 0:tk], kh[h, pl.ds(k0, tk), :], preferred_element_type=F32)
                upd = d if upd is None else upd + d
            dq_acc[0:tq, :] += upd
            for t, (k0, tk, _) in enumerate(specs):
                dk_acc[pl.ds(k0, tk), :] += (
                    lax.dot_general(dz_s[t, 0, 0:tq, 0:tk], qs[0, pl.ds(q0, tq), :], _TN, preferred_element_type=F32) +
                    lax.dot_general(dz_s[t, 1, 0:tq, 0:tk], qs[1, pl.ds(q0, tq), :], _TN, preferred_element_type=F32))
                dv_acc[pl.ds(k0, tk), :] += (
                    lax.dot_general(w_s[t, 0, 0:tq, 0:tk], doh[0, pl.ds(q0, tq), :], _TN, preferred_element_type=F32) +
                    lax.dot_general(w_s[t, 1, 0:tq, 0:tk], doh[1, pl.ds(q0, tq), :], _TN, preferred_element_type=F32))

        def start(q0, tq):
            tv = tot_ref[pl.ds(q0, tq), :]
            tot_s[0, 0:tq, :] = jnp.broadcast_to(tv[:, 0:1], (tq, BLOCK))
            tot_s[1, 0:tq, :] = jnp.broadcast_to(tv[:, BLOCK - 1:BLOCK], (tq, BLOCK))
            a_s[...] = jnp.zeros(a_s.shape, F32)
            b_s[...] = jnp.zeros(b_s.shape, F32)
            dq_acc[...] = jnp.zeros(dq_acc.shape, F32)

        def finish(q0, tq):
            dq_ref[pl.ds(q0, tq), :] = (dq_acc[0:tq, :] * 0.125).astype(dq_ref.dtype)

        @pl.when(i == 0)
        def _():
            start(0, BLOCK)
            tiles(0, BLOCK, [(0, BLOCK, "first")])
            finish(0, BLOCK)

        @pl.when(i > 0)
        def _():
            q0 = pl.multiple_of(i * WIDE - BLOCK, BLOCK)
            full = lambda j: (pl.multiple_of(j * WIDE - BLOCK, BLOCK), WIDE, "none")
            diag, meta = (q0, WIDE, "diag"), (0, BLOCK, "pad")
            start(q0, WIDE)

            @pl.when(i == 1)
            def _():
                tiles(q0, WIDE, [meta, diag])

            @pl.when(i >= 2)
            def _():
                odd = i % 2

                @pl.when(odd == 1)
                def _():
                    tiles(q0, WIDE, [meta, full(1)])

                @pl.when(odd == 0)
                def _():
                    tiles(q0, WIDE, [meta])

                def inner(n, carry):
                    j = 1 + odd + 2 * n
                    tiles(q0, WIDE, [full(j), full(j + 1)])
                    return carry

                lax.fori_loop(0, (i - 2) // 2, inner, 0)
                tiles(q0, WIDE, [full(i - 1), diag])

            finish(q0, WIDE)

        @pl.when(i == nq - 1)
        def _():
            dk_ref[...] = dk_acc[...].astype(dk_ref.dtype)
            dv_ref[...] = dv_acc[...].astype(dv_ref.dtype)

        if exchange:
            @pl.when((p == n_pair - 1) & (i == nq - 1))
            def _():
                exchange.run(True, x_in, x_out, *x_sem)

    cq, ck, cv = COL_Q // BLOCK, COL_K // BLOCK, COL_V // BLOCK
    col = lambda c0: (lambda p, i: (0, c0 + p))
    whole = lambda c0: pl.BlockSpec((lp, BLOCK), col(c0))
    tile4 = lambda w, dt: pltpu.VMEM((2, 2, WIDE, w), dt)
    scratch = [pltpu.VMEM((2, lp, BLOCK), BF16), pltpu.VMEM((lp, BLOCK), BF16), pltpu.VMEM((2, lp, BLOCK), BF16),
               pltpu.VMEM((lp, BLOCK), BF16), pltpu.VMEM((2, lp, BLOCK), BF16),
               pltpu.VMEM((WIDE, WIDE), BF16), pltpu.VMEM((BLOCK, BLOCK), BF16),
               pltpu.VMEM((WIDE, WIDE), BF16), pltpu.VMEM((BLOCK, BLOCK), BF16),
               tile4(WIDE, F32), tile4(WIDE, F32), tile4(WIDE, BF16), tile4(WIDE, F32),
               tile4(WIDE, BF16), tile4(WIDE, F32), tile4(WIDE, BF16), tile4(WIDE, BF16),
               pltpu.VMEM((2, WIDE, BLOCK), F32), pltpu.VMEM((2, WIDE, BLOCK), F32), pltpu.VMEM((2, WIDE, BLOCK), F32),
               pltpu.VMEM((WIDE, BLOCK), F32), pltpu.VMEM((lp, BLOCK), F32), pltpu.VMEM((lp, BLOCK), F32)]
    res = _call(body, name="attn_bwd",
                out_shape=[_sds((lp, D_ATTN), BF16)] * 3 + (exchange.out_shape() if exchange else []),
                grid=(n_pair, nq),
                in_specs=[whole(cq), whole(ck), whole(cv), whole(0), whole(0)] + [_ANY] * n_x,
                out_specs=[whole(0), whole(0), whole(0)] + [_ANY] * n_x,
                scratch=scratch + (exchange.scratch() if exchange else []),
                vmem_mb=60)(proj, proj, proj, tot, d_out, *(exchange.arrs if exchange else []))
    return res[0], res[1], res[2], list(res[3:])


def _shift_down(x, d):
    return x if d == 0 else pltpu.roll(x, d, 0)


def _shift_up(x, d):
    return x if d == 0 else pltpu.roll(x, x.shape[0] - d, 0)


def _pool_windows(ext, down):
    shift = _shift_down if down else _shift_up
    outs = []
    for g, w in enumerate(POOL_WINDOWS):
        s = ext[:, g * BLOCK:(g + 1) * BLOCK]
        d = 1
        while d < w:
            s = s + shift(s, d)
            d *= 2
        outs.append(s)
    return jnp.concatenate(outs, axis=1)


def _pool_counts(pos):
    cols = [jnp.broadcast_to(jnp.clip(pos + 1, 1, w).astype(F32), (pos.shape[0], BLOCK)) for w in POOL_WINDOWS]
    return jnp.concatenate(cols, axis=1)


def _group_dot(x, w_ref, transpose):
    outs = []
    for g in range(len(POOL_WINDOWS)):
        xg = x[:, g * BLOCK:(g + 1) * BLOCK].astype(BF16)
        wg = w_ref[g * BLOCK:(g + 1) * BLOCK, :]
        if transpose:
            outs.append(lax.dot_general(xg, wg, _NT, preferred_element_type=F32))
        else:
            outs.append(jnp.dot(xg, wg, preferred_element_type=F32))
    return jnp.concatenate(outs, axis=1)


def _pooled(prev, cur, r):
    rows = cur.shape[0]
    ext = jnp.concatenate([prev[rows - HALO:], cur], axis=0)
    pos = r * rows + lax.broadcasted_iota(jnp.int32, (rows, 1), 0) - PAD
    ws = _pool_windows(ext, down=True)[HALO:]
    return jnp.where(pos >= 0, ws / _pool_counts(pos) - cur, 0.0)


def _pool_fwd(proj, w_pool_bf, scale):
    lp = proj.shape[0]
    rows = BLOCK
    cb = COL_POOL // D_POOL

    def body(prev_ref, cur_ref, w_ref, s_ref, o_ref):
        pooled = _pooled(prev_ref[...], cur_ref[...], pl.program_id(0))
        o_ref[...] = _group_dot(pooled, w_ref, False) * s_ref[...]

    return _call(body, name="pool_fwd", out_shape=_sds((lp, D_POOL), F32), grid=(lp // rows,),
                 in_specs=[pl.BlockSpec((rows, D_POOL), lambda r: (jnp.maximum(r - 1, 0), cb)),
                           pl.BlockSpec((rows, D_POOL), lambda r: (r, cb)),
                           pl.BlockSpec((D_POOL, BLOCK), lambda r: (0, 0)),
                           pl.BlockSpec((1, D_POOL), lambda r: (0, 0))],
                 out_specs=pl.BlockSpec((rows, D_POOL), lambda r: (r, 0)))(proj, proj, w_pool_bf, scale)


def _pool_bwd(proj, d_out, w_pool_bf, scale):
    lp = proj.shape[0]
    rows = BLOCK
    n_chunk = lp // rows
    cb = COL_POOL // D_POOL

    def body(prev_ref, cur_ref, do_ref, don_ref, w_ref, s_ref, du_ref, dw_ref, ds_ref):
        r = pl.program_id(0)

        @pl.when(r == 0)
        def _():
            dw_ref[...] = jnp.zeros(dw_ref.shape, F32)
            ds_ref[...] = jnp.zeros(ds_ref.shape, F32)

        pooled = _pooled(prev_ref[...], cur_ref[...], r)
        d_ext = jnp.concatenate([do_ref[...], don_ref[0:HALO]], axis=0)
        pos = r * rows + lax.broadcasted_iota(jnp.int32, (rows + HALO, 1), 0) - PAD
        dmixed = jnp.where((pos >= 0) & (pos < lp - PAD), d_ext * s_ref[...], 0.0)
        dpooled = _group_dot(dmixed, w_ref, True)
        back = _pool_windows(dpooled / _pool_counts(pos), down=False)[0:rows]
        du = jnp.where(pos[0:rows] >= 0, back - dpooled[0:rows], 0.0)
        du_ref[...] = du.astype(du_ref.dtype)
        mixed = _group_dot(pooled, w_ref, False)
        ds_ref[...] += _colsum(do_ref[...] * mixed)
        pooled_bf = pooled.astype(BF16)
        dm_bf = dmixed[0:rows].astype(BF16)
        for g in range(len(POOL_WINDOWS)):
            sl = slice(g * BLOCK, (g + 1) * BLOCK)
            dw_ref[sl, :] += lax.dot_general(pooled_bf[:, sl], dm_bf[:, sl], _TN, preferred_element_type=F32)

    return _call(body, name="pool_bwd",
                 out_shape=[_sds((lp, D_POOL), BF16), _sds((D_POOL, BLOCK), F32), _sds((1, D_POOL), F32)],
                 grid=(n_chunk,),
                 in_specs=[pl.BlockSpec((rows, D_POOL), lambda r: (jnp.maximum(r - 1, 0), cb)),
                           pl.BlockSpec((rows, D_POOL), lambda r: (r, cb)),
                           pl.BlockSpec((rows, D_POOL), lambda r: (r, 0)),
                           pl.BlockSpec((rows, D_POOL), lambda r: (jnp.minimum(r + 1, n_chunk - 1), 0)),
                           pl.BlockSpec((D_POOL, BLOCK), lambda r: (0, 0)),
                           pl.BlockSpec((1, D_POOL), lambda r: (0, 0))],
                 out_specs=[pl.BlockSpec((rows, D_POOL), lambda r: (r, 0)),
                            pl.BlockSpec((D_POOL, BLOCK), lambda r: (0, 0)),
                            pl.BlockSpec((1, D_POOL), lambda r: (0, 0))])(proj, proj, d_out, d_out, w_pool_bf, scale)


SUBLANES = 8


def _sub_shifts(x, down):
    shift = _shift_down if down else _shift_up
    return [shift(x, b) for b in range(SUBLANES)]


def _lagged(shifts, d, lo, n, down):
    a, b = divmod(d, SUBLANES)
    start = lo - SUBLANES * a if down else lo + SUBLANES * a
    return shifts[b][start:start + n]


def _conv_taps(u_shifts, wdw_ref, lo, n):
    y = None
    for d in range(CONV_WIDTH):
        term = wdw_ref[CONV_WIDTH - 1 - d:CONV_WIDTH - d, :] * _lagged(u_shifts, d, lo, n, True)
        y = term if y is None else y + term
    return y


def _layernorm_stats(y):
    mu = jnp.mean(y, axis=-1, keepdims=True)
    yc = y - mu
    rstd = lax.rsqrt(jnp.mean(yc * yc, axis=-1, keepdims=True) + EPS)
    return yc * rstd, rstd


def _conv_fwd(proj, wdw, bdw, ln_g, ln_b, wpw_bf):
    lp = proj.shape[0]
    rows = BLOCK
    ca, cg = COL_A // D_CONV, COL_G // D_CONV

    def body(ap_ref, a_ref, gp_ref, g_ref, wdw_ref, b_ref, lg_ref, lb_ref, wpw_ref, o_ref, y_ref):
        r = pl.program_id(0)
        a = jnp.concatenate([ap_ref[rows - HALO:rows], a_ref[...]], axis=0)
        g = jnp.concatenate([gp_ref[rows - HALO:rows], g_ref[...]], axis=0)
        u = a * _sigmoid(g)
        y = _conv_taps(_sub_shifts(u, True), wdw_ref, HALO, rows) + b_ref[...]
        y_ref[...] = y
        xhat, _ = _layernorm_stats(y)
        yn = xhat * lg_ref[...] + lb_ref[...]
        pos = r * rows + lax.broadcasted_iota(jnp.int32, (rows, 1), 0) - PAD
        s = jnp.where(pos >= 0, yn * _sigmoid(yn), 0.0)
        o_ref[...] = jnp.dot(s.astype(BF16), wpw_ref[...], preferred_element_type=F32)

    prev = lambda c: (lambda r: (jnp.maximum(r - 1, 0), c))
    cur = lambda c: (lambda r: (r, c))
    const = lambda shape: pl.BlockSpec(shape, lambda r: (0, 0))
    return _call(body, name="conv_fwd", out_shape=[_sds((lp, D_CONV), F32)] * 2, grid=(lp // rows,),
                 in_specs=[pl.BlockSpec((rows, D_CONV), prev(ca)), pl.BlockSpec((rows, D_CONV), cur(ca)),
                           pl.BlockSpec((rows, D_CONV), prev(cg)), pl.BlockSpec((rows, D_CONV), cur(cg)),
                           const((HALO, D_CONV)), const((1, D_CONV)), const((1, D_CONV)), const((1, D_CONV)),
                           const((D_CONV, D_CONV))],
                 out_specs=[pl.BlockSpec((rows, D_CONV), cur(0))] * 2)(
                     proj, proj, proj, proj, wdw, bdw, ln_g, ln_b, wpw_bf)


def _conv_bwd(proj, d_out, y_conv, wdw, ln_g, ln_b, wpw_bf):
    lp = proj.shape[0]
    rows = BLOCK
    n_chunk = lp // rows
    ca, cg = COL_A // D_CONV, COL_G // D_CONV
    ext = rows + HALO

    def body(ap_ref, a_ref, an_ref, gp_ref, g_ref, gn_ref, do_ref, don_ref, y_ref, yn_ref, wdw_ref, lg_ref, lb_ref,
             wpw_ref, da_ref, dg_ref, dwdw_ref, db_ref, dlg_ref, dlb_ref, dwpw_ref):
        r = pl.program_id(0)

        @pl.when(r == 0)
        def _():
            for ref in (dwdw_ref, db_ref, dlg_ref, dlb_ref, dwpw_ref):
                ref[...] = jnp.zeros(ref.shape, F32)

        a3 = jnp.concatenate([ap_ref[rows - HALO:rows], a_ref[...], an_ref[0:HALO]], axis=0)
        g3 = jnp.concatenate([gp_ref[rows - HALO:rows], g_ref[...], gn_ref[0:HALO]], axis=0)
        sig3 = _sigmoid(g3)
        u3 = a3 * sig3
        u_shifts = _sub_shifts(u3, True)
        y = jnp.concatenate([y_ref[...], yn_ref[0:HALO]], axis=0)
        xhat, rstd = _layernorm_stats(y)
        yn = xhat * lg_ref[...] + lb_ref[...]
        sgm = _sigmoid(yn)
        pos = r * rows + lax.broadcasted_iota(jnp.int32, (ext, 1), 0) - PAD
        valid = (pos >= 0) & (pos < lp - PAD)
        d_ext = jnp.concatenate([do_ref[...], don_ref[0:HALO]], axis=0)
        ds = lax.dot_general(d_ext.astype(BF16), wpw_ref[...], _NT, preferred_element_type=F32)
        dyn = jnp.where(valid, ds * (sgm * (1.0 + yn * (1.0 - sgm))), 0.0)
        dxh = dyn * lg_ref[...]
        dy = rstd * (dxh - jnp.mean(dxh, axis=-1, keepdims=True)
                     - xhat * jnp.mean(dxh * xhat, axis=-1, keepdims=True))
        s_cur = jnp.where(valid[0:rows], (yn * sgm)[0:rows], 0.0)
        dwpw_ref[...] += lax.dot_general(s_cur.astype(BF16), do_ref[...].astype(BF16), _TN,
                                         preferred_element_type=F32)
        dlg_ref[...] += _colsum(dyn[0:rows] * xhat[0:rows])
        dlb_ref[...] += _colsum(dyn[0:rows])
        dy_cur = dy[0:rows]
        db_ref[...] += _colsum(dy_cur)
        dy_shifts = _sub_shifts(dy, False)
        du = None
        for d in range(CONV_WIDTH):
            k = CONV_WIDTH - 1 - d
            dwdw_ref[k:k + 1, :] += _colsum(dy_cur * _lagged(u_shifts, d, HALO, rows, True))
            term = wdw_ref[k:k + 1, :] * _lagged(dy_shifts, d, 0, rows, False)
            du = term if du is None else du + term
        du = jnp.where(pos[0:rows] >= 0, du, 0.0)
        sig = sig3[HALO:HALO + rows]
        da_ref[...] = (du * sig).astype(da_ref.dtype)
        dg_ref[...] = (du * a_ref[...] * sig * (1.0 - sig)).astype(dg_ref.dtype)

    prev = lambda c: (lambda r: (jnp.maximum(r - 1, 0), c))
    cur = lambda c: (lambda r: (r, c))
    nxt = lambda c: (lambda r: (jnp.minimum(r + 1, n_chunk - 1), c))
    const = lambda shape: pl.BlockSpec(shape, lambda r: (0, 0))
    blk = lambda f: pl.BlockSpec((rows, D_CONV), f)
    return _call(body, name="conv_bwd",
                 out_shape=[_sds((lp, D_CONV), BF16), _sds((lp, D_CONV), BF16), _sds((HALO, D_CONV), F32),
                            _sds((1, D_CONV), F32), _sds((1, D_CONV), F32), _sds((1, D_CONV), F32),
                            _sds((D_CONV, D_CONV), F32)],
                 grid=(n_chunk,),
                 in_specs=[blk(prev(ca)), blk(cur(ca)), blk(nxt(ca)), blk(prev(cg)), blk(cur(cg)), blk(nxt(cg)),
                           blk(cur(0)), blk(nxt(0)), blk(cur(0)), blk(nxt(0)),
                           const((HALO, D_CONV)), const((1, D_CONV)), const((1, D_CONV)),
                           const((D_CONV, D_CONV))],
                 out_specs=[blk(cur(0)), blk(cur(0)), const((HALO, D_CONV)), const((1, D_CONV)),
                            const((1, D_CONV)), const((1, D_CONV)), const((D_CONV, D_CONV))],
                 vmem_mb=48)(proj, proj, proj, proj, proj, proj, d_out, d_out, y_conv, y_conv, wdw, ln_g, ln_b, wpw_bf)


_ANY = pl.BlockSpec(memory_space=pl.ANY)


def _place():
    return lax.axis_index("x"), lax.axis_index("y"), lax.axis_index("c")


def _xy_peers(x, y):
    return [(1 - x, y), (x, 1 - y), (1 - x, 1 - y)]


def _exchange_now(name, ex):
    n = ex.n

    def body(*refs):
        ins, outs, sems = refs[:n], refs[n:2 * n], refs[2 * n:]
        ex.run(False, ins, outs, *sems[:3])
        ex.run(True, ins, outs, *sems[:3])
        if ex.split:
            ex.forward(False, outs, *sems[3:])
            ex.forward(True, outs, *sems[3:])

    return _call(body, name=name, out_shape=ex.out_shape(), in_specs=[_ANY] * n, out_specs=[_ANY] * n,
                 scratch=ex.scratch())(*ex.arrs)


def _tail_exchange(name, parts, arrs, small):
    ex = _Exchange("scatter", parts)
    n, m = ex.n, len(arrs)
    flips = [(fx, fy, fc) for fx in (0, 1) for fy in (0, 1) for fc in (0, 1)][1:]

    def body(*refs):
        p_in, a_in, g_in = refs[:n], refs[n:n + m], refs[n + m]
        p_out, a_out, g_out = refs[n + m + 1:2 * n + m + 1], refs[2 * n + m + 1:2 * (n + m) + 1], refs[2 * (n + m) + 1]
        send, recv, local, s_send, s_recv, g_send, g_recv, g_local = refs[2 * (n + m) + 2:]
        x, y, c = _place()
        me = 4 * x + 2 * y + c
        ex.run(False, p_in, p_out, send, recv, local)
        others = [pltpu.make_async_remote_copy(src_ref=a_in[a], dst_ref=a_out[a], send_sem=s_send.at[a],
                                               recv_sem=s_recv.at[a], device_id=(x, y, 1 - c), device_id_type=MESH)
                  for a in range(m)]
        for k, (fx, fy, fc) in enumerate(flips):
            peer = (1 - x if fx else x, 1 - y if fy else y, 1 - c if fc else c)
            others.append(pltpu.make_async_remote_copy(src_ref=g_in, dst_ref=g_out.at[me], send_sem=g_send.at[k],
                                                       recv_sem=g_recv.at[k], device_id=peer, device_id_type=MESH))
        others.append(pltpu.make_async_copy(g_in, g_out.at[me], g_local))
        for cp in others:
            cp.start()
        for cp in others:
            cp.wait()
        ex.run(True, p_in, p_out, send, recv, local)

    dma = pltpu.SemaphoreType.DMA
    res = _call(body, name=name,
                out_shape=ex.out_shape() + [_sds(a.shape, a.dtype) for a in arrs] +
                [_sds((N_DEV,) + small.shape, small.dtype)],
                in_specs=[_ANY] * (n + m + 1), out_specs=[_ANY] * (n + m + 1),
                scratch=ex.scratch() + [dma((m,)), dma((m,)), dma((N_DEV - 1,)), dma((N_DEV - 1,)), dma])(
                    *parts, *arrs, small)
    return list(res[:n]), list(res[n:n + m]), res[n + m]


def _swap_core(name, arrs):
    n = len(arrs)

    def body(*refs):
        ins, outs = refs[:n], refs[n:2 * n]
        send, recv = refs[2 * n:]
        x, y, c = _place()
        remote = []
        for a in range(n):
            cp = pltpu.make_async_remote_copy(src_ref=ins[a], dst_ref=outs[a], send_sem=send.at[a],
                                              recv_sem=recv.at[a], device_id=(x, y, 1 - c), device_id_type=MESH)
            cp.start()
            remote.append(cp)
        for cp in remote:
            cp.wait()

    return _call(body, name=name, out_shape=[_sds(a.shape, a.dtype) for a in arrs],
                 in_specs=[_ANY] * n, out_specs=[_ANY] * n,
                 scratch=[pltpu.SemaphoreType.DMA((n,)), pltpu.SemaphoreType.DMA((n,))])(*arrs)


def _sum_slots(name, stacked, out_dtype=F32):
    s, r, c = stacked.shape
    tr = _pick(r, (256, 128, 64, 8))

    def body(in_ref, o_ref):
        acc = in_ref[0].astype(F32)
        for k in range(1, s):
            acc = acc + in_ref[k].astype(F32)
        o_ref[...] = acc.astype(o_ref.dtype)

    return _call(body, name=name, out_shape=_sds((r, c), out_dtype), grid=(r // tr,),
                 in_specs=[pl.BlockSpec((s, tr, c), lambda i: (0, i, 0))],
                 out_specs=pl.BlockSpec((tr, c), lambda i: (i, 0)))(stacked)


def _sum_slots_layers(name, r0, r1):
    s, r, c = r0.shape
    tr = _pick(r, (256, 128))

    def body(a_ref, b_ref, o_ref):
        def total(ref):
            acc = ref[0].astype(F32)
            for k in range(1, s):
                acc = acc + ref[k].astype(F32)
            return acc

        @pl.when(pl.program_id(0) == 0)
        def _():
            o_ref[...] = total(a_ref)

        @pl.when(pl.program_id(0) == 1)
        def _():
            o_ref[...] = total(b_ref)

    return _call(body, name=name, out_shape=_sds((2, r, c), F32), grid=(2, r // tr),
                 in_specs=[pl.BlockSpec((s, tr, c), lambda l, i: (0, i * (1 - l), 0)),
                           pl.BlockSpec((s, tr, c), lambda l, i: (0, i * l, 0))],
                 out_specs=pl.BlockSpec((None, tr, c), lambda l, i: (l, i, 0)))(r0, r1)


def _adamw_math(w, g, m, v):
    m = ADAM_B1 * m + (1.0 - ADAM_B1) * g
    v = ADAM_B2 * v + (1.0 - ADAM_B2) * (g * g)
    m_hat = m / (1.0 - ADAM_B1 ** ADAM_STEP)
    v_hat = v / (1.0 - ADAM_B2 ** ADAM_STEP)
    delta = -ADAM_LR * (m_hat / (jnp.sqrt(v_hat) + ADAM_EPS) + ADAM_WD * w)
    return delta, m, v


def _adamw(name, w, m, v, g_mine, g_other):
    l, r, c = w.shape
    tr = _pick(r, (128, 64, 8))

    def body(w_ref, m_ref, v_ref, ga_ref, gb_ref, g_ref, d_ref, nm_ref, nv_ref):
        g = ga_ref[...] + gb_ref[...]
        delta, nm, nv = _adamw_math(w_ref[...], g, m_ref[...], v_ref[...])
        g_ref[...] = g
        d_ref[...] = delta
        nm_ref[...] = nm
        nv_ref[...] = nv

    spec = pl.BlockSpec((None, tr, c), lambda li, i: (li, i, 0))
    return _call(body, name=name, out_shape=[_sds(w.shape, F32)] * 4, grid=(l, r // tr),
                 in_specs=[spec] * 5, out_specs=[spec] * 4, vmem_mb=48)(w, m, v, g_mine, g_other)


def _adamw_flat(name, w, m, v, g):
    r, c = w.shape
    tr = _pick(r, (256, 128, 64, 8))

    def body(w_ref, m_ref, v_ref, g_ref, d_ref, nm_ref, nv_ref):
        delta, nm, nv = _adamw_math(w_ref[...], g_ref[...], m_ref[...], v_ref[...])
        d_ref[...] = delta
        nm_ref[...] = nm
        nv_ref[...] = nv

    spec = pl.BlockSpec((tr, c), lambda i: (i, 0))
    return _call(body, name=name, out_shape=[_sds(w.shape, F32)] * 3, grid=(r // tr,),
                 in_specs=[spec] * 4, out_specs=[spec] * 3)(w, m, v, g)


def _pack(arrs, row_multiple=256):
    flat = jnp.concatenate([a.reshape(-1).astype(F32) for a in arrs])
    per = BLOCK * row_multiple
    total = -(-flat.shape[0] // per) * per
    return jnp.pad(flat, (0, total - flat.shape[0])).reshape(total // BLOCK, BLOCK)


def _unpack(buf, shapes):
    flat = buf.reshape(-1)
    outs, off = [], 0
    for s in shapes:
        size = 1
        for d in s:
            size *= d
        outs.append(flat[off:off + size].reshape(s))
        off += size
    return outs


def kernel(x, meta_tokens, pre_mix_g, w_in, w_pool, pool_scale, w_dw, b_dw, conv_ln_g, conv_ln_b, w_pw, mix_out_g, w_out, post_mix_g, pre_ffn_g, w_gate, w_up, w_down, post_ffn_g, loss_target, m_meta_tokens, m_pre_mix_g, m_w_in, m_w_pool, m_pool_scale, m_w_dw, m_b_dw, m_conv_ln_g, m_conv_ln_b, m_w_pw, m_mix_out_g, m_w_out, m_post_mix_g, m_pre_ffn_g, m_w_gate, m_w_up, m_w_down, m_post_ffn_g, v_meta_tokens, v_pre_mix_g, v_w_in, v_w_pool, v_pool_scale, v_w_dw, v_b_dw, v_conv_ln_g, v_conv_ln_b, v_w_pw, v_mix_out_g, v_w_out, v_post_mix_g, v_pre_ffn_g, v_w_gate, v_w_up, v_w_down, v_post_ffn_g):
    seq = x.shape[1]
    lp = PAD + N_META + seq
    depth = w_in.shape[0]
    xy = 2 * lax.axis_index("x") + lax.axis_index("y")

    small_shapes = [meta_tokens.shape, w_dw.shape, w_pw.shape]
    small_local = _pack([meta_tokens, w_dw, w_pw], row_multiple=8)
    big_names = ["w_in", "w_out", "w_gate", "w_up", "w_down"]
    big_local = {(k, l): w[l:l + 1].astype(BF16)
                 for k, w in zip(big_names, (w_in, w_out, w_gate, w_up, w_down)) for l in range(depth)}
    wg = {}
    half = w_in.shape[2] // 2
    first = big_local[("w_in", 0)]
    lo_half, hi_half, small_all = _exchange_now(
        "gather_first", _Exchange("gather", [first[:, :, :half], first[:, :, half:], small_local], split=True))
    wg[("w_in", 0)] = jnp.concatenate([lo_half, hi_half], axis=3)
    metas, wdws, wpws = [], [], []
    for s in range(N_SHARD):
        mt, wd, wp = _unpack(small_all[s], small_shapes)
        metas.append(mt)
        wdws.append(wd)
        wpws.append(wp)
    meta_full = jnp.concatenate(metas, axis=1)
    wdw_full = jnp.concatenate(wdws, axis=2)
    wpw_full = jnp.concatenate(wpws, axis=1)
    wdw_pad = jnp.pad(wdw_full, ((0, 0), (0, HALO - CONV_WIDTH), (0, 0)))
    wpw_bf = wpw_full.astype(BF16)
    wpool_bf = w_pool.reshape(depth, D_POOL, BLOCK).astype(BF16)

    row = lambda a, i: a[i][None, :]

    h = jnp.concatenate([jnp.zeros((PAD, D_MODEL), F32), meta_full, x[0]], axis=0)
    light = dict(tm=_pick(lp, (3 * BLOCK, BLOCK)), vmem_mb=48)
    u = _rowwise("pre_mix_norm0", lambda hh, g: _rms(hh, g), [h], [row(pre_mix_g, 0)], [(D_MODEL, BF16)],
                 **light)[0]
    saved = []
    for i in range(depth):
        proj = _mm_nn_col("in_proj%d" % i, u, wg[("w_in", i)], 0, F32)
        ride = [(k, i) for k in big_names[1:]] + ([("w_in", i + 1)] if i + 1 < depth else [])
        o_attn, tot, got = _attn_fwd(proj, _Exchange("gather", [big_local[k] for k in ride], split=True))
        wg.update(zip(ride, got))
        o_pool = _pool_fwd(proj, wpool_bf[i], row(pool_scale, i))
        o_conv, y_conv = _conv_fwd(proj, wdw_pad[i], row(b_dw, i), row(conv_ln_g, i), row(conv_ln_b, i), wpw_bf[i])

        mix_gains = [row(mix_out_g, i)[:, :D_ATTN], row(mix_out_g, i)[:, D_ATTN:D_ATTN + D_POOL],
                     row(mix_out_g, i)[:, D_ATTN + D_POOL:]]
        merged = _rowwise("merge%d" % i, _merge, [o_attn, o_pool, o_conv], mix_gains, [(D_MODEL, BF16)], **light)[0]
        mix = _mm_nn_row("out_proj%d" % i, merged, wg[("w_out", i)], 0)

        def post_mix(hh, mx, g1, g2):
            h1 = hh + _rms(mx, g1)
            return h1, _rms(h1, g2)

        h1, u2 = _rowwise("post_mix%d" % i, post_mix, [h, mix], [row(post_mix_g, i), row(pre_ffn_g, i)],
                          [(D_MODEL, F32), (D_MODEL, BF16)], **light)
        gate = _mm_nn_col("ffn_gate%d" % i, u2, wg[("w_gate", i)], 0, F32)
        up, act = _mm_nn_col_swiglu("ffn_up%d" % i, u2, wg[("w_up", i)], 0, gate)
        ff = _mm_nn_row("ffn_down%d" % i, act, wg[("w_down", i)], 0)
        rec = dict(h=h, u=u, proj=proj, tot=tot, o_attn=o_attn, o_pool=o_pool, o_conv=o_conv, y_conv=y_conv,
                   merged=merged,
                   mix=mix, h1=h1, u2=u2, gate=gate, up=up, act=act, ff=ff)
        saved.append(rec)
        if i + 1 < depth:
            def post_ffn(hh, f, g1, g2):
                h2 = hh + _rms(f, g1)
                return h2, _rms(h2, g2)

            h, u = _rowwise("post_ffn%d" % i, post_ffn, [h1, ff], [row(post_ffn_g, i), row(pre_mix_g, i + 1)],
                            [(D_MODEL, F32), (D_MODEL, BF16)], **light)
        else:
            def head(row0, hh, f, tgt, g1):
                y = hh + _rms(f, g1)
                rid = row0 + lax.broadcasted_iota(jnp.int32, (y.shape[0], 1), 0)
                err = jnp.where(rid >= PAD + N_META, y - tgt, 0.0)
                part = 0.5 * jnp.sum(jnp.mean(err * err, axis=-1, keepdims=True), axis=0, keepdims=True)
                return err * (1.0 / D_MODEL), jnp.broadcast_to(part, (8, BLOCK))

            dh, loss_part = _rowwise("loss_head", head, [h1, ff, loss_target[0]], [row(post_ffn_g, i)],
                                     [(D_MODEL, F32)], accs=[(8, BLOCK)], with_row0=True, headless={2})

    loss = lax.psum(loss_part[0, 0], ("x", "y", "c"))

    small_grads = {}
    big_parts = {}
    received = {}
    for i in reversed(range(depth)):
        rec = saved[i]

        def post_ffn_b(f, d, g):
            _, vjp = jax.vjp(_rms, f, g)
            df, dg = vjp(d)
            return df, dg

        dff, g_post_ffn = _rowwise("post_ffn_b%d" % i, post_ffn_b, [rec["ff"], dh], [row(post_ffn_g, i)],
                                   [(D_MODEL, BF16)], accs=[(1, D_MODEL)], **light)
        big_parts[("w_down", i)] = _mm_tn_row("dw_down%d" % i, rec["act"], dff)
        dgate, dup = _mm_nt_row_swiglu("d_gate_up%d" % i, dff, wg[("w_down", i)], 0, rec["gate"], rec["up"])
        big_parts[("w_gate", i)] = _mm_tn_col("dw_gate%d" % i, rec["u2"], dgate)
        big_parts[("w_up", i)] = _mm_tn_col("dw_up%d" % i, rec["u2"], dup)
        du2 = _mm_nt_col_pair("d_u2_%d" % i, dgate, wg[("w_gate", i)], dup, wg[("w_up", i)], 0)

        def post_mix_b(h1v, mx, d, da, g1, g2):
            _, vjp2 = jax.vjp(_rms, h1v, g2)
            dh1, dg2 = vjp2(da)
            dmid = d + dh1
            _, vjp1 = jax.vjp(_rms, mx, g1)
            dmx, dg1 = vjp1(dmid)
            return dmid, dmx, dg1, dg2

        dmid, dmix, g_post_mix, g_pre_ffn = _rowwise(
            "post_mix_b%d" % i, post_mix_b, [rec["h1"], rec["mix"], dh, du2],
            [row(post_mix_g, i), row(pre_ffn_g, i)], [(D_MODEL, F32), (D_MODEL, BF16)],
            accs=[(1, D_MODEL), (1, D_MODEL)], vmem_mb=48)
        big_parts[("w_out", i)] = _mm_tn_row("dw_out%d" % i, rec["merged"], dmix)
        dmerged = _mm_nt_row("d_merged%d" % i, dmix, wg[("w_out", i)], 0, F32)

        def merge_b(oa, op, oc, d, ga, gp, gc):
            _, vjp = jax.vjp(_merge, oa, op, oc, ga, gp, gc)
            return vjp(d)

        mix_gains = [row(mix_out_g, i)[:, :D_ATTN], row(mix_out_g, i)[:, D_ATTN:D_ATTN + D_POOL],
                     row(mix_out_g, i)[:, D_ATTN + D_POOL:]]
        do_attn, do_pool, do_conv, g_mo_a, g_mo_p, g_mo_c = _rowwise(
            "merge_b%d" % i, merge_b, [rec["o_attn"], rec["o_pool"], rec["o_conv"], dmerged], mix_gains,
            [(D_ATTN, F32), (D_POOL, F32), (D_CONV, F32)], accs=[(1, D_ATTN), (1, D_POOL), (1, D_CONV)], **light)
        g_mix_out = jnp.concatenate([g_mo_a, g_mo_p, g_mo_c], axis=1)
        du_pool, g_w_pool, g_pool_scale = _pool_bwd(rec["proj"], do_pool, wpool_bf[i], row(pool_scale, i))
        da, dgt, g_w_dw, g_b_dw, g_ln_g, g_ln_b, g_w_pw = _conv_bwd(
            rec["proj"], do_conv, rec["y_conv"], wdw_pad[i], row(conv_ln_g, i), row(conv_ln_b, i), wpw_bf[i])
        big_parts[("w_pw", i)] = g_w_pw.reshape(N_SHARD, D_CONV // N_SHARD, D_CONV).astype(BF16)
        ride = [(k, i) for k in big_names[1:] + ["w_pw"]] + ([("w_in", i + 1)] if i + 1 < depth else [])
        dq, dk, dv, got = _attn_bwd(rec["proj"], rec["tot"], do_attn,
                                    _Exchange("scatter", [big_parts[k] for k in ride]))
        received.update(zip(ride, got))
        dproj = jnp.concatenate([dq, dk, dv, du_pool, da, dgt], axis=1)
        big_parts[("w_in", i)] = _mm_tn_col("dw_in%d" % i, rec["u"], dproj)
        du = _mm_nt_col("d_u%d" % i, dproj, wg[("w_in", i)], 0)

        def pre_mix_b(row0, hv, d, dd, g):
            _, vjp = jax.vjp(_rms, hv, g)
            dhh, dg = vjp(dd)
            out = d + dhh
            return out, dg, jnp.where(row0 == 0, out, 0.0)

        dh, g_pre_mix, dh_head = _rowwise("pre_mix_b%d" % i, pre_mix_b, [rec["h"], dmid, du], [row(pre_mix_g, i)],
                                          [(D_MODEL, F32)], accs=[(1, D_MODEL), (BLOCK, D_MODEL)], with_row0=True,
                                          headless={3} if i == 0 else ())
        small_grads[i] = dict(pre_mix_g=g_pre_mix[0], w_pool=g_w_pool, pool_scale=g_pool_scale[0],
                              w_dw=g_w_dw[:CONV_WIDTH], b_dw=g_b_dw[0], conv_ln_g=g_ln_g[0], conv_ln_b=g_ln_b[0],
                              mix_out_g=g_mix_out[0], post_mix_g=g_post_mix[0],
                              pre_ffn_g=g_pre_ffn[0], post_ffn_g=g_post_ffn[0])

    grad_x = dh[None]
    g_meta_part = dh_head[PAD:PAD + N_META]

    rep_names = ["pre_mix_g", "pool_scale", "b_dw", "conv_ln_g", "conv_ln_b", "mix_out_g", "post_mix_g",
                 "pre_ffn_g", "post_ffn_g", "w_pool"]
    stack2 = lambda nme: jnp.stack([small_grads[l][nme] for l in range(depth)])
    small_list = [stack2(nme) for nme in rep_names] + [g_meta_part, stack2("w_dw")]
    small_list[rep_names.index("w_pool")] = small_list[rep_names.index("w_pool")].reshape(w_pool.shape)
    full_shapes = [a.shape for a in small_list]
    packed = _pack(small_list)

    assert depth == 2
    early = big_names[1:] + ["w_pw"]
    plane_sums = {k: _sum_slots_layers("sum_%s" % k, received[(k, 0)], received[(k, 1)]) for k in early}
    last, swapped, small_slots = _tail_exchange("tail_exchange", [big_parts[("w_in", 0)]],
                                                [plane_sums[k] for k in early], packed)
    summed = _sum_slots("sum_small", small_slots)
    full = _unpack(summed, full_shapes)
    rep_grads = dict(zip(rep_names, full[:len(rep_names)]))
    g_meta = lax.dynamic_slice_in_dim(full[-2], xy * meta_tokens.shape[1], meta_tokens.shape[1], axis=1)
    g_w_dw = lax.dynamic_slice_in_dim(full[-1], xy * w_dw.shape[2], w_dw.shape[2], axis=2)

    rep_w = dict(pre_mix_g=pre_mix_g, pool_scale=pool_scale, b_dw=b_dw, conv_ln_g=conv_ln_g, conv_ln_b=conv_ln_b,
                 mix_out_g=mix_out_g, post_mix_g=post_mix_g, pre_ffn_g=pre_ffn_g, post_ffn_g=post_ffn_g,
                 w_pool=w_pool)
    rep_m = dict(pre_mix_g=m_pre_mix_g, pool_scale=m_pool_scale, b_dw=m_b_dw, conv_ln_g=m_conv_ln_g,
                 conv_ln_b=m_conv_ln_b, mix_out_g=m_mix_out_g, post_mix_g=m_post_mix_g, pre_ffn_g=m_pre_ffn_g,
                 post_ffn_g=m_post_ffn_g, w_pool=m_w_pool)
    rep_v = dict(pre_mix_g=v_pre_mix_g, pool_scale=v_pool_scale, b_dw=v_b_dw, conv_ln_g=v_conv_ln_g,
                 conv_ln_b=v_conv_ln_b, mix_out_g=v_mix_out_g, post_mix_g=v_post_mix_g, pre_ffn_g=v_pre_ffn_g,
                 post_ffn_g=v_post_ffn_g, w_pool=v_w_pool)
    sm_names = rep_names + ["meta_tokens", "w_dw"]
    sm_w = [rep_w[k] for k in rep_names] + [meta_tokens, w_dw]
    sm_m = [rep_m[k] for k in rep_names] + [m_meta_tokens, m_w_dw]
    sm_v = [rep_v[k] for k in rep_names] + [v_meta_tokens, v_w_dw]
    sm_g = [rep_grads[k] for k in rep_names] + [g_meta, g_w_dw]
    sm_shapes = [a.shape for a in sm_w]
    sm_delta, sm_nm, sm_nv = _adamw_flat("adamw_small", _pack(sm_w), _pack(sm_m), _pack(sm_v), _pack(sm_g))
    small_out = {}
    for k, g, d, nm, nv in zip(sm_names, sm_g, _unpack(sm_delta, sm_shapes), _unpack(sm_nm, sm_shapes),
                               _unpack(sm_nv, sm_shapes)):
        small_out[k] = (g, d, nm, nv)

    other_sums = dict(zip(early, swapped))
    received[("w_in", 0)] = last[0]
    plane_sums["w_in"] = _sum_slots_layers("sum_w_in", received[("w_in", 0)], received[("w_in", 1)])
    other_sums["w_in"] = _swap_core("swap_core_w_in", [plane_sums["w_in"]])[0]
    big_w = dict(w_in=(w_in, m_w_in, v_w_in), w_out=(w_out, m_w_out, v_w_out), w_gate=(w_gate, m_w_gate, v_w_gate),
                 w_up=(w_up, m_w_up, v_w_up), w_down=(w_down, m_w_down, v_w_down), w_pw=(w_pw, m_w_pw, v_w_pw))
    big_out = {}
    for k in early + ["w_in"]:
        w, m, v = big_w[k]
        big_out[k] = _adamw("adamw_%s" % k, w, m, v, plane_sums[k], other_sums[k])

    order = ["meta_tokens", "pre_mix_g", "w_in", "w_pool", "pool_scale", "w_dw", "b_dw", "conv_ln_g", "conv_ln_b",
             "w_pw", "mix_out_g", "w_out", "post_mix_g", "pre_ffn_g", "w_gate", "w_up", "w_down", "post_ffn_g"]
    res = lambda k: big_out[k] if k in big_out else small_out[k]
    outs = [loss, grad_x]
    for part in range(4):
        outs += [res(k)[part] for k in order]
    return tuple(outs)
```
